```python
import math
import jax
import jax.numpy as jnp
from jax import lax
import numpy as np

D_MODEL = 2048
BATCH = 2
SEQ = 4096
DEPTH = 1

MEM_LEN = 256
HEAD_DIM = 128
HG_HEADS = 8
HG_WIDTH = HG_HEADS * HEAD_DIM
HG_CHUNK = 64
DILATED_GROUPS = ((128, 1), (512, 4), (2048, 16))
N_GROUPS = 3
N_KV_HEADS = 8
N_Q_HEADS = N_GROUPS * N_KV_HEADS
ATT_Q_WIDTH = N_Q_HEADS * HEAD_DIM
ATT_KV_WIDTH = N_KV_HEADS * HEAD_DIM
ATT_WIDTH = ATT_KV_WIDTH
WIN_BLOCK = 128
ROPE_THETA = 10000.0
COL_SIZES = (HG_WIDTH, HG_WIDTH, HG_WIDTH, HG_WIDTH,
             ATT_Q_WIDTH, ATT_KV_WIDTH, ATT_KV_WIDTH,
             D_MODEL, D_MODEL)
IN_COLS = 4 * HG_WIDTH + ATT_Q_WIDTH + 2 * ATT_KV_WIDTH + 2 * D_MODEL
CROSS_HEADS = 4
CROSS_WIDTH = CROSS_HEADS * HEAD_DIM
N_EXPERTS = 32
TOP_K = 4
D_FF = D_MODEL
SWIGLU_ALPHA = 1.702
SWIGLU_LIMIT = 7.0
EXPERT_ROWS = 128
NORM_EPS = 1e-6

kernel_name = "hybrid_hgrn2_dilated_attn_moe_block"


def rms_norm(x, g):
    xf = x.astype(jnp.float32)
    y = xf * lax.rsqrt(jnp.mean(xf * xf, axis=-1, keepdims=True) + NORM_EPS)
    return (y * g.astype(jnp.float32)).astype(x.dtype)


def rotary(x, positions):
    half = x.shape[-1] // 2
    inv_freq = 1.0 / (ROPE_THETA ** (jnp.arange(half, dtype=jnp.float32) / half))
    ang = positions.astype(jnp.float32)[..., None] * inv_freq
    cos = jnp.cos(ang)[:, :, None, :]
    sin = jnp.sin(ang)[:, :, None, :]
    xf = x.astype(jnp.float32)
    x1, x2 = xf[..., :half], xf[..., half:]
    return jnp.concatenate([x1 * cos - x2 * sin, x2 * cos + x1 * sin], axis=-1).astype(x.dtype)


def hgrn2_recurrence(q, k, v, logf):
    B, H, S, dk = q.shape
    dv = v.shape[-1]
    nc = S // HG_CHUNK

    def to_chunks(a):
        return a.astype(jnp.float32).reshape(B, H, nc, HG_CHUNK, a.shape[-1]).transpose(2, 0, 1, 3, 4)

    qc, kc, vc, gc = to_chunks(q), to_chunks(k), to_chunks(v), to_chunks(logf)
    causal = jnp.tril(jnp.ones((HG_CHUNK, HG_CHUNK), dtype=bool))[:, :, None]

    def step(state, inp):
        q_, k_, v_, g_ = inp
        b = jnp.cumsum(g_, axis=-2)
        o_inter = jnp.einsum('bhtk,bhkv->bhtv', q_ * jnp.exp(b), state)
        diff = b[:, :, :, None, :] - b[:, :, None, :, :]
        decay = jnp.exp(jnp.where(causal, diff, -jnp.inf))
        scores = jnp.einsum('bhtk,bhsk,bhtsk->bhts', q_, k_, decay)
        o_intra = jnp.einsum('bhts,bhsv->bhtv', scores, v_)
        b_last = b[:, :, -1:, :]
        new_state = (jnp.exp(b_last[:, :, 0, :])[..., None] * state
                     + jnp.einsum('bhsk,bhsv->bhkv', k_ * jnp.exp(b_last - b), v_))
        return new_state, o_inter + o_intra

    init = jnp.zeros((B, H, dk, dv), jnp.float32)
    _, out = lax.scan(step, init, (qc, kc, vc, gc))
    return out.transpose(1, 2, 0, 3, 4).reshape(B, H, S, dv).astype(q.dtype)


def dilated_window_attention(q, k, v, window, dilation):
    B, S, H, hd = q.shape
    steps = window // dilation
    L = S // dilation
    nb = -(-L // WIN_BLOCK)
    Lp = nb * WIN_BLOCK

    def strided(a):
        a = a.reshape(B, L, dilation, H, hd).transpose(0, 2, 3, 1, 4)
        return jnp.pad(a, ((0, 0), (0, 0), (0, 0), (0, Lp - L), (0, 0)))

    def windows(a):
        a = jnp.pad(a, ((0, 0), (0, 0), (0, 0), (WIN_BLOCK, 0), (0, 0)))
        a = a.reshape(B, dilation, H, nb + 1, WIN_BLOCK, hd)
        return jnp.concatenate([a[:, :, :, :-1], a[:, :, :, 1:]], axis=-2)

    qb = strided(q).reshape(B, dilation, H, nb, WIN_BLOCK, hd)
    kw = windows(strided(k))
    vw = windows(strided(v))
    s = jnp.einsum('brhnqc,brhnkc->brhnqk', qb, kw).astype(jnp.float32) * (HEAD_DIM ** -0.5)
    tq = jnp.arange(WIN_BLOCK)[:, None]
    tk = jnp.arange(2 * WIN_BLOCK)[None, :]
    dist = WIN_BLOCK + tq - tk
    band = (dist >= 0) & (dist <= steps)
    before_start = (jnp.arange(nb)[:, None, None] == 0) & (tk < WIN_BLOCK)[None]
    valid = band[None] & ~before_start
    s = jnp.where(valid, s, -jnp.inf)
    m = jnp.max(s, axis=-1)
    p = jnp.exp(s - m[..., None])
    l = jnp.sum(p, axis=-1)
    numer = jnp.einsum('brhnqk,brhnkc->brhnqc', p, vw.astype(jnp.float32))

    def unstride(a):
        a = a.reshape((B, dilation, H, Lp) + a.shape[5:])[:, :, :, :L]
        perm = (0, 3, 1, 2) + tuple(range(4, a.ndim))
        return a.transpose(perm).reshape((B, S, H) + a.shape[4:])

    return unstride(numer), unstride(m), unstride(l)


def hybrid_mixer(xn, positions, w_in, lb, hgrn_norm_g, w_br_hgrn, w_br_attn, w_out):
    B, S, _ = xn.shape
    proj = xn @ w_in
    parts = []
    start = 0
    for size in COL_SIZES:
        parts.append(proj[..., start:start + size])
        start += size
    hq, hf, hi, hg, aq, ak, av, ga, gb = parts

    lb = lb.astype(jnp.float32)
    hf32 = hf.astype(jnp.float32)
    f = lb + (1.0 - lb) * jax.nn.sigmoid(hf32)
    k_in = (1.0 - lb) * jax.nn.sigmoid(-hf32)
    logf = jnp.log(f)

    def heads(a):
        return a.reshape(B, S, HG_HEADS, HEAD_DIM).transpose(0, 2, 1, 3)

    o_h = hgrn2_recurrence(heads(hq), heads(k_in), heads(hi), heads(logf))
    o_h = o_h.transpose(0, 2, 1, 3)
    o_h = rms_norm(o_h, hgrn_norm_g.reshape(HG_HEADS, HEAD_DIM))
    o_h = o_h.reshape(B, S, HG_WIDTH) * jax.nn.silu(hg)

    q = rotary(aq.reshape(B, S, N_Q_HEADS, HEAD_DIM), positions)
    k = rotary(ak.reshape(B, S, N_KV_HEADS, HEAD_DIM), positions)
    v = av.reshape(B, S, N_KV_HEADS, HEAD_DIM)
    numers, maxes, dens = [], [], []
    for gi, (window, dilation) in enumerate(DILATED_GROUPS):
        qg = q[:, :, gi * N_KV_HEADS:(gi + 1) * N_KV_HEADS]
        n_, m_, l_ = dilated_window_attention(qg, k, v, window, dilation)
        numers.append(n_)
        maxes.append(m_)
        dens.append(l_)
    numers = jnp.stack(numers)
    maxes = jnp.stack(maxes)
    dens = jnp.stack(dens)
    wgt = jnp.exp(maxes - jnp.max(maxes, axis=0, keepdims=True))
    o_a = jnp.sum(wgt[..., None] * numers, axis=0) / jnp.sum(wgt * dens, axis=0)[..., None]
    o_a = o_a.astype(xn.dtype).reshape(B, S, ATT_WIDTH)

    merged = jax.nn.sigmoid(ga) * (o_h @ w_br_hgrn) + jax.nn.sigmoid(gb) * (o_a @ w_br_attn)
    return merged @ w_out


def memory_cross_attention(xn, memn, w_cq, w_ckv, w_co):
    B, S, _ = xn.shape
    M = memn.shape[1]
    q = (xn @ w_cq).reshape(B, S, CROSS_HEADS, HEAD_DIM)
    kv = memn @ w_ckv
    k = kv[..., :CROSS_WIDTH].reshape(B, M, CROSS_HEADS, HEAD_DIM)
    v = kv[..., CROSS_WIDTH:].reshape(B, M, CROSS_HEADS, HEAD_DIM)
    s = jnp.einsum('bqhc,bkhc->bhqk', q, k).astype(jnp.float32) * (HEAD_DIM ** -0.5)
    p = jax.nn.softmax(s, axis=-1).astype(v.dtype)
    o = jnp.einsum('bhqk,bkhc->bqhc', p, v).reshape(B, S, CROSS_WIDTH)
    return o @ w_co


def clamped_swiglu(h):
    h_glu = jnp.minimum(h[..., ::2], SWIGLU_LIMIT)
    h_lin = jnp.clip(h[..., 1::2], -SWIGLU_LIMIT, SWIGLU_LIMIT)
    return h_glu * jax.nn.sigmoid(SWIGLU_ALPHA * h_glu) * (h_lin + 1.0)


def moe_ffn(xn, w_router, b_router, w_mlp1, b_mlp1, w_mlp2, b_mlp2):
    B, S, D = xn.shape
    tok = xn.reshape(-1, D)
    N = tok.shape[0]
    logits = (tok @ w_router + b_router).astype(jnp.float32)
    top_val, top_idx = lax.top_k(logits, TOP_K)
    top_w = jax.nn.softmax(top_val, axis=-1).astype(tok.dtype)
    A = N * TOP_K
    flat_e = top_idx.reshape(A)
    flat_tok = jnp.repeat(jnp.arange(N, dtype=jnp.int32), TOP_K)
    flat_w = top_w.reshape(A)
    order = jnp.argsort(flat_e)
    sorted_e = flat_e[order]
    counts = jnp.bincount(flat_e, length=N_EXPERTS)
    padded = (counts + EXPERT_ROWS - 1) // EXPERT_ROWS * EXPERT_ROWS
    start = jnp.cumsum(counts) - counts
    pstart = jnp.cumsum(padded) - padded
    dest = pstart[sorted_e] + (jnp.arange(A, dtype=jnp.int32) - start[sorted_e])
    R = A + N_EXPERTS * EXPERT_ROWS
    NB = R // EXPERT_ROWS
    row_tok = jnp.zeros((R,), jnp.int32).at[dest].set(flat_tok[order])
    row_w = jnp.zeros((R,), tok.dtype).at[dest].set(flat_w[order])
    block_e = jnp.searchsorted(jnp.cumsum(padded), jnp.arange(NB) * EXPERT_ROWS, side='right')
    block_e = jnp.minimum(block_e, N_EXPERTS - 1).astype(jnp.int32)
    xb = tok[row_tok].reshape(NB, EXPERT_ROWS, D)

    def expert_block(args):
        xblk, e = args
        h = xblk @ w_mlp1[e] + b_mlp1[e]
        return clamped_swiglu(h) @ w_mlp2[e] + b_mlp2[e]

    yb = lax.map(expert_block, (xb, block_e)).reshape(R, D)
    y = jnp.zeros_like(tok).at[row_tok].add(yb * row_w[:, None])
    return y.reshape(B, S, D)


def setup_inputs(seed: int = 0) -> dict:
    key = jax.random.key(seed)
    ks = jax.random.split(key, 24)
    f32 = jnp.float32

    def nrm(k, shape, fan_in):
        return jax.random.normal(k, shape, f32) * (fan_in ** -0.5)

    def gain(k, shape):
        return 1.0 + 0.02 * jax.random.normal(k, shape, f32)

    def small(k, shape):
        return 0.01 * jax.random.normal(k, shape, f32)

    x = jax.random.normal(ks[0], (BATCH, SEQ, D_MODEL), f32)
    mem = jax.random.normal(ks[1], (BATCH, MEM_LEN, D_MODEL), f32)
    offsets = jax.random.randint(ks[2], (BATCH, 1), 0, SEQ, dtype=jnp.int32)
    positions = (jnp.arange(SEQ, dtype=jnp.int32)[None, :] + offsets).astype(jnp.int32)
    return {
        "x": x,
        "mem": mem,
        "positions": positions,
        "norm_mix_g": gain(ks[3], (DEPTH, D_MODEL)),
        "w_in": nrm(ks[4], (DEPTH, D_MODEL, IN_COLS), D_MODEL),
        "lb_raw": 0.1 * jax.random.normal(ks[5], (DEPTH + 1, HG_WIDTH), f32),
        "hgrn_norm_g": gain(ks[6], (DEPTH, HG_WIDTH)),
        "w_br_hgrn": nrm(ks[7], (DEPTH, HG_WIDTH, D_MODEL), HG_WIDTH),
        "w_br_attn": nrm(ks[8], (DEPTH, ATT_WIDTH, D_MODEL), ATT_WIDTH),
        "w_out": nrm(ks[9], (DEPTH, D_MODEL, D_MODEL), D_MODEL),
        "norm_cross_g": gain(ks[10], (DEPTH, D_MODEL)),
        "norm_mem_g": gain(ks[11], (D_MODEL,)),
        "w_cq": nrm(ks[12], (DEPTH, D_MODEL, CROSS_WIDTH), D_MODEL),
        "w_ckv": nrm(ks[13], (DEPTH, D_MODEL, 2 * CROSS_WIDTH), D_MODEL),
        "w_co": nrm(ks[14], (DEPTH, CROSS_WIDTH, D_MODEL), CROSS_WIDTH),
        "norm_moe_g": gain(ks[15], (DEPTH, D_MODEL)),
        "w_router": nrm(ks[16], (DEPTH, D_MODEL, N_EXPERTS), D_MODEL),
        "b_router": small(ks[17], (DEPTH, N_EXPERTS)),
        "w_mlp1": nrm(ks[18], (DEPTH, N_EXPERTS, D_MODEL, 2 * D_FF), D_MODEL),
        "b_mlp1": small(ks[19], (DEPTH, N_EXPERTS, 2 * D_FF)),
        "w_mlp2": nrm(ks[20], (DEPTH, N_EXPERTS, D_FF, D_MODEL), D_FF),
        "b_mlp2": small(ks[21], (DEPTH, N_EXPERTS, D_MODEL)),
        "norm_final_g": gain(ks[22], (D_MODEL,)),
    }


def reference(x, mem, positions, norm_mix_g, w_in, lb_raw, hgrn_norm_g, w_br_hgrn, w_br_attn,
              w_out, norm_cross_g, norm_mem_g, w_cq, w_ckv, w_co, norm_moe_g, w_router, b_router,
              w_mlp1, b_mlp1, w_mlp2, b_mlp2, norm_final_g):
    lower_bounds = jnp.cumsum(jax.nn.softmax(lb_raw.astype(jnp.float32), axis=0), axis=0)
    memn = rms_norm(mem, norm_mem_g)
    h = x
    for layer in range(DEPTH):
        n = rms_norm(h, norm_mix_g[layer])
        h = h + hybrid_mixer(n, positions, w_in[layer], lower_bounds[layer], hgrn_norm_g[layer],
                             w_br_hgrn[layer], w_br_attn[layer], w_out[layer])
        n = rms_norm(h, norm_cross_g[layer])
        h = h + memory_cross_attention(n, memn, w_cq[layer], w_ckv[layer], w_co[layer])
        n = rms_norm(h, norm_moe_g[layer])
        h = h + moe_ffn(n, w_router[layer], b_router[layer], w_mlp1[layer], b_mlp1[layer],
                        w_mlp2[layer], b_mlp2[layer])
    return rms_norm(h, norm_final_g)
```

```python
import functools

import jax
import jax.numpy as jnp
from jax import lax
from jax.experimental import pallas as pl
from jax.experimental.pallas import tpu as pltpu

F32 = jnp.float32
BF16 = jnp.bfloat16
I32 = jnp.int32

D_MODEL = 2048
HEAD_DIM = 128
HG_HEADS = 8
HG_WIDTH = HG_HEADS * HEAD_DIM
N_KV_HEADS = 8
N_GROUPS = 3
ATT_Q_WIDTH = N_GROUPS * N_KV_HEADS * HEAD_DIM
ATT_KV_WIDTH = N_KV_HEADS * HEAD_DIM
WIN_BLOCK = 128
ROPE_THETA = 10000.0
CROSS_HEADS = 4
CROSS_WIDTH = CROSS_HEADS * HEAD_DIM
N_EXPERTS = 32
TOP_K = 4
TOP_K_LOG2 = 2
D_FF = D_MODEL
SWIGLU_ALPHA = 1.702
SWIGLU_LIMIT = 7.0
NORM_EPS = 1e-6

V7X_LANES = 128
V7X_VMEM_LIMIT_BYTES = 56 * 1024 * 1024

DIL_MAX = 16
ATT_TILE = DIL_MAX * WIN_BLOCK
HG_BLOCK = 16
NEG_BIG = -1e30

MOE_TMAX = 1024
MOE_ROW_ALIGN = 128
MOE_TF = 256
MOE_SLAB = D_MODEL // V7X_LANES
DISPATCH_CHUNK = 2048
DMA_WAIT_ROWS = 128


def _rms(x, g):
    ms = jnp.mean(x * x, axis=-1, keepdims=True)
    return x * lax.rsqrt(ms + NORM_EPS) * g


def _cparams(sem, vmem=V7X_VMEM_LIMIT_BYTES):
    return pltpu.CompilerParams(dimension_semantics=sem, vmem_limit_bytes=vmem)


def _in_proj_kernel(x_ref, g_ref, w_ref, o_ref, xn_ref):
    @pl.when(pl.program_id(1) == 0)
    def _():
        xn_ref[...] = _rms(x_ref[...], g_ref[...]).astype(BF16)

    o_ref[...] = jnp.dot(xn_ref[...], w_ref[...], preferred_element_type=F32).astype(o_ref.dtype)


def _in_proj(x2d, g, w, *, tm, tn, name):
    n, d = x2d.shape
    wc = w.shape[1]
    return pl.pallas_call(
        _in_proj_kernel,
        out_shape=jax.ShapeDtypeStruct((n, wc), BF16),
        grid=(n // tm, wc // tn),
        in_specs=[
            pl.BlockSpec((tm, d), lambda i, j: (i, 0)),
            pl.BlockSpec((1, d), lambda i, j: (0, 0)),
            pl.BlockSpec((d, tn), lambda i, j: (0, j)),
        ],
        out_specs=pl.BlockSpec((tm, tn), lambda i, j: (i, j)),
        scratch_shapes=[pltpu.VMEM((tm, d), BF16)],
        compiler_params=_cparams(("parallel", "arbitrary")),
        name=name,
    )(x2d, g.reshape(1, d), w)


def _in_proj_strided_kernel(x_ref, g_ref, w_ref, o_ref, xn_ref, *, c):
    @pl.when(pl.program_id(3) == 0)
    def _():
        for ci in range(c):
            x = x_ref[:, ci * D_MODEL:(ci + 1) * D_MODEL]
            xn_ref[ci * WIN_BLOCK:(ci + 1) * WIN_BLOCK, :] = _rms(x, g_ref[...]).astype(BF16)

    res = jnp.dot(xn_ref[...], w_ref[...], preferred_element_type=F32)
    for ci in range(c):
        o_ref[ci] = res[ci * WIN_BLOCK:(ci + 1) * WIN_BLOCK, :].astype(o_ref.dtype)


def _in_proj_strided(x, g, w, *, c, tn, name):
    b, s, d = x.shape
    nt = s // ATT_TILE
    wc = w.shape[1]
    xv = x.reshape(b, nt, WIN_BLOCK, DIL_MAX * d)
    return pl.pallas_call(
        functools.partial(_in_proj_strided_kernel, c=c),
        out_shape=jax.ShapeDtypeStruct((b, nt, DIL_MAX, WIN_BLOCK, wc), BF16),
        grid=(b, nt, DIL_MAX // c, wc // tn),
        in_specs=[
            pl.BlockSpec((None, None, WIN_BLOCK, c * d), lambda bi, ti, ri, j: (bi, ti, 0, ri)),
            pl.BlockSpec((1, d), lambda bi, ti, ri, j: (0, 0)),
            pl.BlockSpec((d, tn), lambda bi, ti, ri, j: (0, j)),
        ],
        out_specs=pl.BlockSpec((None, None, c, WIN_BLOCK, tn), lambda bi, ti, ri, j: (bi, ti, ri, 0, j)),
        scratch_shapes=[pltpu.VMEM((c * WIN_BLOCK, d), BF16)],
        compiler_params=_cparams(("parallel", "parallel", "parallel", "arbitrary")),
        name=name,
    )(xv, g.reshape(1, d), w)


def _hgrn_kernel(q_ref, f_ref, i_ref, g_ref, lb_ref, gn_ref, o_ref, st_ref, kin_s, b_s, v_s, *, ts):
    @pl.when(pl.program_id(2) == 0)
    def _():
        st_ref[...] = jnp.zeros_like(st_ref)

    lb = lb_ref[...]
    gn = gn_ref[...]
    row = lax.broadcasted_iota(I32, (HG_BLOCK, HEAD_DIM), 0)
    nt_dims = (((1,), (1,)), ((), ()))
    tn_dims = (((0,), (0,)), ((), ()))

    def body(i, carry):
        t0 = pl.multiple_of(i * HG_BLOCK, HG_BLOCK)
        q = q_ref[pl.ds(t0, HG_BLOCK), :].astype(F32)
        hf = f_ref[pl.ds(t0, HG_BLOCK), :].astype(F32)
        v = i_ref[pl.ds(t0, HG_BLOCK), :].astype(F32)
        hg = g_ref[pl.ds(t0, HG_BLOCK), :].astype(F32)
        f = lb + (1.0 - lb) * jax.nn.sigmoid(hf)
        kin = (1.0 - lb) * jax.nn.sigmoid(-hf)
        b = jnp.log(f)
        for sh in (1, 2, 4, 8):
            b = b + jnp.where(row >= sh, pltpu.roll(b, sh, 0), 0.0)
        kin_s[...] = kin
        b_s[...] = b
        v_s[...] = v
        st = st_ref[...]
        qd = (q * jnp.exp(b)).astype(BF16)
        o = lax.dot_general(qd, st.astype(BF16), nt_dims, preferred_element_type=F32)
        for s in range(HG_BLOCK):
            ks = kin_s[s:s + 1, :]
            bs = b_s[s:s + 1, :]
            vs = v_s[s:s + 1, :]
            w = jnp.where(row >= s, q * ks * jnp.exp(jnp.minimum(b - bs, 0.0)), 0.0)
            o = o + jnp.sum(w, axis=-1, keepdims=True) * vs
        bl = b_s[HG_BLOCK - 1:HG_BLOCK, :]
        kd = (kin * jnp.exp(bl - b)).astype(BF16)
        upd = lax.dot_general(v.astype(BF16), kd, tn_dims, preferred_element_type=F32)
        st_ref[...] = st * jnp.exp(bl) + upd
        og = _rms(o, gn) * (hg * jax.nn.sigmoid(hg))
        o_ref[pl.ds(t0, HG_BLOCK), :] = og.astype(o_ref.dtype)
        return carry

    lax.fori_loop(0, ts // HG_BLOCK, body, 0)


def _hgrn(proj_h, lb, gn, *, ts, name="hgrn"):
    b, s, _ = proj_h.shape
    h = HG_HEADS

    def spec(off):
        return pl.BlockSpec((None, ts, HEAD_DIM), lambda bi, hi, si: (bi, si, off + hi))

    vec = pl.BlockSpec((1, HEAD_DIM), lambda bi, hi, si: (0, hi))
    return pl.pallas_call(
        functools.partial(_hgrn_kernel, ts=ts),
        out_shape=jax.ShapeDtypeStruct((b, s, HG_WIDTH), BF16),
        grid=(b, h, s // ts),
        in_specs=[spec(0), spec(h), spec(2 * h), spec(3 * h), vec, vec],
        out_specs=pl.BlockSpec((None, ts, HEAD_DIM), lambda bi, hi, si: (bi, si, hi)),
        scratch_shapes=[
            pltpu.VMEM((HEAD_DIM, HEAD_DIM), F32),
            pltpu.VMEM((HG_BLOCK, HEAD_DIM), F32),
            pltpu.VMEM((HG_BLOCK, HEAD_DIM), F32),
            pltpu.VMEM((HG_BLOCK, HEAD_DIM), F32),
        ],
        compiler_params=_cparams(("parallel", "parallel", "arbitrary")),
        name=name,
    )(proj_h, proj_h, proj_h, proj_h, lb.reshape(1, HG_WIDTH), gn.reshape(1, HG_WIDTH))


def _attn_bias(kind):
    rq = lax.broadcasted_iota(I32, (WIN_BLOCK, 2 * WIN_BLOCK), 0)
    ck = lax.broadcasted_iota(I32, (WIN_BLOCK, 2 * WIN_BLOCK), 1)
    if kind == 2:
        dist = rq + WIN_BLOCK - ck
        first = ck < WIN_BLOCK
    elif kind == 1:
        dist = 4 * ((rq & 31) - (ck & 63) + 32) + ((rq >> 5) - (ck >> 6))
        first = (ck & 63) < 32
    else:
        dist = 16 * ((rq & 7) - (ck & 15) + 8) + ((rq >> 3) - (ck >> 4))
        first = (ck & 15) < 8
    valid = (dist >= 0) & (dist <= WIN_BLOCK)
    return (jnp.where(valid, 0.0, NEG_BIG).astype(F32),
            jnp.where(valid & jnp.logical_not(first), 0.0, NEG_BIG).astype(F32))


def _attn_kernel(q0_ref, q1_ref, q2_ref, k_ref, v_ref, cs_ref, sn_ref, o_hbm,
                 qr, kext, vext, acc, mrun, lrun, bias, stage, sem):
    bi = pl.program_id(0)
    hi = pl.program_id(1)
    ti = pl.program_id(2)
    wb = WIN_BLOCK
    scale = HEAD_DIM ** -0.5

    @pl.when(ti == 0)
    def _():
        kext[:, 0:wb, :] = jnp.zeros((DIL_MAX, wb, HEAD_DIM), F32)
        vext[:, 0:wb, :] = jnp.zeros((DIL_MAX, wb, HEAD_DIM), F32)

    @pl.when(ti > 0)
    def _():
        kext[:, 0:wb, :] = kext[:, wb:2 * wb, :]
        vext[:, 0:wb, :] = vext[:, wb:2 * wb, :]

    for kind in range(N_GROUPS):
        full, nofirst = _attn_bias(kind)
        bias[2 * kind] = full
        bias[2 * kind + 1] = nofirst

    def rope_body(r, carry):
        cs = cs_ref[r]
        sn = sn_ref[r]
        for g, qref in enumerate((q0_ref, q1_ref, q2_ref)):
            q = qref[r].astype(F32)
            qr[g, r] = (q * cs + pltpu.roll(q, HEAD_DIM // 2, 1) * sn) * scale
        k = k_ref[r].astype(F32)
        kext[r, wb:2 * wb, :] = k * cs + pltpu.roll(k, HEAD_DIM // 2, 1) * sn
        vext[r, wb:2 * wb, :] = v_ref[r].astype(F32)
        acc[r] = jnp.zeros((wb, HEAD_DIM), F32)
        mrun[r] = jnp.full((wb, HEAD_DIM), NEG_BIG, F32)
        lrun[r] = jnp.zeros((wb, HEAD_DIM), F32)
        return carry

    lax.fori_loop(0, DIL_MAX, rope_body, 0)

    nt_dims = (((1,), (1,)), ((), ()))

    def block(qb, kb, vb, bias_blk):
        s = lax.dot_general(qb.astype(BF16), kb.astype(BF16), nt_dims, preferred_element_type=F32)
        s = s + bias_blk
        m = jnp.max(s, axis=-1, keepdims=True)
        p = jnp.exp(s - m)
        l = jnp.sum(p, axis=-1, keepdims=True)
        n = jnp.dot(p.astype(BF16), vb.astype(BF16), preferred_element_type=F32)
        return n, jnp.broadcast_to(m, (wb, HEAD_DIM)), jnp.broadcast_to(l, (wb, HEAD_DIM))

    def merge(r, rows, n, m, l):
        m_old = mrun[r, rows, :]
        m_new = jnp.maximum(m_old, m)
        a = jnp.exp(m_old - m_new)
        bb = jnp.exp(m - m_new)
        acc[r, rows, :] = acc[r, rows, :] * a + n * bb
        lrun[r, rows, :] = lrun[r, rows, :] * a + l * bb
        mrun[r, rows, :] = m_new

    first_tile = jnp.where(ti == 0, 1, 0)

    def g2_body(r, carry):
        n, m, l = block(qr[2, r], kext[r], vext[r], bias[4 + first_tile])
        merge(r, pl.ds(0, wb), n, m, l)
        return carry

    lax.fori_loop(0, DIL_MAX, g2_body, 0)

    def g1_body(it, carry):
        r4 = it & 3
        mb = it >> 2
        q0 = pl.multiple_of(32 * mb, 32)
        k0 = pl.multiple_of(96 + 32 * mb, 32)
        qb = jnp.concatenate([qr[1, r4 + 4 * j, pl.ds(q0, 32), :] for j in range(4)], axis=0)
        kb = jnp.concatenate([kext[r4 + 4 * j, pl.ds(k0, 64), :] for j in range(4)], axis=0)
        vb = jnp.concatenate([vext[r4 + 4 * j, pl.ds(k0, 64), :] for j in range(4)], axis=0)
        use_first = jnp.where(mb == 0, first_tile, 0)
        n, m, l = block(qb, kb, vb, bias[2 + use_first])
        for j in range(4):
            sl = slice(32 * j, 32 * (j + 1))
            merge(r4 + 4 * j, pl.ds(q0, 32), n[sl], m[sl], l[sl])
        return carry

    lax.fori_loop(0, DIL_MAX, g1_body, 0)

    def g0_body(mb, carry):
        q0 = pl.multiple_of(8 * mb, 8)
        k0 = pl.multiple_of(120 + 8 * mb, 8)
        qb = jnp.concatenate([qr[0, r, pl.ds(q0, 8), :] for r in range(DIL_MAX)], axis=0)
        kb = jnp.concatenate([kext[r, pl.ds(k0, 16), :] for r in range(DIL_MAX)], axis=0)
        vb = jnp.concatenate([vext[r, pl.ds(k0, 16), :] for r in range(DIL_MAX)], axis=0)
        use_first = jnp.where(mb == 0, first_tile, 0)
        n, m, l = block(qb, kb, vb, bias[use_first])
        for r in range(DIL_MAX):
            sl = slice(8 * r, 8 * (r + 1))
            merge(r, pl.ds(q0, 8), n[sl], m[sl], l[sl])
        return carry

    lax.fori_loop(0, DIL_MAX, g0_body, 0)

    def out_copy(r):
        col = pl.multiple_of((r * N_KV_HEADS + hi) * HEAD_DIM, HEAD_DIM)
        return pltpu.make_async_copy(stage.at[r], o_hbm.at[bi, ti, :, pl.ds(col, HEAD_DIM)], sem)

    for r in range(DIL_MAX):
        stage[r] = (acc[r] / lrun[r]).astype(BF16)
        out_copy(r).start()
    for r in range(DIL_MAX):
        out_copy(r).wait()


def _dilated_attn(pa, cs, sn, *, name="dilated_attn"):
    b, nt = pa.shape[0], pa.shape[1]
    h = N_KV_HEADS

    def spec(off):
        return pl.BlockSpec((None, None, DIL_MAX, WIN_BLOCK, HEAD_DIM),
                            lambda bi, hi, ti: (bi, ti, 0, 0, off + hi))

    tab = pl.BlockSpec((None, None, DIL_MAX, WIN_BLOCK, HEAD_DIM), lambda bi, hi, ti: (bi, ti, 0, 0, 0))
    out = pl.pallas_call(
        _attn_kernel,
        out_shape=jax.ShapeDtypeStruct((b, nt, WIN_BLOCK, DIL_MAX * ATT_KV_WIDTH), BF16),
        grid=(b, h, nt),
        in_specs=[spec(0), spec(h), spec(2 * h), spec(3 * h), spec(4 * h), tab, tab],
        out_specs=pl.BlockSpec(memory_space=pl.ANY),
        scratch_shapes=[
            pltpu.VMEM((N_GROUPS, DIL_MAX, WIN_BLOCK, HEAD_DIM), F32),
            pltpu.VMEM((DIL_MAX, 2 * WIN_BLOCK, HEAD_DIM), F32),
            pltpu.VMEM((DIL_MAX, 2 * WIN_BLOCK, HEAD_DIM), F32),
            pltpu.VMEM((DIL_MAX, WIN_BLOCK, HEAD_DIM), F32),
            pltpu.VMEM((DIL_MAX, WIN_BLOCK, HEAD_DIM), F32),
            pltpu.VMEM((DIL_MAX, WIN_BLOCK, HEAD_DIM), F32),
            pltpu.VMEM((2 * N_GROUPS, WIN_BLOCK, 2 * WIN_BLOCK), F32),
            pltpu.VMEM((DIL_MAX, WIN_BLOCK, HEAD_DIM), BF16),
            pltpu.SemaphoreType.DMA,
        ],
        compiler_params=_cparams(("arbitrary", "arbitrary", "arbitrary")),
        name=name,
    )(pa, pa, pa, pa, pa, cs, sn)
    return out.reshape(b, nt * ATT_TILE, ATT_KV_WIDTH)


def _merge_out_kernel(oh_ref, oa_ref, gate_ref, x_ref, wh_ref, wa_ref, wo_ref, gc_ref, h1_ref, n2_ref):
    ga = gate_ref[:, :D_MODEL].astype(F32)
    gb = gate_ref[:, D_MODEL:].astype(F32)
    yh = jnp.dot(oh_ref[...], wh_ref[...], preferred_element_type=F32)
    ya = jnp.dot(oa_ref[...], wa_ref[...], preferred_element_type=F32)
    merged = jax.nn.sigmoid(ga) * yh + jax.nn.sigmoid(gb) * ya
    h1 = x_ref[...] + jnp.dot(merged.astype(BF16), wo_ref[...], preferred_element_type=F32)
    h1_ref[...] = h1
    n2_ref[...] = _rms(h1, gc_ref[...]).astype(BF16)


def _merge_out(oh, oa, gates, x2d, wh, wa, wo, gc, *, tm, name="merge_out"):
    n, d = x2d.shape

    def const(shape):
        return pl.BlockSpec(shape, lambda i: (0, 0), pipeline_mode=pl.Buffered(1))

    return pl.pallas_call(
        _merge_out_kernel,
        out_shape=(jax.ShapeDtypeStruct((n, d), F32), jax.ShapeDtypeStruct((n, d), BF16)),
        grid=(n // tm,),
        in_specs=[
            pl.BlockSpec((tm, HG_WIDTH), lambda i: (i, 0)),
            pl.BlockSpec((tm, ATT_KV_WIDTH), lambda i: (i, 0)),
            pl.BlockSpec((tm, 2 * d), lambda i: (i, 0)),
            pl.BlockSpec((tm, d), lambda i: (i, 0)),
            const((HG_WIDTH, d)), const((ATT_KV_WIDTH, d)), const((d, d)), const((1, d)),
        ],
        out_specs=(pl.BlockSpec((tm, d), lambda i: (i, 0)), pl.BlockSpec((tm, d), lambda i: (i, 0))),
        compiler_params=_cparams(("parallel",)),
        name=name,
    )(oh, oa, gates, x2d, wh, wa, wo, gc.reshape(1, d))


def _cross_kernel(n2_ref, h1_ref, kv_ref, wq_ref, wo_ref, gm_ref, wrh_ref, wrl_ref, br_ref,
                  h2_ref, n3_ref, idx_ref, tw_ref, rank_ref, cnt_ref, carry_ref, *, tm):
    @pl.when((pl.program_id(0) == 0) & (pl.program_id(1) == 0))
    def _():
        carry_ref[...] = jnp.zeros_like(carry_ref)

    nt_dims = (((1,), (1,)), ((), ()))
    scale = HEAD_DIM ** -0.5
    q = (jnp.dot(n2_ref[...], wq_ref[...], preferred_element_type=F32) * scale).astype(BF16)
    outs = []
    for hh in range(CROSS_HEADS):
        sl = slice(hh * HEAD_DIM, (hh + 1) * HEAD_DIM)
        kh = kv_ref[:, sl]
        vh = kv_ref[:, CROSS_WIDTH + hh * HEAD_DIM:CROSS_WIDTH + (hh + 1) * HEAD_DIM]
        s = lax.dot_general(q[:, sl], kh, nt_dims, preferred_element_type=F32)
        p = jnp.exp(s - jnp.max(s, axis=-1, keepdims=True))
        l = jnp.sum(p, axis=-1, keepdims=True)
        outs.append(jnp.dot(p.astype(BF16), vh, preferred_element_type=F32) / l)
    o = jnp.concatenate(outs, axis=-1).astype(BF16)
    h2 = h1_ref[...] + jnp.dot(o, wo_ref[...], preferred_element_type=F32)
    h2_ref[...] = h2
    n3 = _rms(h2, gm_ref[...])
    n3_ref[...] = n3

    n3h = n3.astype(BF16)
    n3l = (n3 - n3h.astype(F32)).astype(BF16)
    wrh = wrh_ref[...]
    logits = (jnp.dot(n3h, wrh, preferred_element_type=F32)
              + jnp.dot(n3l, wrh, preferred_element_type=F32)
              + jnp.dot(n3h, wrl_ref[...], preferred_element_type=F32)
              + br_ref[...])
    lane = lax.broadcasted_iota(I32, (tm, N_EXPERTS), 1).astype(F32)
    vals, idxs, hots = [], [], []
    cur = logits
    for _ in range(TOP_K):
        mx = jnp.max(cur, axis=-1, keepdims=True)
        ix = jnp.min(jnp.where(cur == mx, lane, float(N_EXPERTS)), axis=-1, keepdims=True)
        hot = lane == ix
        vals.append(mx)
        idxs.append(ix)
        hots.append(hot)
        cur = jnp.where(hot, -jnp.inf, cur)
    es = [jnp.exp(v - vals[0]) for v in vals]
    den = es[0] + es[1] + es[2] + es[3]
    col = lax.broadcasted_iota(I32, (tm, TOP_K), 1)

    def pack(cols):
        out = jnp.broadcast_to(cols[TOP_K - 1], (tm, TOP_K))
        for k in range(TOP_K - 2, -1, -1):
            out = jnp.where(col == k, cols[k], out)
        return out

    idx_ref[...] = pack(idxs).astype(I32)
    tw_ref[...] = pack([e / den for e in es])

    cmat = (hots[0] | hots[1] | hots[2] | hots[3]).astype(F32)
    rr = lax.broadcasted_iota(I32, (tm, tm), 0)
    cc = lax.broadcasted_iota(I32, (tm, tm), 1)
    tri = (cc < rr).astype(BF16)
    before = jnp.dot(tri, cmat.astype(BF16), preferred_element_type=F32) + carry_ref[...]
    ranks = [jnp.sum(jnp.where(hot, before, 0.0), axis=-1, keepdims=True) for hot in hots]
    rank_ref[...] = pack(ranks).astype(I32)
    carry = carry_ref[...] + jnp.sum(cmat, axis=0, keepdims=True)
    carry_ref[...] = carry
    cnt_ref[...] = carry


def _cross(n2, h1, kv, wq, wo, gm, wrh, wrl, br, *, batch, tm, name="cross"):
    n, d = h1.shape
    per_b = n // batch // tm
    mem_len = kv.shape[0] // batch

    def const(shape):
        return pl.BlockSpec(shape, lambda bi, i: (0, 0), pipeline_mode=pl.Buffered(1))

    def row(bi, i):
        return (bi * per_b + i, 0)

    return pl.pallas_call(
        functools.partial(_cross_kernel, tm=tm),
        out_shape=(
            jax.ShapeDtypeStruct((n, d), F32),
            jax.ShapeDtypeStruct((n, d), F32),
            jax.ShapeDtypeStruct((n, TOP_K), I32),
            jax.ShapeDtypeStruct((n, TOP_K), F32),
            jax.ShapeDtypeStruct((n, TOP_K), I32),
            jax.ShapeDtypeStruct((1, N_EXPERTS), F32),
        ),
        grid=(batch, per_b),
        in_specs=[
            pl.BlockSpec((tm, d), row),
            pl.BlockSpec((tm, d), row),
            pl.BlockSpec((mem_len, 2 * CROSS_WIDTH), lambda bi, i: (bi, 0)),
            const((d, CROSS_WIDTH)), const((CROSS_WIDTH, d)), const((1, d)),
            const((d, N_EXPERTS)), const((d, N_EXPERTS)), const((1, N_EXPERTS)),
        ],
        out_specs=(
            pl.BlockSpec((tm, d), row),
            pl.BlockSpec((tm, d), row),
            pl.BlockSpec((tm, TOP_K), row),
            pl.BlockSpec((tm, TOP_K), row),
            pl.BlockSpec((tm, TOP_K), row),
            pl.BlockSpec((1, N_EXPERTS), lambda bi, i: (0, 0)),
        ),
        scratch_shapes=[pltpu.VMEM((1, N_EXPERTS), F32)],
        compiler_params=_cparams(("arbitrary", "arbitrary")),
        name=name,
    )(n2, h1, kv, wq, wo, gm.reshape(1, d), wrh, wrl, br.reshape(1, N_EXPERTS))


def _slab(ref, row):
    return ref.at[pl.ds(pl.multiple_of(row * MOE_SLAB, MOE_SLAB), MOE_SLAB)]


def _row_copies(make_copy, count):
    ngroup = count // DMA_WAIT_ROWS

    def start_group(c):
        def one(j, carry):
            make_copy(c * DMA_WAIT_ROWS + j).start()
            return carry

        lax.fori_loop(0, DMA_WAIT_ROWS, one, 0)

    def wait_group(c):
        def one(j, carry):
            make_copy(c * DMA_WAIT_ROWS + j).wait()
            return carry

        lax.fori_loop(0, DMA_WAIT_ROWS, one, 0)

    def group(c, carry):
        start_group(c)

        @pl.when(c > 0)
        def _():
            wait_group(c - 1)

        return carry

    lax.fori_loop(0, ngroup, group, 0)
    wait_group(ngroup - 1)


def _dispatch_kernel(slot_ref, n3_hbm, zero_hbm, xs_hbm, sem, *, real_steps):
    step = pl.program_id(0)

    @pl.when(step < real_steps)
    def _():
        def make_copy(a):
            tok = (step * DISPATCH_CHUNK + a) >> TOP_K_LOG2
            return pltpu.make_async_copy(_slab(n3_hbm, tok), _slab(xs_hbm, slot_ref[0, a]), sem)

        _row_copies(make_copy, DISPATCH_CHUNK)

    @pl.when(step >= real_steps)
    def _():
        def make_copy(a):
            return pltpu.make_async_copy(zero_hbm, _slab(xs_hbm, slot_ref[0, a]), sem)

        _row_copies(make_copy, DISPATCH_CHUNK)


def _dispatch(slots, n3_rows, zero_row, *, n_tok, rows_total, name="moe_dispatch"):
    return pl.pallas_call(
        functools.partial(_dispatch_kernel, real_steps=n_tok * TOP_K // DISPATCH_CHUNK),
        out_shape=jax.ShapeDtypeStruct((rows_total * MOE_SLAB, V7X_LANES), F32),
        grid=(slots.shape[0],),
        in_specs=[
            pl.BlockSpec((None, 1, DISPATCH_CHUNK), lambda i: (i, 0, 0), memory_space=pltpu.SMEM),
            pl.BlockSpec(memory_space=pl.ANY),
            pl.BlockSpec(memory_space=pl.ANY),
        ],
        out_specs=pl.BlockSpec(memory_space=pl.ANY),
        scratch_shapes=[pltpu.SemaphoreType.DMA],
        compiler_params=_cparams(("arbitrary",)),
        name=name,
    )(slots, n3_rows, zero_row)


def _moe_kernel(te_ref, tv_ref, tstart_ref, trows_ref,
                xs_hbm, w1_ref, b1_ref, w2_ref, b2_ref, perm_ref, yb_hbm,
                rowbuf, x16, acc, w1p, sem, *, nf):
    i = pl.program_id(0)
    f = pl.program_id(1)
    rows = trows_ref[i]
    start = tstart_ref[i]
    nchunk = rows // MOE_ROW_ALIGN
    chunk_slab_rows = MOE_ROW_ALIGN * MOE_SLAB
    half = 2 * V7X_LANES
    nhalf = 2 * MOE_TF // half

    def buf_rows(c):
        return rowbuf.at[pl.ds(pl.multiple_of(c * chunk_slab_rows, chunk_slab_rows), chunk_slab_rows)]

    def hbm_rows(ref, c):
        r0 = pl.multiple_of((start + c * MOE_ROW_ALIGN) * MOE_SLAB, chunk_slab_rows)
        return ref.at[pl.ds(r0, chunk_slab_rows)]

    def in_copy(c):
        return pltpu.make_async_copy(hbm_rows(xs_hbm, c), buf_rows(c), sem)

    def out_copy(c):
        return pltpu.make_async_copy(buf_rows(c), hbm_rows(yb_hbm, c), sem)

    @pl.when((rows > 0) & (f == 0))
    def _():
        def issue(c, carry):
            in_copy(c).start()
            return carry

        lax.fori_loop(0, nchunk, issue, 0)

        def retire(c, carry):
            in_copy(c).wait()
            return carry

        lax.fori_loop(0, nchunk, retire, 0)

        def land(c, carry):
            r0 = pl.multiple_of(c * MOE_ROW_ALIGN, MOE_ROW_ALIGN)
            s0 = pl.multiple_of(c * chunk_slab_rows, chunk_slab_rows)
            for j in range(MOE_SLAB):
                xj = rowbuf[pl.ds(s0 + j, MOE_ROW_ALIGN, stride=MOE_SLAB), :]
                x16[pl.ds(r0, MOE_ROW_ALIGN), j * V7X_LANES:(j + 1) * V7X_LANES] = xj.astype(BF16)
            acc[pl.ds(r0, MOE_ROW_ALIGN), :] = jnp.zeros((MOE_ROW_ALIGN, D_MODEL), F32)
            return carry

        lax.fori_loop(0, nchunk, land, 0)

    @pl.when(rows > 0)
    def _():
        perm = perm_ref[...]
        for c in range(nhalf):
            wb = w1_ref[:, c * half:(c + 1) * half].astype(BF16)
            w1p[:, c * half:(c + 1) * half] = jnp.dot(wb, perm, preferred_element_type=F32).astype(BF16)
        w2 = w2_ref[...].astype(BF16)
        b1 = b1_ref[...]

        def mlp(r0, nr):
            x = x16[pl.ds(r0, nr), :]
            h = jnp.dot(x, w1p[...], preferred_element_type=F32) + b1
            parts = []
            for c in range(nhalf):
                hg = jnp.minimum(h[:, c * half:c * half + V7X_LANES], SWIGLU_LIMIT)
                hl = jnp.clip(h[:, c * half + V7X_LANES:(c + 1) * half], -SWIGLU_LIMIT, SWIGLU_LIMIT)
                parts.append(hg * jax.nn.sigmoid(SWIGLU_ALPHA * hg) * (hl + 1.0))
            a = jnp.concatenate(parts, axis=-1).astype(BF16)
            acc[pl.ds(r0, nr), :] += jnp.dot(a, w2, preferred_element_type=F32)

        def pair(c, carry):
            mlp(pl.multiple_of(c * 2 * MOE_ROW_ALIGN, 2 * MOE_ROW_ALIGN), 2 * MOE_ROW_ALIGN)
            return carry

        lax.fori_loop(0, nchunk // 2, pair, 0)

        @pl.when(nchunk % 2 == 1)
        def _():
            mlp(pl.multiple_of((nchunk - 1) * MOE_ROW_ALIGN, MOE_ROW_ALIGN), MOE_ROW_ALIGN)

    @pl.when((rows > 0) & (f == nf - 1))
    def _():
        def emit(c, carry):
            r0 = pl.multiple_of(c * MOE_ROW_ALIGN, MOE_ROW_ALIGN)
            s0 = pl.multiple_of(c * chunk_slab_rows, chunk_slab_rows)
            y = acc[pl.ds(r0, MOE_ROW_ALIGN), :] + b2_ref[...]
            for j in range(MOE_SLAB):
                rowbuf[pl.ds(s0 + j, MOE_ROW_ALIGN, stride=MOE_SLAB), :] = y[:, j * V7X_LANES:(j + 1) * V7X_LANES]
            out_copy(c).start()
            return carry

        lax.fori_loop(0, nchunk, emit, 0)

        def drain(c, carry):
            out_copy(c).wait()
            return carry

        lax.fori_loop(0, nchunk, drain, 0)


def _moe(tile_e, tile_v, tile_start, tile_rows, xs, w1, b1p, w2, b2, perm, *, name="moe_mlp"):
    nt = tile_e.shape[0]
    nf = D_FF // MOE_TF

    def w1_map(i, f, te, tv, ts, tr):
        return (te[i], 0, jnp.where(tv[i] > 0, f, nf - 1))

    def w2_map(i, f, te, tv, ts, tr):
        return (te[i], jnp.where(tv[i] > 0, f, nf - 1), 0)

    def b2_map(i, f, te, tv, ts, tr):
        return (te[i], 0, 0)

    grid_spec = pltpu.PrefetchScalarGridSpec(
        num_scalar_prefetch=4,
        grid=(nt, nf),
        in_specs=[
            pl.BlockSpec(memory_space=pl.ANY),
            pl.BlockSpec((None, D_MODEL, 2 * MOE_TF), w1_map),
            pl.BlockSpec((None, 1, 2 * MOE_TF), w1_map),
            pl.BlockSpec((None, MOE_TF, D_MODEL), w2_map),
            pl.BlockSpec((None, 1, D_MODEL), b2_map),
            pl.BlockSpec((2 * V7X_LANES, 2 * V7X_LANES), lambda i, f, te, tv, ts, tr: (0, 0)),
        ],
        out_specs=pl.BlockSpec(memory_space=pl.ANY),
        scratch_shapes=[
            pltpu.VMEM((MOE_TMAX * MOE_SLAB, V7X_LANES), F32),
            pltpu.VMEM((MOE_TMAX, D_MODEL), BF16),
            pltpu.VMEM((MOE_TMAX, D_MODEL), F32),
            pltpu.VMEM((D_MODEL, 2 * MOE_TF), BF16),
            pltpu.SemaphoreType.DMA,
        ],
    )
    return pl.pallas_call(
        functools.partial(_moe_kernel, nf=nf),
        out_shape=jax.ShapeDtypeStruct(xs.shape, F32),
        grid_spec=grid_spec,
        input_output_aliases={4: 0},
        compiler_params=_cparams(("arbitrary", "arbitrary")),
        name=name,
    )(tile_e, tile_v, tile_start, tile_rows, xs, w1, b1p.reshape(N_EXPERTS, 1, 2 * D_FF), w2,
      b2.reshape(N_EXPERTS, 1, D_MODEL), perm)


def _combine_kernel(slot_ref, h2_ref, tw_ref, gf_ref, yb_hbm, o_ref, buf, sem, *, tm):
    def make_copy(a):
        t = a >> TOP_K_LOG2
        k = a & (TOP_K - 1)
        return pltpu.make_async_copy(_slab(yb_hbm, slot_ref[0, a]), _slab(buf, k * tm + t), sem)

    _row_copies(make_copy, tm * TOP_K)

    tw = tw_ref[...]
    cols = []
    for j in range(MOE_SLAB):
        cj = None
        for k in range(TOP_K):
            part = tw[:, k:k + 1] * buf[pl.ds(k * tm * MOE_SLAB + j, tm, stride=MOE_SLAB), :]
            cj = part if cj is None else cj + part
        cols.append(cj)
    h3 = h2_ref[...] + jnp.concatenate(cols, axis=-1)
    o_ref[...] = _rms(h3, gf_ref[...])


def _combine(slots, h2, tw, gf, yb, *, tm, name="moe_combine"):
    n, d = h2.shape
    return pl.pallas_call(
        functools.partial(_combine_kernel, tm=tm),
        out_shape=jax.ShapeDtypeStruct((n, d), F32),
        grid=(n // tm,),
        in_specs=[
            pl.BlockSpec((None, 1, tm * TOP_K), lambda i: (i, 0, 0), memory_space=pltpu.SMEM),
            pl.BlockSpec((tm, d), lambda i: (i, 0)),
            pl.BlockSpec((tm, TOP_K), lambda i: (i, 0)),
            pl.BlockSpec((1, d), lambda i: (0, 0)),
            pl.BlockSpec(memory_space=pl.ANY),
        ],
        out_specs=pl.BlockSpec((tm, d), lambda i: (i, 0)),
        scratch_shapes=[
            pltpu.VMEM((tm * TOP_K * MOE_SLAB, V7X_LANES), F32),
            pltpu.SemaphoreType.DMA,
        ],
        compiler_params=_cparams(("arbitrary",)),
        name=name,
    )(slots, h2, tw, gf.reshape(1, d), yb)


def _rope_tables(positions):
    b, s = positions.shape
    half = HEAD_DIM // 2
    inv_freq = 1.0 / (ROPE_THETA ** (jnp.arange(half, dtype=F32) / half))
    ang = positions.astype(F32)[..., None] * inv_freq
    cos = jnp.cos(ang)
    sin = jnp.sin(ang)
    cs = jnp.concatenate([cos, cos], axis=-1)
    sn = jnp.concatenate([-sin, sin], axis=-1)

    def deint(a):
        a = a.reshape(b, s // ATT_TILE, WIN_BLOCK, DIL_MAX, HEAD_DIM)
        return a.transpose(0, 1, 3, 2, 4)

    return deint(cs), deint(sn)


def _routing(top_e, rank, counts, n_tok):
    cnt = counts.reshape(N_EXPERTS).astype(I32)
    padded = (cnt + MOE_ROW_ALIGN - 1) // MOE_ROW_ALIGN * MOE_ROW_ALIGN
    gend = jnp.cumsum(padded)
    gstart = gend - padded
    slot = (gstart[top_e] + rank).reshape(-1)

    n_assign = n_tok * TOP_K
    n_unused = N_EXPERTS * MOE_ROW_ALIGN
    rows_total = n_assign + n_unused
    padcnt = padded - cnt
    pend = jnp.cumsum(padcnt)
    pstart = pend - padcnt
    p = jnp.arange(n_unused, dtype=I32)
    ep = jnp.minimum(jnp.searchsorted(pend, p, side="right"), N_EXPERTS - 1).astype(I32)
    unused = jnp.where(p < pend[-1], gstart[ep] + cnt[ep] + (p - pstart[ep]), gend[-1] + (p - pend[-1]))

    nt_max = rows_total // MOE_TMAX + N_EXPERTS
    nt_e = (padded + MOE_TMAX - 1) // MOE_TMAX
    tend = jnp.cumsum(nt_e)
    tstart = tend - nt_e
    ti = jnp.arange(nt_max, dtype=I32)
    valid = ti < tend[-1]
    tic = jnp.minimum(ti, tend[-1] - 1)
    te = jnp.minimum(jnp.searchsorted(tend, tic, side="right"), N_EXPERTS - 1).astype(I32)
    local = tic - tstart[te]
    row0 = gstart[te] + local * MOE_TMAX
    rows = jnp.where(valid, jnp.clip(padded[te] - local * MOE_TMAX, 0, MOE_TMAX), 0)
    return (slot.astype(I32), unused.astype(I32), rows_total, te, valid.astype(I32),
            row0.astype(I32), rows.astype(I32))


def kernel(x, mem, positions, norm_mix_g, w_in, lb_raw, hgrn_norm_g, w_br_hgrn, w_br_attn, w_out,
           norm_cross_g, norm_mem_g, w_cq, w_ckv, w_co, norm_moe_g, w_router, b_router,
           w_mlp1, b_mlp1, w_mlp2, b_mlp2, norm_final_g):
    bsz, seq, d = x.shape
    assert w_in.shape[0] == 1 and d == D_MODEL and seq % ATT_TILE == 0
    n_tok = bsz * seq
    lower_bounds = jnp.cumsum(jax.nn.softmax(lb_raw.astype(F32), axis=0), axis=0)
    cs, sn = _rope_tables(positions)

    wl = w_in[0]
    c_h = 4 * HG_WIDTH
    c_a = c_h + ATT_Q_WIDTH + 2 * ATT_KV_WIDTH
    x2d = x.reshape(n_tok, d)
    proj_h = _in_proj(x2d, norm_mix_g[0], wl[:, :c_h].astype(BF16), tm=1024, tn=1024, name="in_proj_hgrn")
    proj_g = _in_proj(x2d, norm_mix_g[0], wl[:, c_a:].astype(BF16), tm=1024, tn=1024, name="in_proj_gate")
    pa = _in_proj_strided(x, norm_mix_g[0], wl[:, c_h:c_a].astype(BF16), c=8, tn=1024, name="in_proj_attn")

    o_h = _hgrn(proj_h.reshape(bsz, seq, c_h), lower_bounds[0], hgrn_norm_g[0], ts=512)
    o_a = _dilated_attn(pa, cs, sn)

    h1, n2 = _merge_out(o_h.reshape(n_tok, HG_WIDTH), o_a.reshape(n_tok, ATT_KV_WIDTH), proj_g, x2d,
                        w_br_hgrn[0].astype(BF16), w_br_attn[0].astype(BF16), w_out[0].astype(BF16),
                        norm_cross_g[0], tm=256)

    mem2d = mem.reshape(-1, d)
    kv = _in_proj(mem2d, norm_mem_g, w_ckv[0].astype(BF16), tm=mem2d.shape[0], tn=2 * CROSS_WIDTH, name="mem_kv")
    wrh = w_router[0].astype(BF16)
    wrl = (w_router[0] - wrh.astype(F32)).astype(BF16)
    h2, n3, top_e, top_w, rank, counts = _cross(
        n2, h1, kv, w_cq[0].astype(BF16), w_co[0].astype(BF16), norm_moe_g[0], wrh, wrl, b_router[0],
        batch=bsz, tm=512)

    slot, unused, rows_total, te, tv, tstart, trows = _routing(top_e, rank, counts, n_tok)
    slots_all = jnp.concatenate([slot, unused]).reshape(-1, 1, DISPATCH_CHUNK)
    zero_row = jnp.zeros((MOE_SLAB, V7X_LANES), F32)
    xs = _dispatch(slots_all, n3.reshape(n_tok * MOE_SLAB, V7X_LANES), zero_row,
                   n_tok=n_tok, rows_total=rows_total)

    ii = jnp.arange(2 * V7X_LANES)
    src = jnp.where(ii < V7X_LANES, 2 * ii, 2 * (ii - V7X_LANES) + 1)
    perm = (ii[:, None] == src[None, :]).astype(BF16)
    b1p = b_mlp1[0].reshape(N_EXPERTS, -1, V7X_LANES, 2).transpose(0, 1, 3, 2).reshape(N_EXPERTS, 2 * D_FF)
    yb = _moe(te, tv, tstart, trows, xs, w_mlp1[0], b1p, w_mlp2[0], b_mlp2[0], perm)

    tm_c = 256
    out = _combine(slot.reshape(-1, 1, tm_c * TOP_K), h2, top_w, norm_final_g, yb, tm=tm_c)
    return out.reshape(bsz, seq, d)
```

```python
import functools

import jax
import jax.numpy as jnp
from jax import lax
from jax.experimental import pallas as pl
from jax.experimental.pallas import tpu as pltpu

F32 = jnp.float32
BF16 = jnp.bfloat16
I32 = jnp.int32

D_MODEL = 2048
HEAD_DIM = 128
HG_HEADS = 8
HG_WIDTH = HG_HEADS * HEAD_DIM
N_KV_HEADS = 8
N_GROUPS = 3
ATT_Q_WIDTH = N_GROUPS * N_KV_HEADS * HEAD_DIM
ATT_KV_WIDTH = N_KV_HEADS * HEAD_DIM
WIN_BLOCK = 128
ROPE_THETA = 10000.0
CROSS_HEADS = 4
CROSS_WIDTH = CROSS_HEADS * HEAD_DIM
N_EXPERTS = 32
TOP_K = 4
TOP_K_LOG2 = 2
D_FF = D_MODEL
SWIGLU_ALPHA = 1.702
SWIGLU_LIMIT = 7.0
NORM_EPS = 1e-6

V7X_LANES = 128
V7X_VMEM_LIMIT_BYTES = 56 * 1024 * 1024

DIL_MAX = 16
ATT_TILE = DIL_MAX * WIN_BLOCK
HG_BLOCK = 16
HG_GROUP = 16
NEG_BIG = -1e30

MOE_TMAX = 2048
MOE_ROW_ALIGN = 128
MOE_BLOCK_CHUNKS = 4
MOE_TF = 256
MOE_SLAB = D_MODEL // V7X_LANES
DISPATCH_TOKENS = 256
DMA_WAIT_ROWS = 128


def _rms(x, g):
    ms = jnp.mean(x * x, axis=-1, keepdims=True)
    return x * lax.rsqrt(ms + NORM_EPS) * g


def _cparams(sem, vmem=V7X_VMEM_LIMIT_BYTES):
    return pltpu.CompilerParams(dimension_semantics=sem, vmem_limit_bytes=vmem)


def _in_proj_kernel(x_ref, g_ref, w_ref, o_ref, xn_ref):
    @pl.when(pl.program_id(1) == 0)
    def _():
        xn_ref[...] = _rms(x_ref[...], g_ref[...]).astype(BF16)

    o_ref[...] = jnp.dot(xn_ref[...], w_ref[...], preferred_element_type=F32).astype(o_ref.dtype)


def _in_proj(x2d, g, w, *, tm, tn, name):
    n, d = x2d.shape
    wc = w.shape[1]
    return pl.pallas_call(
        _in_proj_kernel,
        out_shape=jax.ShapeDtypeStruct((n, wc), BF16),
        grid=(n // tm, wc // tn),
        in_specs=[
            pl.BlockSpec((tm, d), lambda i, j: (i, 0)),
            pl.BlockSpec((1, d), lambda i, j: (0, 0)),
            pl.BlockSpec((d, tn), lambda i, j: (0, j)),
        ],
        out_specs=pl.BlockSpec((tm, tn), lambda i, j: (i, j)),
        scratch_shapes=[pltpu.VMEM((tm, d), BF16)],
        compiler_params=_cparams(("parallel", "arbitrary")),
        name=name,
    )(x2d, g.reshape(1, d), w)


def _in_proj_strided_kernel(x_ref, g_ref, w_ref, o_ref, xn_ref, *, c):
    @pl.when(pl.program_id(3) == 0)
    def _():
        for ci in range(c):
            x = x_ref[:, ci * D_MODEL:(ci + 1) * D_MODEL]
            xn_ref[ci * WIN_BLOCK:(ci + 1) * WIN_BLOCK, :] = _rms(x, g_ref[...]).astype(BF16)

    res = jnp.dot(xn_ref[...], w_ref[...], preferred_element_type=F32)
    for ci in range(c):
        o_ref[ci] = res[ci * WIN_BLOCK:(ci + 1) * WIN_BLOCK, :].astype(o_ref.dtype)


def _in_proj_strided(x, g, w, *, c, tn, name):
    b, s, d = x.shape
    nt = s // ATT_TILE
    wc = w.shape[1]
    xv = x.reshape(b, nt, WIN_BLOCK, DIL_MAX * d)
    return pl.pallas_call(
        functools.partial(_in_proj_strided_kernel, c=c),
        out_shape=jax.ShapeDtypeStruct((b, nt, DIL_MAX, WIN_BLOCK, wc), BF16),
        grid=(b, nt, DIL_MAX // c, wc // tn),
        in_specs=[
            pl.BlockSpec((None, None, WIN_BLOCK, c * d), lambda bi, ti, ri, j: (bi, ti, 0, ri)),
            pl.BlockSpec((1, d), lambda bi, ti, ri, j: (0, 0)),
            pl.BlockSpec((d, tn), lambda bi, ti, ri, j: (0, j)),
        ],
        out_specs=pl.BlockSpec((None, None, c, WIN_BLOCK, tn), lambda bi, ti, ri, j: (bi, ti, ri, 0, j)),
        scratch_shapes=[pltpu.VMEM((c * WIN_BLOCK, d), BF16)],
        compiler_params=_cparams(("parallel", "parallel", "parallel", "arbitrary")),
        name=name,
    )(xv, g.reshape(1, d), w)


def _hgrn_kernel(q_ref, f_ref, i_ref, g_ref, lb_ref, gn_ref, o_ref, st_ref, kin_s, b_s, v_s, *, ts):
    @pl.when(pl.program_id(2) == 0)
    def _():
        st_ref[...] = jnp.zeros_like(st_ref)

    lb = lb_ref[...]
    oml = 1.0 - lb
    gn = gn_ref[...]
    half = HG_BLOCK // 2
    row = lax.broadcasted_iota(I32, (HG_BLOCK, HEAD_DIM), 0)
    row8 = lax.broadcasted_iota(I32, (half, HEAD_DIM), 0)
    nt_dims = (((1,), (1,)), ((), ()))
    tn_dims = (((0,), (0,)), ((), ()))

    def front(g, t0):
        sl = pl.ds(t0 + g * HG_BLOCK, HG_BLOCK)
        q = q_ref[sl, :].astype(F32)
        hf = f_ref[sl, :].astype(F32)
        v = i_ref[sl, :].astype(F32)
        kin = oml * jax.nn.sigmoid(-hf)
        b = jnp.log2(lb + oml * jax.nn.sigmoid(hf))
        for sh in (1, 2, 4, 8):
            b = b + jnp.where(row >= sh, pltpu.roll(b, sh, 0), 0.0)
        kin_s[g] = kin
        b_s[g] = b
        v_s[g] = v
        q_lo, q_hi = q[:half], q[half:]
        b_lo, b_hi = b[:half], b[half:]
        o_lo = jnp.zeros((half, HEAD_DIM), F32)
        o_hi = jnp.zeros((half, HEAD_DIM), F32)
        for s in range(HG_BLOCK):
            ks = kin_s[g, s:s + 1, :]
            bs = b_s[g, s:s + 1, :]
            vs = v_s[g, s:s + 1, :]
            if s < half:
                w = q_lo * ks * jnp.exp2(b_lo - bs)
                if s > 0:
                    w = jnp.where(row8 >= s, w, 0.0)
                o_lo = o_lo + jnp.sum(w, axis=-1, keepdims=True) * vs
                w = q_hi * ks * jnp.exp2(b_hi - bs)
            else:
                w = q_hi * ks * jnp.exp2(b_hi - bs)
                if s > half:
                    w = jnp.where(row8 >= s - half, w, 0.0)
            o_hi = o_hi + jnp.sum(w, axis=-1, keepdims=True) * vs
        bl = b_s[g, HG_BLOCK - 1:HG_BLOCK, :]
        qd = (q * jnp.exp2(b)).astype(BF16)
        kd = (kin * jnp.exp2(bl - b)).astype(BF16)
        upd = lax.dot_general(v.astype(BF16), kd, tn_dims, preferred_element_type=F32)
        return jnp.concatenate([o_lo, o_hi], axis=0), qd, upd, jnp.exp2(bl)

    def body(i, carry):
        t0 = pl.multiple_of(i * (HG_GROUP * HG_BLOCK), HG_GROUP * HG_BLOCK)
        fronts = [front(g, t0) for g in range(HG_GROUP)]
        st = st_ref[...]
        for g, (o_diag, qd, upd, dec) in enumerate(fronts):
            o = o_diag + lax.dot_general(qd, st.astype(BF16), nt_dims, preferred_element_type=F32)
            st = st * dec + upd
            sl = pl.ds(t0 + g * HG_BLOCK, HG_BLOCK)
            hg = g_ref[sl, :].astype(F32)
            o_ref[sl, :] = (_rms(o, gn) * (hg * jax.nn.sigmoid(hg))).astype(o_ref.dtype)
        st_ref[...] = st
        return carry

    lax.fori_loop(0, ts // (HG_GROUP * HG_BLOCK), body, 0)


def _hgrn(proj_h, lb, gn, *, ts, name="hgrn"):
    b, s, _ = proj_h.shape
    h = HG_HEADS

    def spec(off):
        return pl.BlockSpec((None, ts, HEAD_DIM), lambda bi, hi, si: (bi, si, off + hi))

    vec = pl.BlockSpec((1, HEAD_DIM), lambda bi, hi, si: (0, hi))
    return pl.pallas_call(
        functools.partial(_hgrn_kernel, ts=ts),
        out_shape=jax.ShapeDtypeStruct((b, s, HG_WIDTH), BF16),
        grid=(b, h, s // ts),
        in_specs=[spec(0), spec(h), spec(2 * h), spec(3 * h), vec, vec],
        out_specs=pl.BlockSpec((None, ts, HEAD_DIM), lambda bi, hi, si: (bi, si, hi)),
        scratch_shapes=[
            pltpu.VMEM((HEAD_DIM, HEAD_DIM), F32),
            pltpu.VMEM((HG_GROUP, HG_BLOCK, HEAD_DIM), F32),
            pltpu.VMEM((HG_GROUP, HG_BLOCK, HEAD_DIM), F32),
            pltpu.VMEM((HG_GROUP, HG_BLOCK, HEAD_DIM), F32),
        ],
        compiler_params=_cparams(("parallel", "parallel", "arbitrary")),
        name=name,
    )(proj_h, proj_h, proj_h, proj_h, lb.reshape(1, HG_WIDTH), gn.reshape(1, HG_WIDTH))


def _attn_bias(kind):
    rq = lax.broadcasted_iota(I32, (WIN_BLOCK, 2 * WIN_BLOCK), 0)
    ck = lax.broadcasted_iota(I32, (WIN_BLOCK, 2 * WIN_BLOCK), 1)
    if kind == 2:
        dist = rq + WIN_BLOCK - ck
        first = ck < WIN_BLOCK
    elif kind == 1:
        dist = 4 * ((rq & 31) - (ck & 63) + 32) + ((rq >> 5) - (ck >> 6))
        first = (ck & 63) < 32
    else:
        dist = 16 * ((rq & 7) - (ck & 15) + 8) + ((rq >> 3) - (ck >> 4))
        first = (ck & 15) < 8
    valid = (dist >= 0) & (dist <= WIN_BLOCK)
    return (jnp.where(valid, 0.0, NEG_BIG).astype(F32),
            jnp.where(valid & jnp.logical_not(first), 0.0, NEG_BIG).astype(F32))


def _attn_kernel(q0_ref, q1_ref, q2_ref, k_ref, v_ref, cs_ref, sn_ref, o_hbm,
                 qr, kext, vext, acc, mrun, lrun, bias, stage, sem):
    bi = pl.program_id(0)
    hi = pl.program_id(1)
    ti = pl.program_id(2)
    wb = WIN_BLOCK
    scale = HEAD_DIM ** -0.5

    @pl.when(ti == 0)
    def _():
        kext[:, 0:wb, :] = jnp.zeros((DIL_MAX, wb, HEAD_DIM), F32)
        vext[:, 0:wb, :] = jnp.zeros((DIL_MAX, wb, HEAD_DIM), F32)

    @pl.when(ti > 0)
    def _():
        kext[:, 0:wb, :] = kext[:, wb:2 * wb, :]
        vext[:, 0:wb, :] = vext[:, wb:2 * wb, :]

    for kind in range(N_GROUPS):
        full, nofirst = _attn_bias(kind)
        bias[2 * kind] = full
        bias[2 * kind + 1] = nofirst

    def rope_body(r, carry):
        cs = cs_ref[r]
        sn = sn_ref[r]
        for g, qref in enumerate((q0_ref, q1_ref, q2_ref)):
            q = qref[r].astype(F32)
            qr[g, r] = (q * cs + pltpu.roll(q, HEAD_DIM // 2, 1) * sn) * scale
        k = k_ref[r].astype(F32)
        kext[r, wb:2 * wb, :] = k * cs + pltpu.roll(k, HEAD_DIM // 2, 1) * sn
        vext[r, wb:2 * wb, :] = v_ref[r].astype(F32)
        acc[r] = jnp.zeros((wb, HEAD_DIM), F32)
        mrun[r] = jnp.full((wb, HEAD_DIM), NEG_BIG, F32)
        lrun[r] = jnp.zeros((wb, HEAD_DIM), F32)
        return carry

    lax.fori_loop(0, DIL_MAX, rope_body, 0)

    nt_dims = (((1,), (1,)), ((), ()))

    def block(qb, kb, vb, bias_blk):
        s = lax.dot_general(qb.astype(BF16), kb.astype(BF16), nt_dims, preferred_element_type=F32)
        s = s + bias_blk
        m = jnp.max(s, axis=-1, keepdims=True)
        p = jnp.exp(s - m)
        l = jnp.sum(p, axis=-1, keepdims=True)
        n = jnp.dot(p.astype(BF16), vb.astype(BF16), preferred_element_type=F32)
        return n, jnp.broadcast_to(m, (wb, HEAD_DIM)), jnp.broadcast_to(l, (wb, HEAD_DIM))

    def merge(r, rows, n, m, l):
        m_old = mrun[r, rows, :]
        m_new = jnp.maximum(m_old, m)
        a = jnp.exp(m_old - m_new)
        bb = jnp.exp(m - m_new)
        acc[r, rows, :] = acc[r, rows, :] * a + n * bb
        lrun[r, rows, :] = lrun[r, rows, :] * a + l * bb
        mrun[r, rows, :] = m_new

    first_tile = jnp.where(ti == 0, 1, 0)

    def g2_body(r, carry):
        n, m, l = block(qr[2, r], kext[r], vext[r], bias[4 + first_tile])
        merge(r, pl.ds(0, wb), n, m, l)
        return carry

    lax.fori_loop(0, DIL_MAX, g2_body, 0)

    def g1_body(it, carry):
        r4 = it & 3
        mb = it >> 2
        q0 = pl.multiple_of(32 * mb, 32)
        k0 = pl.multiple_of(96 + 32 * mb, 32)
        qb = jnp.concatenate([qr[1, r4 + 4 * j, pl.ds(q0, 32), :] for j in range(4)], axis=0)
        kb = jnp.concatenate([kext[r4 + 4 * j, pl.ds(k0, 64), :] for j in range(4)], axis=0)
        vb = jnp.concatenate([vext[r4 + 4 * j, pl.ds(k0, 64), :] for j in range(4)], axis=0)
        use_first = jnp.where(mb == 0, first_tile, 0)
        n, m, l = block(qb, kb, vb, bias[2 + use_first])
        for j in range(4):
            sl = slice(32 * j, 32 * (j + 1))
            merge(r4 + 4 * j, pl.ds(q0, 32), n[sl], m[sl], l[sl])
        return carry

    lax.fori_loop(0, DIL_MAX, g1_body, 0)

    def g0_body(mb, carry):
        q0 = pl.multiple_of(8 * mb, 8)
        k0 = pl.multiple_of(120 + 8 * mb, 8)
        qb = jnp.concatenate([qr[0, r, pl.ds(q0, 8), :] for r in range(DIL_MAX)], axis=0)
        kb = jnp.concatenate([kext[r, pl.ds(k0, 16), :] for r in range(DIL_MAX)], axis=0)
        vb = jnp.concatenate([vext[r, pl.ds(k0, 16), :] for r in range(DIL_MAX)], axis=0)
        use_first = jnp.where(mb == 0, first_tile, 0)
        n, m, l = block(qb, kb, vb, bias[use_first])
        for r in range(DIL_MAX):
            sl = slice(8 * r, 8 * (r + 1))
            merge(r, pl.ds(q0, 8), n[sl], m[sl], l[sl])
        return carry

    lax.fori_loop(0, DIL_MAX, g0_body, 0)

    def out_copy(r):
        col = pl.multiple_of((r * N_KV_HEADS + hi) * HEAD_DIM, HEAD_DIM)
        return pltpu.make_async_copy(stage.at[r], o_hbm.at[bi, ti, :, pl.ds(col, HEAD_DIM)], sem)

    for r in range(DIL_MAX):
        stage[r] = (acc[r] / lrun[r]).astype(BF16)
        out_copy(r).start()
    for r in range(DIL_MAX):
        out_copy(r).wait()


def _dilated_attn(pa, cs, sn, *, name="dilated_attn"):
    b, nt = pa.shape[0], pa.shape[1]
    h = N_KV_HEADS

    def spec(off):
        return pl.BlockSpec((None, None, DIL_MAX, WIN_BLOCK, HEAD_DIM),
                            lambda bi, hi, ti: (bi, ti, 0, 0, off + hi))

    tab = pl.BlockSpec((None, None, DIL_MAX, WIN_BLOCK, HEAD_DIM), lambda bi, hi, ti: (bi, ti, 0, 0, 0))
    out = pl.pallas_call(
        _attn_kernel,
        out_shape=jax.ShapeDtypeStruct((b, nt, WIN_BLOCK, DIL_MAX * ATT_KV_WIDTH), BF16),
        grid=(b, h, nt),
        in_specs=[spec(0), spec(h), spec(2 * h), spec(3 * h), spec(4 * h), tab, tab],
        out_specs=pl.BlockSpec(memory_space=pl.ANY),
        scratch_shapes=[
            pltpu.VMEM((N_GROUPS, DIL_MAX, WIN_BLOCK, HEAD_DIM), F32),
            pltpu.VMEM((DIL_MAX, 2 * WIN_BLOCK, HEAD_DIM), F32),
            pltpu.VMEM((DIL_MAX, 2 * WIN_BLOCK, HEAD_DIM), F32),
            pltpu.VMEM((DIL_MAX, WIN_BLOCK, HEAD_DIM), F32),
            pltpu.VMEM((DIL_MAX, WIN_BLOCK, HEAD_DIM), F32),
            pltpu.VMEM((DIL_MAX, WIN_BLOCK, HEAD_DIM), F32),
            pltpu.VMEM((2 * N_GROUPS, WIN_BLOCK, 2 * WIN_BLOCK), F32),
            pltpu.VMEM((DIL_MAX, WIN_BLOCK, HEAD_DIM), BF16),
            pltpu.SemaphoreType.DMA,
        ],
        compiler_params=_cparams(("arbitrary", "arbitrary", "arbitrary")),
        name=name,
    )(pa, pa, pa, pa, pa, cs, sn)
    return out.reshape(b, nt * ATT_TILE, ATT_KV_WIDTH)


def _merge_out_kernel(oh_ref, oa_ref, gate_ref, x_ref, wh_ref, wa_ref, wo_ref, gc_ref, h1_ref, n2_ref):
    ga = gate_ref[:, :D_MODEL].astype(F32)
    gb = gate_ref[:, D_MODEL:].astype(F32)
    yh = jnp.dot(oh_ref[...], wh_ref[...], preferred_element_type=F32)
    ya = jnp.dot(oa_ref[...], wa_ref[...], preferred_element_type=F32)
    merged = jax.nn.sigmoid(ga) * yh + jax.nn.sigmoid(gb) * ya
    h1 = x_ref[...] + jnp.dot(merged.astype(BF16), wo_ref[...], preferred_element_type=F32)
    h1_ref[...] = h1
    n2_ref[...] = _rms(h1, gc_ref[...]).astype(BF16)


def _merge_out(oh, oa, gates, x2d, wh, wa, wo, gc, *, tm, name="merge_out"):
    n, d = x2d.shape

    def const(shape):
        return pl.BlockSpec(shape, lambda i: (0, 0), pipeline_mode=pl.Buffered(1))

    return pl.pallas_call(
        _merge_out_kernel,
        out_shape=(jax.ShapeDtypeStruct((n, d), F32), jax.ShapeDtypeStruct((n, d), BF16)),
        grid=(n // tm,),
        in_specs=[
            pl.BlockSpec((tm, HG_WIDTH), lambda i: (i, 0)),
            pl.BlockSpec((tm, ATT_KV_WIDTH), lambda i: (i, 0)),
            pl.BlockSpec((tm, 2 * d), lambda i: (i, 0)),
            pl.BlockSpec((tm, d), lambda i: (i, 0)),
            const((HG_WIDTH, d)), const((ATT_KV_WIDTH, d)), const((d, d)), const((1, d)),
        ],
        out_specs=(pl.BlockSpec((tm, d), lambda i: (i, 0)), pl.BlockSpec((tm, d), lambda i: (i, 0))),
        compiler_params=_cparams(("parallel",)),
        name=name,
    )(oh, oa, gates, x2d, wh, wa, wo, gc.reshape(1, d))


def _cross_kernel(n2_ref, h1_ref, kv_ref, wq_ref, wo_ref, gm_ref, wrh_ref, wrl_ref, br_ref,
                  h2_ref, n3_ref, idx_ref, tw_ref, rank_ref, cnt_ref, carry_ref, *, tm):
    @pl.when((pl.program_id(0) == 0) & (pl.program_id(1) == 0))
    def _():
        carry_ref[...] = jnp.zeros_like(carry_ref)

    nt_dims = (((1,), (1,)), ((), ()))
    scale = HEAD_DIM ** -0.5
    q = (jnp.dot(n2_ref[...], wq_ref[...], preferred_element_type=F32) * scale).astype(BF16)
    outs = []
    for hh in range(CROSS_HEADS):
        sl = slice(hh * HEAD_DIM, (hh + 1) * HEAD_DIM)
        kh = kv_ref[:, sl]
        vh = kv_ref[:, CROSS_WIDTH + hh * HEAD_DIM:CROSS_WIDTH + (hh + 1) * HEAD_DIM]
        s = lax.dot_general(q[:, sl], kh, nt_dims, preferred_element_type=F32)
        p = jnp.exp(s - jnp.max(s, axis=-1, keepdims=True))
        l = jnp.sum(p, axis=-1, keepdims=True)
        outs.append(jnp.dot(p.astype(BF16), vh, preferred_element_type=F32) / l)
    o = jnp.concatenate(outs, axis=-1).astype(BF16)
    h2 = h1_ref[...] + jnp.dot(o, wo_ref[...], preferred_element_type=F32)
    h2_ref[...] = h2
    n3 = _rms(h2, gm_ref[...])
    n3_ref[...] = n3

    n3h = n3.astype(BF16)
    n3l = (n3 - n3h.astype(F32)).astype(BF16)
    wrh = wrh_ref[...]
    logits = (jnp.dot(n3h, wrh, preferred_element_type=F32)
              + jnp.dot(n3l, wrh, preferred_element_type=F32)
              + jnp.dot(n3h, wrl_ref[...], preferred_element_type=F32)
              + br_ref[...])
    lane = lax.broadcasted_iota(I32, (tm, N_EXPERTS), 1).astype(F32)
    vals, idxs, hots = [], [], []
    cur = logits
    for _ in range(TOP_K):
        mx = jnp.max(cur, axis=-1, keepdims=True)
        ix = jnp.min(jnp.where(cur == mx, lane, float(N_EXPERTS)), axis=-1, keepdims=True)
        hot = lane == ix
        vals.append(mx)
        idxs.append(ix)
        hots.append(hot)
        cur = jnp.where(hot, -jnp.inf, cur)
    es = [jnp.exp(v - vals[0]) for v in vals]
    den = es[0] + es[1] + es[2] + es[3]
    col = lax.broadcasted_iota(I32, (tm, TOP_K), 1)

    def pack(cols):
        out = jnp.broadcast_to(cols[TOP_K - 1], (tm, TOP_K))
        for k in range(TOP_K - 2, -1, -1):
            out = jnp.where(col == k, cols[k], out)
        return out

    idx_ref[...] = pack(idxs).astype(I32)
    tw_ref[...] = pack([e / den for e in es])

    cmat = (hots[0] | hots[1] | hots[2] | hots[3]).astype(F32)
    rr = lax.broadcasted_iota(I32, (tm, tm), 0)
    cc = lax.broadcasted_iota(I32, (tm, tm), 1)
    tri = (cc < rr).astype(BF16)
    before = jnp.dot(tri, cmat.astype(BF16), preferred_element_type=F32) + carry_ref[...]
    ranks = [jnp.sum(jnp.where(hot, before, 0.0), axis=-1, keepdims=True) for hot in hots]
    rank_ref[...] = pack(ranks).astype(I32)
    carry = carry_ref[...] + jnp.sum(cmat, axis=0, keepdims=True)
    carry_ref[...] = carry
    cnt_ref[...] = carry


def _cross(n2, h1, kv, wq, wo, gm, wrh, wrl, br, *, batch, tm, name="cross"):
    n, d = h1.shape
    per_b = n // batch // tm
    mem_len = kv.shape[0] // batch

    def const(shape):
        return pl.BlockSpec(shape, lambda bi, i: (0, 0), pipeline_mode=pl.Buffered(1))

    def row(bi, i):
        return (bi * per_b + i, 0)

    return pl.pallas_call(
        functools.partial(_cross_kernel, tm=tm),
        out_shape=(
            jax.ShapeDtypeStruct((n, d), F32),
            jax.ShapeDtypeStruct((n, d), F32),
            jax.ShapeDtypeStruct((n, TOP_K), I32),
            jax.ShapeDtypeStruct((n, TOP_K), F32),
            jax.ShapeDtypeStruct((n, TOP_K), I32),
            jax.ShapeDtypeStruct((1, N_EXPERTS), F32),
        ),
        grid=(batch, per_b),
        in_specs=[
            pl.BlockSpec((tm, d), row),
            pl.BlockSpec((tm, d), row),
            pl.BlockSpec((mem_len, 2 * CROSS_WIDTH), lambda bi, i: (bi, 0)),
            const((d, CROSS_WIDTH)), const((CROSS_WIDTH, d)), const((1, d)),
            const((d, N_EXPERTS)), const((d, N_EXPERTS)), const((1, N_EXPERTS)),
        ],
        out_specs=(
            pl.BlockSpec((tm, d), row),
            pl.BlockSpec((tm, d), row),
            pl.BlockSpec((tm, TOP_K), row),
            pl.BlockSpec((tm, TOP_K), row),
            pl.BlockSpec((tm, TOP_K), row),
            pl.BlockSpec((1, N_EXPERTS), lambda bi, i: (0, 0)),
        ),
        scratch_shapes=[pltpu.VMEM((1, N_EXPERTS), F32)],
        compiler_params=_cparams(("arbitrary", "arbitrary")),
        name=name,
    )(n2, h1, kv, wq, wo, gm.reshape(1, d), wrh, wrl, br.reshape(1, N_EXPERTS))


def _slab(ref, row):
    return ref.at[pl.ds(pl.multiple_of(row * MOE_SLAB, MOE_SLAB), MOE_SLAB)]


def _row_copies(make_copy, count):
    ngroup = count // DMA_WAIT_ROWS

    def start_group(c):
        def one(j, carry):
            make_copy(c * DMA_WAIT_ROWS + j).start()
            return carry

        lax.fori_loop(0, DMA_WAIT_ROWS, one, 0)

    def wait_group(c):
        def one(j, carry):
            make_copy(c * DMA_WAIT_ROWS + j).wait()
            return carry

        lax.fori_loop(0, DMA_WAIT_ROWS, one, 0)

    def group(c, carry):
        start_group(c)

        @pl.when(c > 0)
        def _():
            wait_group(c - 1)

        return carry

    lax.fori_loop(0, ngroup, group, 0)
    wait_group(ngroup - 1)


def _dispatch_kernel(slot_ref, n3_ref, xs_hbm, stage, sem, *, real_steps, tm):
    step = pl.program_id(0)

    @pl.when(step < real_steps)
    def _():
        x = n3_ref[...]
        for j in range(MOE_SLAB):
            stage[pl.ds(j, tm, stride=MOE_SLAB), :] = x[:, j * V7X_LANES:(j + 1) * V7X_LANES]

        def make_copy(a):
            return pltpu.make_async_copy(_slab(stage, a >> TOP_K_LOG2), _slab(xs_hbm, slot_ref[0, a]), sem)

        _row_copies(make_copy, tm * TOP_K)

    @pl.when(step >= real_steps)
    def _():
        stage[0:MOE_SLAB, :] = jnp.zeros((MOE_SLAB, V7X_LANES), F32)

        def make_copy(a):
            return pltpu.make_async_copy(_slab(stage, 0), _slab(xs_hbm, slot_ref[0, a]), sem)

        _row_copies(make_copy, tm * TOP_K)


def _dispatch(slots, n3, *, rows_total, tm, name="moe_dispatch"):
    n, d = n3.shape
    real_steps = n // tm
    return pl.pallas_call(
        functools.partial(_dispatch_kernel, real_steps=real_steps, tm=tm),
        out_shape=jax.ShapeDtypeStruct((rows_total * MOE_SLAB, V7X_LANES), F32),
        grid=(slots.shape[0],),
        in_specs=[
            pl.BlockSpec((None, 1, tm * TOP_K), lambda i: (i, 0, 0), memory_space=pltpu.SMEM),
            pl.BlockSpec((tm, d), lambda i: (jnp.minimum(i, real_steps - 1), 0)),
        ],
        out_specs=pl.BlockSpec(memory_space=pl.ANY),
        scratch_shapes=[pltpu.VMEM((tm * MOE_SLAB, V7X_LANES), F32), pltpu.SemaphoreType.DMA],
        compiler_params=_cparams(("arbitrary",)),
        name=name,
    )(slots, n3)


def _moe_kernel(te_ref, tv_ref, tstart_ref, trows_ref,
                xs_hbm, w1_ref, b1_ref, w2_ref, b2_ref, perm_ref, yb_hbm,
                stage, x16, acc, w1p, sems, *, nf):
    i = pl.program_id(0)
    f = pl.program_id(1)
    rows = trows_ref[i]
    start = tstart_ref[i]
    nchunk = rows // MOE_ROW_ALIGN
    chunk_slab_rows = MOE_ROW_ALIGN * MOE_SLAB
    half = 2 * V7X_LANES
    nhalf = 2 * MOE_TF // half

    def slot_rows(c):
        return pl.multiple_of((c & 1) * chunk_slab_rows, chunk_slab_rows)

    def hbm_rows(ref, c):
        r0 = pl.multiple_of((start + c * MOE_ROW_ALIGN) * MOE_SLAB, chunk_slab_rows)
        return ref.at[pl.ds(r0, chunk_slab_rows)]

    def in_copy(c):
        return pltpu.make_async_copy(hbm_rows(xs_hbm, c), stage.at[pl.ds(slot_rows(c), chunk_slab_rows)],
                                     sems.at[c & 1])

    def out_copy(c):
        return pltpu.make_async_copy(stage.at[pl.ds(slot_rows(c), chunk_slab_rows)], hbm_rows(yb_hbm, c),
                                     sems.at[c & 1])

    @pl.when((rows > 0) & (f == 0))
    def _():
        in_copy(0).start()

        def land(c, carry):
            @pl.when(c + 1 < nchunk)
            def _():
                in_copy(c + 1).start()

            in_copy(c).wait()
            r0 = pl.multiple_of(c * MOE_ROW_ALIGN, MOE_ROW_ALIGN)
            s0 = slot_rows(c)
            for j in range(MOE_SLAB):
                xj = stage[pl.ds(s0 + j, MOE_ROW_ALIGN, stride=MOE_SLAB), :]
                x16[pl.ds(r0, MOE_ROW_ALIGN), j * V7X_LANES:(j + 1) * V7X_LANES] = xj.astype(BF16)
            acc[pl.ds(r0, MOE_ROW_ALIGN), :] = jnp.zeros((MOE_ROW_ALIGN, D_MODEL), F32)
            return carry

        lax.fori_loop(0, nchunk, land, 0)

    @pl.when(rows > 0)
    def _():
        perm = perm_ref[...]
        for c in range(nhalf):
            wb = w1_ref[:, c * half:(c + 1) * half].astype(BF16)
            w1p[:, c * half:(c + 1) * half] = jnp.dot(wb, perm, preferred_element_type=F32).astype(BF16)
        w2 = w2_ref[...].astype(BF16)
        b1 = b1_ref[...]

        def mlp(c0, nc):
            r0 = pl.multiple_of(c0 * MOE_ROW_ALIGN, MOE_ROW_ALIGN)
            nr = nc * MOE_ROW_ALIGN
            x = x16[pl.ds(r0, nr), :]
            h = jnp.dot(x, w1p[...], preferred_element_type=F32) + b1
            parts = []
            for c in range(nhalf):
                hg = jnp.minimum(h[:, c * half:c * half + V7X_LANES], SWIGLU_LIMIT)
                hl = jnp.clip(h[:, c * half + V7X_LANES:(c + 1) * half], -SWIGLU_LIMIT, SWIGLU_LIMIT)
                parts.append(hg * jax.nn.sigmoid(SWIGLU_ALPHA * hg) * (hl + 1.0))
            a = jnp.concatenate(parts, axis=-1).astype(BF16)
            acc[pl.ds(r0, nr), :] += jnp.dot(a, w2, preferred_element_type=F32)

        nquad = nchunk // MOE_BLOCK_CHUNKS

        def quad(c, carry):
            mlp(c * MOE_BLOCK_CHUNKS, MOE_BLOCK_CHUNKS)
            return carry

        lax.fori_loop(0, nquad, quad, 0)
        rem = nchunk - nquad * MOE_BLOCK_CHUNKS

        @pl.when(rem >= 2)
        def _():
            mlp(nquad * MOE_BLOCK_CHUNKS, 2)

        @pl.when((rem & 1) == 1)
        def _():
            mlp(nchunk - 1, 1)

    @pl.when((rows > 0) & (f == nf - 1))
    def _():
        def emit(c, carry):
            @pl.when(c >= 2)
            def _():
                out_copy(c - 2).wait()

            r0 = pl.multiple_of(c * MOE_ROW_ALIGN, MOE_ROW_ALIGN)
            s0 = slot_rows(c)
            y = acc[pl.ds(r0, MOE_ROW_ALIGN), :] + b2_ref[...]
            for j in range(MOE_SLAB):
                stage[pl.ds(s0 + j, MOE_ROW_ALIGN, stride=MOE_SLAB), :] = y[:, j * V7X_LANES:(j + 1) * V7X_LANES]
            out_copy(c).start()
            return carry

        lax.fori_loop(0, nchunk, emit, 0)

        @pl.when(nchunk >= 2)
        def _():
            out_copy(nchunk - 2).wait()

        out_copy(nchunk - 1).wait()


def _moe(tile_e, tile_v, tile_start, tile_rows, xs, w1, b1p, w2, b2, perm, *, name="moe_mlp"):
    nt = tile_e.shape[0]
    nf = D_FF // MOE_TF

    def w1_map(i, f, te, tv, ts, tr):
        return (te[i], 0, jnp.where(tv[i] > 0, f, nf - 1))

    def w2_map(i, f, te, tv, ts, tr):
        return (te[i], jnp.where(tv[i] > 0, f, nf - 1), 0)

    def b2_map(i, f, te, tv, ts, tr):
        return (te[i], 0, 0)

    grid_spec = pltpu.PrefetchScalarGridSpec(
        num_scalar_prefetch=4,
        grid=(nt, nf),
        in_specs=[
            pl.BlockSpec(memory_space=pl.ANY),
            pl.BlockSpec((None, D_MODEL, 2 * MOE_TF), w1_map),
            pl.BlockSpec((None, 1, 2 * MOE_TF), w1_map),
            pl.BlockSpec((None, MOE_TF, D_MODEL), w2_map),
            pl.BlockSpec((None, 1, D_MODEL), b2_map),
            pl.BlockSpec((2 * V7X_LANES, 2 * V7X_LANES), lambda i, f, te, tv, ts, tr: (0, 0)),
        ],
        out_specs=pl.BlockSpec(memory_space=pl.ANY),
        scratch_shapes=[
            pltpu.VMEM((2 * MOE_ROW_ALIGN * MOE_SLAB, V7X_LANES), F32),
            pltpu.VMEM((MOE_TMAX, D_MODEL), BF16),
            pltpu.VMEM((MOE_TMAX, D_MODEL), F32),
            pltpu.VMEM((D_MODEL, 2 * MOE_TF), BF16),
            pltpu.SemaphoreType.DMA((2,)),
        ],
    )
    return pl.pallas_call(
        functools.partial(_moe_kernel, nf=nf),
        out_shape=jax.ShapeDtypeStruct(xs.shape, F32),
        grid_spec=grid_spec,
        input_output_aliases={4: 0},
        compiler_params=_cparams(("arbitrary", "arbitrary")),
        name=name,
    )(tile_e, tile_v, tile_start, tile_rows, xs, w1, b1p.reshape(N_EXPERTS, 1, 2 * D_FF), w2,
      b2.reshape(N_EXPERTS, 1, D_MODEL), perm)


def _combine_kernel(slot_ref, h2_ref, tw_ref, gf_ref, yb_hbm, o_ref, buf, sem, *, tm):
    def make_copy(a):
        t = a >> TOP_K_LOG2
        k = a & (TOP_K - 1)
        return pltpu.make_async_copy(_slab(yb_hbm, slot_ref[0, a]), _slab(buf, k * tm + t), sem)

    _row_copies(make_copy, tm * TOP_K)

    tw = tw_ref[...]
    cols = []
    for j in range(MOE_SLAB):
        cj = None
        for k in range(TOP_K):
            part = tw[:, k:k + 1] * buf[pl.ds(k * tm * MOE_SLAB + j, tm, stride=MOE_SLAB), :]
            cj = part if cj is None else cj + part
        cols.append(cj)
    h3 = h2_ref[...] + jnp.concatenate(cols, axis=-1)
    o_ref[...] = _rms(h3, gf_ref[...])


def _combine(slots, h2, tw, gf, yb, *, tm, name="moe_combine"):
    n, d = h2.shape
    return pl.pallas_call(
        functools.partial(_combine_kernel, tm=tm),
        out_shape=jax.ShapeDtypeStruct((n, d), F32),
        grid=(n // tm,),
        in_specs=[
            pl.BlockSpec((None, 1, tm * TOP_K), lambda i: (i, 0, 0), memory_space=pltpu.SMEM),
            pl.BlockSpec((tm, d), lambda i: (i, 0)),
            pl.BlockSpec((tm, TOP_K), lambda i: (i, 0)),
            pl.BlockSpec((1, d), lambda i: (0, 0)),
            pl.BlockSpec(memory_space=pl.ANY),
        ],
        out_specs=pl.BlockSpec((tm, d), lambda i: (i, 0)),
        scratch_shapes=[
            pltpu.VMEM((tm * TOP_K * MOE_SLAB, V7X_LANES), F32),
            pltpu.SemaphoreType.DMA,
        ],
        compiler_params=_cparams(("arbitrary",)),
        name=name,
    )(slots, h2, tw, gf.reshape(1, d), yb)


def _rope_tables(positions):
    b, s = positions.shape
    half = HEAD_DIM // 2
    inv_freq = 1.0 / (ROPE_THETA ** (jnp.arange(half, dtype=F32) / half))
    ang = positions.astype(F32)[..., None] * inv_freq
    cos = jnp.cos(ang)
    sin = jnp.sin(ang)
    cs = jnp.concatenate([cos, cos], axis=-1)
    sn = jnp.concatenate([-sin, sin], axis=-1)

    def deint(a):
        a = a.reshape(b, s // ATT_TILE, WIN_BLOCK, DIL_MAX, HEAD_DIM)
        return a.transpose(0, 1, 3, 2, 4)

    return deint(cs), deint(sn)


def _routing(top_e, rank, counts, n_tok):
    cnt = counts.reshape(N_EXPERTS).astype(I32)
    padded = (cnt + MOE_ROW_ALIGN - 1) // MOE_ROW_ALIGN * MOE_ROW_ALIGN
    gend = jnp.cumsum(padded)
    gstart = gend - padded
    slot = (gstart[top_e] + rank).reshape(-1)

    n_assign = n_tok * TOP_K
    n_unused = N_EXPERTS * MOE_ROW_ALIGN
    rows_total = n_assign + n_unused
    padcnt = padded - cnt
    pend = jnp.cumsum(padcnt)
    pstart = pend - padcnt
    p = jnp.arange(n_unused, dtype=I32)
    ep = jnp.minimum(jnp.searchsorted(pend, p, side="right"), N_EXPERTS - 1).astype(I32)
    unused = jnp.where(p < pend[-1], gstart[ep] + cnt[ep] + (p - pstart[ep]), gend[-1] + (p - pend[-1]))

    nt_max = rows_total // MOE_TMAX + N_EXPERTS
    nt_e = (padded + MOE_TMAX - 1) // MOE_TMAX
    tend = jnp.cumsum(nt_e)
    tstart = tend - nt_e
    ti = jnp.arange(nt_max, dtype=I32)
    valid = ti < tend[-1]
    tic = jnp.minimum(ti, tend[-1] - 1)
    te = jnp.minimum(jnp.searchsorted(tend, tic, side="right"), N_EXPERTS - 1).astype(I32)
    local = tic - tstart[te]
    row0 = gstart[te] + local * MOE_TMAX
    rows = jnp.where(valid, jnp.clip(padded[te] - local * MOE_TMAX, 0, MOE_TMAX), 0)
    return (slot.astype(I32), unused.astype(I32), rows_total, te, valid.astype(I32),
            row0.astype(I32), rows.astype(I32))


def kernel(x, mem, positions, norm_mix_g, w_in, lb_raw, hgrn_norm_g, w_br_hgrn, w_br_attn, w_out,
           norm_cross_g, norm_mem_g, w_cq, w_ckv, w_co, norm_moe_g, w_router, b_router,
           w_mlp1, b_mlp1, w_mlp2, b_mlp2, norm_final_g):
    bsz, seq, d = x.shape
    assert w_in.shape[0] == 1 and d == D_MODEL and seq % ATT_TILE == 0
    n_tok = bsz * seq
    lower_bounds = jnp.cumsum(jax.nn.softmax(lb_raw.astype(F32), axis=0), axis=0)
    cs, sn = _rope_tables(positions)

    wl = w_in[0]
    c_h = 4 * HG_WIDTH
    c_a = c_h + ATT_Q_WIDTH + 2 * ATT_KV_WIDTH
    x2d = x.reshape(n_tok, d)
    proj_h = _in_proj(x2d, norm_mix_g[0], wl[:, :c_h].astype(BF16), tm=1024, tn=1024, name="in_proj_hgrn")
    proj_g = _in_proj(x2d, norm_mix_g[0], wl[:, c_a:].astype(BF16), tm=1024, tn=1024, name="in_proj_gate")
    pa = _in_proj_strided(x, norm_mix_g[0], wl[:, c_h:c_a].astype(BF16), c=8, tn=1024, name="in_proj_attn")

    o_h = _hgrn(proj_h.reshape(bsz, seq, c_h), lower_bounds[0], hgrn_norm_g[0], ts=512)
    o_a = _dilated_attn(pa, cs, sn)

    h1, n2 = _merge_out(o_h.reshape(n_tok, HG_WIDTH), o_a.reshape(n_tok, ATT_KV_WIDTH), proj_g, x2d,
                        w_br_hgrn[0].astype(BF16), w_br_attn[0].astype(BF16), w_out[0].astype(BF16),
                        norm_cross_g[0], tm=256)

    mem2d = mem.reshape(-1, d)
    kv = _in_proj(mem2d, norm_mem_g, w_ckv[0].astype(BF16), tm=mem2d.shape[0], tn=2 * CROSS_WIDTH, name="mem_kv")
    wrh = w_router[0].astype(BF16)
    wrl = (w_router[0] - wrh.astype(F32)).astype(BF16)
    h2, n3, top_e, top_w, rank, counts = _cross(
        n2, h1, kv, w_cq[0].astype(BF16), w_co[0].astype(BF16), norm_moe_g[0], wrh, wrl, b_router[0],
        batch=bsz, tm=512)

    slot, unused, rows_total, te, tv, tstart, trows = _routing(top_e, rank, counts, n_tok)
    tm_d = DISPATCH_TOKENS
    slots_all = jnp.concatenate([slot, unused]).reshape(-1, 1, tm_d * TOP_K)
    xs = _dispatch(slots_all, n3, rows_total=rows_total, tm=tm_d)

    ii = jnp.arange(2 * V7X_LANES)
    src = jnp.where(ii < V7X_LANES, 2 * ii, 2 * (ii - V7X_LANES) + 1)
    perm = (ii[:, None] == src[None, :]).astype(BF16)
    b1p = b_mlp1[0].reshape(N_EXPERTS, -1, V7X_LANES, 2).transpose(0, 1, 3, 2).reshape(N_EXPERTS, 2 * D_FF)
    yb = _moe(te, tv, tstart, trows, xs, w_mlp1[0], b1p, w_mlp2[0], b_mlp2[0], perm)

    tm_c = 256
    out = _combine(slot.reshape(-1, 1, tm_c * TOP_K), h2, top_w, norm_final_g, yb, tm=tm_c)
    return out.reshape(bsz, seq, d)
```

```python
import functools

import jax
import jax.numpy as jnp
from jax import lax
from jax.experimental import pallas as pl
from jax.experimental.pallas import tpu as pltpu

F32 = jnp.float32
BF16 = jnp.bfloat16
I32 = jnp.int32

D_MODEL = 2048
HEAD_DIM = 128
HG_HEADS = 8
HG_WIDTH = HG_HEADS * HEAD_DIM
N_KV_HEADS = 8
N_GROUPS = 3
ATT_Q_WIDTH = N_GROUPS * N_KV_HEADS * HEAD_DIM
ATT_KV_WIDTH = N_KV_HEADS * HEAD_DIM
WIN_BLOCK = 128
ROPE_THETA = 10000.0
CROSS_HEADS = 4
CROSS_WIDTH = CROSS_HEADS * HEAD_DIM
N_EXPERTS = 32
TOP_K = 4
TOP_K_LOG2 = 2
D_FF = D_MODEL
SWIGLU_ALPHA = 1.702
SWIGLU_LIMIT = 7.0
NORM_EPS = 1e-6

V7X_LANES = 128
V7X_VMEM_LIMIT_BYTES = 56 * 1024 * 1024

DIL_MAX = 16
ATT_TILE = DIL_MAX * WIN_BLOCK
ATT_UNROLL = 8
HG_BLOCK = 16
HG_GROUP = 16
NEG_BIG = -1e30

MOE_TMAX = 2048
MOE_ROW_ALIGN = 128
MOE_BLOCK_CHUNKS = 4
MOE_TF = 256
MOE_SLAB = D_MODEL // V7X_LANES
DISPATCH_TOKENS = 256
DMA_WAIT_ROWS = 128


def _rms(x, g):
    ms = jnp.mean(x * x, axis=-1, keepdims=True)
    return x * lax.rsqrt(ms + NORM_EPS) * g


def _cparams(sem, vmem=V7X_VMEM_LIMIT_BYTES):
    return pltpu.CompilerParams(dimension_semantics=sem, vmem_limit_bytes=vmem)


def _in_proj_kernel(x_ref, g_ref, w_ref, o_ref, xn_ref):
    @pl.when(pl.program_id(1) == 0)
    def _():
        xn_ref[...] = _rms(x_ref[...], g_ref[...]).astype(BF16)

    o_ref[...] = jnp.dot(xn_ref[...], w_ref[...], preferred_element_type=F32).astype(o_ref.dtype)


def _in_proj(x2d, g, w, *, tm, tn, name):
    n, d = x2d.shape
    wc = w.shape[1]
    return pl.pallas_call(
        _in_proj_kernel,
        out_shape=jax.ShapeDtypeStruct((n, wc), BF16),
        grid=(n // tm, wc // tn),
        in_specs=[
            pl.BlockSpec((tm, d), lambda i, j: (i, 0)),
            pl.BlockSpec((1, d), lambda i, j: (0, 0)),
            pl.BlockSpec((d, tn), lambda i, j: (0, j)),
        ],
        out_specs=pl.BlockSpec((tm, tn), lambda i, j: (i, j)),
        scratch_shapes=[pltpu.VMEM((tm, d), BF16)],
        compiler_params=_cparams(("parallel", "arbitrary")),
        name=name,
    )(x2d, g.reshape(1, d), w)


def _in_proj_strided_kernel(x_ref, g_ref, w_ref, o_ref, xn_ref, *, c):
    @pl.when(pl.program_id(3) == 0)
    def _():
        for ci in range(c):
            x = x_ref[:, ci * D_MODEL:(ci + 1) * D_MODEL]
            xn_ref[ci * WIN_BLOCK:(ci + 1) * WIN_BLOCK, :] = _rms(x, g_ref[...]).astype(BF16)

    res = jnp.dot(xn_ref[...], w_ref[...], preferred_element_type=F32)
    for ci in range(c):
        o_ref[ci] = res[ci * WIN_BLOCK:(ci + 1) * WIN_BLOCK, :].astype(o_ref.dtype)


def _in_proj_strided(x, g, w, *, c, tn, name):
    b, s, d = x.shape
    nt = s // ATT_TILE
    wc = w.shape[1]
    xv = x.reshape(b, nt, WIN_BLOCK, DIL_MAX * d)
    return pl.pallas_call(
        functools.partial(_in_proj_strided_kernel, c=c),
        out_shape=jax.ShapeDtypeStruct((b, nt, DIL_MAX, WIN_BLOCK, wc), BF16),
        grid=(b, nt, DIL_MAX // c, wc // tn),
        in_specs=[
            pl.BlockSpec((None, None, WIN_BLOCK, c * d), lambda bi, ti, ri, j: (bi, ti, 0, ri)),
            pl.BlockSpec((1, d), lambda bi, ti, ri, j: (0, 0)),
            pl.BlockSpec((d, tn), lambda bi, ti, ri, j: (0, j)),
        ],
        out_specs=pl.BlockSpec((None, None, c, WIN_BLOCK, tn), lambda bi, ti, ri, j: (bi, ti, ri, 0, j)),
        scratch_shapes=[pltpu.VMEM((c * WIN_BLOCK, d), BF16)],
        compiler_params=_cparams(("parallel", "parallel", "parallel", "arbitrary")),
        name=name,
    )(xv, g.reshape(1, d), w)


def _hgrn_kernel(q_ref, f_ref, i_ref, g_ref, lb_ref, gn_ref, o_ref, st_ref, kin_s, b_s, v_s, *, ts):
    @pl.when(pl.program_id(2) == 0)
    def _():
        st_ref[...] = jnp.zeros_like(st_ref)

    lb = lb_ref[...]
    oml = 1.0 - lb
    gn = gn_ref[...]
    half = HG_BLOCK // 2
    row = lax.broadcasted_iota(I32, (HG_BLOCK, HEAD_DIM), 0)
    row8 = lax.broadcasted_iota(I32, (half, HEAD_DIM), 0)
    nt_dims = (((1,), (1,)), ((), ()))
    tn_dims = (((0,), (0,)), ((), ()))

    def front(g, t0):
        sl = pl.ds(t0 + g * HG_BLOCK, HG_BLOCK)
        q = q_ref[sl, :].astype(F32)
        hf = f_ref[sl, :].astype(F32)
        v = i_ref[sl, :].astype(F32)
        kin = oml * jax.nn.sigmoid(-hf)
        b = jnp.log2(lb + oml * jax.nn.sigmoid(hf))
        for sh in (1, 2, 4, 8):
            b = b + jnp.where(row >= sh, pltpu.roll(b, sh, 0), 0.0)
        kin_s[g] = kin
        b_s[g] = b
        v_s[g] = v
        q_lo, q_hi = q[:half], q[half:]
        b_lo, b_hi = b[:half], b[half:]
        o_lo = jnp.zeros((half, HEAD_DIM), F32)
        o_hi = jnp.zeros((half, HEAD_DIM), F32)
        for s in range(HG_BLOCK):
            ks = kin_s[g, s:s + 1, :]
            bs = b_s[g, s:s + 1, :]
            vs = v_s[g, s:s + 1, :]
            if s < half:
                w = q_lo * ks * jnp.exp2(b_lo - bs)
                if s > 0:
                    w = jnp.where(row8 >= s, w, 0.0)
                o_lo = o_lo + jnp.sum(w, axis=-1, keepdims=True) * vs
                w = q_hi * ks * jnp.exp2(b_hi - bs)
            else:
                w = q_hi * ks * jnp.exp2(b_hi - bs)
                if s > half:
                    w = jnp.where(row8 >= s - half, w, 0.0)
            o_hi = o_hi + jnp.sum(w, axis=-1, keepdims=True) * vs
        bl = b_s[g, HG_BLOCK - 1:HG_BLOCK, :]
        qd = (q * jnp.exp2(b)).astype(BF16)
        kd = (kin * jnp.exp2(bl - b)).astype(BF16)
        upd = lax.dot_general(v.astype(BF16), kd, tn_dims, preferred_element_type=F32)
        return jnp.concatenate([o_lo, o_hi], axis=0), qd, upd, jnp.exp2(bl)

    def body(i, carry):
        t0 = pl.multiple_of(i * (HG_GROUP * HG_BLOCK), HG_GROUP * HG_BLOCK)
        fronts = [front(g, t0) for g in range(HG_GROUP)]
        st = st_ref[...]
        for g, (o_diag, qd, upd, dec) in enumerate(fronts):
            o = o_diag + lax.dot_general(qd, st.astype(BF16), nt_dims, preferred_element_type=F32)
            st = st * dec + upd
            sl = pl.ds(t0 + g * HG_BLOCK, HG_BLOCK)
            hg = g_ref[sl, :].astype(F32)
            o_ref[sl, :] = (_rms(o, gn) * (hg * jax.nn.sigmoid(hg))).astype(o_ref.dtype)
        st_ref[...] = st
        return carry

    lax.fori_loop(0, ts // (HG_GROUP * HG_BLOCK), body, 0)


def _hgrn(proj_h, lb, gn, *, ts, name="hgrn"):
    b, s, _ = proj_h.shape
    h = HG_HEADS

    def spec(off):
        return pl.BlockSpec((None, ts, HEAD_DIM), lambda bi, hi, si: (bi, si, off + hi))

    vec = pl.BlockSpec((1, HEAD_DIM), lambda bi, hi, si: (0, hi))
    return pl.pallas_call(
        functools.partial(_hgrn_kernel, ts=ts),
        out_shape=jax.ShapeDtypeStruct((b, s, HG_WIDTH), BF16),
        grid=(b, h, s // ts),
        in_specs=[spec(0), spec(h), spec(2 * h), spec(3 * h), vec, vec],
        out_specs=pl.BlockSpec((None, ts, HEAD_DIM), lambda bi, hi, si: (bi, si, hi)),
        scratch_shapes=[
            pltpu.VMEM((HEAD_DIM, HEAD_DIM), F32),
            pltpu.VMEM((HG_GROUP, HG_BLOCK, HEAD_DIM), F32),
            pltpu.VMEM((HG_GROUP, HG_BLOCK, HEAD_DIM), F32),
            pltpu.VMEM((HG_GROUP, HG_BLOCK, HEAD_DIM), F32),
        ],
        compiler_params=_cparams(("parallel", "parallel", "arbitrary")),
        name=name,
    )(proj_h, proj_h, proj_h, proj_h, lb.reshape(1, HG_WIDTH), gn.reshape(1, HG_WIDTH))


def _attn_bias(kind):
    rq = lax.broadcasted_iota(I32, (WIN_BLOCK, 2 * WIN_BLOCK), 0)
    ck = lax.broadcasted_iota(I32, (WIN_BLOCK, 2 * WIN_BLOCK), 1)
    if kind == 2:
        dist = rq + WIN_BLOCK - ck
        first = ck < WIN_BLOCK
    elif kind == 1:
        dist = 4 * ((rq & 31) - (ck & 63) + 32) + ((rq >> 5) - (ck >> 6))
        first = (ck & 63) < 32
    else:
        dist = 16 * ((rq & 7) - (ck & 15) + 8) + ((rq >> 3) - (ck >> 4))
        first = (ck & 15) < 8
    valid = (dist >= 0) & (dist <= WIN_BLOCK)
    return (jnp.where(valid, 0.0, NEG_BIG).astype(F32),
            jnp.where(valid & jnp.logical_not(first), 0.0, NEG_BIG).astype(F32))


def _attn_kernel(q0_ref, q1_ref, q2_ref, k_ref, v_ref, cs_ref, sn_ref, o_hbm,
                 qr, kext, vext, acc, mrun, lrun, bias, stage, sem):
    bi = pl.program_id(0)
    hi = pl.program_id(1)
    ti = pl.program_id(2)
    wb = WIN_BLOCK
    scale = HEAD_DIM ** -0.5

    @pl.when(ti == 0)
    def _():
        kext[:, 0:wb, :] = jnp.zeros((DIL_MAX, wb, HEAD_DIM), F32)
        vext[:, 0:wb, :] = jnp.zeros((DIL_MAX, wb, HEAD_DIM), F32)

    @pl.when(ti > 0)
    def _():
        kext[:, 0:wb, :] = kext[:, wb:2 * wb, :]
        vext[:, 0:wb, :] = vext[:, wb:2 * wb, :]

    for kind in range(N_GROUPS):
        full, nofirst = _attn_bias(kind)
        bias[2 * kind] = full
        bias[2 * kind + 1] = nofirst

    def rope_body(r, carry):
        cs = cs_ref[r]
        sn = sn_ref[r]
        for g, qref in enumerate((q0_ref, q1_ref, q2_ref)):
            q = qref[r].astype(F32)
            qr[g, r] = (q * cs + pltpu.roll(q, HEAD_DIM // 2, 1) * sn) * scale
        k = k_ref[r].astype(F32)
        kext[r, wb:2 * wb, :] = k * cs + pltpu.roll(k, HEAD_DIM // 2, 1) * sn
        vext[r, wb:2 * wb, :] = v_ref[r].astype(F32)
        acc[r] = jnp.zeros((wb, HEAD_DIM), F32)
        mrun[r] = jnp.full((wb, HEAD_DIM), NEG_BIG, F32)
        lrun[r] = jnp.zeros((wb, HEAD_DIM), F32)
        return carry

    lax.fori_loop(0, DIL_MAX, rope_body, 0)

    nt_dims = (((1,), (1,)), ((), ()))

    def block(qb, kb, vb, bias_blk):
        s = lax.dot_general(qb.astype(BF16), kb.astype(BF16), nt_dims, preferred_element_type=F32)
        s = s + bias_blk
        m = jnp.max(s, axis=-1, keepdims=True)
        p = jnp.exp(s - m)
        l = jnp.sum(p, axis=-1, keepdims=True)
        n = jnp.dot(p.astype(BF16), vb.astype(BF16), preferred_element_type=F32)
        return n, jnp.broadcast_to(m, (wb, HEAD_DIM)), jnp.broadcast_to(l, (wb, HEAD_DIM))

    def merge(r, rows, n, m, l):
        m_old = mrun[r, rows, :]
        m_new = jnp.maximum(m_old, m)
        a = jnp.exp(m_old - m_new)
        bb = jnp.exp(m - m_new)
        acc[r, rows, :] = acc[r, rows, :] * a + n * bb
        lrun[r, rows, :] = lrun[r, rows, :] * a + l * bb
        mrun[r, rows, :] = m_new

    first_tile = jnp.where(ti == 0, 1, 0)


    def g2_body(i, carry):
        rs = [i * ATT_UNROLL + u for u in range(ATT_UNROLL)]
        res = [block(qr[2, r], kext[r], vext[r], bias[4 + first_tile]) for r in rs]
        for r, (n, m, l) in zip(rs, res):
            merge(r, pl.ds(0, wb), n, m, l)
        return carry

    lax.fori_loop(0, DIL_MAX // ATT_UNROLL, g2_body, 0)

    def g1_body(i, carry):
        res = []
        for u in range(ATT_UNROLL // 4):
            mb = i * (ATT_UNROLL // 4) + u
            q0 = pl.multiple_of(32 * mb, 32)
            k0 = pl.multiple_of(96 + 32 * mb, 32)
            use_first = jnp.where(mb == 0, first_tile, 0)
            for r4 in range(4):
                qb = jnp.concatenate([qr[1, r4 + 4 * j, pl.ds(q0, 32), :] for j in range(4)], axis=0)
                kb = jnp.concatenate([kext[r4 + 4 * j, pl.ds(k0, 64), :] for j in range(4)], axis=0)
                vb = jnp.concatenate([vext[r4 + 4 * j, pl.ds(k0, 64), :] for j in range(4)], axis=0)
                res.append((r4, q0, block(qb, kb, vb, bias[2 + use_first])))
        for r4, q0, (n, m, l) in res:
            for j in range(4):
                sl = slice(32 * j, 32 * (j + 1))
                merge(r4 + 4 * j, pl.ds(q0, 32), n[sl], m[sl], l[sl])
        return carry

    lax.fori_loop(0, DIL_MAX // ATT_UNROLL, g1_body, 0)

    def g0_body(i, carry):
        res = []
        for u in range(ATT_UNROLL):
            mb = i * ATT_UNROLL + u
            q0 = pl.multiple_of(8 * mb, 8)
            k0 = pl.multiple_of(120 + 8 * mb, 8)
            qb = jnp.concatenate([qr[0, r, pl.ds(q0, 8), :] for r in range(DIL_MAX)], axis=0)
            kb = jnp.concatenate([kext[r, pl.ds(k0, 16), :] for r in range(DIL_MAX)], axis=0)
            vb = jnp.concatenate([vext[r, pl.ds(k0, 16), :] for r in range(DIL_MAX)], axis=0)
            use_first = jnp.where(mb == 0, first_tile, 0)
            res.append((q0, block(qb, kb, vb, bias[use_first])))
        for q0, (n, m, l) in res:
            for r in range(DIL_MAX):
                sl = slice(8 * r, 8 * (r + 1))
                merge(r, pl.ds(q0, 8), n[sl], m[sl], l[sl])
        return carry

    lax.fori_loop(0, DIL_MAX // ATT_UNROLL, g0_body, 0)

    def out_copy(r):
        col = pl.multiple_of((r * N_KV_HEADS + hi) * HEAD_DIM, HEAD_DIM)
        return pltpu.make_async_copy(stage.at[r], o_hbm.at[bi, ti, :, pl.ds(col, HEAD_DIM)], sem)

    for r in range(DIL_MAX):
        stage[r] = (acc[r] / lrun[r]).astype(BF16)
        out_copy(r).start()
    for r in range(DIL_MAX):
        out_copy(r).wait()


def _dilated_attn(pa, cs, sn, *, name="dilated_attn"):
    b, nt = pa.shape[0], pa.shape[1]
    h = N_KV_HEADS

    def spec(off):
        return pl.BlockSpec((None, None, DIL_MAX, WIN_BLOCK, HEAD_DIM),
                            lambda bi, hi, ti: (bi, ti, 0, 0, off + hi))

    tab = pl.BlockSpec((None, None, DIL_MAX, WIN_BLOCK, HEAD_DIM), lambda bi, hi, ti: (bi, ti, 0, 0, 0))
    out = pl.pallas_call(
        _attn_kernel,
        out_shape=jax.ShapeDtypeStruct((b, nt, WIN_BLOCK, DIL_MAX * ATT_KV_WIDTH), BF16),
        grid=(b, h, nt),
        in_specs=[spec(0), spec(h), spec(2 * h), spec(3 * h), spec(4 * h), tab, tab],
        out_specs=pl.BlockSpec(memory_space=pl.ANY),
        scratch_shapes=[
            pltpu.VMEM((N_GROUPS, DIL_MAX, WIN_BLOCK, HEAD_DIM), F32),
            pltpu.VMEM((DIL_MAX, 2 * WIN_BLOCK, HEAD_DIM), F32),
            pltpu.VMEM((DIL_MAX, 2 * WIN_BLOCK, HEAD_DIM), F32),
            pltpu.VMEM((DIL_MAX, WIN_BLOCK, HEAD_DIM), F32),
            pltpu.VMEM((DIL_MAX, WIN_BLOCK, HEAD_DIM), F32),
            pltpu.VMEM((DIL_MAX, WIN_BLOCK, HEAD_DIM), F32),
            pltpu.VMEM((2 * N_GROUPS, WIN_BLOCK, 2 * WIN_BLOCK), F32),
            pltpu.VMEM((DIL_MAX, WIN_BLOCK, HEAD_DIM), BF16),
            pltpu.SemaphoreType.DMA,
        ],
        compiler_params=_cparams(("arbitrary", "arbitrary", "arbitrary")),
        name=name,
    )(pa, pa, pa, pa, pa, cs, sn)
    return out.reshape(b, nt * ATT_TILE, ATT_KV_WIDTH)


def _merge_out_kernel(oh_ref, oa_ref, gate_ref, x_ref, wh_ref, wa_ref, wo_ref, gc_ref, h1_ref, n2_ref):
    ga = gate_ref[:, :D_MODEL].astype(F32)
    gb = gate_ref[:, D_MODEL:].astype(F32)
    yh = jnp.dot(oh_ref[...], wh_ref[...], preferred_element_type=F32)
    ya = jnp.dot(oa_ref[...], wa_ref[...], preferred_element_type=F32)
    merged = jax.nn.sigmoid(ga) * yh + jax.nn.sigmoid(gb) * ya
    h1 = x_ref[...] + jnp.dot(merged.astype(BF16), wo_ref[...], preferred_element_type=F32)
    h1_ref[...] = h1
    n2_ref[...] = _rms(h1, gc_ref[...]).astype(BF16)


def _merge_out(oh, oa, gates, x2d, wh, wa, wo, gc, *, tm, name="merge_out"):
    n, d = x2d.shape

    def const(shape):
        return pl.BlockSpec(shape, lambda i: (0, 0), pipeline_mode=pl.Buffered(1))

    return pl.pallas_call(
        _merge_out_kernel,
        out_shape=(jax.ShapeDtypeStruct((n, d), F32), jax.ShapeDtypeStruct((n, d), BF16)),
        grid=(n // tm,),
        in_specs=[
            pl.BlockSpec((tm, HG_WIDTH), lambda i: (i, 0)),
            pl.BlockSpec((tm, ATT_KV_WIDTH), lambda i: (i, 0)),
            pl.BlockSpec((tm, 2 * d), lambda i: (i, 0)),
            pl.BlockSpec((tm, d), lambda i: (i, 0)),
            const((HG_WIDTH, d)), const((ATT_KV_WIDTH, d)), const((d, d)), const((1, d)),
        ],
        out_specs=(pl.BlockSpec((tm, d), lambda i: (i, 0)), pl.BlockSpec((tm, d), lambda i: (i, 0))),
        compiler_params=_cparams(("parallel",)),
        name=name,
    )(oh, oa, gates, x2d, wh, wa, wo, gc.reshape(1, d))


def _cross_kernel(n2_ref, h1_ref, kv_ref, wq_ref, wo_ref, gm_ref, wrh_ref, wrl_ref, br_ref,
                  h2_ref, n3_ref, idx_ref, tw_ref, rank_ref, cnt_ref, carry_ref, *, tm):
    @pl.when((pl.program_id(0) == 0) & (pl.program_id(1) == 0))
    def _():
        carry_ref[...] = jnp.zeros_like(carry_ref)

    nt_dims = (((1,), (1,)), ((), ()))
    scale = HEAD_DIM ** -0.5
    q = (jnp.dot(n2_ref[...], wq_ref[...], preferred_element_type=F32) * scale).astype(BF16)
    outs = []
    for hh in range(CROSS_HEADS):
        sl = slice(hh * HEAD_DIM, (hh + 1) * HEAD_DIM)
        kh = kv_ref[:, sl]
        vh = kv_ref[:, CROSS_WIDTH + hh * HEAD_DIM:CROSS_WIDTH + (hh + 1) * HEAD_DIM]
        s = lax.dot_general(q[:, sl], kh, nt_dims, preferred_element_type=F32)
        p = jnp.exp(s - jnp.max(s, axis=-1, keepdims=True))
        l = jnp.sum(p, axis=-1, keepdims=True)
        outs.append(jnp.dot(p.astype(BF16), vh, preferred_element_type=F32) / l)
    o = jnp.concatenate(outs, axis=-1).astype(BF16)
    h2 = h1_ref[...] + jnp.dot(o, wo_ref[...], preferred_element_type=F32)
    h2_ref[...] = h2
    n3 = _rms(h2, gm_ref[...])
    n3_ref[...] = n3

    n3h = n3.astype(BF16)
    n3l = (n3 - n3h.astype(F32)).astype(BF16)
    wrh = wrh_ref[...]
    logits = (jnp.dot(n3h, wrh, preferred_element_type=F32)
              + jnp.dot(n3l, wrh, preferred_element_type=F32)
              + jnp.dot(n3h, wrl_ref[...], preferred_element_type=F32)
              + br_ref[...])
    lane = lax.broadcasted_iota(I32, (tm, N_EXPERTS), 1).astype(F32)
    vals, idxs, hots = [], [], []
    cur = logits
    for _ in range(TOP_K):
        mx = jnp.max(cur, axis=-1, keepdims=True)
        ix = jnp.min(jnp.where(cur == mx, lane, float(N_EXPERTS)), axis=-1, keepdims=True)
        hot = lane == ix
        vals.append(mx)
        idxs.append(ix)
        hots.append(hot)
        cur = jnp.where(hot, -jnp.inf, cur)
    es = [jnp.exp(v - vals[0]) for v in vals]
    den = es[0] + es[1] + es[2] + es[3]
    col = lax.broadcasted_iota(I32, (tm, TOP_K), 1)

    def pack(cols):
        out = jnp.broadcast_to(cols[TOP_K - 1], (tm, TOP_K))
        for k in range(TOP_K - 2, -1, -1):
            out = jnp.where(col == k, cols[k], out)
        return out

    idx_ref[...] = pack(idxs).astype(I32)
    tw_ref[...] = pack([e / den for e in es])

    cmat = (hots[0] | hots[1] | hots[2] | hots[3]).astype(F32)
    rr = lax.broadcasted_iota(I32, (tm, tm), 0)
    cc = lax.broadcasted_iota(I32, (tm, tm), 1)
    tri = (cc < rr).astype(BF16)
    before = jnp.dot(tri, cmat.astype(BF16), preferred_element_type=F32) + carry_ref[...]
    ranks = [jnp.sum(jnp.where(hot, before, 0.0), axis=-1, keepdims=True) for hot in hots]
    rank_ref[...] = pack(ranks).astype(I32)
    carry = carry_ref[...] + jnp.sum(cmat, axis=0, keepdims=True)
    carry_ref[...] = carry
    cnt_ref[...] = carry


def _cross(n2, h1, kv, wq, wo, gm, wrh, wrl, br, *, batch, tm, name="cross"):
    n, d = h1.shape
    per_b = n // batch // tm
    mem_len = kv.shape[0] // batch

    def const(shape):
        return pl.BlockSpec(shape, lambda bi, i: (0, 0), pipeline_mode=pl.Buffered(1))

    def row(bi, i):
        return (bi * per_b + i, 0)

    return pl.pallas_call(
        functools.partial(_cross_kernel, tm=tm),
        out_shape=(
            jax.ShapeDtypeStruct((n, d), F32),
            jax.ShapeDtypeStruct((n, d), F32),
            jax.ShapeDtypeStruct((n, TOP_K), I32),
            jax.ShapeDtypeStruct((n, TOP_K), F32),
            jax.ShapeDtypeStruct((n, TOP_K), I32),
            jax.ShapeDtypeStruct((1, N_EXPERTS), F32),
        ),
        grid=(batch, per_b),
        in_specs=[
            pl.BlockSpec((tm, d), row),
            pl.BlockSpec((tm, d), row),
            pl.BlockSpec((mem_len, 2 * CROSS_WIDTH), lambda bi, i: (bi, 0)),
            const((d, CROSS_WIDTH)), const((CROSS_WIDTH, d)), const((1, d)),
            const((d, N_EXPERTS)), const((d, N_EXPERTS)), const((1, N_EXPERTS)),
        ],
        out_specs=(
            pl.BlockSpec((tm, d), row),
            pl.BlockSpec((tm, d), row),
            pl.BlockSpec((tm, TOP_K), row),
            pl.BlockSpec((tm, TOP_K), row),
            pl.BlockSpec((tm, TOP_K), row),
            pl.BlockSpec((1, N_EXPERTS), lambda bi, i: (0, 0)),
        ),
        scratch_shapes=[pltpu.VMEM((1, N_EXPERTS), F32)],
        compiler_params=_cparams(("arbitrary", "arbitrary")),
        name=name,
    )(n2, h1, kv, wq, wo, gm.reshape(1, d), wrh, wrl, br.reshape(1, N_EXPERTS))


def _slab(ref, row):
    return ref.at[pl.ds(pl.multiple_of(row * MOE_SLAB, MOE_SLAB), MOE_SLAB)]


def _row_copies(make_copy, count):
    ngroup = count // DMA_WAIT_ROWS

    def start_group(c):
        def one(j, carry):
            make_copy(c * DMA_WAIT_ROWS + j).start()
            return carry

        lax.fori_loop(0, DMA_WAIT_ROWS, one, 0)

    def wait_group(c):
        def one(j, carry):
            make_copy(c * DMA_WAIT_ROWS + j).wait()
            return carry

        lax.fori_loop(0, DMA_WAIT_ROWS, one, 0)

    def group(c, carry):
        start_group(c)

        @pl.when(c > 0)
        def _():
            wait_group(c - 1)

        return carry

    lax.fori_loop(0, ngroup, group, 0)
    wait_group(ngroup - 1)


def _dispatch_kernel(slot_ref, n3_ref, xs_hbm, stage, sem, *, real_steps, tm):
    step = pl.program_id(0)

    @pl.when(step < real_steps)
    def _():
        x = n3_ref[...]
        for j in range(MOE_SLAB):
            stage[pl.ds(j, tm, stride=MOE_SLAB), :] = x[:, j * V7X_LANES:(j + 1) * V7X_LANES]

        def make_copy(a):
            return pltpu.make_async_copy(_slab(stage, a >> TOP_K_LOG2), _slab(xs_hbm, slot_ref[0, a]), sem)

        _row_copies(make_copy, tm * TOP_K)

    @pl.when(step >= real_steps)
    def _():
        stage[0:MOE_SLAB, :] = jnp.zeros((MOE_SLAB, V7X_LANES), F32)

        def make_copy(a):
            return pltpu.make_async_copy(_slab(stage, 0), _slab(xs_hbm, slot_ref[0, a]), sem)

        _row_copies(make_copy, tm * TOP_K)


def _dispatch(slots, n3, *, rows_total, tm, name="moe_dispatch"):
    n, d = n3.shape
    real_steps = n // tm
    return pl.pallas_call(
        functools.partial(_dispatch_kernel, real_steps=real_steps, tm=tm),
        out_shape=jax.ShapeDtypeStruct((rows_total * MOE_SLAB, V7X_LANES), F32),
        grid=(slots.shape[0],),
        in_specs=[
            pl.BlockSpec((None, 1, tm * TOP_K), lambda i: (i, 0, 0), memory_space=pltpu.SMEM),
            pl.BlockSpec((tm, d), lambda i: (jnp.minimum(i, real_steps - 1), 0)),
        ],
        out_specs=pl.BlockSpec(memory_space=pl.ANY),
        scratch_shapes=[pltpu.VMEM((tm * MOE_SLAB, V7X_LANES), F32), pltpu.SemaphoreType.DMA],
        compiler_params=_cparams(("arbitrary",)),
        name=name,
    )(slots, n3)


def _moe_kernel(te_ref, tv_ref, tstart_ref, trows_ref,
                xs_hbm, w1_ref, b1_ref, w2_ref, b2_ref, perm_ref, yb_hbm,
                stage, x16, acc, w1p, sems, *, nf):
    i = pl.program_id(0)
    f = pl.program_id(1)
    rows = trows_ref[i]
    start = tstart_ref[i]
    nchunk = rows // MOE_ROW_ALIGN
    chunk_slab_rows = MOE_ROW_ALIGN * MOE_SLAB
    half = 2 * V7X_LANES
    nhalf = 2 * MOE_TF // half

    def slot_rows(c):
        return pl.multiple_of((c & 1) * chunk_slab_rows, chunk_slab_rows)

    def hbm_rows(ref, c):
        r0 = pl.multiple_of((start + c * MOE_ROW_ALIGN) * MOE_SLAB, chunk_slab_rows)
        return ref.at[pl.ds(r0, chunk_slab_rows)]

    def in_copy(c):
        return pltpu.make_async_copy(hbm_rows(xs_hbm, c), stage.at[pl.ds(slot_rows(c), chunk_slab_rows)],
                                     sems.at[c & 1])

    def out_copy(c):
        return pltpu.make_async_copy(stage.at[pl.ds(slot_rows(c), chunk_slab_rows)], hbm_rows(yb_hbm, c),
                                     sems.at[c & 1])

    @pl.when((rows > 0) & (f == 0))
    def _():
        in_copy(0).start()

        def land(c, carry):
            @pl.when(c + 1 < nchunk)
            def _():
                in_copy(c + 1).start()

            in_copy(c).wait()
            r0 = pl.multiple_of(c * MOE_ROW_ALIGN, MOE_ROW_ALIGN)
            s0 = slot_rows(c)
            for j in range(MOE_SLAB):
                xj = stage[pl.ds(s0 + j, MOE_ROW_ALIGN, stride=MOE_SLAB), :]
                x16[pl.ds(r0, MOE_ROW_ALIGN), j * V7X_LANES:(j + 1) * V7X_LANES] = xj.astype(BF16)
            acc[pl.ds(r0, MOE_ROW_ALIGN), :] = jnp.zeros((MOE_ROW_ALIGN, D_MODEL), F32)
            return carry

        lax.fori_loop(0, nchunk, land, 0)

    @pl.when(rows > 0)
    def _():
        perm = perm_ref[...]
        for c in range(nhalf):
            wb = w1_ref[:, c * half:(c + 1) * half].astype(BF16)
            w1p[:, c * half:(c + 1) * half] = jnp.dot(wb, perm, preferred_element_type=F32).astype(BF16)
        w2 = w2_ref[...].astype(BF16)
        b1 = b1_ref[...]

        def mlp(c0, nc):
            r0 = pl.multiple_of(c0 * MOE_ROW_ALIGN, MOE_ROW_ALIGN)
            nr = nc * MOE_ROW_ALIGN
            x = x16[pl.ds(r0, nr), :]
            h = jnp.dot(x, w1p[...], preferred_element_type=F32) + b1
            parts = []
            for c in range(nhalf):
                hg = jnp.minimum(h[:, c * half:c * half + V7X_LANES], SWIGLU_LIMIT)
                hl = jnp.clip(h[:, c * half + V7X_LANES:(c + 1) * half], -SWIGLU_LIMIT, SWIGLU_LIMIT)
                parts.append(hg * jax.nn.sigmoid(SWIGLU_ALPHA * hg) * (hl + 1.0))
            a = jnp.concatenate(parts, axis=-1).astype(BF16)
            acc[pl.ds(r0, nr), :] += jnp.dot(a, w2, preferred_element_type=F32)

        nquad = nchunk // MOE_BLOCK_CHUNKS

        def quad(c, carry):
            mlp(c * MOE_BLOCK_CHUNKS, MOE_BLOCK_CHUNKS)
            return carry

        lax.fori_loop(0, nquad, quad, 0)
        rem = nchunk - nquad * MOE_BLOCK_CHUNKS

        @pl.when(rem >= 2)
        def _():
            mlp(nquad * MOE_BLOCK_CHUNKS, 2)

        @pl.when((rem & 1) == 1)
        def _():
            mlp(nchunk - 1, 1)

    @pl.when((rows > 0) & (f == nf - 1))
    def _():
        def emit(c, carry):
            @pl.when(c >= 2)
            def _():
                out_copy(c - 2).wait()

            r0 = pl.multiple_of(c * MOE_ROW_ALIGN, MOE_ROW_ALIGN)
            s0 = slot_rows(c)
            y = acc[pl.ds(r0, MOE_ROW_ALIGN), :] + b2_ref[...]
            for j in range(MOE_SLAB):
                stage[pl.ds(s0 + j, MOE_ROW_ALIGN, stride=MOE_SLAB), :] = y[:, j * V7X_LANES:(j + 1) * V7X_LANES]
            out_copy(c).start()
            return carry

        lax.fori_loop(0, nchunk, emit, 0)

        @pl.when(nchunk >= 2)
        def _():
            out_copy(nchunk - 2).wait()

        out_copy(nchunk - 1).wait()


def _moe(tile_e, tile_v, tile_start, tile_rows, xs, w1, b1p, w2, b2, perm, *, name="moe_mlp"):
    nt = tile_e.shape[0]
    nf = D_FF // MOE_TF

    def w1_map(i, f, te, tv, ts, tr):
        return (te[i], 0, jnp.where(tv[i] > 0, f, nf - 1))

    def w2_map(i, f, te, tv, ts, tr):
        return (te[i], jnp.where(tv[i] > 0, f, nf - 1), 0)

    def b2_map(i, f, te, tv, ts, tr):
        return (te[i], 0, 0)

    grid_spec = pltpu.PrefetchScalarGridSpec(
        num_scalar_prefetch=4,
        grid=(nt, nf),
        in_specs=[
            pl.BlockSpec(memory_space=pl.ANY),
            pl.BlockSpec((None, D_MODEL, 2 * MOE_TF), w1_map),
            pl.BlockSpec((None, 1, 2 * MOE_TF), w1_map),
            pl.BlockSpec((None, MOE_TF, D_MODEL), w2_map),
            pl.BlockSpec((None, 1, D_MODEL), b2_map),
            pl.BlockSpec((2 * V7X_LANES, 2 * V7X_LANES), lambda i, f, te, tv, ts, tr: (0, 0)),
        ],
        out_specs=pl.BlockSpec(memory_space=pl.ANY),
        scratch_shapes=[
            pltpu.VMEM((2 * MOE_ROW_ALIGN * MOE_SLAB, V7X_LANES), F32),
            pltpu.VMEM((MOE_TMAX, D_MODEL), BF16),
            pltpu.VMEM((MOE_TMAX, D_MODEL), F32),
            pltpu.VMEM((D_MODEL, 2 * MOE_TF), BF16),
            pltpu.SemaphoreType.DMA((2,)),
        ],
    )
    return pl.pallas_call(
        functools.partial(_moe_kernel, nf=nf),
        out_shape=jax.ShapeDtypeStruct(xs.shape, F32),
        grid_spec=grid_spec,
        input_output_aliases={4: 0},
        compiler_params=_cparams(("arbitrary", "arbitrary")),
        name=name,
    )(tile_e, tile_v, tile_start, tile_rows, xs, w1, b1p.reshape(N_EXPERTS, 1, 2 * D_FF), w2,
      b2.reshape(N_EXPERTS, 1, D_MODEL), perm)


def _combine_kernel(slot_ref, h2_ref, tw_ref, gf_ref, yb_hbm, o_ref, buf, sem, *, tm):
    def make_copy(a):
        t = a >> TOP_K_LOG2
        k = a & (TOP_K - 1)
        return pltpu.make_async_copy(_slab(yb_hbm, slot_ref[0, a]), _slab(buf, k * tm + t), sem)

    _row_copies(make_copy, tm * TOP_K)

    tw = tw_ref[...]
    cols = []
    for j in range(MOE_SLAB):
        cj = None
        for k in range(TOP_K):
            part = tw[:, k:k + 1] * buf[pl.ds(k * tm * MOE_SLAB + j, tm, stride=MOE_SLAB), :]
            cj = part if cj is None else cj + part
        cols.append(cj)
    h3 = h2_ref[...] + jnp.concatenate(cols, axis=-1)
    o_ref[...] = _rms(h3, gf_ref[...])


def _combine(slots, h2, tw, gf, yb, *, tm, name="moe_combine"):
    n, d = h2.shape
    return pl.pallas_call(
        functools.partial(_combine_kernel, tm=tm),
        out_shape=jax.ShapeDtypeStruct((n, d), F32),
        grid=(n // tm,),
        in_specs=[
            pl.BlockSpec((None, 1, tm * TOP_K), lambda i: (i, 0, 0), memory_space=pltpu.SMEM),
            pl.BlockSpec((tm, d), lambda i: (i, 0)),
            pl.BlockSpec((tm, TOP_K), lambda i: (i, 0)),
            pl.BlockSpec((1, d), lambda i: (0, 0)),
            pl.BlockSpec(memory_space=pl.ANY),
        ],
        out_specs=pl.BlockSpec((tm, d), lambda i: (i, 0)),
        scratch_shapes=[
            pltpu.VMEM((tm * TOP_K * MOE_SLAB, V7X_LANES), F32),
            pltpu.SemaphoreType.DMA,
        ],
        compiler_params=_cparams(("arbitrary",)),
        name=name,
    )(slots, h2, tw, gf.reshape(1, d), yb)


def _rope_tables(positions):
    b, s = positions.shape
    half = HEAD_DIM // 2
    inv_freq = 1.0 / (ROPE_THETA ** (jnp.arange(half, dtype=F32) / half))
    ang = positions.astype(F32)[..., None] * inv_freq
    cos = jnp.cos(ang)
    sin = jnp.sin(ang)
    cs = jnp.concatenate([cos, cos], axis=-1)
    sn = jnp.concatenate([-sin, sin], axis=-1)

    def deint(a):
        a = a.reshape(b, s // ATT_TILE, WIN_BLOCK, DIL_MAX, HEAD_DIM)
        return a.transpose(0, 1, 3, 2, 4)

    return deint(cs), deint(sn)


def _routing(top_e, rank, counts, n_tok):
    cnt = counts.reshape(N_EXPERTS).astype(I32)
    padded = (cnt + MOE_ROW_ALIGN - 1) // MOE_ROW_ALIGN * MOE_ROW_ALIGN
    gend = jnp.cumsum(padded)
    gstart = gend - padded
    slot = (gstart[top_e] + rank).reshape(-1)

    n_assign = n_tok * TOP_K
    n_unused = N_EXPERTS * MOE_ROW_ALIGN
    rows_total = n_assign + n_unused
    padcnt = padded - cnt
    pend = jnp.cumsum(padcnt)
    pstart = pend - padcnt
    def first_above(ends, q):
        return jnp.minimum(jnp.sum((ends[None, :] <= q[:, None]).astype(I32), axis=1), N_EXPERTS - 1)

    p = jnp.arange(n_unused, dtype=I32)
    ep = first_above(pend, p)
    unused = jnp.where(p < pend[-1], gstart[ep] + cnt[ep] + (p - pstart[ep]), gend[-1] + (p - pend[-1]))

    nt_max = rows_total // MOE_TMAX + N_EXPERTS
    nt_e = (padded + MOE_TMAX - 1) // MOE_TMAX
    tend = jnp.cumsum(nt_e)
    tstart = tend - nt_e
    ti = jnp.arange(nt_max, dtype=I32)
    valid = ti < tend[-1]
    tic = jnp.minimum(ti, tend[-1] - 1)
    te = first_above(tend, tic)
    local = tic - tstart[te]
    row0 = gstart[te] + local * MOE_TMAX
    rows = jnp.where(valid, jnp.clip(padded[te] - local * MOE_TMAX, 0, MOE_TMAX), 0)
    return (slot.astype(I32), unused.astype(I32), rows_total, te, valid.astype(I32),
            row0.astype(I32), rows.astype(I32))


def kernel(x, mem, positions, norm_mix_g, w_in, lb_raw, hgrn_norm_g, w_br_hgrn, w_br_attn, w_out,
           norm_cross_g, norm_mem_g, w_cq, w_ckv, w_co, norm_moe_g, w_router, b_router,
           w_mlp1, b_mlp1, w_mlp2, b_mlp2, norm_final_g):
    bsz, seq, d = x.shape
    assert w_in.shape[0] == 1 and d == D_MODEL and seq % ATT_TILE == 0
    n_tok = bsz * seq
    lower_bounds = jnp.cumsum(jax.nn.softmax(lb_raw.astype(F32), axis=0), axis=0)
    cs, sn = _rope_tables(positions)

    wl = w_in[0]
    c_h = 4 * HG_WIDTH
    c_a = c_h + ATT_Q_WIDTH + 2 * ATT_KV_WIDTH
    x2d = x.reshape(n_tok, d)
    proj_h = _in_proj(x2d, norm_mix_g[0], wl[:, :c_h].astype(BF16), tm=1024, tn=1024, name="in_proj_hgrn")
    proj_g = _in_proj(x2d, norm_mix_g[0], wl[:, c_a:].astype(BF16), tm=1024, tn=1024, name="in_proj_gate")
    pa = _in_proj_strided(x, norm_mix_g[0], wl[:, c_h:c_a].astype(BF16), c=8, tn=1024, name="in_proj_attn")

    o_h = _hgrn(proj_h.reshape(bsz, seq, c_h), lower_bounds[0], hgrn_norm_g[0], ts=512)
    o_a = _dilated_attn(pa, cs, sn)

    h1, n2 = _merge_out(o_h.reshape(n_tok, HG_WIDTH), o_a.reshape(n_tok, ATT_KV_WIDTH), proj_g, x2d,
                        w_br_hgrn[0].astype(BF16), w_br_attn[0].astype(BF16), w_out[0].astype(BF16),
                        norm_cross_g[0], tm=256)

    mem2d = mem.reshape(-1, d)
    kv = _in_proj(mem2d, norm_mem_g, w_ckv[0].astype(BF16), tm=mem2d.shape[0], tn=2 * CROSS_WIDTH, name="mem_kv")
    wrh = w_router[0].astype(BF16)
    wrl = (w_router[0] - wrh.astype(F32)).astype(BF16)
    h2, n3, top_e, top_w, rank, counts = _cross(
        n2, h1, kv, w_cq[0].astype(BF16), w_co[0].astype(BF16), norm_moe_g[0], wrh, wrl, b_router[0],
        batch=bsz, tm=512)

    slot, unused, rows_total, te, tv, tstart, trows = _routing(top_e, rank, counts, n_tok)
    tm_d = DISPATCH_TOKENS
    slots_all = jnp.concatenate([slot, unused]).reshape(-1, 1, tm_d * TOP_K)
    xs = _dispatch(slots_all, n3, rows_total=rows_total, tm=tm_d)

    ii = jnp.arange(2 * V7X_LANES)
    src = jnp.where(ii < V7X_LANES, 2 * ii, 2 * (ii - V7X_LANES) + 1)
    perm = (ii[:, None] == src[None, :]).astype(BF16)
    b1p = b_mlp1[0].reshape(N_EXPERTS, -1, V7X_LANES, 2).transpose(0, 1, 3, 2).reshape(N_EXPERTS, 2 * D_FF)
    yb = _moe(te, tv, tstart, trows, xs, w_mlp1[0], b1p, w_mlp2[0], b_mlp2[0], perm)

    tm_c = 256
    out = _combine(slot.reshape(-1, 1, tm_c * TOP_K), h2, top_w, norm_final_g, yb, tm=tm_c)
    return out.reshape(bsz, seq, d)
```

```python
import functools

import jax
import jax.numpy as jnp
from jax import lax
from jax.experimental import pallas as pl
from jax.experimental.pallas import tpu as pltpu

F32 = jnp.float32
BF16 = jnp.bfloat16
I32 = jnp.int32
U32 = jnp.uint32

D_MODEL = 2048
HEAD_DIM = 128
HG_HEADS = 8
HG_WIDTH = HG_HEADS * HEAD_DIM
N_KV_HEADS = 8
N_GROUPS = 3
ATT_Q_WIDTH = N_GROUPS * N_KV_HEADS * HEAD_DIM
ATT_KV_WIDTH = N_KV_HEADS * HEAD_DIM
WIN_BLOCK = 128
ROPE_THETA = 10000.0
CROSS_HEADS = 4
CROSS_WIDTH = CROSS_HEADS * HEAD_DIM
N_EXPERTS = 32
TOP_K = 4
TOP_K_LOG2 = 2
D_FF = D_MODEL
SWIGLU_ALPHA = 1.702
SWIGLU_LIMIT = 7.0
NORM_EPS = 1e-6

V7X_LANES = 128
V7X_VMEM_LIMIT_BYTES = 56 * 1024 * 1024

DIL_MAX = 16
ATT_TILE = DIL_MAX * WIN_BLOCK
ATT_UNROLL = 8
HG_BLOCK = 16
HG_GROUP = 16
NEG_BIG = -1e30

MOE_TMAX = 2048
MOE_ROW_ALIGN = 128
MOE_BLOCK_CHUNKS = 4
MOE_TF = 256
MOE_SLAB = D_MODEL // 2 // V7X_LANES
DISPATCH_TOKENS = 256
DMA_WAIT_ROWS = 128
DMA_ISSUE_UNROLL = 8


def _rms(x, g):
    ms = jnp.mean(x * x, axis=-1, keepdims=True)
    return x * lax.rsqrt(ms + NORM_EPS) * g


def _cparams(sem, vmem=V7X_VMEM_LIMIT_BYTES):
    return pltpu.CompilerParams(dimension_semantics=sem, vmem_limit_bytes=vmem)


def _in_proj_kernel(x_ref, g_ref, w_ref, o_ref, xn_ref):
    @pl.when(pl.program_id(1) == 0)
    def _():
        xn_ref[...] = _rms(x_ref[...], g_ref[...]).astype(BF16)

    o_ref[...] = jnp.dot(xn_ref[...], w_ref[...], preferred_element_type=F32).astype(o_ref.dtype)


def _in_proj(x2d, g, w, *, tm, tn, name):
    n, d = x2d.shape
    wc = w.shape[1]
    return pl.pallas_call(
        _in_proj_kernel,
        out_shape=jax.ShapeDtypeStruct((n, wc), BF16),
        grid=(n // tm, wc // tn),
        in_specs=[
            pl.BlockSpec((tm, d), lambda i, j: (i, 0)),
            pl.BlockSpec((1, d), lambda i, j: (0, 0)),
            pl.BlockSpec((d, tn), lambda i, j: (0, j)),
        ],
        out_specs=pl.BlockSpec((tm, tn), lambda i, j: (i, j)),
        scratch_shapes=[pltpu.VMEM((tm, d), BF16)],
        compiler_params=_cparams(("parallel", "arbitrary")),
        name=name,
    )(x2d, g.reshape(1, d), w)


def _in_proj_strided_kernel(x_ref, g_ref, w_ref, o_ref, xn_ref, *, c):
    @pl.when(pl.program_id(3) == 0)
    def _():
        for ci in range(c):
            x = x_ref[:, ci * D_MODEL:(ci + 1) * D_MODEL]
            xn_ref[ci * WIN_BLOCK:(ci + 1) * WIN_BLOCK, :] = _rms(x, g_ref[...]).astype(BF16)

    res = jnp.dot(xn_ref[...], w_ref[...], preferred_element_type=F32)
    for ci in range(c):
        o_ref[ci] = res[ci * WIN_BLOCK:(ci + 1) * WIN_BLOCK, :].astype(o_ref.dtype)


def _in_proj_strided(x, g, w, *, c, tn, name):
    b, s, d = x.shape
    nt = s // ATT_TILE
    wc = w.shape[1]
    xv = x.reshape(b, nt, WIN_BLOCK, DIL_MAX * d)
    return pl.pallas_call(
        functools.partial(_in_proj_strided_kernel, c=c),
        out_shape=jax.ShapeDtypeStruct((b, nt, DIL_MAX, WIN_BLOCK, wc), BF16),
        grid=(b, nt, DIL_MAX // c, wc // tn),
        in_specs=[
            pl.BlockSpec((None, None, WIN_BLOCK, c * d), lambda bi, ti, ri, j: (bi, ti, 0, ri)),
            pl.BlockSpec((1, d), lambda bi, ti, ri, j: (0, 0)),
            pl.BlockSpec((d, tn), lambda bi, ti, ri, j: (0, j)),
        ],
        out_specs=pl.BlockSpec((None, None, c, WIN_BLOCK, tn), lambda bi, ti, ri, j: (bi, ti, ri, 0, j)),
        scratch_shapes=[pltpu.VMEM((c * WIN_BLOCK, d), BF16)],
        compiler_params=_cparams(("parallel", "parallel", "parallel", "arbitrary")),
        name=name,
    )(xv, g.reshape(1, d), w)


def _hgrn_kernel(q_ref, f_ref, i_ref, g_ref, lb_ref, gn_ref, o_ref, st_ref, kin_s, b_s, v_s, *, ts):
    @pl.when(pl.program_id(2) == 0)
    def _():
        st_ref[...] = jnp.zeros_like(st_ref)

    lb = lb_ref[...]
    oml = 1.0 - lb
    gn = gn_ref[...]
    half = HG_BLOCK // 2
    row = lax.broadcasted_iota(I32, (HG_BLOCK, HEAD_DIM), 0)
    row8 = lax.broadcasted_iota(I32, (half, HEAD_DIM), 0)
    nt_dims = (((1,), (1,)), ((), ()))
    tn_dims = (((0,), (0,)), ((), ()))

    def front(g, t0):
        sl = pl.ds(t0 + g * HG_BLOCK, HG_BLOCK)
        q = q_ref[sl, :].astype(F32)
        hf = f_ref[sl, :].astype(F32)
        v = i_ref[sl, :].astype(F32)
        kin = oml * jax.nn.sigmoid(-hf)
        b = jnp.log2(lb + oml * jax.nn.sigmoid(hf))
        for sh in (1, 2, 4, 8):
            b = b + jnp.where(row >= sh, pltpu.roll(b, sh, 0), 0.0)
        kin_s[g] = kin
        b_s[g] = b
        v_s[g] = v
        q_lo, q_hi = q[:half], q[half:]
        b_lo, b_hi = b[:half], b[half:]
        o_lo = jnp.zeros((half, HEAD_DIM), F32)
        o_hi = jnp.zeros((half, HEAD_DIM), F32)
        for s in range(HG_BLOCK):
            ks = kin_s[g, s:s + 1, :]
            bs = b_s[g, s:s + 1, :]
            vs = v_s[g, s:s + 1, :]
            if s < half:
                w = q_lo * ks * jnp.exp2(b_lo - bs)
                if s > 0:
                    w = jnp.where(row8 >= s, w, 0.0)
                o_lo = o_lo + jnp.sum(w, axis=-1, keepdims=True) * vs
                w = q_hi * ks * jnp.exp2(b_hi - bs)
            else:
                w = q_hi * ks * jnp.exp2(b_hi - bs)
                if s > half:
                    w = jnp.where(row8 >= s - half, w, 0.0)
            o_hi = o_hi + jnp.sum(w, axis=-1, keepdims=True) * vs
        bl = b_s[g, HG_BLOCK - 1:HG_BLOCK, :]
        qd = (q * jnp.exp2(b)).astype(BF16)
        kd = (kin * jnp.exp2(bl - b)).astype(BF16)
        upd = lax.dot_general(v.astype(BF16), kd, tn_dims, preferred_element_type=F32)
        return jnp.concatenate([o_lo, o_hi], axis=0), qd, upd, jnp.exp2(bl)

    def body(i, carry):
        t0 = pl.multiple_of(i * (HG_GROUP * HG_BLOCK), HG_GROUP * HG_BLOCK)
        fronts = [front(g, t0) for g in range(HG_GROUP)]
        st = st_ref[...]
        for g, (o_diag, qd, upd, dec) in enumerate(fronts):
            o = o_diag + lax.dot_general(qd, st.astype(BF16), nt_dims, preferred_element_type=F32)
            st = st * dec + upd
            sl = pl.ds(t0 + g * HG_BLOCK, HG_BLOCK)
            hg = g_ref[sl, :].astype(F32)
            o_ref[sl, :] = (_rms(o, gn) * (hg * jax.nn.sigmoid(hg))).astype(o_ref.dtype)
        st_ref[...] = st
        return carry

    lax.fori_loop(0, ts // (HG_GROUP * HG_BLOCK), body, 0)


def _hgrn(proj_h, lb, gn, *, ts, name="hgrn"):
    b, s, _ = proj_h.shape
    h = HG_HEADS

    def spec(off):
        return pl.BlockSpec((None, ts, HEAD_DIM), lambda bi, hi, si: (bi, si, off + hi))

    vec = pl.BlockSpec((1, HEAD_DIM), lambda bi, hi, si: (0, hi))
    return pl.pallas_call(
        functools.partial(_hgrn_kernel, ts=ts),
        out_shape=jax.ShapeDtypeStruct((b, s, HG_WIDTH), BF16),
        grid=(b, h, s // ts),
        in_specs=[spec(0), spec(h), spec(2 * h), spec(3 * h), vec, vec],
        out_specs=pl.BlockSpec((None, ts, HEAD_DIM), lambda bi, hi, si: (bi, si, hi)),
        scratch_shapes=[
            pltpu.VMEM((HEAD_DIM, HEAD_DIM), F32),
            pltpu.VMEM((HG_GROUP, HG_BLOCK, HEAD_DIM), F32),
            pltpu.VMEM((HG_GROUP, HG_BLOCK, HEAD_DIM), F32),
            pltpu.VMEM((HG_GROUP, HG_BLOCK, HEAD_DIM), F32),
        ],
        compiler_params=_cparams(("parallel", "parallel", "arbitrary")),
        name=name,
    )(proj_h, proj_h, proj_h, proj_h, lb.reshape(1, HG_WIDTH), gn.reshape(1, HG_WIDTH))


def _attn_bias(kind):
    rq = lax.broadcasted_iota(I32, (WIN_BLOCK, 2 * WIN_BLOCK), 0)
    ck = lax.broadcasted_iota(I32, (WIN_BLOCK, 2 * WIN_BLOCK), 1)
    if kind == 2:
        dist = rq + WIN_BLOCK - ck
        first = ck < WIN_BLOCK
    elif kind == 1:
        dist = 4 * ((rq & 31) - (ck & 63) + 32) + ((rq >> 5) - (ck >> 6))
        first = (ck & 63) < 32
    else:
        dist = 16 * ((rq & 7) - (ck & 15) + 8) + ((rq >> 3) - (ck >> 4))
        first = (ck & 15) < 8
    valid = (dist >= 0) & (dist <= WIN_BLOCK)
    return (jnp.where(valid, 0.0, NEG_BIG).astype(F32),
            jnp.where(valid & jnp.logical_not(first), 0.0, NEG_BIG).astype(F32))


def _attn_kernel(q0_ref, q1_ref, q2_ref, k_ref, v_ref, cs_ref, sn_ref, o_hbm,
                 qr, kext, vext, acc, mrun, lrun, bias, stage, sem):
    bi = pl.program_id(0)
    hi = pl.program_id(1)
    ti = pl.program_id(2)
    wb = WIN_BLOCK
    scale = HEAD_DIM ** -0.5

    @pl.when(ti == 0)
    def _():
        kext[:, 0:wb, :] = jnp.zeros((DIL_MAX, wb, HEAD_DIM), F32)
        vext[:, 0:wb, :] = jnp.zeros((DIL_MAX, wb, HEAD_DIM), F32)

    @pl.when(ti > 0)
    def _():
        kext[:, 0:wb, :] = kext[:, wb:2 * wb, :]
        vext[:, 0:wb, :] = vext[:, wb:2 * wb, :]

    for kind in range(N_GROUPS):
        full, nofirst = _attn_bias(kind)
        bias[2 * kind] = full
        bias[2 * kind + 1] = nofirst

    def rope_body(r, carry):
        cs = cs_ref[r]
        sn = sn_ref[r]
        for g, qref in enumerate((q0_ref, q1_ref, q2_ref)):
            q = qref[r].astype(F32)
            qr[g, r] = (q * cs + pltpu.roll(q, HEAD_DIM // 2, 1) * sn) * scale
        k = k_ref[r].astype(F32)
        kext[r, wb:2 * wb, :] = k * cs + pltpu.roll(k, HEAD_DIM // 2, 1) * sn
        vext[r, wb:2 * wb, :] = v_ref[r].astype(F32)
        acc[r] = jnp.zeros((wb, HEAD_DIM), F32)
        mrun[r] = jnp.full((wb, HEAD_DIM), NEG_BIG, F32)
        lrun[r] = jnp.zeros((wb, HEAD_DIM), F32)
        return carry

    lax.fori_loop(0, DIL_MAX, rope_body, 0)

    nt_dims = (((1,), (1,)), ((), ()))

    def block(qb, kb, vb, bias_blk):
        s = lax.dot_general(qb.astype(BF16), kb.astype(BF16), nt_dims, preferred_element_type=F32)
        s = s + bias_blk
        m = jnp.max(s, axis=-1, keepdims=True)
        p = jnp.exp(s - m)
        l = jnp.sum(p, axis=-1, keepdims=True)
        n = jnp.dot(p.astype(BF16), vb.astype(BF16), preferred_element_type=F32)
        return n, jnp.broadcast_to(m, (wb, HEAD_DIM)), jnp.broadcast_to(l, (wb, HEAD_DIM))

    def merge(r, rows, n, m, l):
        m_old = mrun[r, rows, :]
        m_new = jnp.maximum(m_old, m)
        a = jnp.exp(m_old - m_new)
        bb = jnp.exp(m - m_new)
        acc[r, rows, :] = acc[r, rows, :] * a + n * bb
        lrun[r, rows, :] = lrun[r, rows, :] * a + l * bb
        mrun[r, rows, :] = m_new

    first_tile = jnp.where(ti == 0, 1, 0)


    def g2_body(i, carry):
        rs = [i * ATT_UNROLL + u for u in range(ATT_UNROLL)]
        res = [block(qr[2, r], kext[r], vext[r], bias[4 + first_tile]) for r in rs]
        for r, (n, m, l) in zip(rs, res):
            merge(r, pl.ds(0, wb), n, m, l)
        return carry

    lax.fori_loop(0, DIL_MAX // ATT_UNROLL, g2_body, 0)

    def g1_body(i, carry):
        res = []
        for u in range(ATT_UNROLL // 4):
            mb = i * (ATT_UNROLL // 4) + u
            q0 = pl.multiple_of(32 * mb, 32)
            k0 = pl.multiple_of(96 + 32 * mb, 32)
            use_first = jnp.where(mb == 0, first_tile, 0)
            for r4 in range(4):
                qb = jnp.concatenate([qr[1, r4 + 4 * j, pl.ds(q0, 32), :] for j in range(4)], axis=0)
                kb = jnp.concatenate([kext[r4 + 4 * j, pl.ds(k0, 64), :] for j in range(4)], axis=0)
                vb = jnp.concatenate([vext[r4 + 4 * j, pl.ds(k0, 64), :] for j in range(4)], axis=0)
                res.append((r4, q0, block(qb, kb, vb, bias[2 + use_first])))
        for r4, q0, (n, m, l) in res:
            for j in range(4):
                sl = slice(32 * j, 32 * (j + 1))
                merge(r4 + 4 * j, pl.ds(q0, 32), n[sl], m[sl], l[sl])
        return carry

    lax.fori_loop(0, DIL_MAX // ATT_UNROLL, g1_body, 0)

    def g0_body(i, carry):
        res = []
        for u in range(ATT_UNROLL):
            mb = i * ATT_UNROLL + u
            q0 = pl.multiple_of(8 * mb, 8)
            k0 = pl.multiple_of(120 + 8 * mb, 8)
            qb = jnp.concatenate([qr[0, r, pl.ds(q0, 8), :] for r in range(DIL_MAX)], axis=0)
            kb = jnp.concatenate([kext[r, pl.ds(k0, 16), :] for r in range(DIL_MAX)], axis=0)
            vb = jnp.concatenate([vext[r, pl.ds(k0, 16), :] for r in range(DIL_MAX)], axis=0)
            use_first = jnp.where(mb == 0, first_tile, 0)
            res.append((q0, block(qb, kb, vb, bias[use_first])))
        for q0, (n, m, l) in res:
            for r in range(DIL_MAX):
                sl = slice(8 * r, 8 * (r + 1))
                merge(r, pl.ds(q0, 8), n[sl], m[sl], l[sl])
        return carry

    lax.fori_loop(0, DIL_MAX // ATT_UNROLL, g0_body, 0)

    def out_copy(r):
        col = pl.multiple_of((r * N_KV_HEADS + hi) * HEAD_DIM, HEAD_DIM)
        return pltpu.make_async_copy(stage.at[r], o_hbm.at[bi, ti, :, pl.ds(col, HEAD_DIM)], sem)

    for r in range(DIL_MAX):
        stage[r] = (acc[r] / lrun[r]).astype(BF16)
        out_copy(r).start()
    for r in range(DIL_MAX):
        out_copy(r).wait()


def _dilated_attn(pa, cs, sn, *, name="dilated_attn"):
    b, nt = pa.shape[0], pa.shape[1]
    h = N_KV_HEADS

    def spec(off):
        return pl.BlockSpec((None, None, DIL_MAX, WIN_BLOCK, HEAD_DIM),
                            lambda bi, hi, ti: (bi, ti, 0, 0, off + hi))

    tab = pl.BlockSpec((None, None, DIL_MAX, WIN_BLOCK, HEAD_DIM), lambda bi, hi, ti: (bi, ti, 0, 0, 0))
    out = pl.pallas_call(
        _attn_kernel,
        out_shape=jax.ShapeDtypeStruct((b, nt, WIN_BLOCK, DIL_MAX * ATT_KV_WIDTH), BF16),
        grid=(b, h, nt),
        in_specs=[spec(0), spec(h), spec(2 * h), spec(3 * h), spec(4 * h), tab, tab],
        out_specs=pl.BlockSpec(memory_space=pl.ANY),
        scratch_shapes=[
            pltpu.VMEM((N_GROUPS, DIL_MAX, WIN_BLOCK, HEAD_DIM), F32),
            pltpu.VMEM((DIL_MAX, 2 * WIN_BLOCK, HEAD_DIM), F32),
            pltpu.VMEM((DIL_MAX, 2 * WIN_BLOCK, HEAD_DIM), F32),
            pltpu.VMEM((DIL_MAX, WIN_BLOCK, HEAD_DIM), F32),
            pltpu.VMEM((DIL_MAX, WIN_BLOCK, HEAD_DIM), F32),
            pltpu.VMEM((DIL_MAX, WIN_BLOCK, HEAD_DIM), F32),
            pltpu.VMEM((2 * N_GROUPS, WIN_BLOCK, 2 * WIN_BLOCK), F32),
            pltpu.VMEM((DIL_MAX, WIN_BLOCK, HEAD_DIM), BF16),
            pltpu.SemaphoreType.DMA,
        ],
        compiler_params=_cparams(("arbitrary", "arbitrary", "arbitrary")),
        name=name,
    )(pa, pa, pa, pa, pa, cs, sn)
    return out.reshape(b, nt * ATT_TILE, ATT_KV_WIDTH)


def _merge_out_kernel(oh_ref, oa_ref, gate_ref, x_ref, wh_ref, wa_ref, wo_ref, gc_ref, h1_ref, n2_ref):
    ga = gate_ref[:, :D_MODEL].astype(F32)
    gb = gate_ref[:, D_MODEL:].astype(F32)
    yh = jnp.dot(oh_ref[...], wh_ref[...], preferred_element_type=F32)
    ya = jnp.dot(oa_ref[...], wa_ref[...], preferred_element_type=F32)
    merged = jax.nn.sigmoid(ga) * yh + jax.nn.sigmoid(gb) * ya
    h1 = x_ref[...] + jnp.dot(merged.astype(BF16), wo_ref[...], preferred_element_type=F32)
    h1_ref[...] = h1
    n2_ref[...] = _rms(h1, gc_ref[...]).astype(BF16)


def _merge_out(oh, oa, gates, x2d, wh, wa, wo, gc, *, tm, name="merge_out"):
    n, d = x2d.shape

    def const(shape):
        return pl.BlockSpec(shape, lambda i: (0, 0), pipeline_mode=pl.Buffered(1))

    return pl.pallas_call(
        _merge_out_kernel,
        out_shape=(jax.ShapeDtypeStruct((n, d), F32), jax.ShapeDtypeStruct((n, d), BF16)),
        grid=(n // tm,),
        in_specs=[
            pl.BlockSpec((tm, HG_WIDTH), lambda i: (i, 0)),
            pl.BlockSpec((tm, ATT_KV_WIDTH), lambda i: (i, 0)),
            pl.BlockSpec((tm, 2 * d), lambda i: (i, 0)),
            pl.BlockSpec((tm, d), lambda i: (i, 0)),
            const((HG_WIDTH, d)), const((ATT_KV_WIDTH, d)), const((d, d)), const((1, d)),
        ],
        out_specs=(pl.BlockSpec((tm, d), lambda i: (i, 0)), pl.BlockSpec((tm, d), lambda i: (i, 0))),
        compiler_params=_cparams(("parallel",)),
        name=name,
    )(oh, oa, gates, x2d, wh, wa, wo, gc.reshape(1, d))


def _cross_kernel(n2_ref, h1_ref, kv_ref, wq_ref, wo_ref, gm_ref, wrh_ref, wrl_ref, br_ref,
                  h2_ref, n3_ref, idx_ref, tw_ref, rank_ref, cnt_ref, carry_ref, *, tm):
    @pl.when((pl.program_id(0) == 0) & (pl.program_id(1) == 0))
    def _():
        carry_ref[...] = jnp.zeros_like(carry_ref)

    nt_dims = (((1,), (1,)), ((), ()))
    scale = HEAD_DIM ** -0.5
    q = (jnp.dot(n2_ref[...], wq_ref[...], preferred_element_type=F32) * scale).astype(BF16)
    outs = []
    for hh in range(CROSS_HEADS):
        sl = slice(hh * HEAD_DIM, (hh + 1) * HEAD_DIM)
        kh = kv_ref[:, sl]
        vh = kv_ref[:, CROSS_WIDTH + hh * HEAD_DIM:CROSS_WIDTH + (hh + 1) * HEAD_DIM]
        s = lax.dot_general(q[:, sl], kh, nt_dims, preferred_element_type=F32)
        p = jnp.exp(s - jnp.max(s, axis=-1, keepdims=True))
        l = jnp.sum(p, axis=-1, keepdims=True)
        outs.append(jnp.dot(p.astype(BF16), vh, preferred_element_type=F32) / l)
    o = jnp.concatenate(outs, axis=-1).astype(BF16)
    h2 = h1_ref[...] + jnp.dot(o, wo_ref[...], preferred_element_type=F32)
    h2_ref[...] = h2
    n3 = _rms(h2, gm_ref[...])
    n3_ref[...] = n3

    n3h = n3.astype(BF16)
    n3l = (n3 - n3h.astype(F32)).astype(BF16)
    wrh = wrh_ref[...]
    logits = (jnp.dot(n3h, wrh, preferred_element_type=F32)
              + jnp.dot(n3l, wrh, preferred_element_type=F32)
              + jnp.dot(n3h, wrl_ref[...], preferred_element_type=F32)
              + br_ref[...])
    lane = lax.broadcasted_iota(I32, (tm, N_EXPERTS), 1).astype(F32)
    vals, idxs, hots = [], [], []
    cur = logits
    for _ in range(TOP_K):
        mx = jnp.max(cur, axis=-1, keepdims=True)
        ix = jnp.min(jnp.where(cur == mx, lane, float(N_EXPERTS)), axis=-1, keepdims=True)
        hot = lane == ix
        vals.append(mx)
        idxs.append(ix)
        hots.append(hot)
        cur = jnp.where(hot, -jnp.inf, cur)
    es = [jnp.exp(v - vals[0]) for v in vals]
    den = es[0] + es[1] + es[2] + es[3]
    col = lax.broadcasted_iota(I32, (tm, TOP_K), 1)

    def pack(cols):
        out = jnp.broadcast_to(cols[TOP_K - 1], (tm, TOP_K))
        for k in range(TOP_K - 2, -1, -1):
            out = jnp.where(col == k, cols[k], out)
        return out

    idx_ref[...] = pack(idxs).astype(I32)
    tw_ref[...] = pack([e / den for e in es])

    cmat = (hots[0] | hots[1] | hots[2] | hots[3]).astype(F32)
    rr = lax.broadcasted_iota(I32, (tm, tm), 0)
    cc = lax.broadcasted_iota(I32, (tm, tm), 1)
    tri = (cc < rr).astype(BF16)
    before = jnp.dot(tri, cmat.astype(BF16), preferred_element_type=F32) + carry_ref[...]
    ranks = [jnp.sum(jnp.where(hot, before, 0.0), axis=-1, keepdims=True) for hot in hots]
    rank_ref[...] = pack(ranks).astype(I32)
    carry = carry_ref[...] + jnp.sum(cmat, axis=0, keepdims=True)
    carry_ref[...] = carry
    cnt_ref[...] = carry


def _cross(n2, h1, kv, wq, wo, gm, wrh, wrl, br, *, batch, tm, name="cross"):
    n, d = h1.shape
    per_b = n // batch // tm
    mem_len = kv.shape[0] // batch

    def const(shape):
        return pl.BlockSpec(shape, lambda bi, i: (0, 0), pipeline_mode=pl.Buffered(1))

    def row(bi, i):
        return (bi * per_b + i, 0)

    return pl.pallas_call(
        functools.partial(_cross_kernel, tm=tm),
        out_shape=(
            jax.ShapeDtypeStruct((n, d), F32),
            jax.ShapeDtypeStruct((n, d), F32),
            jax.ShapeDtypeStruct((n, TOP_K), I32),
            jax.ShapeDtypeStruct((n, TOP_K), F32),
            jax.ShapeDtypeStruct((n, TOP_K), I32),
            jax.ShapeDtypeStruct((1, N_EXPERTS), F32),
        ),
        grid=(batch, per_b),
        in_specs=[
            pl.BlockSpec((tm, d), row),
            pl.BlockSpec((tm, d), row),
            pl.BlockSpec((mem_len, 2 * CROSS_WIDTH), lambda bi, i: (bi, 0)),
            const((d, CROSS_WIDTH)), const((CROSS_WIDTH, d)), const((1, d)),
            const((d, N_EXPERTS)), const((d, N_EXPERTS)), const((1, N_EXPERTS)),
        ],
        out_specs=(
            pl.BlockSpec((tm, d), row),
            pl.BlockSpec((tm, d), row),
            pl.BlockSpec((tm, TOP_K), row),
            pl.BlockSpec((tm, TOP_K), row),
            pl.BlockSpec((tm, TOP_K), row),
            pl.BlockSpec((1, N_EXPERTS), lambda bi, i: (0, 0)),
        ),
        scratch_shapes=[pltpu.VMEM((1, N_EXPERTS), F32)],
        compiler_params=_cparams(("arbitrary", "arbitrary")),
        name=name,
    )(n2, h1, kv, wq, wo, gm.reshape(1, d), wrh, wrl, br.reshape(1, N_EXPERTS))


_HI_MASK = 0xFFFF0000


def _pack_rows(x):
    half = x.shape[1] // 2
    lo = lax.bitcast_convert_type(x[:, :half].astype(BF16).astype(F32), U32) >> 16
    hi = lax.bitcast_convert_type(x[:, half:].astype(BF16).astype(F32), U32) & jnp.uint32(_HI_MASK)
    return lo | hi


def _unpack_rows(u):
    lo = lax.bitcast_convert_type(u << 16, F32)
    hi = lax.bitcast_convert_type(u & jnp.uint32(_HI_MASK), F32)
    return lo, hi


def _slab(ref, row):
    return ref.at[pl.ds(pl.multiple_of(row * MOE_SLAB, MOE_SLAB), MOE_SLAB)]


def _row_copies(make_copy, make_group_wait, count):
    ngroup = count // DMA_WAIT_ROWS

    def group(c, carry):
        def one(j, carry2):
            make_copy(c * DMA_WAIT_ROWS + j).start()
            return carry2

        lax.fori_loop(0, DMA_WAIT_ROWS, one, 0, unroll=DMA_ISSUE_UNROLL)

        @pl.when(c > 0)
        def _():
            make_group_wait().wait()

        return carry

    lax.fori_loop(0, ngroup, group, 0)
    make_group_wait().wait()


def _dispatch_kernel(slot_ref, n3_ref, xs_hbm, stage, sem, *, real_steps, tm):
    step = pl.program_id(0)
    group_rows = DMA_WAIT_ROWS * MOE_SLAB

    def group_wait():
        return pltpu.make_async_copy(stage.at[pl.ds(0, group_rows)], xs_hbm.at[pl.ds(0, group_rows)], sem)

    @pl.when(step < real_steps)
    def _():
        u = _pack_rows(n3_ref[...])
        for j in range(MOE_SLAB):
            stage[pl.ds(j, tm, stride=MOE_SLAB), :] = u[:, j * V7X_LANES:(j + 1) * V7X_LANES]

        def make_copy(a):
            return pltpu.make_async_copy(_slab(stage, a >> TOP_K_LOG2), _slab(xs_hbm, slot_ref[0, a]), sem)

        _row_copies(make_copy, group_wait, tm * TOP_K)

    @pl.when(step >= real_steps)
    def _():
        stage[0:MOE_SLAB, :] = jnp.zeros((MOE_SLAB, V7X_LANES), U32)

        def make_copy(a):
            return pltpu.make_async_copy(_slab(stage, 0), _slab(xs_hbm, slot_ref[0, a]), sem)

        _row_copies(make_copy, group_wait, tm * TOP_K)


def _dispatch(slots, n3, *, rows_total, tm, name="moe_dispatch"):
    n, d = n3.shape
    real_steps = n // tm
    return pl.pallas_call(
        functools.partial(_dispatch_kernel, real_steps=real_steps, tm=tm),
        out_shape=jax.ShapeDtypeStruct((rows_total * MOE_SLAB, V7X_LANES), U32),
        grid=(slots.shape[0],),
        in_specs=[
            pl.BlockSpec((None, 1, tm * TOP_K), lambda i: (i, 0, 0), memory_space=pltpu.SMEM),
            pl.BlockSpec((tm, d), lambda i: (jnp.minimum(i, real_steps - 1), 0)),
        ],
        out_specs=pl.BlockSpec(memory_space=pl.ANY),
        scratch_shapes=[pltpu.VMEM((tm * MOE_SLAB, V7X_LANES), U32), pltpu.SemaphoreType.DMA],
        compiler_params=_cparams(("arbitrary",)),
        name=name,
    )(slots, n3)


def _moe_kernel(te_ref, tv_ref, tstart_ref, trows_ref,
                xs_hbm, w1_ref, b1_ref, w2_ref, b2_ref, perm_ref, yb_hbm,
                stage, x16, acc, w1p, sems, *, nf):
    i = pl.program_id(0)
    f = pl.program_id(1)
    rows = trows_ref[i]
    start = tstart_ref[i]
    nchunk = rows // MOE_ROW_ALIGN
    chunk_slab_rows = MOE_ROW_ALIGN * MOE_SLAB
    half = 2 * V7X_LANES
    nhalf = 2 * MOE_TF // half

    def slot_rows(c):
        return pl.multiple_of((c & 1) * chunk_slab_rows, chunk_slab_rows)

    def hbm_rows(ref, c):
        r0 = pl.multiple_of((start + c * MOE_ROW_ALIGN) * MOE_SLAB, chunk_slab_rows)
        return ref.at[pl.ds(r0, chunk_slab_rows)]

    def in_copy(c):
        return pltpu.make_async_copy(hbm_rows(xs_hbm, c), stage.at[pl.ds(slot_rows(c), chunk_slab_rows)],
                                     sems.at[c & 1])

    def out_copy(c):
        return pltpu.make_async_copy(stage.at[pl.ds(slot_rows(c), chunk_slab_rows)], hbm_rows(yb_hbm, c),
                                     sems.at[c & 1])

    @pl.when((rows > 0) & (f == 0))
    def _():
        in_copy(0).start()

        def land(c, carry):
            @pl.when(c + 1 < nchunk)
            def _():
                in_copy(c + 1).start()

            in_copy(c).wait()
            r0 = pl.multiple_of(c * MOE_ROW_ALIGN, MOE_ROW_ALIGN)
            s0 = slot_rows(c)
            for j in range(MOE_SLAB):
                lo, hi = _unpack_rows(stage[pl.ds(s0 + j, MOE_ROW_ALIGN, stride=MOE_SLAB), :])
                x16[pl.ds(r0, MOE_ROW_ALIGN), j * V7X_LANES:(j + 1) * V7X_LANES] = lo.astype(BF16)
                x16[pl.ds(r0, MOE_ROW_ALIGN), (MOE_SLAB + j) * V7X_LANES:(MOE_SLAB + j + 1) * V7X_LANES] = (
                    hi.astype(BF16))
            acc[pl.ds(r0, MOE_ROW_ALIGN), :] = jnp.zeros((MOE_ROW_ALIGN, D_MODEL), F32)
            return carry

        lax.fori_loop(0, nchunk, land, 0)

    @pl.when(rows > 0)
    def _():
        perm = perm_ref[...]
        for c in range(nhalf):
            wb = w1_ref[:, c * half:(c + 1) * half].astype(BF16)
            w1p[:, c * half:(c + 1) * half] = jnp.dot(wb, perm, preferred_element_type=F32).astype(BF16)
        w2 = w2_ref[...].astype(BF16)
        b1 = b1_ref[...]

        def mlp(c0, nc):
            r0 = pl.multiple_of(c0 * MOE_ROW_ALIGN, MOE_ROW_ALIGN)
            nr = nc * MOE_ROW_ALIGN
            x = x16[pl.ds(r0, nr), :]
            h = jnp.dot(x, w1p[...], preferred_element_type=F32) + b1
            parts = []
            for c in range(nhalf):
                hg = jnp.minimum(h[:, c * half:c * half + V7X_LANES], SWIGLU_LIMIT)
                hl = jnp.clip(h[:, c * half + V7X_LANES:(c + 1) * half], -SWIGLU_LIMIT, SWIGLU_LIMIT)
                parts.append(hg * jax.nn.sigmoid(SWIGLU_ALPHA * hg) * (hl + 1.0))
            a = jnp.concatenate(parts, axis=-1).astype(BF16)
            acc[pl.ds(r0, nr), :] += jnp.dot(a, w2, preferred_element_type=F32)

        nquad = nchunk // MOE_BLOCK_CHUNKS

        def quad(c, carry):
            mlp(c * MOE_BLOCK_CHUNKS, MOE_BLOCK_CHUNKS)
            return carry

        lax.fori_loop(0, nquad, quad, 0)
        rem = nchunk - nquad * MOE_BLOCK_CHUNKS

        @pl.when(rem >= 2)
        def _():
            mlp(nquad * MOE_BLOCK_CHUNKS, 2)

        @pl.when((rem & 1) == 1)
        def _():
            mlp(nchunk - 1, 1)

    @pl.when((rows > 0) & (f == nf - 1))
    def _():
        def emit(c, carry):
            @pl.when(c >= 2)
            def _():
                out_copy(c - 2).wait()

            r0 = pl.multiple_of(c * MOE_ROW_ALIGN, MOE_ROW_ALIGN)
            s0 = slot_rows(c)
            u = _pack_rows(acc[pl.ds(r0, MOE_ROW_ALIGN), :] + b2_ref[...])
            for j in range(MOE_SLAB):
                stage[pl.ds(s0 + j, MOE_ROW_ALIGN, stride=MOE_SLAB), :] = u[:, j * V7X_LANES:(j + 1) * V7X_LANES]
            out_copy(c).start()
            return carry

        lax.fori_loop(0, nchunk, emit, 0)

        @pl.when(nchunk >= 2)
        def _():
            out_copy(nchunk - 2).wait()

        out_copy(nchunk - 1).wait()


def _moe(tile_e, tile_v, tile_start, tile_rows, xs, w1, b1p, w2, b2, perm, *, name="moe_mlp"):
    nt = tile_e.shape[0]
    nf = D_FF // MOE_TF

    def w1_map(i, f, te, tv, ts, tr):
        return (te[i], 0, jnp.where(tv[i] > 0, f, nf - 1))

    def w2_map(i, f, te, tv, ts, tr):
        return (te[i], jnp.where(tv[i] > 0, f, nf - 1), 0)

    def b2_map(i, f, te, tv, ts, tr):
        return (te[i], 0, 0)

    grid_spec = pltpu.PrefetchScalarGridSpec(
        num_scalar_prefetch=4,
        grid=(nt, nf),
        in_specs=[
            pl.BlockSpec(memory_space=pl.ANY),
            pl.BlockSpec((None, D_MODEL, 2 * MOE_TF), w1_map),
            pl.BlockSpec((None, 1, 2 * MOE_TF), w1_map),
            pl.BlockSpec((None, MOE_TF, D_MODEL), w2_map),
            pl.BlockSpec((None, 1, D_MODEL), b2_map),
            pl.BlockSpec((2 * V7X_LANES, 2 * V7X_LANES), lambda i, f, te, tv, ts, tr: (0, 0)),
        ],
        out_specs=pl.BlockSpec(memory_space=pl.ANY),
        scratch_shapes=[
            pltpu.VMEM((2 * MOE_ROW_ALIGN * MOE_SLAB, V7X_LANES), U32),
            pltpu.VMEM((MOE_TMAX, D_MODEL), BF16),
            pltpu.VMEM((MOE_TMAX, D_MODEL), F32),
            pltpu.VMEM((D_MODEL, 2 * MOE_TF), BF16),
            pltpu.SemaphoreType.DMA((2,)),
        ],
    )
    return pl.pallas_call(
        functools.partial(_moe_kernel, nf=nf),
        out_shape=jax.ShapeDtypeStruct(xs.shape, xs.dtype),
        grid_spec=grid_spec,
        input_output_aliases={4: 0},
        compiler_params=_cparams(("arbitrary", "arbitrary")),
        name=name,
    )(tile_e, tile_v, tile_start, tile_rows, xs, w1, b1p.reshape(N_EXPERTS, 1, 2 * D_FF), w2,
      b2.reshape(N_EXPERTS, 1, D_MODEL), perm)


def _combine_kernel(slot_ref, h2_ref, tw_ref, gf_ref, yb_hbm, o_ref, buf, sem, *, tm):
    group_rows = DMA_WAIT_ROWS * MOE_SLAB

    def make_copy(a):
        t = a >> TOP_K_LOG2
        k = a & (TOP_K - 1)
        return pltpu.make_async_copy(_slab(yb_hbm, slot_ref[0, a]), _slab(buf, k * tm + t), sem)

    def group_wait():
        return pltpu.make_async_copy(yb_hbm.at[pl.ds(0, group_rows)], buf.at[pl.ds(0, group_rows)], sem)

    _row_copies(make_copy, group_wait, tm * TOP_K)

    tw = tw_ref[...]
    cols_lo, cols_hi = [], []
    for j in range(MOE_SLAB):
        clo = chi = None
        for k in range(TOP_K):
            lo, hi = _unpack_rows(buf[pl.ds(k * tm * MOE_SLAB + j, tm, stride=MOE_SLAB), :])
            w = tw[:, k:k + 1]
            clo = w * lo if clo is None else clo + w * lo
            chi = w * hi if chi is None else chi + w * hi
        cols_lo.append(clo)
        cols_hi.append(chi)
    h3 = h2_ref[...] + jnp.concatenate(cols_lo + cols_hi, axis=-1)
    o_ref[...] = _rms(h3, gf_ref[...])


def _combine(slots, h2, tw, gf, yb, *, tm, name="moe_combine"):
    n, d = h2.shape
    return pl.pallas_call(
        functools.partial(_combine_kernel, tm=tm),
        out_shape=jax.ShapeDtypeStruct((n, d), F32),
        grid=(n // tm,),
        in_specs=[
            pl.BlockSpec((None, 1, tm * TOP_K), lambda i: (i, 0, 0), memory_space=pltpu.SMEM),
            pl.BlockSpec((tm, d), lambda i: (i, 0)),
            pl.BlockSpec((tm, TOP_K), lambda i: (i, 0)),
            pl.BlockSpec((1, d), lambda i: (0, 0)),
            pl.BlockSpec(memory_space=pl.ANY),
        ],
        out_specs=pl.BlockSpec((tm, d), lambda i: (i, 0)),
        scratch_shapes=[
            pltpu.VMEM((tm * TOP_K * MOE_SLAB, V7X_LANES), U32),
            pltpu.SemaphoreType.DMA,
        ],
        compiler_params=_cparams(("arbitrary",)),
        name=name,
    )(slots, h2, tw, gf.reshape(1, d), yb)


def _rope_tables(positions):
    b, s = positions.shape
    half = HEAD_DIM // 2
    inv_freq = 1.0 / (ROPE_THETA ** (jnp.arange(half, dtype=F32) / half))
    ang = positions.astype(F32)[..., None] * inv_freq
    cos = jnp.cos(ang)
    sin = jnp.sin(ang)
    cs = jnp.concatenate([cos, cos], axis=-1)
    sn = jnp.concatenate([-sin, sin], axis=-1)

    def deint(a):
        a = a.reshape(b, s // ATT_TILE, WIN_BLOCK, DIL_MAX, HEAD_DIM)
        return a.transpose(0, 1, 3, 2, 4)

    return deint(cs), deint(sn)


def _routing(top_e, rank, counts, n_tok):
    cnt = counts.reshape(N_EXPERTS).astype(I32)
    padded = (cnt + MOE_ROW_ALIGN - 1) // MOE_ROW_ALIGN * MOE_ROW_ALIGN
    gend = jnp.cumsum(padded)
    gstart = gend - padded
    slot = (gstart[top_e] + rank).reshape(-1)

    n_assign = n_tok * TOP_K
    n_unused = N_EXPERTS * MOE_ROW_ALIGN
    rows_total = n_assign + n_unused
    padcnt = padded - cnt
    pend = jnp.cumsum(padcnt)
    pstart = pend - padcnt
    def first_above(ends, q):
        return jnp.minimum(jnp.sum((ends[None, :] <= q[:, None]).astype(I32), axis=1), N_EXPERTS - 1)

    p = jnp.arange(n_unused, dtype=I32)
    ep = first_above(pend, p)
    unused = jnp.where(p < pend[-1], gstart[ep] + cnt[ep] + (p - pstart[ep]), gend[-1] + (p - pend[-1]))

    nt_max = rows_total // MOE_TMAX + N_EXPERTS
    nt_e = (padded + MOE_TMAX - 1) // MOE_TMAX
    tend = jnp.cumsum(nt_e)
    tstart = tend - nt_e
    ti = jnp.arange(nt_max, dtype=I32)
    valid = ti < tend[-1]
    tic = jnp.minimum(ti, tend[-1] - 1)
    te = first_above(tend, tic)
    local = tic - tstart[te]
    row0 = gstart[te] + local * MOE_TMAX
    rows = jnp.where(valid, jnp.clip(padded[te] - local * MOE_TMAX, 0, MOE_TMAX), 0)
    return (slot.astype(I32), unused.astype(I32), rows_total, te, valid.astype(I32),
            row0.astype(I32), rows.astype(I32))


def kernel(x, mem, positions, norm_mix_g, w_in, lb_raw, hgrn_norm_g, w_br_hgrn, w_br_attn, w_out,
           norm_cross_g, norm_mem_g, w_cq, w_ckv, w_co, norm_moe_g, w_router, b_router,
           w_mlp1, b_mlp1, w_mlp2, b_mlp2, norm_final_g):
    bsz, seq, d = x.shape
    assert w_in.shape[0] == 1 and d == D_MODEL and seq % ATT_TILE == 0
    n_tok = bsz * seq
    lower_bounds = jnp.cumsum(jax.nn.softmax(lb_raw.astype(F32), axis=0), axis=0)
    cs, sn = _rope_tables(positions)

    wl = w_in[0]
    c_h = 4 * HG_WIDTH
    c_a = c_h + ATT_Q_WIDTH + 2 * ATT_KV_WIDTH
    x2d = x.reshape(n_tok, d)
    proj_h = _in_proj(x2d, norm_mix_g[0], wl[:, :c_h].astype(BF16), tm=1024, tn=1024, name="in_proj_hgrn")
    proj_g = _in_proj(x2d, norm_mix_g[0], wl[:, c_a:].astype(BF16), tm=1024, tn=1024, name="in_proj_gate")
    pa = _in_proj_strided(x, norm_mix_g[0], wl[:, c_h:c_a].astype(BF16), c=8, tn=1024, name="in_proj_attn")

    o_h = _hgrn(proj_h.reshape(bsz, seq, c_h), lower_bounds[0], hgrn_norm_g[0], ts=512)
    o_a = _dilated_attn(pa, cs, sn)

    h1, n2 = _merge_out(o_h.reshape(n_tok, HG_WIDTH), o_a.reshape(n_tok, ATT_KV_WIDTH), proj_g, x2d,
                        w_br_hgrn[0].astype(BF16), w_br_attn[0].astype(BF16), w_out[0].astype(BF16),
                        norm_cross_g[0], tm=256)

    mem2d = mem.reshape(-1, d)
    kv = _in_proj(mem2d, norm_mem_g, w_ckv[0].astype(BF16), tm=mem2d.shape[0], tn=2 * CROSS_WIDTH, name="mem_kv")
    wrh = w_router[0].astype(BF16)
    wrl = (w_router[0] - wrh.astype(F32)).astype(BF16)
    h2, n3, top_e, top_w, rank, counts = _cross(
        n2, h1, kv, w_cq[0].astype(BF16), w_co[0].astype(BF16), norm_moe_g[0], wrh, wrl, b_router[0],
        batch=bsz, tm=512)

    slot, unused, rows_total, te, tv, tstart, trows = _routing(top_e, rank, counts, n_tok)
    tm_d = DISPATCH_TOKENS
    slots_all = jnp.concatenate([slot, unused]).reshape(-1, 1, tm_d * TOP_K)
    xs = _dispatch(slots_all, n3, rows_total=rows_total, tm=tm_d)

    ii = jnp.arange(2 * V7X_LANES)
    src = jnp.where(ii < V7X_LANES, 2 * ii, 2 * (ii - V7X_LANES) + 1)
    perm = (ii[:, None] == src[None, :]).astype(BF16)
    b1p = b_mlp1[0].reshape(N_EXPERTS, -1, V7X_LANES, 2).transpose(0, 1, 3, 2).reshape(N_EXPERTS, 2 * D_FF)
    yb = _moe(te, tv, tstart, trows, xs, w_mlp1[0], b1p, w_mlp2[0], b_mlp2[0], perm)

    tm_c = 256
    out = _combine(slot.reshape(-1, 1, tm_c * TOP_K), h2, top_w, norm_final_g, yb, tm=tm_c)
    return out.reshape(bsz, seq, d)
```

```python
import functools

import jax
import jax.numpy as jnp
from jax import lax
from jax.experimental import pallas as pl
from jax.experimental.pallas import tpu as pltpu

F32 = jnp.float32
BF16 = jnp.bfloat16
I32 = jnp.int32
U32 = jnp.uint32

D_MODEL = 2048
HEAD_DIM = 128
HG_HEADS = 8
HG_WIDTH = HG_HEADS * HEAD_DIM
N_KV_HEADS = 8
N_GROUPS = 3
ATT_Q_WIDTH = N_GROUPS * N_KV_HEADS * HEAD_DIM
ATT_KV_WIDTH = N_KV_HEADS * HEAD_DIM
WIN_BLOCK = 128
ROPE_THETA = 10000.0
CROSS_HEADS = 4
CROSS_WIDTH = CROSS_HEADS * HEAD_DIM
N_EXPERTS = 32
TOP_K = 4
TOP_K_LOG2 = 2
D_FF = D_MODEL
SWIGLU_ALPHA = 1.702
SWIGLU_LIMIT = 7.0
NORM_EPS = 1e-6

V7X_LANES = 128
V7X_VMEM_LIMIT_BYTES = 56 * 1024 * 1024

DIL_MAX = 16
ATT_TILE = DIL_MAX * WIN_BLOCK
ATT_UNROLL = 8
HG_BLOCK = 16
HG_GROUP = 16
NEG_BIG = -1e30

MOE_TMAX = 2048
MOE_ROW_ALIGN = 128
MOE_BLOCK_CHUNKS = 8
MOE_TF = 256
MOE_SLAB = D_MODEL // 2 // V7X_LANES
DISPATCH_TOKENS = 256
DMA_WAIT_ROWS = 128
DMA_ISSUE_UNROLL = 8


def _rms(x, g):
    ms = jnp.mean(x * x, axis=-1, keepdims=True)
    return x * lax.rsqrt(ms + NORM_EPS) * g


def _cparams(sem, vmem=V7X_VMEM_LIMIT_BYTES):
    return pltpu.CompilerParams(dimension_semantics=sem, vmem_limit_bytes=vmem)


def _in_proj_kernel(x_ref, g_ref, w_ref, o_ref, xn_ref):
    @pl.when(pl.program_id(1) == 0)
    def _():
        xn_ref[...] = _rms(x_ref[...], g_ref[...]).astype(BF16)

    o_ref[...] = jnp.dot(xn_ref[...], w_ref[...], preferred_element_type=F32).astype(o_ref.dtype)


def _in_proj(x2d, g, w, *, tm, tn, name, col0=0, ncols=None):
    n, d = x2d.shape
    wc = w.shape[1] if ncols is None else ncols
    cb = col0 // tn
    return pl.pallas_call(
        _in_proj_kernel,
        out_shape=jax.ShapeDtypeStruct((n, wc), BF16),
        grid=(n // tm, wc // tn),
        in_specs=[
            pl.BlockSpec((tm, d), lambda i, j: (i, 0)),
            pl.BlockSpec((1, d), lambda i, j: (0, 0)),
            pl.BlockSpec((d, tn), lambda i, j: (0, cb + j)),
        ],
        out_specs=pl.BlockSpec((tm, tn), lambda i, j: (i, j)),
        scratch_shapes=[pltpu.VMEM((tm, d), BF16)],
        compiler_params=_cparams(("parallel", "arbitrary")),
        name=name,
    )(x2d, g.reshape(1, d), w)


def _in_proj_strided_kernel(x_ref, g_ref, w_ref, o_ref, xn_ref, *, c):
    @pl.when(pl.program_id(3) == 0)
    def _():
        for ci in range(c):
            x = x_ref[:, ci * D_MODEL:(ci + 1) * D_MODEL]
            xn_ref[ci * WIN_BLOCK:(ci + 1) * WIN_BLOCK, :] = _rms(x, g_ref[...]).astype(BF16)

    res = jnp.dot(xn_ref[...], w_ref[...], preferred_element_type=F32)
    for ci in range(c):
        o_ref[ci] = res[ci * WIN_BLOCK:(ci + 1) * WIN_BLOCK, :].astype(o_ref.dtype)


def _in_proj_strided(x, g, w, *, c, tn, name, col0, ncols):
    b, s, d = x.shape
    nt = s // ATT_TILE
    wc = ncols
    cb = col0 // tn
    xv = x.reshape(b, nt, WIN_BLOCK, DIL_MAX * d)
    return pl.pallas_call(
        functools.partial(_in_proj_strided_kernel, c=c),
        out_shape=jax.ShapeDtypeStruct((b, nt, DIL_MAX, WIN_BLOCK, wc), BF16),
        grid=(b, nt, DIL_MAX // c, wc // tn),
        in_specs=[
            pl.BlockSpec((None, None, WIN_BLOCK, c * d), lambda bi, ti, ri, j: (bi, ti, 0, ri)),
            pl.BlockSpec((1, d), lambda bi, ti, ri, j: (0, 0)),
            pl.BlockSpec((d, tn), lambda bi, ti, ri, j: (0, cb + j)),
        ],
        out_specs=pl.BlockSpec((None, None, c, WIN_BLOCK, tn), lambda bi, ti, ri, j: (bi, ti, ri, 0, j)),
        scratch_shapes=[pltpu.VMEM((c * WIN_BLOCK, d), BF16)],
        compiler_params=_cparams(("parallel", "parallel", "parallel", "arbitrary")),
        name=name,
    )(xv, g.reshape(1, d), w)


def _hgrn_kernel(q_ref, f_ref, i_ref, g_ref, lb_ref, gn_ref, o_ref, st_ref, kin_s, b_s, v_s, *, ts):
    @pl.when(pl.program_id(2) == 0)
    def _():
        st_ref[...] = jnp.zeros_like(st_ref)

    lb = lb_ref[...]
    oml = 1.0 - lb
    gn = gn_ref[...]
    half = HG_BLOCK // 2
    row = lax.broadcasted_iota(I32, (HG_BLOCK, HEAD_DIM), 0)
    row8 = lax.broadcasted_iota(I32, (half, HEAD_DIM), 0)
    nt_dims = (((1,), (1,)), ((), ()))
    tn_dims = (((0,), (0,)), ((), ()))

    def front(g, t0):
        sl = pl.ds(t0 + g * HG_BLOCK, HG_BLOCK)
        q = q_ref[sl, :].astype(F32)
        hf = f_ref[sl, :].astype(F32)
        v = i_ref[sl, :].astype(F32)
        kin = oml * jax.nn.sigmoid(-hf)
        b = jnp.log2(lb + oml * jax.nn.sigmoid(hf))
        for sh in (1, 2, 4, 8):
            b = b + jnp.where(row >= sh, pltpu.roll(b, sh, 0), 0.0)
        kin_s[g] = kin
        b_s[g] = b
        v_s[g] = v
        q_lo, q_hi = q[:half], q[half:]
        b_lo, b_hi = b[:half], b[half:]
        o_lo = jnp.zeros((half, HEAD_DIM), F32)
        o_hi = jnp.zeros((half, HEAD_DIM), F32)
        for s in range(HG_BLOCK):
            ks = kin_s[g, s:s + 1, :]
            bs = b_s[g, s:s + 1, :]
            vs = v_s[g, s:s + 1, :]
            if s < half:
                w = q_lo * ks * jnp.exp2(b_lo - bs)
                if s > 0:
                    w = jnp.where(row8 >= s, w, 0.0)
                o_lo = o_lo + jnp.sum(w, axis=-1, keepdims=True) * vs
                w = q_hi * ks * jnp.exp2(b_hi - bs)
            else:
                w = q_hi * ks * jnp.exp2(b_hi - bs)
                if s > half:
                    w = jnp.where(row8 >= s - half, w, 0.0)
            o_hi = o_hi + jnp.sum(w, axis=-1, keepdims=True) * vs
        bl = b_s[g, HG_BLOCK - 1:HG_BLOCK, :]
        qd = (q * jnp.exp2(b)).astype(BF16)
        kd = (kin * jnp.exp2(bl - b)).astype(BF16)
        upd = lax.dot_general(v.astype(BF16), kd, tn_dims, preferred_element_type=F32)
        return jnp.concatenate([o_lo, o_hi], axis=0), qd, upd, jnp.exp2(bl)

    def body(i, carry):
        t0 = pl.multiple_of(i * (HG_GROUP * HG_BLOCK), HG_GROUP * HG_BLOCK)
        fronts = [front(g, t0) for g in range(HG_GROUP)]
        st = st_ref[...]
        for g, (o_diag, qd, upd, dec) in enumerate(fronts):
            o = o_diag + lax.dot_general(qd, st.astype(BF16), nt_dims, preferred_element_type=F32)
            st = st * dec + upd
            sl = pl.ds(t0 + g * HG_BLOCK, HG_BLOCK)
            hg = g_ref[sl, :].astype(F32)
            o_ref[sl, :] = (_rms(o, gn) * (hg * jax.nn.sigmoid(hg))).astype(o_ref.dtype)
        st_ref[...] = st
        return carry

    lax.fori_loop(0, ts // (HG_GROUP * HG_BLOCK), body, 0)


def _hgrn(proj_h, lb, gn, *, ts, name="hgrn"):
    b, s, _ = proj_h.shape
    h = HG_HEADS

    def spec(off):
        return pl.BlockSpec((None, ts, HEAD_DIM), lambda bi, hi, si: (bi, si, off + hi))

    vec = pl.BlockSpec((1, HEAD_DIM), lambda bi, hi, si: (0, hi))
    return pl.pallas_call(
        functools.partial(_hgrn_kernel, ts=ts),
        out_shape=jax.ShapeDtypeStruct((b, s, HG_WIDTH), BF16),
        grid=(b, h, s // ts),
        in_specs=[spec(0), spec(h), spec(2 * h), spec(3 * h), vec, vec],
        out_specs=pl.BlockSpec((None, ts, HEAD_DIM), lambda bi, hi, si: (bi, si, hi)),
        scratch_shapes=[
            pltpu.VMEM((HEAD_DIM, HEAD_DIM), F32),
            pltpu.VMEM((HG_GROUP, HG_BLOCK, HEAD_DIM), F32),
            pltpu.VMEM((HG_GROUP, HG_BLOCK, HEAD_DIM), F32),
            pltpu.VMEM((HG_GROUP, HG_BLOCK, HEAD_DIM), F32),
        ],
        compiler_params=_cparams(("parallel", "parallel", "arbitrary")),
        name=name,
    )(proj_h, proj_h, proj_h, proj_h, lb.reshape(1, HG_WIDTH), gn.reshape(1, HG_WIDTH))


def _attn_bias(kind):
    rq = lax.broadcasted_iota(I32, (WIN_BLOCK, 2 * WIN_BLOCK), 0)
    ck = lax.broadcasted_iota(I32, (WIN_BLOCK, 2 * WIN_BLOCK), 1)
    if kind == 2:
        dist = rq + WIN_BLOCK - ck
        first = ck < WIN_BLOCK
    elif kind == 1:
        dist = 4 * ((rq & 31) - (ck & 63) + 32) + ((rq >> 5) - (ck >> 6))
        first = (ck & 63) < 32
    else:
        dist = 16 * ((rq & 7) - (ck & 15) + 8) + ((rq >> 3) - (ck >> 4))
        first = (ck & 15) < 8
    valid = (dist >= 0) & (dist <= WIN_BLOCK)
    return (jnp.where(valid, 0.0, NEG_BIG).astype(F32),
            jnp.where(valid & jnp.logical_not(first), 0.0, NEG_BIG).astype(F32))


def _attn_kernel(q0_ref, q1_ref, q2_ref, k_ref, v_ref, cs_ref, sn_ref, o_ref,
                 qr, kext, vext, acc, mrun, lrun, bias, onat):
    ti = pl.program_id(2)
    wb = WIN_BLOCK
    scale = HEAD_DIM ** -0.5

    @pl.when(ti == 0)
    def _():
        kext[:, 0:wb, :] = jnp.zeros((DIL_MAX, wb, HEAD_DIM), F32)
        vext[:, 0:wb, :] = jnp.zeros((DIL_MAX, wb, HEAD_DIM), F32)

    @pl.when(ti > 0)
    def _():
        kext[:, 0:wb, :] = kext[:, wb:2 * wb, :]
        vext[:, 0:wb, :] = vext[:, wb:2 * wb, :]

    for kind in range(N_GROUPS):
        full, nofirst = _attn_bias(kind)
        bias[2 * kind] = full
        bias[2 * kind + 1] = nofirst

    def rope_body(r, carry):
        cs = cs_ref[r]
        sn = sn_ref[r]
        for g, qref in enumerate((q0_ref, q1_ref, q2_ref)):
            q = qref[r].astype(F32)
            qr[g, r] = (q * cs + pltpu.roll(q, HEAD_DIM // 2, 1) * sn) * scale
        k = k_ref[r].astype(F32)
        kext[r, wb:2 * wb, :] = k * cs + pltpu.roll(k, HEAD_DIM // 2, 1) * sn
        vext[r, wb:2 * wb, :] = v_ref[r].astype(F32)
        acc[r] = jnp.zeros((wb, HEAD_DIM), F32)
        mrun[r] = jnp.full((wb, HEAD_DIM), NEG_BIG, F32)
        lrun[r] = jnp.zeros((wb, HEAD_DIM), F32)
        return carry

    lax.fori_loop(0, DIL_MAX, rope_body, 0)

    nt_dims = (((1,), (1,)), ((), ()))

    def block(qb, kb, vb, bias_blk):
        s = lax.dot_general(qb.astype(BF16), kb.astype(BF16), nt_dims, preferred_element_type=F32)
        s = s + bias_blk
        m = jnp.max(s, axis=-1, keepdims=True)
        p = jnp.exp(s - m)
        l = jnp.sum(p, axis=-1, keepdims=True)
        n = jnp.dot(p.astype(BF16), vb.astype(BF16), preferred_element_type=F32)
        return n, jnp.broadcast_to(m, (wb, HEAD_DIM)), jnp.broadcast_to(l, (wb, HEAD_DIM))

    def merge(r, rows, n, m, l):
        m_old = mrun[r, rows, :]
        m_new = jnp.maximum(m_old, m)
        a = jnp.exp(m_old - m_new)
        bb = jnp.exp(m - m_new)
        acc[r, rows, :] = acc[r, rows, :] * a + n * bb
        lrun[r, rows, :] = lrun[r, rows, :] * a + l * bb
        mrun[r, rows, :] = m_new

    first_tile = jnp.where(ti == 0, 1, 0)


    def g2_body(i, carry):
        rs = [i * ATT_UNROLL + u for u in range(ATT_UNROLL)]
        res = [block(qr[2, r], kext[r], vext[r], bias[4 + first_tile]) for r in rs]
        for r, (n, m, l) in zip(rs, res):
            merge(r, pl.ds(0, wb), n, m, l)
        return carry

    lax.fori_loop(0, DIL_MAX // ATT_UNROLL, g2_body, 0)

    def g1_body(i, carry):
        res = []
        for u in range(ATT_UNROLL // 4):
            mb = i * (ATT_UNROLL // 4) + u
            q0 = pl.multiple_of(32 * mb, 32)
            k0 = pl.multiple_of(96 + 32 * mb, 32)
            use_first = jnp.where(mb == 0, first_tile, 0)
            for r4 in range(4):
                qb = jnp.concatenate([qr[1, r4 + 4 * j, pl.ds(q0, 32), :] for j in range(4)], axis=0)
                kb = jnp.concatenate([kext[r4 + 4 * j, pl.ds(k0, 64), :] for j in range(4)], axis=0)
                vb = jnp.concatenate([vext[r4 + 4 * j, pl.ds(k0, 64), :] for j in range(4)], axis=0)
                res.append((r4, q0, block(qb, kb, vb, bias[2 + use_first])))
        for r4, q0, (n, m, l) in res:
            for j in range(4):
                sl = slice(32 * j, 32 * (j + 1))
                merge(r4 + 4 * j, pl.ds(q0, 32), n[sl], m[sl], l[sl])
        return carry

    lax.fori_loop(0, DIL_MAX // ATT_UNROLL, g1_body, 0)

    def g0_body(i, carry):
        res = []
        for u in range(ATT_UNROLL):
            mb = i * ATT_UNROLL + u
            q0 = pl.multiple_of(8 * mb, 8)
            k0 = pl.multiple_of(120 + 8 * mb, 8)
            qb = jnp.concatenate([qr[0, r, pl.ds(q0, 8), :] for r in range(DIL_MAX)], axis=0)
            kb = jnp.concatenate([kext[r, pl.ds(k0, 16), :] for r in range(DIL_MAX)], axis=0)
            vb = jnp.concatenate([vext[r, pl.ds(k0, 16), :] for r in range(DIL_MAX)], axis=0)
            use_first = jnp.where(mb == 0, first_tile, 0)
            res.append((q0, block(qb, kb, vb, bias[use_first])))
        for q0, (n, m, l) in res:
            for r in range(DIL_MAX):
                sl = slice(8 * r, 8 * (r + 1))
                merge(r, pl.ds(q0, 8), n[sl], m[sl], l[sl])
        return carry

    lax.fori_loop(0, DIL_MAX // ATT_UNROLL, g0_body, 0)

    for r in range(DIL_MAX):
        onat[pl.ds(r, wb, stride=DIL_MAX), :] = acc[r] / lrun[r]
    o_ref[...] = onat[...].astype(o_ref.dtype)


def _dilated_attn(pa, cs, sn, *, name="dilated_attn"):
    b, nt = pa.shape[0], pa.shape[1]
    h = N_KV_HEADS

    def spec(off):
        return pl.BlockSpec((None, None, DIL_MAX, WIN_BLOCK, HEAD_DIM),
                            lambda bi, hi, ti: (bi, ti, 0, 0, off + hi))

    tab = pl.BlockSpec((None, None, DIL_MAX, WIN_BLOCK, HEAD_DIM), lambda bi, hi, ti: (bi, ti, 0, 0, 0))
    return pl.pallas_call(
        _attn_kernel,
        out_shape=jax.ShapeDtypeStruct((b, nt * ATT_TILE, ATT_KV_WIDTH), BF16),
        grid=(b, h, nt),
        in_specs=[spec(0), spec(h), spec(2 * h), spec(3 * h), spec(4 * h), tab, tab],
        out_specs=pl.BlockSpec((None, ATT_TILE, HEAD_DIM), lambda bi, hi, ti: (bi, ti, hi)),
        scratch_shapes=[
            pltpu.VMEM((N_GROUPS, DIL_MAX, WIN_BLOCK, HEAD_DIM), F32),
            pltpu.VMEM((DIL_MAX, 2 * WIN_BLOCK, HEAD_DIM), F32),
            pltpu.VMEM((DIL_MAX, 2 * WIN_BLOCK, HEAD_DIM), F32),
            pltpu.VMEM((DIL_MAX, WIN_BLOCK, HEAD_DIM), F32),
            pltpu.VMEM((DIL_MAX, WIN_BLOCK, HEAD_DIM), F32),
            pltpu.VMEM((DIL_MAX, WIN_BLOCK, HEAD_DIM), F32),
            pltpu.VMEM((2 * N_GROUPS, WIN_BLOCK, 2 * WIN_BLOCK), F32),
            pltpu.VMEM((ATT_TILE, HEAD_DIM), F32),
        ],
        compiler_params=_cparams(("arbitrary", "arbitrary", "arbitrary")),
        name=name,
    )(pa, pa, pa, pa, pa, cs, sn)


def _merge_out_kernel(oh_ref, oa_ref, gate_ref, x_ref, wh_ref, wa_ref, wo_ref, gc_ref, h1_ref, n2_ref):
    ga = gate_ref[:, :D_MODEL].astype(F32)
    gb = gate_ref[:, D_MODEL:].astype(F32)
    yh = jnp.dot(oh_ref[...], wh_ref[...], preferred_element_type=F32)
    ya = jnp.dot(oa_ref[...], wa_ref[...], preferred_element_type=F32)
    merged = jax.nn.sigmoid(ga) * yh + jax.nn.sigmoid(gb) * ya
    h1 = x_ref[...] + jnp.dot(merged.astype(BF16), wo_ref[...], preferred_element_type=F32)
    h1_ref[...] = h1
    n2_ref[...] = _rms(h1, gc_ref[...]).astype(BF16)


def _merge_out(oh, oa, gates, x2d, wh, wa, wo, gc, *, tm, name="merge_out"):
    n, d = x2d.shape

    def const(shape):
        return pl.BlockSpec(shape, lambda i: (0, 0), pipeline_mode=pl.Buffered(1))

    return pl.pallas_call(
        _merge_out_kernel,
        out_shape=(jax.ShapeDtypeStruct((n, d), F32), jax.ShapeDtypeStruct((n, d), BF16)),
        grid=(n // tm,),
        in_specs=[
            pl.BlockSpec((tm, HG_WIDTH), lambda i: (i, 0)),
            pl.BlockSpec((tm, ATT_KV_WIDTH), lambda i: (i, 0)),
            pl.BlockSpec((tm, 2 * d), lambda i: (i, 0)),
            pl.BlockSpec((tm, d), lambda i: (i, 0)),
            const((HG_WIDTH, d)), const((ATT_KV_WIDTH, d)), const((d, d)), const((1, d)),
        ],
        out_specs=(pl.BlockSpec((tm, d), lambda i: (i, 0)), pl.BlockSpec((tm, d), lambda i: (i, 0))),
        compiler_params=_cparams(("parallel",)),
        name=name,
    )(oh, oa, gates, x2d, wh, wa, wo, gc.reshape(1, d))


def _cross_kernel(n2_ref, h1_ref, kv_ref, wq_ref, wo_ref, gm_ref, wrh_ref, wrl_ref, br_ref,
                  h2_ref, n3_ref, idx_ref, tw_ref, rank_ref, cnt_ref, carry_ref, *, tm):
    @pl.when((pl.program_id(0) == 0) & (pl.program_id(1) == 0))
    def _():
        carry_ref[...] = jnp.zeros_like(carry_ref)

    nt_dims = (((1,), (1,)), ((), ()))
    scale = HEAD_DIM ** -0.5
    q = (jnp.dot(n2_ref[...], wq_ref[...], preferred_element_type=F32) * scale).astype(BF16)
    outs = []
    for hh in range(CROSS_HEADS):
        sl = slice(hh * HEAD_DIM, (hh + 1) * HEAD_DIM)
        kh = kv_ref[:, sl]
        vh = kv_ref[:, CROSS_WIDTH + hh * HEAD_DIM:CROSS_WIDTH + (hh + 1) * HEAD_DIM]
        s = lax.dot_general(q[:, sl], kh, nt_dims, preferred_element_type=F32)
        p = jnp.exp(s - jnp.max(s, axis=-1, keepdims=True))
        l = jnp.sum(p, axis=-1, keepdims=True)
        outs.append(jnp.dot(p.astype(BF16), vh, preferred_element_type=F32) / l)
    o = jnp.concatenate(outs, axis=-1).astype(BF16)
    h2 = h1_ref[...] + jnp.dot(o, wo_ref[...], preferred_element_type=F32)
    h2_ref[...] = h2
    n3 = _rms(h2, gm_ref[...])
    n3_ref[...] = n3

    n3h = n3.astype(BF16)
    n3l = (n3 - n3h.astype(F32)).astype(BF16)
    wrh = wrh_ref[...]
    logits = (jnp.dot(n3h, wrh, preferred_element_type=F32)
              + jnp.dot(n3l, wrh, preferred_element_type=F32)
              + jnp.dot(n3h, wrl_ref[...], preferred_element_type=F32)
              + br_ref[...])
    lane = lax.broadcasted_iota(I32, (tm, N_EXPERTS), 1).astype(F32)
    vals, idxs, hots = [], [], []
    cur = logits
    for _ in range(TOP_K):
        mx = jnp.max(cur, axis=-1, keepdims=True)
        ix = jnp.min(jnp.where(cur == mx, lane, float(N_EXPERTS)), axis=-1, keepdims=True)
        hot = lane == ix
        vals.append(mx)
        idxs.append(ix)
        hots.append(hot)
        cur = jnp.where(hot, -jnp.inf, cur)
    es = [jnp.exp(v - vals[0]) for v in vals]
    den = es[0] + es[1] + es[2] + es[3]
    col = lax.broadcasted_iota(I32, (tm, TOP_K), 1)

    def pack(cols):
        out = jnp.broadcast_to(cols[TOP_K - 1], (tm, TOP_K))
        for k in range(TOP_K - 2, -1, -1):
            out = jnp.where(col == k, cols[k], out)
        return out

    idx_ref[...] = pack(idxs).astype(I32)
    tw_ref[...] = pack([e / den for e in es])

    cmat = (hots[0] | hots[1] | hots[2] | hots[3]).astype(F32)
    rr = lax.broadcasted_iota(I32, (tm, tm), 0)
    cc = lax.broadcasted_iota(I32, (tm, tm), 1)
    tri = (cc < rr).astype(BF16)
    before = jnp.dot(tri, cmat.astype(BF16), preferred_element_type=F32) + carry_ref[...]
    ranks = [jnp.sum(jnp.where(hot, before, 0.0), axis=-1, keepdims=True) for hot in hots]
    rank_ref[...] = pack(ranks).astype(I32)
    carry = carry_ref[...] + jnp.sum(cmat, axis=0, keepdims=True)
    carry_ref[...] = carry
    cnt_ref[...] = carry


def _cross(n2, h1, kv, wq, wo, gm, wrh, wrl, br, *, batch, tm, name="cross"):
    n, d = h1.shape
    per_b = n // batch // tm
    mem_len = kv.shape[0] // batch

    def const(shape):
        return pl.BlockSpec(shape, lambda bi, i: (0, 0), pipeline_mode=pl.Buffered(1))

    def row(bi, i):
        return (bi * per_b + i, 0)

    return pl.pallas_call(
        functools.partial(_cross_kernel, tm=tm),
        out_shape=(
            jax.ShapeDtypeStruct((n, d), F32),
            jax.ShapeDtypeStruct((n, d), F32),
            jax.ShapeDtypeStruct((n, TOP_K), I32),
            jax.ShapeDtypeStruct((n, TOP_K), F32),
            jax.ShapeDtypeStruct((n, TOP_K), I32),
            jax.ShapeDtypeStruct((1, N_EXPERTS), F32),
        ),
        grid=(batch, per_b),
        in_specs=[
            pl.BlockSpec((tm, d), row),
            pl.BlockSpec((tm, d), row),
            pl.BlockSpec((mem_len, 2 * CROSS_WIDTH), lambda bi, i: (bi, 0)),
            const((d, CROSS_WIDTH)), const((CROSS_WIDTH, d)), const((1, d)),
            const((d, N_EXPERTS)), const((d, N_EXPERTS)), const((1, N_EXPERTS)),
        ],
        out_specs=(
            pl.BlockSpec((tm, d), row),
            pl.BlockSpec((tm, d), row),
            pl.BlockSpec((tm, TOP_K), row),
            pl.BlockSpec((tm, TOP_K), row),
            pl.BlockSpec((tm, TOP_K), row),
            pl.BlockSpec((1, N_EXPERTS), lambda bi, i: (0, 0)),
        ),
        scratch_shapes=[pltpu.VMEM((1, N_EXPERTS), F32)],
        compiler_params=_cparams(("arbitrary", "arbitrary")),
        name=name,
    )(n2, h1, kv, wq, wo, gm.reshape(1, d), wrh, wrl, br.reshape(1, N_EXPERTS))


_HI_MASK = 0xFFFF0000


def _pack_rows(x):
    half = x.shape[1] // 2
    lo = lax.bitcast_convert_type(x[:, :half].astype(BF16).astype(F32), U32) >> 16
    hi = lax.bitcast_convert_type(x[:, half:].astype(BF16).astype(F32), U32) & jnp.uint32(_HI_MASK)
    return lo | hi


def _unpack_rows(u):
    lo = lax.bitcast_convert_type(u << 16, F32)
    hi = lax.bitcast_convert_type(u & jnp.uint32(_HI_MASK), F32)
    return lo, hi


def _slab(ref, row):
    return ref.at[pl.ds(pl.multiple_of(row * MOE_SLAB, MOE_SLAB), MOE_SLAB)]


def _row_copies(make_copy, make_group_wait, count):
    ngroup = count // DMA_WAIT_ROWS

    def group(c, carry):
        def one(j, carry2):
            make_copy(c * DMA_WAIT_ROWS + j).start()
            return carry2

        lax.fori_loop(0, DMA_WAIT_ROWS, one, 0, unroll=DMA_ISSUE_UNROLL)

        @pl.when(c > 0)
        def _():
            make_group_wait().wait()

        return carry

    lax.fori_loop(0, ngroup, group, 0)
    make_group_wait().wait()


def _dispatch_kernel(slot_ref, n3_ref, xs_hbm, stage, sem, *, real_steps, tm):
    step = pl.program_id(0)
    group_rows = DMA_WAIT_ROWS * MOE_SLAB

    def group_wait():
        return pltpu.make_async_copy(stage.at[pl.ds(0, group_rows)], xs_hbm.at[pl.ds(0, group_rows)], sem)

    @pl.when(step < real_steps)
    def _():
        u = _pack_rows(n3_ref[...])
        for j in range(MOE_SLAB):
            stage[pl.ds(j, tm, stride=MOE_SLAB), :] = u[:, j * V7X_LANES:(j + 1) * V7X_LANES]

        def make_copy(a):
            return pltpu.make_async_copy(_slab(stage, a >> TOP_K_LOG2), _slab(xs_hbm, slot_ref[0, a]), sem)

        _row_copies(make_copy, group_wait, tm * TOP_K)

    @pl.when(step >= real_steps)
    def _():
        stage[0:MOE_SLAB, :] = jnp.zeros((MOE_SLAB, V7X_LANES), U32)

        def make_copy(a):
            return pltpu.make_async_copy(_slab(stage, 0), _slab(xs_hbm, slot_ref[0, a]), sem)

        _row_copies(make_copy, group_wait, tm * TOP_K)


def _dispatch(slots, n3, *, rows_total, tm, name="moe_dispatch"):
    n, d = n3.shape
    real_steps = n // tm
    return pl.pallas_call(
        functools.partial(_dispatch_kernel, real_steps=real_steps, tm=tm),
        out_shape=jax.ShapeDtypeStruct((rows_total * MOE_SLAB, V7X_LANES), U32),
        grid=(slots.shape[0],),
        in_specs=[
            pl.BlockSpec((None, 1, tm * TOP_K), lambda i: (i, 0, 0), memory_space=pltpu.SMEM),
            pl.BlockSpec((tm, d), lambda i: (jnp.minimum(i, real_steps - 1), 0)),
        ],
        out_specs=pl.BlockSpec(memory_space=pl.ANY),
        scratch_shapes=[pltpu.VMEM((tm * MOE_SLAB, V7X_LANES), U32), pltpu.SemaphoreType.DMA],
        compiler_params=_cparams(("arbitrary",)),
        name=name,
    )(slots, n3)


def _moe_kernel(te_ref, tv_ref, tstart_ref, trows_ref,
                xs_hbm, w1_ref, b1_ref, w2_ref, b2_ref, perm_ref, yb_hbm,
                stage, x16, acc, w1p, sems, *, nf):
    i = pl.program_id(0)
    f = pl.program_id(1)
    rows = trows_ref[i]
    start = tstart_ref[i]
    nchunk = rows // MOE_ROW_ALIGN
    chunk_slab_rows = MOE_ROW_ALIGN * MOE_SLAB
    half = 2 * V7X_LANES
    nhalf = 2 * MOE_TF // half

    def slot_rows(c):
        return pl.multiple_of((c & 1) * chunk_slab_rows, chunk_slab_rows)

    def hbm_rows(ref, c):
        r0 = pl.multiple_of((start + c * MOE_ROW_ALIGN) * MOE_SLAB, chunk_slab_rows)
        return ref.at[pl.ds(r0, chunk_slab_rows)]

    def in_copy(c):
        return pltpu.make_async_copy(hbm_rows(xs_hbm, c), stage.at[pl.ds(slot_rows(c), chunk_slab_rows)],
                                     sems.at[c & 1])

    def out_copy(c):
        return pltpu.make_async_copy(stage.at[pl.ds(slot_rows(c), chunk_slab_rows)], hbm_rows(yb_hbm, c),
                                     sems.at[c & 1])

    @pl.when((rows > 0) & (f == 0))
    def _():
        in_copy(0).start()

        def land(c, carry):
            @pl.when(c + 1 < nchunk)
            def _():
                in_copy(c + 1).start()

            in_copy(c).wait()
            r0 = pl.multiple_of(c * MOE_ROW_ALIGN, MOE_ROW_ALIGN)
            s0 = slot_rows(c)
            for j in range(MOE_SLAB):
                lo, hi = _unpack_rows(stage[pl.ds(s0 + j, MOE_ROW_ALIGN, stride=MOE_SLAB), :])
                x16[pl.ds(r0, MOE_ROW_ALIGN), j * V7X_LANES:(j + 1) * V7X_LANES] = lo.astype(BF16)
                x16[pl.ds(r0, MOE_ROW_ALIGN), (MOE_SLAB + j) * V7X_LANES:(MOE_SLAB + j + 1) * V7X_LANES] = (
                    hi.astype(BF16))
            acc[pl.ds(r0, MOE_ROW_ALIGN), :] = jnp.zeros((MOE_ROW_ALIGN, D_MODEL), F32)
            return carry

        lax.fori_loop(0, nchunk, land, 0)

    @pl.when(rows > 0)
    def _():
        perm = perm_ref[...]
        for c in range(nhalf):
            wb = w1_ref[:, c * half:(c + 1) * half].astype(BF16)
            w1p[:, c * half:(c + 1) * half] = jnp.dot(wb, perm, preferred_element_type=F32).astype(BF16)
        w2 = w2_ref[...].astype(BF16)
        b1 = b1_ref[...]

        def mlp(c0, nc):
            r0 = pl.multiple_of(c0 * MOE_ROW_ALIGN, MOE_ROW_ALIGN)
            nr = nc * MOE_ROW_ALIGN
            x = x16[pl.ds(r0, nr), :]
            h = jnp.dot(x, w1p[...], preferred_element_type=F32) + b1
            parts = []
            for c in range(nhalf):
                hg = jnp.minimum(h[:, c * half:c * half + V7X_LANES], SWIGLU_LIMIT)
                hl = jnp.clip(h[:, c * half + V7X_LANES:(c + 1) * half], -SWIGLU_LIMIT, SWIGLU_LIMIT)
                parts.append(hg * jax.nn.sigmoid(SWIGLU_ALPHA * hg) * (hl + 1.0))
            a = jnp.concatenate(parts, axis=-1).astype(BF16)
            acc[pl.ds(r0, nr), :] += jnp.dot(a, w2, preferred_element_type=F32)

        nquad = nchunk // MOE_BLOCK_CHUNKS

        def quad(c, carry):
            mlp(c * MOE_BLOCK_CHUNKS, MOE_BLOCK_CHUNKS)
            return carry

        lax.fori_loop(0, nquad, quad, 0)
        size = MOE_BLOCK_CHUNKS // 2
        while size >= 1:
            done = nchunk & ~(2 * size - 1)

            @pl.when((nchunk & size) != 0)
            def _(done=done, size=size):
                mlp(done, size)

            size //= 2

    @pl.when((rows > 0) & (f == nf - 1))
    def _():
        def emit(c, carry):
            @pl.when(c >= 2)
            def _():
                out_copy(c - 2).wait()

            r0 = pl.multiple_of(c * MOE_ROW_ALIGN, MOE_ROW_ALIGN)
            s0 = slot_rows(c)
            u = _pack_rows(acc[pl.ds(r0, MOE_ROW_ALIGN), :] + b2_ref[...])
            for j in range(MOE_SLAB):
                stage[pl.ds(s0 + j, MOE_ROW_ALIGN, stride=MOE_SLAB), :] = u[:, j * V7X_LANES:(j + 1) * V7X_LANES]
            out_copy(c).start()
            return carry

        lax.fori_loop(0, nchunk, emit, 0)

        @pl.when(nchunk >= 2)
        def _():
            out_copy(nchunk - 2).wait()

        out_copy(nchunk - 1).wait()


def _moe(tile_e, tile_v, tile_start, tile_rows, xs, w1, b1p, w2, b2, perm, *, name="moe_mlp"):
    nt = tile_e.shape[0]
    nf = D_FF // MOE_TF

    def w1_map(i, f, te, tv, ts, tr):
        return (te[i], 0, jnp.where(tv[i] > 0, f, nf - 1))

    def w2_map(i, f, te, tv, ts, tr):
        return (te[i], jnp.where(tv[i] > 0, f, nf - 1), 0)

    def b2_map(i, f, te, tv, ts, tr):
        return (te[i], 0, 0)

    grid_spec = pltpu.PrefetchScalarGridSpec(
        num_scalar_prefetch=4,
        grid=(nt, nf),
        in_specs=[
            pl.BlockSpec(memory_space=pl.ANY),
            pl.BlockSpec((None, D_MODEL, 2 * MOE_TF), w1_map),
            pl.BlockSpec((None, 1, 2 * MOE_TF), w1_map),
            pl.BlockSpec((None, MOE_TF, D_MODEL), w2_map),
            pl.BlockSpec((None, 1, D_MODEL), b2_map),
            pl.BlockSpec((2 * V7X_LANES, 2 * V7X_LANES), lambda i, f, te, tv, ts, tr: (0, 0)),
        ],
        out_specs=pl.BlockSpec(memory_space=pl.ANY),
        scratch_shapes=[
            pltpu.VMEM((2 * MOE_ROW_ALIGN * MOE_SLAB, V7X_LANES), U32),
            pltpu.VMEM((MOE_TMAX, D_MODEL), BF16),
            pltpu.VMEM((MOE_TMAX, D_MODEL), F32),
            pltpu.VMEM((D_MODEL, 2 * MOE_TF), BF16),
            pltpu.SemaphoreType.DMA((2,)),
        ],
    )
    return pl.pallas_call(
        functools.partial(_moe_kernel, nf=nf),
        out_shape=jax.ShapeDtypeStruct(xs.shape, xs.dtype),
        grid_spec=grid_spec,
        input_output_aliases={4: 0},
        compiler_params=_cparams(("arbitrary", "arbitrary")),
        name=name,
    )(tile_e, tile_v, tile_start, tile_rows, xs, w1, b1p.reshape(N_EXPERTS, 1, 2 * D_FF), w2,
      b2.reshape(N_EXPERTS, 1, D_MODEL), perm)


def _combine_kernel(slot_ref, h2_ref, tw_ref, gf_ref, yb_hbm, o_ref, buf, sem, *, tm):
    group_rows = DMA_WAIT_ROWS * MOE_SLAB

    def make_copy(a):
        t = a >> TOP_K_LOG2
        k = a & (TOP_K - 1)
        return pltpu.make_async_copy(_slab(yb_hbm, slot_ref[0, a]), _slab(buf, k * tm + t), sem)

    def group_wait():
        return pltpu.make_async_copy(yb_hbm.at[pl.ds(0, group_rows)], buf.at[pl.ds(0, group_rows)], sem)

    _row_copies(make_copy, group_wait, tm * TOP_K)

    tw = tw_ref[...]
    cols_lo, cols_hi = [], []
    for j in range(MOE_SLAB):
        clo = chi = None
        for k in range(TOP_K):
            lo, hi = _unpack_rows(buf[pl.ds(k * tm * MOE_SLAB + j, tm, stride=MOE_SLAB), :])
            w = tw[:, k:k + 1]
            clo = w * lo if clo is None else clo + w * lo
            chi = w * hi if chi is None else chi + w * hi
        cols_lo.append(clo)
        cols_hi.append(chi)
    h3 = h2_ref[...] + jnp.concatenate(cols_lo + cols_hi, axis=-1)
    o_ref[...] = _rms(h3, gf_ref[...])


def _combine(slots, h2, tw, gf, yb, *, tm, name="moe_combine"):
    n, d = h2.shape
    return pl.pallas_call(
        functools.partial(_combine_kernel, tm=tm),
        out_shape=jax.ShapeDtypeStruct((n, d), F32),
        grid=(n // tm,),
        in_specs=[
            pl.BlockSpec((None, 1, tm * TOP_K), lambda i: (i, 0, 0), memory_space=pltpu.SMEM),
            pl.BlockSpec((tm, d), lambda i: (i, 0)),
            pl.BlockSpec((tm, TOP_K), lambda i: (i, 0)),
            pl.BlockSpec((1, d), lambda i: (0, 0)),
            pl.BlockSpec(memory_space=pl.ANY),
        ],
        out_specs=pl.BlockSpec((tm, d), lambda i: (i, 0)),
        scratch_shapes=[
            pltpu.VMEM((tm * TOP_K * MOE_SLAB, V7X_LANES), U32),
            pltpu.SemaphoreType.DMA,
        ],
        compiler_params=_cparams(("arbitrary",)),
        name=name,
    )(slots, h2, tw, gf.reshape(1, d), yb)


def _rope_tables(positions):
    b, s = positions.shape
    half = HEAD_DIM // 2
    inv_freq = 1.0 / (ROPE_THETA ** (jnp.arange(half, dtype=F32) / half))
    ang = positions.astype(F32)[..., None] * inv_freq
    cos = jnp.cos(ang)
    sin = jnp.sin(ang)
    cs = jnp.concatenate([cos, cos], axis=-1)
    sn = jnp.concatenate([-sin, sin], axis=-1)

    def deint(a):
        a = a.reshape(b, s // ATT_TILE, WIN_BLOCK, DIL_MAX, HEAD_DIM)
        return a.transpose(0, 1, 3, 2, 4)

    return deint(cs), deint(sn)


def _routing(top_e, rank, counts, n_tok):
    cnt = counts.reshape(N_EXPERTS).astype(I32)
    padded = (cnt + MOE_ROW_ALIGN - 1) // MOE_ROW_ALIGN * MOE_ROW_ALIGN
    gend = jnp.cumsum(padded)
    gstart = gend - padded
    slot = (gstart[top_e] + rank).reshape(-1)

    n_assign = n_tok * TOP_K
    n_unused = N_EXPERTS * MOE_ROW_ALIGN
    rows_total = n_assign + n_unused
    padcnt = padded - cnt
    pend = jnp.cumsum(padcnt)
    pstart = pend - padcnt
    def first_above(ends, q):
        return jnp.minimum(jnp.sum((ends[None, :] <= q[:, None]).astype(I32), axis=1), N_EXPERTS - 1)

    p = jnp.arange(n_unused, dtype=I32)
    ep = first_above(pend, p)
    unused = jnp.where(p < pend[-1], gstart[ep] + cnt[ep] + (p - pstart[ep]), gend[-1] + (p - pend[-1]))

    main_rows = jnp.minimum(padded, MOE_TMAX)
    main = (jnp.arange(N_EXPERTS, dtype=I32), (main_rows > 0).astype(I32), gstart.astype(I32),
            main_rows.astype(I32))

    over = padded - main_rows
    nt_e = (over + MOE_TMAX - 1) // MOE_TMAX
    tend = jnp.cumsum(nt_e)
    tstart = tend - nt_e
    n_over = tend[-1]
    ti = jnp.arange(rows_total // MOE_TMAX, dtype=I32)
    valid = ti < n_over
    tic = jnp.clip(ti, 0, jnp.maximum(n_over - 1, 0))
    te = first_above(tend, tic)
    local = tic - tstart[te]
    row0 = gstart[te] + (local + 1) * MOE_TMAX
    rows = jnp.where(valid, jnp.clip(over[te] - local * MOE_TMAX, 0, MOE_TMAX), 0)
    overflow = (te, valid.astype(I32), row0.astype(I32), rows.astype(I32))
    return slot.astype(I32), unused.astype(I32), rows_total, main, overflow, n_over > 0


def kernel(x, mem, positions, norm_mix_g, w_in, lb_raw, hgrn_norm_g, w_br_hgrn, w_br_attn, w_out,
           norm_cross_g, norm_mem_g, w_cq, w_ckv, w_co, norm_moe_g, w_router, b_router,
           w_mlp1, b_mlp1, w_mlp2, b_mlp2, norm_final_g):
    bsz, seq, d = x.shape
    assert w_in.shape[0] == 1 and d == D_MODEL and seq % ATT_TILE == 0
    n_tok = bsz * seq
    lower_bounds = jnp.cumsum(jax.nn.softmax(lb_raw.astype(F32), axis=0), axis=0)
    cs, sn = _rope_tables(positions)

    wl = w_in[0].astype(BF16)
    c_h = 4 * HG_WIDTH
    c_a = c_h + ATT_Q_WIDTH + 2 * ATT_KV_WIDTH
    x2d = x.reshape(n_tok, d)
    proj_h = _in_proj(x2d, norm_mix_g[0], wl, tm=1024, tn=1024, name="in_proj_hgrn", col0=0, ncols=c_h)
    proj_g = _in_proj(x2d, norm_mix_g[0], wl, tm=1024, tn=1024, name="in_proj_gate", col0=c_a,
                      ncols=2 * D_MODEL)
    pa = _in_proj_strided(x, norm_mix_g[0], wl, c=8, tn=1024, name="in_proj_attn", col0=c_h, ncols=c_a - c_h)

    o_h = _hgrn(proj_h.reshape(bsz, seq, c_h), lower_bounds[0], hgrn_norm_g[0], ts=512)
    o_a = _dilated_attn(pa, cs, sn)

    h1, n2 = _merge_out(o_h.reshape(n_tok, HG_WIDTH), o_a.reshape(n_tok, ATT_KV_WIDTH), proj_g, x2d,
                        w_br_hgrn[0].astype(BF16), w_br_attn[0].astype(BF16), w_out[0].astype(BF16),
                        norm_cross_g[0], tm=256)

    mem2d = mem.reshape(-1, d)
    kv = _in_proj(mem2d, norm_mem_g, w_ckv[0].astype(BF16), tm=mem2d.shape[0], tn=2 * CROSS_WIDTH, name="mem_kv")
    wrh = w_router[0].astype(BF16)
    wrl = (w_router[0] - wrh.astype(F32)).astype(BF16)
    h2, n3, top_e, top_w, rank, counts = _cross(
        n2, h1, kv, w_cq[0].astype(BF16), w_co[0].astype(BF16), norm_moe_g[0], wrh, wrl, b_router[0],
        batch=bsz, tm=512)

    slot, unused, rows_total, main_tiles, over_tiles, has_over = _routing(top_e, rank, counts, n_tok)
    tm_d = DISPATCH_TOKENS
    slots_all = jnp.concatenate([slot, unused]).reshape(-1, 1, tm_d * TOP_K)
    xs = _dispatch(slots_all, n3, rows_total=rows_total, tm=tm_d)

    ii = jnp.arange(2 * V7X_LANES)
    src = jnp.where(ii < V7X_LANES, 2 * ii, 2 * (ii - V7X_LANES) + 1)
    perm = (ii[:, None] == src[None, :]).astype(BF16)
    b1p = b_mlp1[0].reshape(N_EXPERTS, -1, V7X_LANES, 2).transpose(0, 1, 3, 2).reshape(N_EXPERTS, 2 * D_FF)
    def run_moe(tiles, rows_buf, name):
        return _moe(*tiles, rows_buf, w_mlp1[0], b1p, w_mlp2[0], b_mlp2[0], perm, name=name)

    yb = run_moe(main_tiles, xs, "moe_mlp")
    yb = lax.cond(has_over, lambda rows_buf: run_moe(over_tiles, rows_buf, "moe_mlp_overflow"),
                  lambda rows_buf: rows_buf, yb)

    tm_c = 256
    out = _combine(slot.reshape(-1, 1, tm_c * TOP_K), h2, top_w, norm_final_g, yb, tm=tm_c)
    return out.reshape(bsz, seq, d)
```

```python
import functools

import jax
import jax.numpy as jnp
from jax import lax
from jax.experimental import pallas as pl
from jax.experimental.pallas import tpu as pltpu

F32 = jnp.float32
BF16 = jnp.bfloat16
I32 = jnp.int32
U32 = jnp.uint32

D_MODEL = 2048
HEAD_DIM = 128
HG_HEADS = 8
HG_WIDTH = HG_HEADS * HEAD_DIM
N_KV_HEADS = 8
N_GROUPS = 3
ATT_Q_WIDTH = N_GROUPS * N_KV_HEADS * HEAD_DIM
ATT_KV_WIDTH = N_KV_HEADS * HEAD_DIM
WIN_BLOCK = 128
ROPE_THETA = 10000.0
CROSS_HEADS = 4
CROSS_WIDTH = CROSS_HEADS * HEAD_DIM
N_EXPERTS = 32
TOP_K = 4
TOP_K_LOG2 = 2
D_FF = D_MODEL
SWIGLU_ALPHA = 1.702
SWIGLU_LIMIT = 7.0
NORM_EPS = 1e-6

V7X_LANES = 128
V7X_VMEM_LIMIT_BYTES = 56 * 1024 * 1024

DIL_MAX = 16
ATT_TILE = DIL_MAX * WIN_BLOCK
ATT_UNROLL = 8
HG_BLOCK = 16
HG_GROUP = 16
NEG_BIG = -1e30

MOE_TMAX = 2048
MOE_ROW_ALIGN = 128
MOE_BLOCK_CHUNKS = 8
MOE_TF = 256
MOE_SLAB = D_MODEL // 2 // V7X_LANES
DISPATCH_TOKENS = 256
COMBINE_TOKENS = 256
DMA_WAIT_ROWS = 128
DMA_ISSUE_UNROLL = 8


def _rms(x, g):
    ms = jnp.mean(x * x, axis=-1, keepdims=True)
    return x * lax.rsqrt(ms + NORM_EPS) * g


def _cparams(sem, vmem=V7X_VMEM_LIMIT_BYTES):
    return pltpu.CompilerParams(dimension_semantics=sem, vmem_limit_bytes=vmem)


def _in_proj_kernel(x_ref, g_ref, w_ref, o_ref, xn_ref):
    @pl.when(pl.program_id(1) == 0)
    def _():
        xn_ref[...] = _rms(x_ref[...], g_ref[...]).astype(BF16)

    o_ref[...] = jnp.dot(xn_ref[...], w_ref[...], preferred_element_type=F32).astype(o_ref.dtype)


def _in_proj(x2d, g, w, *, tm, tn, name, col0=0, ncols=None):
    n, d = x2d.shape
    wc = w.shape[1] if ncols is None else ncols
    cb = col0 // tn
    return pl.pallas_call(
        _in_proj_kernel,
        out_shape=jax.ShapeDtypeStruct((n, wc), BF16),
        grid=(n // tm, wc // tn),
        in_specs=[
            pl.BlockSpec((tm, d), lambda i, j: (i, 0)),
            pl.BlockSpec((1, d), lambda i, j: (0, 0)),
            pl.BlockSpec((d, tn), lambda i, j: (0, cb + j)),
        ],
        out_specs=pl.BlockSpec((tm, tn), lambda i, j: (i, j)),
        scratch_shapes=[pltpu.VMEM((tm, d), BF16)],
        compiler_params=_cparams(("parallel", "arbitrary")),
        name=name,
    )(x2d, g.reshape(1, d), w)


def _in_proj_strided_kernel(x_ref, g_ref, w_ref, o_ref, xn_ref, *, c):
    @pl.when(pl.program_id(3) == 0)
    def _():
        for ci in range(c):
            x = x_ref[:, ci * D_MODEL:(ci + 1) * D_MODEL]
            xn_ref[ci * WIN_BLOCK:(ci + 1) * WIN_BLOCK, :] = _rms(x, g_ref[...]).astype(BF16)

    res = jnp.dot(xn_ref[...], w_ref[...], preferred_element_type=F32)
    for ci in range(c):
        o_ref[ci] = res[ci * WIN_BLOCK:(ci + 1) * WIN_BLOCK, :].astype(o_ref.dtype)


def _in_proj_strided(x, g, w, *, c, tn, name, col0, ncols):
    b, s, d = x.shape
    nt = s // ATT_TILE
    wc = ncols
    cb = col0 // tn
    xv = x.reshape(b, nt, WIN_BLOCK, DIL_MAX * d)
    return pl.pallas_call(
        functools.partial(_in_proj_strided_kernel, c=c),
        out_shape=jax.ShapeDtypeStruct((b, nt, DIL_MAX, WIN_BLOCK, wc), BF16),
        grid=(b, nt, DIL_MAX // c, wc // tn),
        in_specs=[
            pl.BlockSpec((None, None, WIN_BLOCK, c * d), lambda bi, ti, ri, j: (bi, ti, 0, ri)),
            pl.BlockSpec((1, d), lambda bi, ti, ri, j: (0, 0)),
            pl.BlockSpec((d, tn), lambda bi, ti, ri, j: (0, cb + j)),
        ],
        out_specs=pl.BlockSpec((None, None, c, WIN_BLOCK, tn), lambda bi, ti, ri, j: (bi, ti, ri, 0, j)),
        scratch_shapes=[pltpu.VMEM((c * WIN_BLOCK, d), BF16)],
        compiler_params=_cparams(("parallel", "parallel", "parallel", "arbitrary")),
        name=name,
    )(xv, g.reshape(1, d), w)


def _hgrn_kernel(q_ref, f_ref, i_ref, g_ref, lb_ref, gn_ref, o_ref, st_ref, kin_s, b_s, v_s, *, ts):
    @pl.when(pl.program_id(2) == 0)
    def _():
        st_ref[...] = jnp.zeros_like(st_ref)

    lb = lb_ref[...]
    oml = 1.0 - lb
    gn = gn_ref[...]
    half = HG_BLOCK // 2
    row = lax.broadcasted_iota(I32, (HG_BLOCK, HEAD_DIM), 0)
    row8 = lax.broadcasted_iota(I32, (half, HEAD_DIM), 0)
    nt_dims = (((1,), (1,)), ((), ()))
    tn_dims = (((0,), (0,)), ((), ()))

    def front(g, t0):
        sl = pl.ds(t0 + g * HG_BLOCK, HG_BLOCK)
        q = q_ref[sl, :].astype(F32)
        hf = f_ref[sl, :].astype(F32)
        v = i_ref[sl, :].astype(F32)
        kin = oml * jax.nn.sigmoid(-hf)
        b = jnp.log2(lb + oml * jax.nn.sigmoid(hf))
        for sh in (1, 2, 4, 8):
            b = b + jnp.where(row >= sh, pltpu.roll(b, sh, 0), 0.0)
        kin_s[g] = kin
        b_s[g] = b
        v_s[g] = v
        q_lo, q_hi = q[:half], q[half:]
        b_lo, b_hi = b[:half], b[half:]
        o_lo = jnp.zeros((half, HEAD_DIM), F32)
        o_hi = jnp.zeros((half, HEAD_DIM), F32)
        for s in range(HG_BLOCK):
            ks = kin_s[g, s:s + 1, :]
            bs = b_s[g, s:s + 1, :]
            vs = v_s[g, s:s + 1, :]
            if s < half:
                w = q_lo * ks * jnp.exp2(b_lo - bs)
                if s > 0:
                    w = jnp.where(row8 >= s, w, 0.0)
                o_lo = o_lo + jnp.sum(w, axis=-1, keepdims=True) * vs
                w = q_hi * ks * jnp.exp2(b_hi - bs)
            else:
                w = q_hi * ks * jnp.exp2(b_hi - bs)
                if s > half:
                    w = jnp.where(row8 >= s - half, w, 0.0)
            o_hi = o_hi + jnp.sum(w, axis=-1, keepdims=True) * vs
        bl = b_s[g, HG_BLOCK - 1:HG_BLOCK, :]
        qd = (q * jnp.exp2(b)).astype(BF16)
        kd = (kin * jnp.exp2(bl - b)).astype(BF16)
        upd = lax.dot_general(v.astype(BF16), kd, tn_dims, preferred_element_type=F32)
        return jnp.concatenate([o_lo, o_hi], axis=0), qd, upd, jnp.exp2(bl)

    def body(i, carry):
        t0 = pl.multiple_of(i * (HG_GROUP * HG_BLOCK), HG_GROUP * HG_BLOCK)
        fronts = [front(g, t0) for g in range(HG_GROUP)]
        st = st_ref[...]
        for g, (o_diag, qd, upd, dec) in enumerate(fronts):
            o = o_diag + lax.dot_general(qd, st.astype(BF16), nt_dims, preferred_element_type=F32)
            st = st * dec + upd
            sl = pl.ds(t0 + g * HG_BLOCK, HG_BLOCK)
            hg = g_ref[sl, :].astype(F32)
            o_ref[sl, :] = (_rms(o, gn) * (hg * jax.nn.sigmoid(hg))).astype(o_ref.dtype)
        st_ref[...] = st
        return carry

    lax.fori_loop(0, ts // (HG_GROUP * HG_BLOCK), body, 0)


def _hgrn(proj_h, lb, gn, *, ts, name="hgrn"):
    b, s, _ = proj_h.shape
    h = HG_HEADS

    def spec(off):
        return pl.BlockSpec((None, ts, HEAD_DIM), lambda bi, hi, si: (bi, si, off + hi))

    vec = pl.BlockSpec((1, HEAD_DIM), lambda bi, hi, si: (0, hi))
    return pl.pallas_call(
        functools.partial(_hgrn_kernel, ts=ts),
        out_shape=jax.ShapeDtypeStruct((b, s, HG_WIDTH), BF16),
        grid=(b, h, s // ts),
        in_specs=[spec(0), spec(h), spec(2 * h), spec(3 * h), vec, vec],
        out_specs=pl.BlockSpec((None, ts, HEAD_DIM), lambda bi, hi, si: (bi, si, hi)),
        scratch_shapes=[
            pltpu.VMEM((HEAD_DIM, HEAD_DIM), F32),
            pltpu.VMEM((HG_GROUP, HG_BLOCK, HEAD_DIM), F32),
            pltpu.VMEM((HG_GROUP, HG_BLOCK, HEAD_DIM), F32),
            pltpu.VMEM((HG_GROUP, HG_BLOCK, HEAD_DIM), F32),
        ],
        compiler_params=_cparams(("parallel", "parallel", "arbitrary")),
        name=name,
    )(proj_h, proj_h, proj_h, proj_h, lb.reshape(1, HG_WIDTH), gn.reshape(1, HG_WIDTH))


def _attn_bias(kind):
    rq = lax.broadcasted_iota(I32, (WIN_BLOCK, 2 * WIN_BLOCK), 0)
    ck = lax.broadcasted_iota(I32, (WIN_BLOCK, 2 * WIN_BLOCK), 1)
    if kind == 2:
        dist = rq + WIN_BLOCK - ck
        first = ck < WIN_BLOCK
    elif kind == 1:
        dist = 4 * ((rq & 31) - (ck & 63) + 32) + ((rq >> 5) - (ck >> 6))
        first = (ck & 63) < 32
    else:
        dist = 16 * ((rq & 7) - (ck & 15) + 8) + ((rq >> 3) - (ck >> 4))
        first = (ck & 15) < 8
    valid = (dist >= 0) & (dist <= WIN_BLOCK)
    return (jnp.where(valid, 0.0, NEG_BIG).astype(F32),
            jnp.where(valid & jnp.logical_not(first), 0.0, NEG_BIG).astype(F32))


def _attn_kernel(q0_ref, q1_ref, q2_ref, k_ref, v_ref, cs_ref, sn_ref, o_ref,
                 qr, kext, vext, acc, mrun, lrun, bias, onat):
    ti = pl.program_id(2)
    wb = WIN_BLOCK
    scale = HEAD_DIM ** -0.5

    @pl.when(ti == 0)
    def _():
        kext[:, 0:wb, :] = jnp.zeros((DIL_MAX, wb, HEAD_DIM), F32)
        vext[:, 0:wb, :] = jnp.zeros((DIL_MAX, wb, HEAD_DIM), F32)

    @pl.when(ti > 0)
    def _():
        kext[:, 0:wb, :] = kext[:, wb:2 * wb, :]
        vext[:, 0:wb, :] = vext[:, wb:2 * wb, :]

    for kind in range(N_GROUPS):
        full, nofirst = _attn_bias(kind)
        bias[2 * kind] = full
        bias[2 * kind + 1] = nofirst

    def rope_body(r, carry):
        cs = cs_ref[r]
        sn = sn_ref[r]
        for g, qref in enumerate((q0_ref, q1_ref, q2_ref)):
            q = qref[r].astype(F32)
            qr[g, r] = (q * cs + pltpu.roll(q, HEAD_DIM // 2, 1) * sn) * scale
        k = k_ref[r].astype(F32)
        kext[r, wb:2 * wb, :] = k * cs + pltpu.roll(k, HEAD_DIM // 2, 1) * sn
        vext[r, wb:2 * wb, :] = v_ref[r].astype(F32)
        acc[r] = jnp.zeros((wb, HEAD_DIM), F32)
        mrun[r] = jnp.full((wb, HEAD_DIM), NEG_BIG, F32)
        lrun[r] = jnp.zeros((wb, HEAD_DIM), F32)
        return carry

    lax.fori_loop(0, DIL_MAX, rope_body, 0)

    nt_dims = (((1,), (1,)), ((), ()))

    def block(qb, kb, vb, bias_blk):
        s = lax.dot_general(qb.astype(BF16), kb.astype(BF16), nt_dims, preferred_element_type=F32)
        s = s + bias_blk
        m = jnp.max(s, axis=-1, keepdims=True)
        p = jnp.exp(s - m)
        l = jnp.sum(p, axis=-1, keepdims=True)
        n = jnp.dot(p.astype(BF16), vb.astype(BF16), preferred_element_type=F32)
        return n, jnp.broadcast_to(m, (wb, HEAD_DIM)), jnp.broadcast_to(l, (wb, HEAD_DIM))

    def merge(r, rows, n, m, l):
        m_old = mrun[r, rows, :]
        m_new = jnp.maximum(m_old, m)
        a = jnp.exp(m_old - m_new)
        bb = jnp.exp(m - m_new)
        acc[r, rows, :] = acc[r, rows, :] * a + n * bb
        lrun[r, rows, :] = lrun[r, rows, :] * a + l * bb
        mrun[r, rows, :] = m_new

    first_tile = jnp.where(ti == 0, 1, 0)


    def g2_body(i, carry):
        rs = [i * ATT_UNROLL + u for u in range(ATT_UNROLL)]
        res = [block(qr[2, r], kext[r], vext[r], bias[4 + first_tile]) for r in rs]
        for r, (n, m, l) in zip(rs, res):
            merge(r, pl.ds(0, wb), n, m, l)
        return carry

    lax.fori_loop(0, DIL_MAX // ATT_UNROLL, g2_body, 0)

    def g1_body(i, carry):
        res = []
        for u in range(ATT_UNROLL // 4):
            mb = i * (ATT_UNROLL // 4) + u
            q0 = pl.multiple_of(32 * mb, 32)
            k0 = pl.multiple_of(96 + 32 * mb, 32)
            use_first = jnp.where(mb == 0, first_tile, 0)
            for r4 in range(4):
                qb = jnp.concatenate([qr[1, r4 + 4 * j, pl.ds(q0, 32), :] for j in range(4)], axis=0)
                kb = jnp.concatenate([kext[r4 + 4 * j, pl.ds(k0, 64), :] for j in range(4)], axis=0)
                vb = jnp.concatenate([vext[r4 + 4 * j, pl.ds(k0, 64), :] for j in range(4)], axis=0)
                res.append((r4, q0, block(qb, kb, vb, bias[2 + use_first])))
        for r4, q0, (n, m, l) in res:
            for j in range(4):
                sl = slice(32 * j, 32 * (j + 1))
                merge(r4 + 4 * j, pl.ds(q0, 32), n[sl], m[sl], l[sl])
        return carry

    lax.fori_loop(0, DIL_MAX // ATT_UNROLL, g1_body, 0)

    def g0_body(i, carry):
        res = []
        for u in range(ATT_UNROLL):
            mb = i * ATT_UNROLL + u
            q0 = pl.multiple_of(8 * mb, 8)
            k0 = pl.multiple_of(120 + 8 * mb, 8)
            qb = jnp.concatenate([qr[0, r, pl.ds(q0, 8), :] for r in range(DIL_MAX)], axis=0)
            kb = jnp.concatenate([kext[r, pl.ds(k0, 16), :] for r in range(DIL_MAX)], axis=0)
            vb = jnp.concatenate([vext[r, pl.ds(k0, 16), :] for r in range(DIL_MAX)], axis=0)
            use_first = jnp.where(mb == 0, first_tile, 0)
            res.append((q0, block(qb, kb, vb, bias[use_first])))
        for q0, (n, m, l) in res:
            for r in range(DIL_MAX):
                sl = slice(8 * r, 8 * (r + 1))
                merge(r, pl.ds(q0, 8), n[sl], m[sl], l[sl])
        return carry

    lax.fori_loop(0, DIL_MAX // ATT_UNROLL, g0_body, 0)

    for r in range(DIL_MAX):
        onat[pl.ds(r, wb, stride=DIL_MAX), :] = acc[r] / lrun[r]
    o_ref[...] = onat[...].astype(o_ref.dtype)


def _dilated_attn(pa, cs, sn, *, name="dilated_attn"):
    b, nt = pa.shape[0], pa.shape[1]
    h = N_KV_HEADS

    def spec(off):
        return pl.BlockSpec((None, None, DIL_MAX, WIN_BLOCK, HEAD_DIM),
                            lambda bi, hi, ti: (bi, ti, 0, 0, off + hi))

    tab = pl.BlockSpec((None, None, DIL_MAX, WIN_BLOCK, HEAD_DIM), lambda bi, hi, ti: (bi, ti, 0, 0, 0))
    return pl.pallas_call(
        _attn_kernel,
        out_shape=jax.ShapeDtypeStruct((b, nt * ATT_TILE, ATT_KV_WIDTH), BF16),
        grid=(b, h, nt),
        in_specs=[spec(0), spec(h), spec(2 * h), spec(3 * h), spec(4 * h), tab, tab],
        out_specs=pl.BlockSpec((None, ATT_TILE, HEAD_DIM), lambda bi, hi, ti: (bi, ti, hi)),
        scratch_shapes=[
            pltpu.VMEM((N_GROUPS, DIL_MAX, WIN_BLOCK, HEAD_DIM), F32),
            pltpu.VMEM((DIL_MAX, 2 * WIN_BLOCK, HEAD_DIM), F32),
            pltpu.VMEM((DIL_MAX, 2 * WIN_BLOCK, HEAD_DIM), F32),
            pltpu.VMEM((DIL_MAX, WIN_BLOCK, HEAD_DIM), F32),
            pltpu.VMEM((DIL_MAX, WIN_BLOCK, HEAD_DIM), F32),
            pltpu.VMEM((DIL_MAX, WIN_BLOCK, HEAD_DIM), F32),
            pltpu.VMEM((2 * N_GROUPS, WIN_BLOCK, 2 * WIN_BLOCK), F32),
            pltpu.VMEM((ATT_TILE, HEAD_DIM), F32),
        ],
        compiler_params=_cparams(("arbitrary", "arbitrary", "arbitrary")),
        name=name,
    )(pa, pa, pa, pa, pa, cs, sn)


def _merge_out_kernel(oh_ref, oa_ref, gate_ref, x_ref, wh_ref, wa_ref, wo_ref, gc_ref, h1_ref, n2_ref):
    ga = gate_ref[:, :D_MODEL].astype(F32)
    gb = gate_ref[:, D_MODEL:].astype(F32)
    yh = jnp.dot(oh_ref[...], wh_ref[...], preferred_element_type=F32)
    ya = jnp.dot(oa_ref[...], wa_ref[...], preferred_element_type=F32)
    merged = jax.nn.sigmoid(ga) * yh + jax.nn.sigmoid(gb) * ya
    h1 = x_ref[...] + jnp.dot(merged.astype(BF16), wo_ref[...], preferred_element_type=F32)
    h1_ref[...] = h1
    n2_ref[...] = _rms(h1, gc_ref[...]).astype(BF16)


def _merge_out(oh, oa, gates, x2d, wh, wa, wo, gc, *, tm, name="merge_out"):
    n, d = x2d.shape

    def const(shape):
        return pl.BlockSpec(shape, lambda i: (0, 0), pipeline_mode=pl.Buffered(1))

    return pl.pallas_call(
        _merge_out_kernel,
        out_shape=(jax.ShapeDtypeStruct((n, d), F32), jax.ShapeDtypeStruct((n, d), BF16)),
        grid=(n // tm,),
        in_specs=[
            pl.BlockSpec((tm, HG_WIDTH), lambda i: (i, 0)),
            pl.BlockSpec((tm, ATT_KV_WIDTH), lambda i: (i, 0)),
            pl.BlockSpec((tm, 2 * d), lambda i: (i, 0)),
            pl.BlockSpec((tm, d), lambda i: (i, 0)),
            const((HG_WIDTH, d)), const((ATT_KV_WIDTH, d)), const((d, d)), const((1, d)),
        ],
        out_specs=(pl.BlockSpec((tm, d), lambda i: (i, 0)), pl.BlockSpec((tm, d), lambda i: (i, 0))),
        compiler_params=_cparams(("parallel",)),
        name=name,
    )(oh, oa, gates, x2d, wh, wa, wo, gc.reshape(1, d))


def _cross_kernel(n2_ref, h1_ref, kv_ref, wq_ref, wo_ref, gm_ref, wrh_ref, wrl_ref, br_ref,
                  h2_ref, n3_ref, idx_ref, tw_ref, rank_ref, cnt_ref, carry_ref, *, tm):
    @pl.when((pl.program_id(0) == 0) & (pl.program_id(1) == 0))
    def _():
        carry_ref[...] = jnp.zeros_like(carry_ref)

    nt_dims = (((1,), (1,)), ((), ()))
    scale = HEAD_DIM ** -0.5
    q = (jnp.dot(n2_ref[...], wq_ref[...], preferred_element_type=F32) * scale).astype(BF16)
    outs = []
    for hh in range(CROSS_HEADS):
        sl = slice(hh * HEAD_DIM, (hh + 1) * HEAD_DIM)
        kh = kv_ref[:, sl]
        vh = kv_ref[:, CROSS_WIDTH + hh * HEAD_DIM:CROSS_WIDTH + (hh + 1) * HEAD_DIM]
        s = lax.dot_general(q[:, sl], kh, nt_dims, preferred_element_type=F32)
        p = jnp.exp(s - jnp.max(s, axis=-1, keepdims=True))
        l = jnp.sum(p, axis=-1, keepdims=True)
        outs.append(jnp.dot(p.astype(BF16), vh, preferred_element_type=F32) / l)
    o = jnp.concatenate(outs, axis=-1).astype(BF16)
    h2 = h1_ref[...] + jnp.dot(o, wo_ref[...], preferred_element_type=F32)
    h2_ref[...] = h2
    n3 = _rms(h2, gm_ref[...])
    n3_ref[...] = n3

    n3h = n3.astype(BF16)
    n3l = (n3 - n3h.astype(F32)).astype(BF16)
    wrh = wrh_ref[...]
    logits = (jnp.dot(n3h, wrh, preferred_element_type=F32)
              + jnp.dot(n3l, wrh, preferred_element_type=F32)
              + jnp.dot(n3h, wrl_ref[...], preferred_element_type=F32)
              + br_ref[...])
    lane = lax.broadcasted_iota(I32, (tm, N_EXPERTS), 1).astype(F32)
    vals, idxs, hots = [], [], []
    cur = logits
    for _ in range(TOP_K):
        mx = jnp.max(cur, axis=-1, keepdims=True)
        ix = jnp.min(jnp.where(cur == mx, lane, float(N_EXPERTS)), axis=-1, keepdims=True)
        hot = lane == ix
        vals.append(mx)
        idxs.append(ix)
        hots.append(hot)
        cur = jnp.where(hot, -jnp.inf, cur)
    es = [jnp.exp(v - vals[0]) for v in vals]
    den = es[0] + es[1] + es[2] + es[3]
    col = lax.broadcasted_iota(I32, (tm, TOP_K), 1)

    def pack(cols):
        out = jnp.broadcast_to(cols[TOP_K - 1], (tm, TOP_K))
        for k in range(TOP_K - 2, -1, -1):
            out = jnp.where(col == k, cols[k], out)
        return out

    idx_ref[...] = pack(idxs).astype(I32)
    tw_ref[...] = pack([e / den for e in es])

    cmat = (hots[0] | hots[1] | hots[2] | hots[3]).astype(F32)
    rr = lax.broadcasted_iota(I32, (tm, tm), 0)
    cc = lax.broadcasted_iota(I32, (tm, tm), 1)
    tri = (cc < rr).astype(BF16)
    before = jnp.dot(tri, cmat.astype(BF16), preferred_element_type=F32) + carry_ref[...]
    ranks = [jnp.sum(jnp.where(hot, before, 0.0), axis=-1, keepdims=True) for hot in hots]
    rank_ref[...] = pack(ranks).astype(I32)
    carry = carry_ref[...] + jnp.sum(cmat, axis=0, keepdims=True)
    carry_ref[...] = carry
    cnt_ref[...] = carry


def _cross(n2, h1, kv, wq, wo, gm, wrh, wrl, br, *, batch, tm, name="cross"):
    n, d = h1.shape
    per_b = n // batch // tm
    mem_len = kv.shape[0] // batch

    def const(shape):
        return pl.BlockSpec(shape, lambda bi, i: (0, 0), pipeline_mode=pl.Buffered(1))

    def row(bi, i):
        return (bi * per_b + i, 0)

    return pl.pallas_call(
        functools.partial(_cross_kernel, tm=tm),
        out_shape=(
            jax.ShapeDtypeStruct((n, d), F32),
            jax.ShapeDtypeStruct((n, d), F32),
            jax.ShapeDtypeStruct((n, TOP_K), I32),
            jax.ShapeDtypeStruct((n, TOP_K), F32),
            jax.ShapeDtypeStruct((n, TOP_K), I32),
            jax.ShapeDtypeStruct((1, N_EXPERTS), F32),
        ),
        grid=(batch, per_b),
        in_specs=[
            pl.BlockSpec((tm, d), row),
            pl.BlockSpec((tm, d), row),
            pl.BlockSpec((mem_len, 2 * CROSS_WIDTH), lambda bi, i: (bi, 0)),
            const((d, CROSS_WIDTH)), const((CROSS_WIDTH, d)), const((1, d)),
            const((d, N_EXPERTS)), const((d, N_EXPERTS)), const((1, N_EXPERTS)),
        ],
        out_specs=(
            pl.BlockSpec((tm, d), row),
            pl.BlockSpec((tm, d), row),
            pl.BlockSpec((tm, TOP_K), row),
            pl.BlockSpec((tm, TOP_K), row),
            pl.BlockSpec((tm, TOP_K), row),
            pl.BlockSpec((1, N_EXPERTS), lambda bi, i: (0, 0)),
        ),
        scratch_shapes=[pltpu.VMEM((1, N_EXPERTS), F32)],
        compiler_params=_cparams(("arbitrary", "arbitrary")),
        name=name,
    )(n2, h1, kv, wq, wo, gm.reshape(1, d), wrh, wrl, br.reshape(1, N_EXPERTS))


_HI_MASK = 0xFFFF0000


def _pack_rows(x):
    half = x.shape[1] // 2
    lo = lax.bitcast_convert_type(x[:, :half].astype(BF16).astype(F32), U32) >> 16
    hi = lax.bitcast_convert_type(x[:, half:].astype(BF16).astype(F32), U32) & jnp.uint32(_HI_MASK)
    return lo | hi


def _unpack_rows(u):
    lo = lax.bitcast_convert_type(u << 16, F32)
    hi = lax.bitcast_convert_type(u & jnp.uint32(_HI_MASK), F32)
    return lo, hi


def _slab(ref, row):
    return ref.at[pl.ds(pl.multiple_of(row * MOE_SLAB, MOE_SLAB), MOE_SLAB)]


def _row_copies(make_copy, make_group_wait, count):
    ngroup = count // DMA_WAIT_ROWS

    def group(c, carry):
        def pair(j, carry2):
            a = c * DMA_WAIT_ROWS + 2 * j
            make_copy(a).start(priority=0)
            make_copy(a + 1).start(priority=1)
            return carry2

        lax.fori_loop(0, DMA_WAIT_ROWS // 2, pair, 0, unroll=DMA_ISSUE_UNROLL // 2)

        @pl.when(c > 0)
        def _():
            make_group_wait().wait()

        return carry

    lax.fori_loop(0, ngroup, group, 0)
    make_group_wait().wait()


def _dispatch_kernel(gstart_ref, cnt_ref, padded_ref, e_ref, rank_ref, n3_ref, xs_hbm, stage, sem,
                     *, real_steps, tm, rows_total):
    step = pl.program_id(0)
    group_rows = DMA_WAIT_ROWS * MOE_SLAB

    def group_wait():
        return pltpu.make_async_copy(stage.at[pl.ds(0, group_rows)], xs_hbm.at[pl.ds(0, group_rows)], sem)

    @pl.when(step < real_steps)
    def _():
        u = _pack_rows(n3_ref[...])
        for j in range(MOE_SLAB):
            stage[pl.ds(j, tm, stride=MOE_SLAB), :] = u[:, j * V7X_LANES:(j + 1) * V7X_LANES]

        def make_copy(a):
            slot = gstart_ref[e_ref[0, a]] + rank_ref[0, a]
            return pltpu.make_async_copy(_slab(stage, a >> TOP_K_LOG2), _slab(xs_hbm, slot), sem)

        _row_copies(make_copy, group_wait, tm * TOP_K)

    @pl.when(step == real_steps)
    def _():
        stage[0:MOE_SLAB, :] = jnp.zeros((MOE_SLAB, V7X_LANES), U32)

        def fill(lo, hi):
            def one(r, carry):
                pltpu.make_async_copy(_slab(stage, 0), _slab(xs_hbm, r), sem).start()
                return carry

            lax.fori_loop(lo, hi, one, 0)

        def per_expert(e, carry):
            fill(gstart_ref[e] + cnt_ref[e], gstart_ref[e] + padded_ref[e])
            return carry

        lax.fori_loop(0, N_EXPERTS, per_expert, 0)
        fill(gstart_ref[N_EXPERTS - 1] + padded_ref[N_EXPERTS - 1], rows_total)

        def retire(c, carry):
            group_wait().wait()
            return carry

        lax.fori_loop(0, (rows_total - real_steps * tm * TOP_K) // DMA_WAIT_ROWS, retire, 0)


def _dispatch(gstart, cnt, padded, top_e, rank, n3, *, rows_total, tm, name="moe_dispatch"):
    n, d = n3.shape
    real_steps = n // tm
    blk = pl.BlockSpec((None, 1, tm * TOP_K), lambda i, *_: (jnp.minimum(i, real_steps - 1), 0, 0),
                       memory_space=pltpu.SMEM)
    grid_spec = pltpu.PrefetchScalarGridSpec(
        num_scalar_prefetch=3,
        grid=(real_steps + 1,),
        in_specs=[blk, blk, pl.BlockSpec((tm, d), lambda i, *_: (jnp.minimum(i, real_steps - 1), 0))],
        out_specs=pl.BlockSpec(memory_space=pl.ANY),
        scratch_shapes=[pltpu.VMEM((tm * MOE_SLAB, V7X_LANES), U32), pltpu.SemaphoreType.DMA],
    )
    return pl.pallas_call(
        functools.partial(_dispatch_kernel, real_steps=real_steps, tm=tm, rows_total=rows_total),
        out_shape=jax.ShapeDtypeStruct((rows_total * MOE_SLAB, V7X_LANES), U32),
        grid_spec=grid_spec,
        compiler_params=_cparams(("arbitrary",)),
        name=name,
    )(gstart, cnt, padded, top_e.reshape(real_steps, 1, tm * TOP_K), rank.reshape(real_steps, 1, tm * TOP_K), n3)


def _moe_kernel(te_ref, tv_ref, tstart_ref, trows_ref,
                xs_hbm, w1_ref, b1_ref, w2_ref, b2_ref, perm_ref, yb_hbm,
                stage, x16, acc, w1p, sems, *, nf):
    i = pl.program_id(0)
    f = pl.program_id(1)
    rows = trows_ref[i]
    start = tstart_ref[i]
    nchunk = rows // MOE_ROW_ALIGN
    chunk_slab_rows = MOE_ROW_ALIGN * MOE_SLAB
    half = 2 * V7X_LANES
    nhalf = 2 * MOE_TF // half

    def slot_rows(c):
        return pl.multiple_of((c & 1) * chunk_slab_rows, chunk_slab_rows)

    def hbm_rows(ref, c):
        r0 = pl.multiple_of((start + c * MOE_ROW_ALIGN) * MOE_SLAB, chunk_slab_rows)
        return ref.at[pl.ds(r0, chunk_slab_rows)]

    def in_copy(c):
        return pltpu.make_async_copy(hbm_rows(xs_hbm, c), stage.at[pl.ds(slot_rows(c), chunk_slab_rows)],
                                     sems.at[c & 1])

    def out_copy(c):
        return pltpu.make_async_copy(stage.at[pl.ds(slot_rows(c), chunk_slab_rows)], hbm_rows(yb_hbm, c),
                                     sems.at[c & 1])

    @pl.when((rows > 0) & (f == 0))
    def _():
        in_copy(0).start()

        def land(c, carry):
            @pl.when(c + 1 < nchunk)
            def _():
                in_copy(c + 1).start()

            in_copy(c).wait()
            r0 = pl.multiple_of(c * MOE_ROW_ALIGN, MOE_ROW_ALIGN)
            s0 = slot_rows(c)
            for j in range(MOE_SLAB):
                lo, hi = _unpack_rows(stage[pl.ds(s0 + j, MOE_ROW_ALIGN, stride=MOE_SLAB), :])
                x16[pl.ds(r0, MOE_ROW_ALIGN), j * V7X_LANES:(j + 1) * V7X_LANES] = lo.astype(BF16)
                x16[pl.ds(r0, MOE_ROW_ALIGN), (MOE_SLAB + j) * V7X_LANES:(MOE_SLAB + j + 1) * V7X_LANES] = (
                    hi.astype(BF16))
            acc[pl.ds(r0, MOE_ROW_ALIGN), :] = jnp.zeros((MOE_ROW_ALIGN, D_MODEL), F32)
            return carry

        lax.fori_loop(0, nchunk, land, 0)

    @pl.when(rows > 0)
    def _():
        perm = perm_ref[...]
        for c in range(nhalf):
            wb = w1_ref[:, c * half:(c + 1) * half].astype(BF16)
            w1p[:, c * half:(c + 1) * half] = jnp.dot(wb, perm, preferred_element_type=F32).astype(BF16)
        w2 = w2_ref[...].astype(BF16)
        b1 = b1_ref[...]

        def mlp(c0, nc):
            r0 = pl.multiple_of(c0 * MOE_ROW_ALIGN, MOE_ROW_ALIGN)
            nr = nc * MOE_ROW_ALIGN
            x = x16[pl.ds(r0, nr), :]
            h = jnp.dot(x, w1p[...], preferred_element_type=F32) + b1
            parts = []
            for c in range(nhalf):
                hg = jnp.minimum(h[:, c * half:c * half + V7X_LANES], SWIGLU_LIMIT)
                hl = jnp.clip(h[:, c * half + V7X_LANES:(c + 1) * half], -SWIGLU_LIMIT, SWIGLU_LIMIT)
                parts.append(hg * jax.nn.sigmoid(SWIGLU_ALPHA * hg) * (hl + 1.0))
            a = jnp.concatenate(parts, axis=-1).astype(BF16)
            acc[pl.ds(r0, nr), :] += jnp.dot(a, w2, preferred_element_type=F32)

        nquad = nchunk // MOE_BLOCK_CHUNKS

        def quad(c, carry):
            mlp(c * MOE_BLOCK_CHUNKS, MOE_BLOCK_CHUNKS)
            return carry

        lax.fori_loop(0, nquad, quad, 0)
        size = MOE_BLOCK_CHUNKS // 2
        while size >= 1:
            done = nchunk & ~(2 * size - 1)

            @pl.when((nchunk & size) != 0)
            def _(done=done, size=size):
                mlp(done, size)

            size //= 2

    @pl.when((rows > 0) & (f == nf - 1))
    def _():
        def emit(c, carry):
            @pl.when(c >= 2)
            def _():
                out_copy(c - 2).wait()

            r0 = pl.multiple_of(c * MOE_ROW_ALIGN, MOE_ROW_ALIGN)
            s0 = slot_rows(c)
            u = _pack_rows(acc[pl.ds(r0, MOE_ROW_ALIGN), :] + b2_ref[...])
            for j in range(MOE_SLAB):
                stage[pl.ds(s0 + j, MOE_ROW_ALIGN, stride=MOE_SLAB), :] = u[:, j * V7X_LANES:(j + 1) * V7X_LANES]
            out_copy(c).start()
            return carry

        lax.fori_loop(0, nchunk, emit, 0)

        @pl.when(nchunk >= 2)
        def _():
            out_copy(nchunk - 2).wait()

        out_copy(nchunk - 1).wait()


def _moe(tile_e, tile_v, tile_start, tile_rows, xs, w1, b1p, w2, b2, perm, *, name="moe_mlp"):
    nt = tile_e.shape[0]
    nf = D_FF // MOE_TF

    def w1_map(i, f, te, tv, ts, tr):
        return (te[i], 0, jnp.where(tv[i] > 0, f, nf - 1))

    def w2_map(i, f, te, tv, ts, tr):
        return (te[i], jnp.where(tv[i] > 0, f, nf - 1), 0)

    def b2_map(i, f, te, tv, ts, tr):
        return (te[i], 0, 0)

    grid_spec = pltpu.PrefetchScalarGridSpec(
        num_scalar_prefetch=4,
        grid=(nt, nf),
        in_specs=[
            pl.BlockSpec(memory_space=pl.ANY),
            pl.BlockSpec((None, D_MODEL, 2 * MOE_TF), w1_map),
            pl.BlockSpec((None, 1, 2 * MOE_TF), w1_map),
            pl.BlockSpec((None, MOE_TF, D_MODEL), w2_map),
            pl.BlockSpec((None, 1, D_MODEL), b2_map),
            pl.BlockSpec((2 * V7X_LANES, 2 * V7X_LANES), lambda i, f, te, tv, ts, tr: (0, 0)),
        ],
        out_specs=pl.BlockSpec(memory_space=pl.ANY),
        scratch_shapes=[
            pltpu.VMEM((2 * MOE_ROW_ALIGN * MOE_SLAB, V7X_LANES), U32),
            pltpu.VMEM((MOE_TMAX, D_MODEL), BF16),
            pltpu.VMEM((MOE_TMAX, D_MODEL), F32),
            pltpu.VMEM((D_MODEL, 2 * MOE_TF), BF16),
            pltpu.SemaphoreType.DMA((2,)),
        ],
    )
    return pl.pallas_call(
        functools.partial(_moe_kernel, nf=nf),
        out_shape=jax.ShapeDtypeStruct(xs.shape, xs.dtype),
        grid_spec=grid_spec,
        input_output_aliases={4: 0},
        compiler_params=_cparams(("arbitrary", "arbitrary")),
        name=name,
    )(tile_e, tile_v, tile_start, tile_rows, xs, w1, b1p.reshape(N_EXPERTS, 1, 2 * D_FF), w2,
      b2.reshape(N_EXPERTS, 1, D_MODEL), perm)


def _combine_kernel(gstart_ref, e_ref, rank_ref, h2_ref, tw_ref, gf_ref, yb_hbm, o_ref, buf, sem, *, tm):
    group_rows = DMA_WAIT_ROWS * MOE_SLAB

    def make_copy(a):
        t = a >> TOP_K_LOG2
        k = a & (TOP_K - 1)
        slot = gstart_ref[e_ref[0, a]] + rank_ref[0, a]
        return pltpu.make_async_copy(_slab(yb_hbm, slot), _slab(buf, k * tm + t), sem)

    def group_wait():
        return pltpu.make_async_copy(yb_hbm.at[pl.ds(0, group_rows)], buf.at[pl.ds(0, group_rows)], sem)

    _row_copies(make_copy, group_wait, tm * TOP_K)

    tw = tw_ref[...]
    cols_lo, cols_hi = [], []
    for j in range(MOE_SLAB):
        clo = chi = None
        for k in range(TOP_K):
            lo, hi = _unpack_rows(buf[pl.ds(k * tm * MOE_SLAB + j, tm, stride=MOE_SLAB), :])
            w = tw[:, k:k + 1]
            clo = w * lo if clo is None else clo + w * lo
            chi = w * hi if chi is None else chi + w * hi
        cols_lo.append(clo)
        cols_hi.append(chi)
    h3 = h2_ref[...] + jnp.concatenate(cols_lo + cols_hi, axis=-1)
    o_ref[...] = _rms(h3, gf_ref[...])


def _combine(gstart, top_e, rank, h2, tw, gf, yb, *, tm, name="moe_combine"):
    n, d = h2.shape
    steps = n // tm
    blk = pl.BlockSpec((None, 1, tm * TOP_K), lambda i, gs: (i, 0, 0), memory_space=pltpu.SMEM)
    grid_spec = pltpu.PrefetchScalarGridSpec(
        num_scalar_prefetch=1,
        grid=(steps,),
        in_specs=[
            blk, blk,
            pl.BlockSpec((tm, d), lambda i, gs: (i, 0)),
            pl.BlockSpec((tm, TOP_K), lambda i, gs: (i, 0)),
            pl.BlockSpec((1, d), lambda i, gs: (0, 0)),
            pl.BlockSpec(memory_space=pl.ANY),
        ],
        out_specs=pl.BlockSpec((tm, d), lambda i, gs: (i, 0)),
        scratch_shapes=[
            pltpu.VMEM((tm * TOP_K * MOE_SLAB, V7X_LANES), U32),
            pltpu.SemaphoreType.DMA,
        ],
    )
    return pl.pallas_call(
        functools.partial(_combine_kernel, tm=tm),
        out_shape=jax.ShapeDtypeStruct((n, d), F32),
        grid_spec=grid_spec,
        compiler_params=_cparams(("arbitrary",)),
        name=name,
    )(gstart, top_e.reshape(steps, 1, tm * TOP_K), rank.reshape(steps, 1, tm * TOP_K), h2, tw,
      gf.reshape(1, d), yb)


def _rope_tables(positions):
    b, s = positions.shape
    half = HEAD_DIM // 2
    inv_freq = 1.0 / (ROPE_THETA ** (jnp.arange(half, dtype=F32) / half))
    ang = positions.astype(F32)[..., None] * inv_freq
    cos = jnp.cos(ang)
    sin = jnp.sin(ang)
    cs = jnp.concatenate([cos, cos], axis=-1)
    sn = jnp.concatenate([-sin, sin], axis=-1)

    def deint(a):
        a = a.reshape(b, s // ATT_TILE, WIN_BLOCK, DIL_MAX, HEAD_DIM)
        return a.transpose(0, 1, 3, 2, 4)

    return deint(cs), deint(sn)


def _routing(counts, n_tok):
    cnt = counts.reshape(N_EXPERTS).astype(I32)
    padded = (cnt + MOE_ROW_ALIGN - 1) // MOE_ROW_ALIGN * MOE_ROW_ALIGN
    gstart = (jnp.cumsum(padded) - padded).astype(I32)
    rows_total = n_tok * TOP_K + N_EXPERTS * MOE_ROW_ALIGN

    def first_above(ends, q):
        return jnp.minimum(jnp.sum((ends[None, :] <= q[:, None]).astype(I32), axis=1), N_EXPERTS - 1)

    main_rows = jnp.minimum(padded, MOE_TMAX)
    main = (jnp.arange(N_EXPERTS, dtype=I32), (main_rows > 0).astype(I32), gstart.astype(I32),
            main_rows.astype(I32))

    over = padded - main_rows
    nt_e = (over + MOE_TMAX - 1) // MOE_TMAX
    tend = jnp.cumsum(nt_e)
    tstart = tend - nt_e
    n_over = tend[-1]
    ti = jnp.arange(rows_total // MOE_TMAX, dtype=I32)
    valid = ti < n_over
    tic = jnp.clip(ti, 0, jnp.maximum(n_over - 1, 0))
    te = first_above(tend, tic)
    local = tic - tstart[te]
    row0 = gstart[te] + (local + 1) * MOE_TMAX
    rows = jnp.where(valid, jnp.clip(over[te] - local * MOE_TMAX, 0, MOE_TMAX), 0)
    overflow = (te, valid.astype(I32), row0.astype(I32), rows.astype(I32))
    return gstart, cnt, padded, rows_total, main, overflow, n_over > 0


def kernel(x, mem, positions, norm_mix_g, w_in, lb_raw, hgrn_norm_g, w_br_hgrn, w_br_attn, w_out,
           norm_cross_g, norm_mem_g, w_cq, w_ckv, w_co, norm_moe_g, w_router, b_router,
           w_mlp1, b_mlp1, w_mlp2, b_mlp2, norm_final_g):
    bsz, seq, d = x.shape
    assert w_in.shape[0] == 1 and d == D_MODEL and seq % ATT_TILE == 0
    n_tok = bsz * seq
    lower_bounds = jnp.cumsum(jax.nn.softmax(lb_raw.astype(F32), axis=0), axis=0)
    cs, sn = _rope_tables(positions)

    wl = w_in[0].astype(BF16)
    c_h = 4 * HG_WIDTH
    c_a = c_h + ATT_Q_WIDTH + 2 * ATT_KV_WIDTH
    x2d = x.reshape(n_tok, d)
    proj_h = _in_proj(x2d, norm_mix_g[0], wl, tm=1024, tn=1024, name="in_proj_hgrn", col0=0, ncols=c_h)
    proj_g = _in_proj(x2d, norm_mix_g[0], wl, tm=1024, tn=1024, name="in_proj_gate", col0=c_a,
                      ncols=2 * D_MODEL)
    pa = _in_proj_strided(x, norm_mix_g[0], wl, c=8, tn=1024, name="in_proj_attn", col0=c_h, ncols=c_a - c_h)

    o_h = _hgrn(proj_h.reshape(bsz, seq, c_h), lower_bounds[0], hgrn_norm_g[0], ts=512)
    o_a = _dilated_attn(pa, cs, sn)

    h1, n2 = _merge_out(o_h.reshape(n_tok, HG_WIDTH), o_a.reshape(n_tok, ATT_KV_WIDTH), proj_g, x2d,
                        w_br_hgrn[0].astype(BF16), w_br_attn[0].astype(BF16), w_out[0].astype(BF16),
                        norm_cross_g[0], tm=256)

    mem2d = mem.reshape(-1, d)
    kv = _in_proj(mem2d, norm_mem_g, w_ckv[0].astype(BF16), tm=mem2d.shape[0], tn=2 * CROSS_WIDTH, name="mem_kv")
    wrh = w_router[0].astype(BF16)
    wrl = (w_router[0] - wrh.astype(F32)).astype(BF16)
    h2, n3, top_e, top_w, rank, counts = _cross(
        n2, h1, kv, w_cq[0].astype(BF16), w_co[0].astype(BF16), norm_moe_g[0], wrh, wrl, b_router[0],
        batch=bsz, tm=512)

    gstart, cnt, padded, rows_total, main_tiles, over_tiles, has_over = _routing(counts, n_tok)
    xs = _dispatch(gstart, cnt, padded, top_e, rank, n3, rows_total=rows_total, tm=DISPATCH_TOKENS)

    ii = jnp.arange(2 * V7X_LANES)
    src = jnp.where(ii < V7X_LANES, 2 * ii, 2 * (ii - V7X_LANES) + 1)
    perm = (ii[:, None] == src[None, :]).astype(BF16)
    b1p = b_mlp1[0].reshape(N_EXPERTS, -1, V7X_LANES, 2).transpose(0, 1, 3, 2).reshape(N_EXPERTS, 2 * D_FF)
    def run_moe(tiles, rows_buf, name):
        return _moe(*tiles, rows_buf, w_mlp1[0], b1p, w_mlp2[0], b_mlp2[0], perm, name=name)

    yb = run_moe(main_tiles, xs, "moe_mlp")
    yb = lax.cond(has_over, lambda rows_buf: run_moe(over_tiles, rows_buf, "moe_mlp_overflow"),
                  lambda rows_buf: rows_buf, yb)

    out = _combine(gstart, top_e, rank, h2, top_w, norm_final_g, yb, tm=COMBINE_TOKENS)
    return out.reshape(bsz, seq, d)
```

```python
import functools

import jax
import jax.numpy as jnp
from jax import lax
from jax.experimental import pallas as pl
from jax.experimental.pallas import tpu as pltpu

F32 = jnp.float32
BF16 = jnp.bfloat16
I32 = jnp.int32
U32 = jnp.uint32

D_MODEL = 2048
HEAD_DIM = 128
HG_HEADS = 8
HG_WIDTH = HG_HEADS * HEAD_DIM
N_KV_HEADS = 8
N_GROUPS = 3
ATT_Q_WIDTH = N_GROUPS * N_KV_HEADS * HEAD_DIM
ATT_KV_WIDTH = N_KV_HEADS * HEAD_DIM
WIN_BLOCK = 128
ROPE_THETA = 10000.0
CROSS_HEADS = 4
CROSS_WIDTH = CROSS_HEADS * HEAD_DIM
N_EXPERTS = 32
TOP_K = 4
TOP_K_LOG2 = 2
D_FF = D_MODEL
SWIGLU_ALPHA = 1.702
SWIGLU_LIMIT = 7.0
NORM_EPS = 1e-6

V7X_LANES = 128
V7X_VMEM_LIMIT_BYTES = 56 * 1024 * 1024

DIL_MAX = 16
ATT_TILE = DIL_MAX * WIN_BLOCK
ATT_UNROLL = 16
HG_BLOCK = 16
HG_GROUP = 16
NEG_BIG = -1e30

MOE_TMAX = 1536
MOE_ROW_ALIGN = 128
MOE_BLOCK_CHUNKS = 8
MOE_TF = 512
MOE_SLAB = D_MODEL // 2 // V7X_LANES
DISPATCH_TOKENS = 512
COMBINE_TOKENS = 512
DMA_WAIT_ROWS = 128
DMA_ISSUE_UNROLL = 8


def _rms(x, g):
    ms = jnp.mean(x * x, axis=-1, keepdims=True)
    return x * lax.rsqrt(ms + NORM_EPS) * g


def _cparams(sem, vmem=V7X_VMEM_LIMIT_BYTES):
    return pltpu.CompilerParams(dimension_semantics=sem, vmem_limit_bytes=vmem)


def _in_proj_kernel(x_ref, g_ref, w_ref, o_ref, xn_ref):
    @pl.when(pl.program_id(1) == 0)
    def _():
        xn_ref[...] = _rms(x_ref[...], g_ref[...]).astype(BF16)

    o_ref[...] = jnp.dot(xn_ref[...], w_ref[...], preferred_element_type=F32).astype(o_ref.dtype)


def _in_proj(x2d, g, w, *, tm, tn, name, col0=0, ncols=None):
    n, d = x2d.shape
    wc = w.shape[1] if ncols is None else ncols
    cb = col0 // tn
    return pl.pallas_call(
        _in_proj_kernel,
        out_shape=jax.ShapeDtypeStruct((n, wc), BF16),
        grid=(n // tm, wc // tn),
        in_specs=[
            pl.BlockSpec((tm, d), lambda i, j: (i, 0)),
            pl.BlockSpec((1, d), lambda i, j: (0, 0)),
            pl.BlockSpec((d, tn), lambda i, j: (0, cb + j)),
        ],
        out_specs=pl.BlockSpec((tm, tn), lambda i, j: (i, j)),
        scratch_shapes=[pltpu.VMEM((tm, d), BF16)],
        compiler_params=_cparams(("parallel", "arbitrary")),
        name=name,
    )(x2d, g.reshape(1, d), w)


def _in_proj_strided_kernel(x_ref, g_ref, w_ref, o_ref, xn_ref, *, c):
    @pl.when(pl.program_id(3) == 0)
    def _():
        for ci in range(c):
            x = x_ref[:, ci * D_MODEL:(ci + 1) * D_MODEL]
            xn_ref[ci * WIN_BLOCK:(ci + 1) * WIN_BLOCK, :] = _rms(x, g_ref[...]).astype(BF16)

    res = jnp.dot(xn_ref[...], w_ref[...], preferred_element_type=F32)
    for ci in range(c):
        o_ref[ci] = res[ci * WIN_BLOCK:(ci + 1) * WIN_BLOCK, :].astype(o_ref.dtype)


def _in_proj_strided(x, g, w, *, c, tn, name, col0, ncols):
    b, s, d = x.shape
    nt = s // ATT_TILE
    wc = ncols
    cb = col0 // tn
    xv = x.reshape(b, nt, WIN_BLOCK, DIL_MAX * d)
    return pl.pallas_call(
        functools.partial(_in_proj_strided_kernel, c=c),
        out_shape=jax.ShapeDtypeStruct((b, nt, DIL_MAX, WIN_BLOCK, wc), BF16),
        grid=(b, nt, DIL_MAX // c, wc // tn),
        in_specs=[
            pl.BlockSpec((None, None, WIN_BLOCK, c * d), lambda bi, ti, ri, j: (bi, ti, 0, ri)),
            pl.BlockSpec((1, d), lambda bi, ti, ri, j: (0, 0)),
            pl.BlockSpec((d, tn), lambda bi, ti, ri, j: (0, cb + j)),
        ],
        out_specs=pl.BlockSpec((None, None, c, WIN_BLOCK, tn), lambda bi, ti, ri, j: (bi, ti, ri, 0, j)),
        scratch_shapes=[pltpu.VMEM((c * WIN_BLOCK, d), BF16)],
        compiler_params=_cparams(("parallel", "parallel", "parallel", "arbitrary")),
        name=name,
    )(xv, g.reshape(1, d), w)


def _hgrn_kernel(q_ref, f_ref, i_ref, g_ref, lb_ref, gn_ref, o_ref, st_ref, kin_s, b_s, v_s, *, ts):
    @pl.when(pl.program_id(2) == 0)
    def _():
        st_ref[...] = jnp.zeros_like(st_ref)

    lb = lb_ref[...]
    oml = 1.0 - lb
    gn = gn_ref[...]
    half = HG_BLOCK // 2
    row = lax.broadcasted_iota(I32, (HG_BLOCK, HEAD_DIM), 0)
    row8 = lax.broadcasted_iota(I32, (half, HEAD_DIM), 0)
    nt_dims = (((1,), (1,)), ((), ()))
    tn_dims = (((0,), (0,)), ((), ()))

    def front(g, t0):
        sl = pl.ds(t0 + g * HG_BLOCK, HG_BLOCK)
        q = q_ref[sl, :].astype(F32)
        hf = f_ref[sl, :].astype(F32)
        v = i_ref[sl, :].astype(F32)
        kin = oml * jax.nn.sigmoid(-hf)
        b = jnp.log2(lb + oml * jax.nn.sigmoid(hf))
        for sh in (1, 2, 4, 8):
            b = b + jnp.where(row >= sh, pltpu.roll(b, sh, 0), 0.0)
        kin_s[g] = kin
        b_s[g] = b
        v_s[g] = v
        q_lo, q_hi = q[:half], q[half:]
        b_lo, b_hi = b[:half], b[half:]
        o_lo = jnp.zeros((half, HEAD_DIM), F32)
        o_hi = jnp.zeros((half, HEAD_DIM), F32)
        for s in range(HG_BLOCK):
            ks = kin_s[g, s:s + 1, :]
            bs = b_s[g, s:s + 1, :]
            vs = v_s[g, s:s + 1, :]
            if s < half:
                w = q_lo * ks * jnp.exp2(b_lo - bs)
                if s > 0:
                    w = jnp.where(row8 >= s, w, 0.0)
                o_lo = o_lo + jnp.sum(w, axis=-1, keepdims=True) * vs
                w = q_hi * ks * jnp.exp2(b_hi - bs)
            else:
                w = q_hi * ks * jnp.exp2(b_hi - bs)
                if s > half:
                    w = jnp.where(row8 >= s - half, w, 0.0)
            o_hi = o_hi + jnp.sum(w, axis=-1, keepdims=True) * vs
        bl = b_s[g, HG_BLOCK - 1:HG_BLOCK, :]
        qd = (q * jnp.exp2(b)).astype(BF16)
        kd = (kin * jnp.exp2(bl - b)).astype(BF16)
        upd = lax.dot_general(v.astype(BF16), kd, tn_dims, preferred_element_type=F32)
        return jnp.concatenate([o_lo, o_hi], axis=0), qd, upd, jnp.exp2(bl)

    def body(i, carry):
        t0 = pl.multiple_of(i * (HG_GROUP * HG_BLOCK), HG_GROUP * HG_BLOCK)
        fronts = [front(g, t0) for g in range(HG_GROUP)]
        st = st_ref[...]
        for g, (o_diag, qd, upd, dec) in enumerate(fronts):
            o = o_diag + lax.dot_general(qd, st.astype(BF16), nt_dims, preferred_element_type=F32)
            st = st * dec + upd
            sl = pl.ds(t0 + g * HG_BLOCK, HG_BLOCK)
            hg = g_ref[sl, :].astype(F32)
            o_ref[sl, :] = (_rms(o, gn) * (hg * jax.nn.sigmoid(hg))).astype(o_ref.dtype)
        st_ref[...] = st
        return carry

    lax.fori_loop(0, ts // (HG_GROUP * HG_BLOCK), body, 0)


def _hgrn(proj_h, lb, gn, *, ts, name="hgrn"):
    b, s, _ = proj_h.shape
    h = HG_HEADS

    def spec(off):
        return pl.BlockSpec((None, ts, HEAD_DIM), lambda bi, hi, si: (bi, si, off + hi))

    vec = pl.BlockSpec((1, HEAD_DIM), lambda bi, hi, si: (0, hi))
    return pl.pallas_call(
        functools.partial(_hgrn_kernel, ts=ts),
        out_shape=jax.ShapeDtypeStruct((b, s, HG_WIDTH), BF16),
        grid=(b, h, s // ts),
        in_specs=[spec(0), spec(h), spec(2 * h), spec(3 * h), vec, vec],
        out_specs=pl.BlockSpec((None, ts, HEAD_DIM), lambda bi, hi, si: (bi, si, hi)),
        scratch_shapes=[
            pltpu.VMEM((HEAD_DIM, HEAD_DIM), F32),
            pltpu.VMEM((HG_GROUP, HG_BLOCK, HEAD_DIM), F32),
            pltpu.VMEM((HG_GROUP, HG_BLOCK, HEAD_DIM), F32),
            pltpu.VMEM((HG_GROUP, HG_BLOCK, HEAD_DIM), F32),
        ],
        compiler_params=_cparams(("parallel", "parallel", "arbitrary")),
        name=name,
    )(proj_h, proj_h, proj_h, proj_h, lb.reshape(1, HG_WIDTH), gn.reshape(1, HG_WIDTH))


def _attn_bias(kind):
    rq = lax.broadcasted_iota(I32, (WIN_BLOCK, 2 * WIN_BLOCK), 0)
    ck = lax.broadcasted_iota(I32, (WIN_BLOCK, 2 * WIN_BLOCK), 1)
    if kind == 2:
        dist = rq + WIN_BLOCK - ck
        first = ck < WIN_BLOCK
    elif kind == 1:
        dist = 4 * ((rq & 31) - (ck & 63) + 32) + ((rq >> 5) - (ck >> 6))
        first = (ck & 63) < 32
    else:
        dist = 16 * ((rq & 7) - (ck & 15) + 8) + ((rq >> 3) - (ck >> 4))
        first = (ck & 15) < 8
    valid = (dist >= 0) & (dist <= WIN_BLOCK)
    return (jnp.where(valid, 0.0, NEG_BIG).astype(F32),
            jnp.where(valid & jnp.logical_not(first), 0.0, NEG_BIG).astype(F32))


def _attn_kernel(q0_ref, q1_ref, q2_ref, k_ref, v_ref, cs_ref, sn_ref, o_ref,
                 qr, kext, vext, acc, mrun, lrun, bias, onat):
    ti = pl.program_id(2)
    wb = WIN_BLOCK
    scale = HEAD_DIM ** -0.5

    @pl.when(ti == 0)
    def _():
        kext[:, 0:wb, :] = jnp.zeros((DIL_MAX, wb, HEAD_DIM), F32)
        vext[:, 0:wb, :] = jnp.zeros((DIL_MAX, wb, HEAD_DIM), F32)

    @pl.when(ti > 0)
    def _():
        kext[:, 0:wb, :] = kext[:, wb:2 * wb, :]
        vext[:, 0:wb, :] = vext[:, wb:2 * wb, :]

    for kind in range(N_GROUPS):
        full, nofirst = _attn_bias(kind)
        bias[2 * kind] = full
        bias[2 * kind + 1] = nofirst

    def rope_body(r, carry):
        cs = cs_ref[r]
        sn = sn_ref[r]
        for g, qref in enumerate((q0_ref, q1_ref, q2_ref)):
            q = qref[r].astype(F32)
            qr[g, r] = (q * cs + pltpu.roll(q, HEAD_DIM // 2, 1) * sn) * scale
        k = k_ref[r].astype(F32)
        kext[r, wb:2 * wb, :] = k * cs + pltpu.roll(k, HEAD_DIM // 2, 1) * sn
        vext[r, wb:2 * wb, :] = v_ref[r].astype(F32)
        acc[r] = jnp.zeros((wb, HEAD_DIM), F32)
        mrun[r] = jnp.full((wb, HEAD_DIM), NEG_BIG, F32)
        lrun[r] = jnp.zeros((wb, HEAD_DIM), F32)
        return carry

    lax.fori_loop(0, DIL_MAX, rope_body, 0)

    nt_dims = (((1,), (1,)), ((), ()))

    def block(qb, kb, vb, bias_blk):
        s = lax.dot_general(qb.astype(BF16), kb.astype(BF16), nt_dims, preferred_element_type=F32)
        s = s + bias_blk
        m = jnp.max(s, axis=-1, keepdims=True)
        p = jnp.exp(s - m)
        l = jnp.sum(p, axis=-1, keepdims=True)
        n = jnp.dot(p.astype(BF16), vb.astype(BF16), preferred_element_type=F32)
        return n, jnp.broadcast_to(m, (wb, HEAD_DIM)), jnp.broadcast_to(l, (wb, HEAD_DIM))

    def merge(r, rows, n, m, l):
        m_old = mrun[r, rows, :]
        m_new = jnp.maximum(m_old, m)
        a = jnp.exp(m_old - m_new)
        bb = jnp.exp(m - m_new)
        acc[r, rows, :] = acc[r, rows, :] * a + n * bb
        lrun[r, rows, :] = lrun[r, rows, :] * a + l * bb
        mrun[r, rows, :] = m_new

    first_tile = jnp.where(ti == 0, 1, 0)


    def g2_body(i, carry):
        rs = [i * ATT_UNROLL + u for u in range(ATT_UNROLL)]
        res = [block(qr[2, r], kext[r], vext[r], bias[4 + first_tile]) for r in rs]
        for r, (n, m, l) in zip(rs, res):
            merge(r, pl.ds(0, wb), n, m, l)
        return carry

    lax.fori_loop(0, DIL_MAX // ATT_UNROLL, g2_body, 0)

    def g1_body(i, carry):
        res = []
        for u in range(ATT_UNROLL // 4):
            mb = i * (ATT_UNROLL // 4) + u
            q0 = pl.multiple_of(32 * mb, 32)
            k0 = pl.multiple_of(96 + 32 * mb, 32)
            use_first = jnp.where(mb == 0, first_tile, 0)
            for r4 in range(4):
                qb = jnp.concatenate([qr[1, r4 + 4 * j, pl.ds(q0, 32), :] for j in range(4)], axis=0)
                kb = jnp.concatenate([kext[r4 + 4 * j, pl.ds(k0, 64), :] for j in range(4)], axis=0)
                vb = jnp.concatenate([vext[r4 + 4 * j, pl.ds(k0, 64), :] for j in range(4)], axis=0)
                res.append((r4, q0, block(qb, kb, vb, bias[2 + use_first])))
        for r4, q0, (n, m, l) in res:
            for j in range(4):
                sl = slice(32 * j, 32 * (j + 1))
                merge(r4 + 4 * j, pl.ds(q0, 32), n[sl], m[sl], l[sl])
        return carry

    lax.fori_loop(0, DIL_MAX // ATT_UNROLL, g1_body, 0)

    def g0_body(i, carry):
        res = []
        for u in range(ATT_UNROLL):
            mb = i * ATT_UNROLL + u
            q0 = pl.multiple_of(8 * mb, 8)
            k0 = pl.multiple_of(120 + 8 * mb, 8)
            qb = jnp.concatenate([qr[0, r, pl.ds(q0, 8), :] for r in range(DIL_MAX)], axis=0)
            kb = jnp.concatenate([kext[r, pl.ds(k0, 16), :] for r in range(DIL_MAX)], axis=0)
            vb = jnp.concatenate([vext[r, pl.ds(k0, 16), :] for r in range(DIL_MAX)], axis=0)
            use_first = jnp.where(mb == 0, first_tile, 0)
            res.append((q0, block(qb, kb, vb, bias[use_first])))
        for q0, (n, m, l) in res:
            for r in range(DIL_MAX):
                sl = slice(8 * r, 8 * (r + 1))
                merge(r, pl.ds(q0, 8), n[sl], m[sl], l[sl])
        return carry

    lax.fori_loop(0, DIL_MAX // ATT_UNROLL, g0_body, 0)

    for r in range(DIL_MAX):
        onat[pl.ds(r, wb, stride=DIL_MAX), :] = acc[r] / lrun[r]
    o_ref[...] = onat[...].astype(o_ref.dtype)


def _dilated_attn(pa, cs, sn, *, name="dilated_attn"):
    b, nt = pa.shape[0], pa.shape[1]
    h = N_KV_HEADS

    def spec(off):
        return pl.BlockSpec((None, None, DIL_MAX, WIN_BLOCK, HEAD_DIM),
                            lambda bi, hi, ti: (bi, ti, 0, 0, off + hi))

    tab = pl.BlockSpec((None, None, DIL_MAX, WIN_BLOCK, HEAD_DIM), lambda bi, hi, ti: (bi, ti, 0, 0, 0))
    return pl.pallas_call(
        _attn_kernel,
        out_shape=jax.ShapeDtypeStruct((b, nt * ATT_TILE, ATT_KV_WIDTH), BF16),
        grid=(b, h, nt),
        in_specs=[spec(0), spec(h), spec(2 * h), spec(3 * h), spec(4 * h), tab, tab],
        out_specs=pl.BlockSpec((None, ATT_TILE, HEAD_DIM), lambda bi, hi, ti: (bi, ti, hi)),
        scratch_shapes=[
            pltpu.VMEM((N_GROUPS, DIL_MAX, WIN_BLOCK, HEAD_DIM), F32),
            pltpu.VMEM((DIL_MAX, 2 * WIN_BLOCK, HEAD_DIM), F32),
            pltpu.VMEM((DIL_MAX, 2 * WIN_BLOCK, HEAD_DIM), F32),
            pltpu.VMEM((DIL_MAX, WIN_BLOCK, HEAD_DIM), F32),
            pltpu.VMEM((DIL_MAX, WIN_BLOCK, HEAD_DIM), F32),
            pltpu.VMEM((DIL_MAX, WIN_BLOCK, HEAD_DIM), F32),
            pltpu.VMEM((2 * N_GROUPS, WIN_BLOCK, 2 * WIN_BLOCK), F32),
            pltpu.VMEM((ATT_TILE, HEAD_DIM), F32),
        ],
        compiler_params=_cparams(("arbitrary", "arbitrary", "arbitrary")),
        name=name,
    )(pa, pa, pa, pa, pa, cs, sn)


def _merge_out_kernel(oh_ref, oa_ref, gate_ref, x_ref, wh_ref, wa_ref, wo_ref, gc_ref, h1_ref, n2_ref):
    ga = gate_ref[:, :D_MODEL].astype(F32)
    gb = gate_ref[:, D_MODEL:].astype(F32)
    yh = jnp.dot(oh_ref[...], wh_ref[...], preferred_element_type=F32)
    ya = jnp.dot(oa_ref[...], wa_ref[...], preferred_element_type=F32)
    merged = jax.nn.sigmoid(ga) * yh + jax.nn.sigmoid(gb) * ya
    h1 = x_ref[...] + jnp.dot(merged.astype(BF16), wo_ref[...], preferred_element_type=F32)
    h1_ref[...] = h1
    n2_ref[...] = _rms(h1, gc_ref[...]).astype(BF16)


def _merge_out(oh, oa, gates, x2d, wh, wa, wo, gc, *, tm, name="merge_out"):
    n, d = x2d.shape

    def const(shape):
        return pl.BlockSpec(shape, lambda i: (0, 0), pipeline_mode=pl.Buffered(1))

    return pl.pallas_call(
        _merge_out_kernel,
        out_shape=(jax.ShapeDtypeStruct((n, d), F32), jax.ShapeDtypeStruct((n, d), BF16)),
        grid=(n // tm,),
        in_specs=[
            pl.BlockSpec((tm, HG_WIDTH), lambda i: (i, 0)),
            pl.BlockSpec((tm, ATT_KV_WIDTH), lambda i: (i, 0)),
            pl.BlockSpec((tm, 2 * d), lambda i: (i, 0)),
            pl.BlockSpec((tm, d), lambda i: (i, 0)),
            const((HG_WIDTH, d)), const((ATT_KV_WIDTH, d)), const((d, d)), const((1, d)),
        ],
        out_specs=(pl.BlockSpec((tm, d), lambda i: (i, 0)), pl.BlockSpec((tm, d), lambda i: (i, 0))),
        compiler_params=_cparams(("parallel",)),
        name=name,
    )(oh, oa, gates, x2d, wh, wa, wo, gc.reshape(1, d))


def _cross_kernel(n2_ref, h1_ref, kv_ref, wq_ref, wo_ref, gm_ref, wrh_ref, wrl_ref, br_ref,
                  h2_ref, n3_ref, idx_ref, tw_ref, rank_ref, cnt_ref, carry_ref, *, tm):
    @pl.when((pl.program_id(0) == 0) & (pl.program_id(1) == 0))
    def _():
        carry_ref[...] = jnp.zeros_like(carry_ref)

    nt_dims = (((1,), (1,)), ((), ()))
    scale = HEAD_DIM ** -0.5
    q = (jnp.dot(n2_ref[...], wq_ref[...], preferred_element_type=F32) * scale).astype(BF16)
    outs = []
    for hh in range(CROSS_HEADS):
        sl = slice(hh * HEAD_DIM, (hh + 1) * HEAD_DIM)
        kh = kv_ref[:, sl]
        vh = kv_ref[:, CROSS_WIDTH + hh * HEAD_DIM:CROSS_WIDTH + (hh + 1) * HEAD_DIM]
        s = lax.dot_general(q[:, sl], kh, nt_dims, preferred_element_type=F32)
        p = jnp.exp(s - jnp.max(s, axis=-1, keepdims=True))
        l = jnp.sum(p, axis=-1, keepdims=True)
        outs.append(jnp.dot(p.astype(BF16), vh, preferred_element_type=F32) / l)
    o = jnp.concatenate(outs, axis=-1).astype(BF16)
    h2 = h1_ref[...] + jnp.dot(o, wo_ref[...], preferred_element_type=F32)
    h2_ref[...] = h2
    n3 = _rms(h2, gm_ref[...])
    n3_ref[...] = n3

    n3h = n3.astype(BF16)
    n3l = (n3 - n3h.astype(F32)).astype(BF16)
    wrh = wrh_ref[...]
    logits = (jnp.dot(n3h, wrh, preferred_element_type=F32)
              + jnp.dot(n3l, wrh, preferred_element_type=F32)
              + jnp.dot(n3h, wrl_ref[...], preferred_element_type=F32)
              + br_ref[...])
    lane = lax.broadcasted_iota(I32, (tm, N_EXPERTS), 1).astype(F32)
    vals, idxs, hots = [], [], []
    cur = logits
    for _ in range(TOP_K):
        mx = jnp.max(cur, axis=-1, keepdims=True)
        ix = jnp.min(jnp.where(cur == mx, lane, float(N_EXPERTS)), axis=-1, keepdims=True)
        hot = lane == ix
        vals.append(mx)
        idxs.append(ix)
        hots.append(hot)
        cur = jnp.where(hot, -jnp.inf, cur)
    es = [jnp.exp(v - vals[0]) for v in vals]
    den = es[0] + es[1] + es[2] + es[3]
    col = lax.broadcasted_iota(I32, (tm, TOP_K), 1)

    def pack(cols):
        out = jnp.broadcast_to(cols[TOP_K - 1], (tm, TOP_K))
        for k in range(TOP_K - 2, -1, -1):
            out = jnp.where(col == k, cols[k], out)
        return out

    idx_ref[...] = pack(idxs).astype(I32)
    tw_ref[...] = pack([e / den for e in es])

    cmat = (hots[0] | hots[1] | hots[2] | hots[3]).astype(F32)
    rr = lax.broadcasted_iota(I32, (tm, tm), 0)
    cc = lax.broadcasted_iota(I32, (tm, tm), 1)
    tri = (cc < rr).astype(BF16)
    before = jnp.dot(tri, cmat.astype(BF16), preferred_element_type=F32) + carry_ref[...]
    ranks = [jnp.sum(jnp.where(hot, before, 0.0), axis=-1, keepdims=True) for hot in hots]
    rank_ref[...] = pack(ranks).astype(I32)
    carry = carry_ref[...] + jnp.sum(cmat, axis=0, keepdims=True)
    carry_ref[...] = carry
    cnt_ref[...] = carry


def _cross(n2, h1, kv, wq, wo, gm, wrh, wrl, br, *, batch, tm, name="cross"):
    n, d = h1.shape
    per_b = n // batch // tm
    mem_len = kv.shape[0] // batch

    def const(shape):
        return pl.BlockSpec(shape, lambda bi, i: (0, 0), pipeline_mode=pl.Buffered(1))

    def row(bi, i):
        return (bi * per_b + i, 0)

    return pl.pallas_call(
        functools.partial(_cross_kernel, tm=tm),
        out_shape=(
            jax.ShapeDtypeStruct((n, d), F32),
            jax.ShapeDtypeStruct((n, d), F32),
            jax.ShapeDtypeStruct((n, TOP_K), I32),
            jax.ShapeDtypeStruct((n, TOP_K), F32),
            jax.ShapeDtypeStruct((n, TOP_K), I32),
            jax.ShapeDtypeStruct((1, N_EXPERTS), F32),
        ),
        grid=(batch, per_b),
        in_specs=[
            pl.BlockSpec((tm, d), row),
            pl.BlockSpec((tm, d), row),
            pl.BlockSpec((mem_len, 2 * CROSS_WIDTH), lambda bi, i: (bi, 0)),
            const((d, CROSS_WIDTH)), const((CROSS_WIDTH, d)), const((1, d)),
            const((d, N_EXPERTS)), const((d, N_EXPERTS)), const((1, N_EXPERTS)),
        ],
        out_specs=(
            pl.BlockSpec((tm, d), row),
            pl.BlockSpec((tm, d), row),
            pl.BlockSpec((tm, TOP_K), row),
            pl.BlockSpec((tm, TOP_K), row),
            pl.BlockSpec((tm, TOP_K), row),
            pl.BlockSpec((1, N_EXPERTS), lambda bi, i: (0, 0)),
        ),
        scratch_shapes=[pltpu.VMEM((1, N_EXPERTS), F32)],
        compiler_params=_cparams(("arbitrary", "arbitrary")),
        name=name,
    )(n2, h1, kv, wq, wo, gm.reshape(1, d), wrh, wrl, br.reshape(1, N_EXPERTS))


_HI_MASK = 0xFFFF0000


def _pack_rows(x):
    half = x.shape[1] // 2
    lo = lax.bitcast_convert_type(x[:, :half].astype(BF16).astype(F32), U32) >> 16
    hi = lax.bitcast_convert_type(x[:, half:].astype(BF16).astype(F32), U32) & jnp.uint32(_HI_MASK)
    return lo | hi


def _unpack_rows(u):
    lo = lax.bitcast_convert_type(u << 16, F32)
    hi = lax.bitcast_convert_type(u & jnp.uint32(_HI_MASK), F32)
    return lo, hi


def _slab(ref, row):
    return ref.at[pl.ds(pl.multiple_of(row * MOE_SLAB, MOE_SLAB), MOE_SLAB)]


def _row_copies(make_copy, make_group_wait, count):
    ngroup = count // DMA_WAIT_ROWS

    def group(c, carry):
        def pair(j, carry2):
            a = c * DMA_WAIT_ROWS + 2 * j
            make_copy(a).start(priority=0)
            make_copy(a + 1).start(priority=1)
            return carry2

        lax.fori_loop(0, DMA_WAIT_ROWS // 2, pair, 0, unroll=DMA_ISSUE_UNROLL // 2)

        @pl.when(c > 0)
        def _():
            make_group_wait().wait()

        return carry

    lax.fori_loop(0, ngroup, group, 0)
    make_group_wait().wait()


def _dispatch_kernel(gstart_ref, cnt_ref, padded_ref, e_ref, rank_ref, n3_ref, xs_hbm, stage, sem,
                     *, real_steps, tm, rows_total):
    step = pl.program_id(0)
    group_rows = DMA_WAIT_ROWS * MOE_SLAB

    def group_wait():
        return pltpu.make_async_copy(stage.at[pl.ds(0, group_rows)], xs_hbm.at[pl.ds(0, group_rows)], sem)

    @pl.when(step < real_steps)
    def _():
        u = _pack_rows(n3_ref[...])
        for j in range(MOE_SLAB):
            stage[pl.ds(j, tm, stride=MOE_SLAB), :] = u[:, j * V7X_LANES:(j + 1) * V7X_LANES]

        def make_copy(a):
            slot = gstart_ref[e_ref[0, a]] + rank_ref[0, a]
            return pltpu.make_async_copy(_slab(stage, a >> TOP_K_LOG2), _slab(xs_hbm, slot), sem)

        _row_copies(make_copy, group_wait, tm * TOP_K)

    @pl.when(step == real_steps)
    def _():
        stage[0:MOE_SLAB, :] = jnp.zeros((MOE_SLAB, V7X_LANES), U32)

        def fill(lo, hi):
            def one(r, carry):
                pltpu.make_async_copy(_slab(stage, 0), _slab(xs_hbm, r), sem).start()
                return carry

            lax.fori_loop(lo, hi, one, 0)

        def per_expert(e, carry):
            fill(gstart_ref[e] + cnt_ref[e], gstart_ref[e] + padded_ref[e])
            return carry

        lax.fori_loop(0, N_EXPERTS, per_expert, 0)
        fill(gstart_ref[N_EXPERTS - 1] + padded_ref[N_EXPERTS - 1], rows_total)

        def retire(c, carry):
            group_wait().wait()
            return carry

        lax.fori_loop(0, (rows_total - real_steps * tm * TOP_K) // DMA_WAIT_ROWS, retire, 0)


def _dispatch(gstart, cnt, padded, top_e, rank, n3, *, rows_total, tm, name="moe_dispatch"):
    n, d = n3.shape
    real_steps = n // tm
    blk = pl.BlockSpec((None, 1, tm * TOP_K), lambda i, *_: (jnp.minimum(i, real_steps - 1), 0, 0),
                       memory_space=pltpu.SMEM)
    grid_spec = pltpu.PrefetchScalarGridSpec(
        num_scalar_prefetch=3,
        grid=(real_steps + 1,),
        in_specs=[blk, blk, pl.BlockSpec((tm, d), lambda i, *_: (jnp.minimum(i, real_steps - 1), 0))],
        out_specs=pl.BlockSpec(memory_space=pl.ANY),
        scratch_shapes=[pltpu.VMEM((tm * MOE_SLAB, V7X_LANES), U32), pltpu.SemaphoreType.DMA],
    )
    return pl.pallas_call(
        functools.partial(_dispatch_kernel, real_steps=real_steps, tm=tm, rows_total=rows_total),
        out_shape=jax.ShapeDtypeStruct((rows_total * MOE_SLAB, V7X_LANES), U32),
        grid_spec=grid_spec,
        compiler_params=_cparams(("arbitrary",)),
        name=name,
    )(gstart, cnt, padded, top_e.reshape(real_steps, 1, tm * TOP_K), rank.reshape(real_steps, 1, tm * TOP_K), n3)


def _moe_kernel(te_ref, tv_ref, tstart_ref, trows_ref,
                xs_hbm, w1_ref, b1_ref, w2_ref, b2_ref, perm_ref, yb_hbm,
                stage, x16, acc, w1p, sems, *, nf):
    i = pl.program_id(0)
    f = pl.program_id(1)
    rows = trows_ref[i]
    start = tstart_ref[i]
    nchunk = rows // MOE_ROW_ALIGN
    chunk_slab_rows = MOE_ROW_ALIGN * MOE_SLAB
    half = 2 * V7X_LANES
    nhalf = 2 * MOE_TF // half

    def slot_rows(c):
        return pl.multiple_of((c & 1) * chunk_slab_rows, chunk_slab_rows)

    def hbm_rows(ref, c):
        r0 = pl.multiple_of((start + c * MOE_ROW_ALIGN) * MOE_SLAB, chunk_slab_rows)
        return ref.at[pl.ds(r0, chunk_slab_rows)]

    def in_copy(c):
        return pltpu.make_async_copy(hbm_rows(xs_hbm, c), stage.at[pl.ds(slot_rows(c), chunk_slab_rows)],
                                     sems.at[c & 1])

    def out_copy(c):
        return pltpu.make_async_copy(stage.at[pl.ds(slot_rows(c), chunk_slab_rows)], hbm_rows(yb_hbm, c),
                                     sems.at[c & 1])

    @pl.when((rows > 0) & (f == 0))
    def _():
        in_copy(0).start()

        def land(c, carry):
            @pl.when(c + 1 < nchunk)
            def _():
                in_copy(c + 1).start()

            in_copy(c).wait()
            r0 = pl.multiple_of(c * MOE_ROW_ALIGN, MOE_ROW_ALIGN)
            s0 = slot_rows(c)
            for j in range(MOE_SLAB):
                lo, hi = _unpack_rows(stage[pl.ds(s0 + j, MOE_ROW_ALIGN, stride=MOE_SLAB), :])
                x16[pl.ds(r0, MOE_ROW_ALIGN), j * V7X_LANES:(j + 1) * V7X_LANES] = lo.astype(BF16)
                x16[pl.ds(r0, MOE_ROW_ALIGN), (MOE_SLAB + j) * V7X_LANES:(MOE_SLAB + j + 1) * V7X_LANES] = (
                    hi.astype(BF16))
            acc[pl.ds(r0, MOE_ROW_ALIGN), :] = jnp.zeros((MOE_ROW_ALIGN, D_MODEL), F32)
            return carry

        lax.fori_loop(0, nchunk, land, 0)

    @pl.when(rows > 0)
    def _():
        perm = perm_ref[...]
        for c in range(nhalf):
            wb = w1_ref[:, c * half:(c + 1) * half].astype(BF16)
            w1p[:, c * half:(c + 1) * half] = jnp.dot(wb, perm, preferred_element_type=F32).astype(BF16)
        w2 = w2_ref[...].astype(BF16)
        b1 = b1_ref[...]

        def mlp(c0, nc):
            r0 = pl.multiple_of(c0 * MOE_ROW_ALIGN, MOE_ROW_ALIGN)
            nr = nc * MOE_ROW_ALIGN
            x = x16[pl.ds(r0, nr), :]
            h = jnp.dot(x, w1p[...], preferred_element_type=F32) + b1
            parts = []
            for c in range(nhalf):
                hg = jnp.minimum(h[:, c * half:c * half + V7X_LANES], SWIGLU_LIMIT)
                hl = jnp.clip(h[:, c * half + V7X_LANES:(c + 1) * half], -SWIGLU_LIMIT, SWIGLU_LIMIT)
                parts.append(hg * jax.nn.sigmoid(SWIGLU_ALPHA * hg) * (hl + 1.0))
            a = jnp.concatenate(parts, axis=-1).astype(BF16)
            acc[pl.ds(r0, nr), :] += jnp.dot(a, w2, preferred_element_type=F32)

        nquad = nchunk // MOE_BLOCK_CHUNKS

        def quad(c, carry):
            mlp(c * MOE_BLOCK_CHUNKS, MOE_BLOCK_CHUNKS)
            return carry

        lax.fori_loop(0, nquad, quad, 0)
        size = MOE_BLOCK_CHUNKS // 2
        while size >= 1:
            done = nchunk & ~(2 * size - 1)

            @pl.when((nchunk & size) != 0)
            def _(done=done, size=size):
                mlp(done, size)

            size //= 2

    @pl.when((rows > 0) & (f == nf - 1))
    def _():
        def emit(c, carry):
            @pl.when(c >= 2)
            def _():
                out_copy(c - 2).wait()

            r0 = pl.multiple_of(c * MOE_ROW_ALIGN, MOE_ROW_ALIGN)
            s0 = slot_rows(c)
            u = _pack_rows(acc[pl.ds(r0, MOE_ROW_ALIGN), :] + b2_ref[...])
            for j in range(MOE_SLAB):
                stage[pl.ds(s0 + j, MOE_ROW_ALIGN, stride=MOE_SLAB), :] = u[:, j * V7X_LANES:(j + 1) * V7X_LANES]
            out_copy(c).start()
            return carry

        lax.fori_loop(0, nchunk, emit, 0)

        @pl.when(nchunk >= 2)
        def _():
            out_copy(nchunk - 2).wait()

        out_copy(nchunk - 1).wait()


def _moe(tile_e, tile_v, tile_start, tile_rows, xs, w1, b1p, w2, b2, perm, *, name="moe_mlp"):
    nt = tile_e.shape[0]
    nf = D_FF // MOE_TF

    def w1_map(i, f, te, tv, ts, tr):
        return (te[i], 0, jnp.where(tv[i] > 0, f, nf - 1))

    def w2_map(i, f, te, tv, ts, tr):
        return (te[i], jnp.where(tv[i] > 0, f, nf - 1), 0)

    def b2_map(i, f, te, tv, ts, tr):
        return (te[i], 0, 0)

    grid_spec = pltpu.PrefetchScalarGridSpec(
        num_scalar_prefetch=4,
        grid=(nt, nf),
        in_specs=[
            pl.BlockSpec(memory_space=pl.ANY),
            pl.BlockSpec((None, D_MODEL, 2 * MOE_TF), w1_map),
            pl.BlockSpec((None, 1, 2 * MOE_TF), w1_map),
            pl.BlockSpec((None, MOE_TF, D_MODEL), w2_map),
            pl.BlockSpec((None, 1, D_MODEL), b2_map),
            pl.BlockSpec((2 * V7X_LANES, 2 * V7X_LANES), lambda i, f, te, tv, ts, tr: (0, 0)),
        ],
        out_specs=pl.BlockSpec(memory_space=pl.ANY),
        scratch_shapes=[
            pltpu.VMEM((2 * MOE_ROW_ALIGN * MOE_SLAB, V7X_LANES), U32),
            pltpu.VMEM((MOE_TMAX, D_MODEL), BF16),
            pltpu.VMEM((MOE_TMAX, D_MODEL), F32),
            pltpu.VMEM((D_MODEL, 2 * MOE_TF), BF16),
            pltpu.SemaphoreType.DMA((2,)),
        ],
    )
    return pl.pallas_call(
        functools.partial(_moe_kernel, nf=nf),
        out_shape=jax.ShapeDtypeStruct(xs.shape, xs.dtype),
        grid_spec=grid_spec,
        input_output_aliases={4: 0},
        compiler_params=_cparams(("arbitrary", "arbitrary")),
        name=name,
    )(tile_e, tile_v, tile_start, tile_rows, xs, w1, b1p.reshape(N_EXPERTS, 1, 2 * D_FF), w2,
      b2.reshape(N_EXPERTS, 1, D_MODEL), perm)


def _combine_kernel(gstart_ref, e_ref, rank_ref, h2_ref, tw_ref, gf_ref, yb_hbm, o_ref, buf, sem, *, tm):
    group_rows = DMA_WAIT_ROWS * MOE_SLAB

    def make_copy(a):
        t = a >> TOP_K_LOG2
        k = a & (TOP_K - 1)
        slot = gstart_ref[e_ref[0, a]] + rank_ref[0, a]
        return pltpu.make_async_copy(_slab(yb_hbm, slot), _slab(buf, k * tm + t), sem)

    def group_wait():
        return pltpu.make_async_copy(yb_hbm.at[pl.ds(0, group_rows)], buf.at[pl.ds(0, group_rows)], sem)

    _row_copies(make_copy, group_wait, tm * TOP_K)

    tw = tw_ref[...]
    cols_lo, cols_hi = [], []
    for j in range(MOE_SLAB):
        clo = chi = None
        for k in range(TOP_K):
            lo, hi = _unpack_rows(buf[pl.ds(k * tm * MOE_SLAB + j, tm, stride=MOE_SLAB), :])
            w = tw[:, k:k + 1]
            clo = w * lo if clo is None else clo + w * lo
            chi = w * hi if chi is None else chi + w * hi
        cols_lo.append(clo)
        cols_hi.append(chi)
    h3 = h2_ref[...] + jnp.concatenate(cols_lo + cols_hi, axis=-1)
    o_ref[...] = _rms(h3, gf_ref[...])


def _combine(gstart, top_e, rank, h2, tw, gf, yb, *, tm, name="moe_combine"):
    n, d = h2.shape
    steps = n // tm
    blk = pl.BlockSpec((None, 1, tm * TOP_K), lambda i, gs: (i, 0, 0), memory_space=pltpu.SMEM)
    grid_spec = pltpu.PrefetchScalarGridSpec(
        num_scalar_prefetch=1,
        grid=(steps,),
        in_specs=[
            blk, blk,
            pl.BlockSpec((tm, d), lambda i, gs: (i, 0)),
            pl.BlockSpec((tm, TOP_K), lambda i, gs: (i, 0)),
            pl.BlockSpec((1, d), lambda i, gs: (0, 0)),
            pl.BlockSpec(memory_space=pl.ANY),
        ],
        out_specs=pl.BlockSpec((tm, d), lambda i, gs: (i, 0)),
        scratch_shapes=[
            pltpu.VMEM((tm * TOP_K * MOE_SLAB, V7X_LANES), U32),
            pltpu.SemaphoreType.DMA,
        ],
    )
    return pl.pallas_call(
        functools.partial(_combine_kernel, tm=tm),
        out_shape=jax.ShapeDtypeStruct((n, d), F32),
        grid_spec=grid_spec,
        compiler_params=_cparams(("arbitrary",)),
        name=name,
    )(gstart, top_e.reshape(steps, 1, tm * TOP_K), rank.reshape(steps, 1, tm * TOP_K), h2, tw,
      gf.reshape(1, d), yb)


def _rope_tables(positions):
    b, s = positions.shape
    half = HEAD_DIM // 2
    inv_freq = 1.0 / (ROPE_THETA ** (jnp.arange(half, dtype=F32) / half))
    ang = positions.astype(F32)[..., None] * inv_freq
    cos = jnp.cos(ang)
    sin = jnp.sin(ang)
    cs = jnp.concatenate([cos, cos], axis=-1)
    sn = jnp.concatenate([-sin, sin], axis=-1)

    def deint(a):
        a = a.reshape(b, s // ATT_TILE, WIN_BLOCK, DIL_MAX, HEAD_DIM)
        return a.transpose(0, 1, 3, 2, 4)

    return deint(cs), deint(sn)


def _routing(counts, n_tok):
    cnt = counts.reshape(N_EXPERTS).astype(I32)
    padded = (cnt + MOE_ROW_ALIGN - 1) // MOE_ROW_ALIGN * MOE_ROW_ALIGN
    gstart = (jnp.cumsum(padded) - padded).astype(I32)
    rows_total = n_tok * TOP_K + N_EXPERTS * MOE_ROW_ALIGN

    def first_above(ends, q):
        return jnp.minimum(jnp.sum((ends[None, :] <= q[:, None]).astype(I32), axis=1), N_EXPERTS - 1)

    main_rows = jnp.minimum(padded, MOE_TMAX)
    main = (jnp.arange(N_EXPERTS, dtype=I32), (main_rows > 0).astype(I32), gstart.astype(I32),
            main_rows.astype(I32))

    over = padded - main_rows
    nt_e = (over + MOE_TMAX - 1) // MOE_TMAX
    tend = jnp.cumsum(nt_e)
    tstart = tend - nt_e
    n_over = tend[-1]
    ti = jnp.arange(rows_total // MOE_TMAX, dtype=I32)
    valid = ti < n_over
    tic = jnp.clip(ti, 0, jnp.maximum(n_over - 1, 0))
    te = first_above(tend, tic)
    local = tic - tstart[te]
    row0 = gstart[te] + (local + 1) * MOE_TMAX
    rows = jnp.where(valid, jnp.clip(over[te] - local * MOE_TMAX, 0, MOE_TMAX), 0)
    overflow = (te, valid.astype(I32), row0.astype(I32), rows.astype(I32))
    return gstart, cnt, padded, rows_total, main, overflow, n_over > 0


def kernel(x, mem, positions, norm_mix_g, w_in, lb_raw, hgrn_norm_g, w_br_hgrn, w_br_attn, w_out,
           norm_cross_g, norm_mem_g, w_cq, w_ckv, w_co, norm_moe_g, w_router, b_router,
           w_mlp1, b_mlp1, w_mlp2, b_mlp2, norm_final_g):
    bsz, seq, d = x.shape
    assert w_in.shape[0] == 1 and d == D_MODEL and seq % ATT_TILE == 0
    n_tok = bsz * seq
    lower_bounds = jnp.cumsum(jax.nn.softmax(lb_raw.astype(F32), axis=0), axis=0)
    cs, sn = _rope_tables(positions)

    wl = w_in[0].astype(BF16)
    c_h = 4 * HG_WIDTH
    c_a = c_h + ATT_Q_WIDTH + 2 * ATT_KV_WIDTH
    x2d = x.reshape(n_tok, d)
    proj_h = _in_proj(x2d, norm_mix_g[0], wl, tm=1024, tn=1024, name="in_proj_hgrn", col0=0, ncols=c_h)
    proj_g = _in_proj(x2d, norm_mix_g[0], wl, tm=1024, tn=1024, name="in_proj_gate", col0=c_a,
                      ncols=2 * D_MODEL)
    pa = _in_proj_strided(x, norm_mix_g[0], wl, c=8, tn=1024, name="in_proj_attn", col0=c_h, ncols=c_a - c_h)

    o_h = _hgrn(proj_h.reshape(bsz, seq, c_h), lower_bounds[0], hgrn_norm_g[0], ts=512)
    o_a = _dilated_attn(pa, cs, sn)

    h1, n2 = _merge_out(o_h.reshape(n_tok, HG_WIDTH), o_a.reshape(n_tok, ATT_KV_WIDTH), proj_g, x2d,
                        w_br_hgrn[0].astype(BF16), w_br_attn[0].astype(BF16), w_out[0].astype(BF16),
                        norm_cross_g[0], tm=256)

    mem2d = mem.reshape(-1, d)
    kv = _in_proj(mem2d, norm_mem_g, w_ckv[0].astype(BF16), tm=mem2d.shape[0], tn=2 * CROSS_WIDTH, name="mem_kv")
    wrh = w_router[0].astype(BF16)
    wrl = (w_router[0] - wrh.astype(F32)).astype(BF16)
    h2, n3, top_e, top_w, rank, counts = _cross(
        n2, h1, kv, w_cq[0].astype(BF16), w_co[0].astype(BF16), norm_moe_g[0], wrh, wrl, b_router[0],
        batch=bsz, tm=512)

    gstart, cnt, padded, rows_total, main_tiles, over_tiles, has_over = _routing(counts, n_tok)
    xs = _dispatch(gstart, cnt, padded, top_e, rank, n3, rows_total=rows_total, tm=DISPATCH_TOKENS)

    ii = jnp.arange(2 * V7X_LANES)
    src = jnp.where(ii < V7X_LANES, 2 * ii, 2 * (ii - V7X_LANES) + 1)
    perm = (ii[:, None] == src[None, :]).astype(BF16)
    b1p = b_mlp1[0].reshape(N_EXPERTS, -1, V7X_LANES, 2).transpose(0, 1, 3, 2).reshape(N_EXPERTS, 2 * D_FF)
    def run_moe(tiles, rows_buf, name):
        return _moe(*tiles, rows_buf, w_mlp1[0], b1p, w_mlp2[0], b_mlp2[0], perm, name=name)

    yb = run_moe(main_tiles, xs, "moe_mlp")
    yb = lax.cond(has_over, lambda rows_buf: run_moe(over_tiles, rows_buf, "moe_mlp_overflow"),
                  lambda rows_buf: rows_buf, yb)

    out = _combine(gstart, top_e, rank, h2, top_w, norm_final_g, yb, tm=COMBINE_TOKENS)
    return out.reshape(bsz, seq, d)
```

```python
import functools

import jax
import jax.numpy as jnp
from jax import lax
from jax.experimental import pallas as pl
from jax.experimental.pallas import tpu as pltpu

F32 = jnp.float32
BF16 = jnp.bfloat16
I32 = jnp.int32
U32 = jnp.uint32

D_MODEL = 2048
HEAD_DIM = 128
HG_HEADS = 8
HG_WIDTH = HG_HEADS * HEAD_DIM
N_KV_HEADS = 8
N_GROUPS = 3
ATT_Q_WIDTH = N_GROUPS * N_KV_HEADS * HEAD_DIM
ATT_KV_WIDTH = N_KV_HEADS * HEAD_DIM
WIN_BLOCK = 128
ROPE_THETA = 10000.0
CROSS_HEADS = 4
CROSS_WIDTH = CROSS_HEADS * HEAD_DIM
N_EXPERTS = 32
TOP_K = 4
TOP_K_LOG2 = 2
D_FF = D_MODEL
SWIGLU_ALPHA = 1.702
SWIGLU_LIMIT = 7.0
NORM_EPS = 1e-6

V7X_LANES = 128
V7X_VMEM_LIMIT_BYTES = 56 * 1024 * 1024

DIL_MAX = 16
ATT_TILE = DIL_MAX * WIN_BLOCK
ATT_UNROLL = 16
HG_BLOCK = 16
HG_GROUP = 16
NEG_BIG = -1e30

MOE_TMAX = 1536
MOE_ROW_ALIGN = 128
MOE_BLOCK_CHUNKS = 8
MOE_TF = 512
MOE_SLAB = D_MODEL // 2 // V7X_LANES
DISPATCH_TOKENS = 512
COMBINE_TOKENS = 512
DMA_WAIT_ROWS = 128
DMA_ISSUE_UNROLL = 8


def _rms(x, g):
    ms = jnp.mean(x * x, axis=-1, keepdims=True)
    return x * lax.rsqrt(ms + NORM_EPS) * g


def _cparams(sem, vmem=V7X_VMEM_LIMIT_BYTES):
    return pltpu.CompilerParams(dimension_semantics=sem, vmem_limit_bytes=vmem)


def _in_proj_kernel(x_ref, g_ref, w_ref, o_ref, xn_ref):
    @pl.when(pl.program_id(1) == 0)
    def _():
        xn_ref[...] = _rms(x_ref[...], g_ref[...]).astype(BF16)

    o_ref[...] = jnp.dot(xn_ref[...], w_ref[...].astype(BF16), preferred_element_type=F32).astype(o_ref.dtype)


def _in_proj(x2d, g, w, *, tm, tn, name, col_block=lambda j: j, ncols=None):
    n, d = x2d.shape
    wc = w.shape[1] if ncols is None else ncols
    return pl.pallas_call(
        _in_proj_kernel,
        out_shape=jax.ShapeDtypeStruct((n, wc), BF16),
        grid=(n // tm, wc // tn),
        in_specs=[
            pl.BlockSpec((tm, d), lambda i, j: (i, 0)),
            pl.BlockSpec((1, d), lambda i, j: (0, 0)),
            pl.BlockSpec((d, tn), lambda i, j: (0, col_block(j))),
        ],
        out_specs=pl.BlockSpec((tm, tn), lambda i, j: (i, j)),
        scratch_shapes=[pltpu.VMEM((tm, d), BF16)],
        compiler_params=_cparams(("parallel", "arbitrary")),
        name=name,
    )(x2d, g.reshape(1, d), w)


def _in_proj_strided_kernel(x_ref, g_ref, w_ref, o_ref, xn_ref, *, c):
    @pl.when(pl.program_id(3) == 0)
    def _():
        for ci in range(c):
            x = x_ref[:, ci * D_MODEL:(ci + 1) * D_MODEL]
            xn_ref[ci * WIN_BLOCK:(ci + 1) * WIN_BLOCK, :] = _rms(x, g_ref[...]).astype(BF16)

    res = jnp.dot(xn_ref[...], w_ref[...].astype(BF16), preferred_element_type=F32)
    for ci in range(c):
        o_ref[ci] = res[ci * WIN_BLOCK:(ci + 1) * WIN_BLOCK, :].astype(o_ref.dtype)


def _in_proj_strided(x, g, w, *, c, tn, name, col0, ncols):
    b, s, d = x.shape
    nt = s // ATT_TILE
    wc = ncols
    cb = col0 // tn
    xv = x.reshape(b, nt, WIN_BLOCK, DIL_MAX * d)
    return pl.pallas_call(
        functools.partial(_in_proj_strided_kernel, c=c),
        out_shape=jax.ShapeDtypeStruct((b, nt, DIL_MAX, WIN_BLOCK, wc), BF16),
        grid=(b, nt, DIL_MAX // c, wc // tn),
        in_specs=[
            pl.BlockSpec((None, None, WIN_BLOCK, c * d), lambda bi, ti, ri, j: (bi, ti, 0, ri)),
            pl.BlockSpec((1, d), lambda bi, ti, ri, j: (0, 0)),
            pl.BlockSpec((d, tn), lambda bi, ti, ri, j: (0, cb + j)),
        ],
        out_specs=pl.BlockSpec((None, None, c, WIN_BLOCK, tn), lambda bi, ti, ri, j: (bi, ti, ri, 0, j)),
        scratch_shapes=[pltpu.VMEM((c * WIN_BLOCK, d), BF16)],
        compiler_params=_cparams(("parallel", "parallel", "parallel", "arbitrary")),
        name=name,
    )(xv, g.reshape(1, d), w)


def _hgrn_kernel(q_ref, f_ref, i_ref, g_ref, lb_ref, gn_ref, o_ref, st_ref, kin_s, b_s, v_s, *, ts):
    @pl.when(pl.program_id(2) == 0)
    def _():
        st_ref[...] = jnp.zeros_like(st_ref)

    lb = lb_ref[...]
    oml = 1.0 - lb
    gn = gn_ref[...]
    half = HG_BLOCK // 2
    row = lax.broadcasted_iota(I32, (HG_BLOCK, HEAD_DIM), 0)
    row8 = lax.broadcasted_iota(I32, (half, HEAD_DIM), 0)
    nt_dims = (((1,), (1,)), ((), ()))
    tn_dims = (((0,), (0,)), ((), ()))

    def front(g, t0):
        sl = pl.ds(t0 + g * HG_BLOCK, HG_BLOCK)
        q = q_ref[sl, :].astype(F32)
        hf = f_ref[sl, :].astype(F32)
        v = i_ref[sl, :].astype(F32)
        kin = oml * jax.nn.sigmoid(-hf)
        b = jnp.log2(lb + oml * jax.nn.sigmoid(hf))
        for sh in (1, 2, 4, 8):
            b = b + jnp.where(row >= sh, pltpu.roll(b, sh, 0), 0.0)
        kin_s[g] = kin
        b_s[g] = b
        v_s[g] = v
        q_lo, q_hi = q[:half], q[half:]
        b_lo, b_hi = b[:half], b[half:]
        o_lo = jnp.zeros((half, HEAD_DIM), F32)
        o_hi = jnp.zeros((half, HEAD_DIM), F32)
        for s in range(HG_BLOCK):
            ks = kin_s[g, s:s + 1, :]
            bs = b_s[g, s:s + 1, :]
            vs = v_s[g, s:s + 1, :]
            if s < half:
                w = q_lo * ks * jnp.exp2(b_lo - bs)
                if s > 0:
                    w = jnp.where(row8 >= s, w, 0.0)
                o_lo = o_lo + jnp.sum(w, axis=-1, keepdims=True) * vs
                w = q_hi * ks * jnp.exp2(b_hi - bs)
            else:
                w = q_hi * ks * jnp.exp2(b_hi - bs)
                if s > half:
                    w = jnp.where(row8 >= s - half, w, 0.0)
            o_hi = o_hi + jnp.sum(w, axis=-1, keepdims=True) * vs
        bl = b_s[g, HG_BLOCK - 1:HG_BLOCK, :]
        qd = (q * jnp.exp2(b)).astype(BF16)
        kd = (kin * jnp.exp2(bl - b)).astype(BF16)
        upd = lax.dot_general(v.astype(BF16), kd, tn_dims, preferred_element_type=F32)
        return jnp.concatenate([o_lo, o_hi], axis=0), qd, upd, jnp.exp2(bl)

    def body(i, carry):
        t0 = pl.multiple_of(i * (HG_GROUP * HG_BLOCK), HG_GROUP * HG_BLOCK)
        fronts = [front(g, t0) for g in range(HG_GROUP)]
        st = st_ref[...]
        for g, (o_diag, qd, upd, dec) in enumerate(fronts):
            o = o_diag + lax.dot_general(qd, st.astype(BF16), nt_dims, preferred_element_type=F32)
            st = st * dec + upd
            sl = pl.ds(t0 + g * HG_BLOCK, HG_BLOCK)
            hg = g_ref[sl, :].astype(F32)
            o_ref[sl, :] = (_rms(o, gn) * (hg * jax.nn.sigmoid(hg))).astype(o_ref.dtype)
        st_ref[...] = st
        return carry

    lax.fori_loop(0, ts // (HG_GROUP * HG_BLOCK), body, 0)


def _hgrn(proj_h, lb, gn, *, ts, name="hgrn"):
    b, s, _ = proj_h.shape
    h = HG_HEADS

    def spec(off):
        return pl.BlockSpec((None, ts, HEAD_DIM), lambda bi, hi, si: (bi, si, off + hi))

    vec = pl.BlockSpec((1, HEAD_DIM), lambda bi, hi, si: (0, hi))
    return pl.pallas_call(
        functools.partial(_hgrn_kernel, ts=ts),
        out_shape=jax.ShapeDtypeStruct((b, s, HG_WIDTH), BF16),
        grid=(b, h, s // ts),
        in_specs=[spec(0), spec(h), spec(2 * h), spec(3 * h), vec, vec],
        out_specs=pl.BlockSpec((None, ts, HEAD_DIM), lambda bi, hi, si: (bi, si, hi)),
        scratch_shapes=[
            pltpu.VMEM((HEAD_DIM, HEAD_DIM), F32),
            pltpu.VMEM((HG_GROUP, HG_BLOCK, HEAD_DIM), F32),
            pltpu.VMEM((HG_GROUP, HG_BLOCK, HEAD_DIM), F32),
            pltpu.VMEM((HG_GROUP, HG_BLOCK, HEAD_DIM), F32),
        ],
        compiler_params=_cparams(("parallel", "parallel", "arbitrary")),
        name=name,
    )(proj_h, proj_h, proj_h, proj_h, lb.reshape(1, HG_WIDTH), gn.reshape(1, HG_WIDTH))


def _attn_bias(kind):
    rq = lax.broadcasted_iota(I32, (WIN_BLOCK, 2 * WIN_BLOCK), 0)
    ck = lax.broadcasted_iota(I32, (WIN_BLOCK, 2 * WIN_BLOCK), 1)
    if kind == 2:
        dist = rq + WIN_BLOCK - ck
        first = ck < WIN_BLOCK
    elif kind == 1:
        dist = 4 * ((rq & 31) - (ck & 63) + 32) + ((rq >> 5) - (ck >> 6))
        first = (ck & 63) < 32
    else:
        dist = 16 * ((rq & 7) - (ck & 15) + 8) + ((rq >> 3) - (ck >> 4))
        first = (ck & 15) < 8
    valid = (dist >= 0) & (dist <= WIN_BLOCK)
    return (jnp.where(valid, 0.0, NEG_BIG).astype(F32),
            jnp.where(valid & jnp.logical_not(first), 0.0, NEG_BIG).astype(F32))


def _attn_kernel(q0_ref, q1_ref, q2_ref, k_ref, v_ref, cs_ref, sn_ref, o_ref,
                 qr, kext, vext, acc, mrun, lrun, bias, onat):
    ti = pl.program_id(2)
    wb = WIN_BLOCK
    scale = HEAD_DIM ** -0.5

    @pl.when(ti == 0)
    def _():
        kext[:, 0:wb, :] = jnp.zeros((DIL_MAX, wb, HEAD_DIM), F32)
        vext[:, 0:wb, :] = jnp.zeros((DIL_MAX, wb, HEAD_DIM), F32)

    @pl.when(ti > 0)
    def _():
        kext[:, 0:wb, :] = kext[:, wb:2 * wb, :]
        vext[:, 0:wb, :] = vext[:, wb:2 * wb, :]

    for kind in range(N_GROUPS):
        full, nofirst = _attn_bias(kind)
        bias[2 * kind] = full
        bias[2 * kind + 1] = nofirst

    def rope_body(r, carry):
        cs = cs_ref[r]
        sn = sn_ref[r]
        for g, qref in enumerate((q0_ref, q1_ref, q2_ref)):
            q = qref[r].astype(F32)
            qr[g, r] = (q * cs + pltpu.roll(q, HEAD_DIM // 2, 1) * sn) * scale
        k = k_ref[r].astype(F32)
        kext[r, wb:2 * wb, :] = k * cs + pltpu.roll(k, HEAD_DIM // 2, 1) * sn
        vext[r, wb:2 * wb, :] = v_ref[r].astype(F32)
        acc[r] = jnp.zeros((wb, HEAD_DIM), F32)
        mrun[r] = jnp.full((wb, HEAD_DIM), NEG_BIG, F32)
        lrun[r] = jnp.zeros((wb, HEAD_DIM), F32)
        return carry

    lax.fori_loop(0, DIL_MAX, rope_body, 0)

    nt_dims = (((1,), (1,)), ((), ()))

    def block(qb, kb, vb, bias_blk):
        s = lax.dot_general(qb.astype(BF16), kb.astype(BF16), nt_dims, preferred_element_type=F32)
        s = s + bias_blk
        m = jnp.max(s, axis=-1, keepdims=True)
        p = jnp.exp(s - m)
        l = jnp.sum(p, axis=-1, keepdims=True)
        n = jnp.dot(p.astype(BF16), vb.astype(BF16), preferred_element_type=F32)
        return n, jnp.broadcast_to(m, (wb, HEAD_DIM)), jnp.broadcast_to(l, (wb, HEAD_DIM))

    def merge(r, rows, n, m, l):
        m_old = mrun[r, rows, :]
        m_new = jnp.maximum(m_old, m)
        a = jnp.exp(m_old - m_new)
        bb = jnp.exp(m - m_new)
        acc[r, rows, :] = acc[r, rows, :] * a + n * bb
        lrun[r, rows, :] = lrun[r, rows, :] * a + l * bb
        mrun[r, rows, :] = m_new

    first_tile = jnp.where(ti == 0, 1, 0)


    def g2_body(i, carry):
        rs = [i * ATT_UNROLL + u for u in range(ATT_UNROLL)]
        res = [block(qr[2, r], kext[r], vext[r], bias[4 + first_tile]) for r in rs]
        for r, (n, m, l) in zip(rs, res):
            merge(r, pl.ds(0, wb), n, m, l)
        return carry

    lax.fori_loop(0, DIL_MAX // ATT_UNROLL, g2_body, 0)

    def g1_body(i, carry):
        res = []
        for u in range(ATT_UNROLL // 4):
            mb = i * (ATT_UNROLL // 4) + u
            q0 = pl.multiple_of(32 * mb, 32)
            k0 = pl.multiple_of(96 + 32 * mb, 32)
            use_first = jnp.where(mb == 0, first_tile, 0)
            for r4 in range(4):
                qb = jnp.concatenate([qr[1, r4 + 4 * j, pl.ds(q0, 32), :] for j in range(4)], axis=0)
                kb = jnp.concatenate([kext[r4 + 4 * j, pl.ds(k0, 64), :] for j in range(4)], axis=0)
                vb = jnp.concatenate([vext[r4 + 4 * j, pl.ds(k0, 64), :] for j in range(4)], axis=0)
                res.append((r4, q0, block(qb, kb, vb, bias[2 + use_first])))
        for r4, q0, (n, m, l) in res:
            for j in range(4):
                sl = slice(32 * j, 32 * (j + 1))
                merge(r4 + 4 * j, pl.ds(q0, 32), n[sl], m[sl], l[sl])
        return carry

    lax.fori_loop(0, DIL_MAX // ATT_UNROLL, g1_body, 0)

    def g0_body(i, carry):
        res = []
        for u in range(ATT_UNROLL):
            mb = i * ATT_UNROLL + u
            q0 = pl.multiple_of(8 * mb, 8)
            k0 = pl.multiple_of(120 + 8 * mb, 8)
            qb = jnp.concatenate([qr[0, r, pl.ds(q0, 8), :] for r in range(DIL_MAX)], axis=0)
            kb = jnp.concatenate([kext[r, pl.ds(k0, 16), :] for r in range(DIL_MAX)], axis=0)
            vb = jnp.concatenate([vext[r, pl.ds(k0, 16), :] for r in range(DIL_MAX)], axis=0)
            use_first = jnp.where(mb == 0, first_tile, 0)
            res.append((q0, block(qb, kb, vb, bias[use_first])))
        for q0, (n, m, l) in res:
            for r in range(DIL_MAX):
                sl = slice(8 * r, 8 * (r + 1))
                merge(r, pl.ds(q0, 8), n[sl], m[sl], l[sl])
        return carry

    lax.fori_loop(0, DIL_MAX // ATT_UNROLL, g0_body, 0)

    for r in range(DIL_MAX):
        onat[pl.ds(r, wb, stride=DIL_MAX), :] = acc[r] / lrun[r]
    o_ref[...] = onat[...].astype(o_ref.dtype)


def _dilated_attn(pa, cs, sn, *, name="dilated_attn"):
    b, nt = pa.shape[0], pa.shape[1]
    h = N_KV_HEADS

    def spec(off):
        return pl.BlockSpec((None, None, DIL_MAX, WIN_BLOCK, HEAD_DIM),
                            lambda bi, hi, ti: (bi, ti, 0, 0, off + hi))

    tab = pl.BlockSpec((None, None, DIL_MAX, WIN_BLOCK, HEAD_DIM), lambda bi, hi, ti: (bi, ti, 0, 0, 0))
    return pl.pallas_call(
        _attn_kernel,
        out_shape=jax.ShapeDtypeStruct((b, nt * ATT_TILE, ATT_KV_WIDTH), BF16),
        grid=(b, h, nt),
        in_specs=[spec(0), spec(h), spec(2 * h), spec(3 * h), spec(4 * h), tab, tab],
        out_specs=pl.BlockSpec((None, ATT_TILE, HEAD_DIM), lambda bi, hi, ti: (bi, ti, hi)),
        scratch_shapes=[
            pltpu.VMEM((N_GROUPS, DIL_MAX, WIN_BLOCK, HEAD_DIM), F32),
            pltpu.VMEM((DIL_MAX, 2 * WIN_BLOCK, HEAD_DIM), F32),
            pltpu.VMEM((DIL_MAX, 2 * WIN_BLOCK, HEAD_DIM), F32),
            pltpu.VMEM((DIL_MAX, WIN_BLOCK, HEAD_DIM), F32),
            pltpu.VMEM((DIL_MAX, WIN_BLOCK, HEAD_DIM), F32),
            pltpu.VMEM((DIL_MAX, WIN_BLOCK, HEAD_DIM), F32),
            pltpu.VMEM((2 * N_GROUPS, WIN_BLOCK, 2 * WIN_BLOCK), F32),
            pltpu.VMEM((ATT_TILE, HEAD_DIM), F32),
        ],
        compiler_params=_cparams(("arbitrary", "arbitrary", "arbitrary")),
        name=name,
    )(pa, pa, pa, pa, pa, cs, sn)


def _merge_out_kernel(oh_ref, oa_ref, gate_ref, x_ref, wh_ref, wa_ref, wo_ref, gc_ref, h1_ref, n2_ref):
    ga = gate_ref[:, :D_MODEL].astype(F32)
    gb = gate_ref[:, D_MODEL:].astype(F32)
    yh = jnp.dot(oh_ref[...], wh_ref[...], preferred_element_type=F32)
    ya = jnp.dot(oa_ref[...], wa_ref[...], preferred_element_type=F32)
    merged = jax.nn.sigmoid(ga) * yh + jax.nn.sigmoid(gb) * ya
    h1 = x_ref[...] + jnp.dot(merged.astype(BF16), wo_ref[...], preferred_element_type=F32)
    h1_ref[...] = h1
    n2_ref[...] = _rms(h1, gc_ref[...]).astype(BF16)


def _merge_out(oh, oa, gates, x2d, wh, wa, wo, gc, *, tm, gate_block, name="merge_out"):
    n, d = x2d.shape

    def const(shape):
        return pl.BlockSpec(shape, lambda i: (0, 0), pipeline_mode=pl.Buffered(1))

    return pl.pallas_call(
        _merge_out_kernel,
        out_shape=(jax.ShapeDtypeStruct((n, d), F32), jax.ShapeDtypeStruct((n, d), BF16)),
        grid=(n // tm,),
        in_specs=[
            pl.BlockSpec((tm, HG_WIDTH), lambda i: (i, 0)),
            pl.BlockSpec((tm, ATT_KV_WIDTH), lambda i: (i, 0)),
            pl.BlockSpec((tm, 2 * d), lambda i: (i, gate_block)),
            pl.BlockSpec((tm, d), lambda i: (i, 0)),
            const((HG_WIDTH, d)), const((ATT_KV_WIDTH, d)), const((d, d)), const((1, d)),
        ],
        out_specs=(pl.BlockSpec((tm, d), lambda i: (i, 0)), pl.BlockSpec((tm, d), lambda i: (i, 0))),
        compiler_params=_cparams(("parallel",)),
        name=name,
    )(oh, oa, gates, x2d, wh, wa, wo, gc.reshape(1, d))


def _cross_kernel(n2_ref, h1_ref, kv_ref, wq_ref, wo_ref, gm_ref, wrh_ref, wrl_ref, br_ref,
                  h2_ref, n3_ref, idx_ref, tw_ref, rank_ref, cnt_ref, carry_ref, *, tm):
    @pl.when((pl.program_id(0) == 0) & (pl.program_id(1) == 0))
    def _():
        carry_ref[...] = jnp.zeros_like(carry_ref)

    nt_dims = (((1,), (1,)), ((), ()))
    scale = HEAD_DIM ** -0.5
    q = (jnp.dot(n2_ref[...], wq_ref[...], preferred_element_type=F32) * scale).astype(BF16)
    outs = []
    for hh in range(CROSS_HEADS):
        sl = slice(hh * HEAD_DIM, (hh + 1) * HEAD_DIM)
        kh = kv_ref[:, sl]
        vh = kv_ref[:, CROSS_WIDTH + hh * HEAD_DIM:CROSS_WIDTH + (hh + 1) * HEAD_DIM]
        s = lax.dot_general(q[:, sl], kh, nt_dims, preferred_element_type=F32)
        p = jnp.exp(s - jnp.max(s, axis=-1, keepdims=True))
        l = jnp.sum(p, axis=-1, keepdims=True)
        outs.append(jnp.dot(p.astype(BF16), vh, preferred_element_type=F32) / l)
    o = jnp.concatenate(outs, axis=-1).astype(BF16)
    h2 = h1_ref[...] + jnp.dot(o, wo_ref[...], preferred_element_type=F32)
    h2_ref[...] = h2
    n3 = _rms(h2, gm_ref[...])
    n3_ref[...] = n3

    n3h = n3.astype(BF16)
    n3l = (n3 - n3h.astype(F32)).astype(BF16)
    wrh = wrh_ref[...]
    logits = (jnp.dot(n3h, wrh, preferred_element_type=F32)
              + jnp.dot(n3l, wrh, preferred_element_type=F32)
              + jnp.dot(n3h, wrl_ref[...], preferred_element_type=F32)
              + br_ref[...])
    lane = lax.broadcasted_iota(I32, (tm, N_EXPERTS), 1).astype(F32)
    vals, idxs, hots = [], [], []
    cur = logits
    for _ in range(TOP_K):
        mx = jnp.max(cur, axis=-1, keepdims=True)
        ix = jnp.min(jnp.where(cur == mx, lane, float(N_EXPERTS)), axis=-1, keepdims=True)
        hot = lane == ix
        vals.append(mx)
        idxs.append(ix)
        hots.append(hot)
        cur = jnp.where(hot, -jnp.inf, cur)
    es = [jnp.exp(v - vals[0]) for v in vals]
    den = es[0] + es[1] + es[2] + es[3]
    col = lax.broadcasted_iota(I32, (tm, TOP_K), 1)

    def pack(cols):
        out = jnp.broadcast_to(cols[TOP_K - 1], (tm, TOP_K))
        for k in range(TOP_K - 2, -1, -1):
            out = jnp.where(col == k, cols[k], out)
        return out

    idx_ref[...] = pack(idxs).astype(I32)
    tw_ref[...] = pack([e / den for e in es])

    cmat = (hots[0] | hots[1] | hots[2] | hots[3]).astype(F32)
    rr = lax.broadcasted_iota(I32, (tm, tm), 0)
    cc = lax.broadcasted_iota(I32, (tm, tm), 1)
    tri = (cc < rr).astype(BF16)
    before = jnp.dot(tri, cmat.astype(BF16), preferred_element_type=F32) + carry_ref[...]
    ranks = [jnp.sum(jnp.where(hot, before, 0.0), axis=-1, keepdims=True) for hot in hots]
    rank_ref[...] = pack(ranks).astype(I32)
    carry = carry_ref[...] + jnp.sum(cmat, axis=0, keepdims=True)
    carry_ref[...] = carry
    cnt_ref[...] = carry


def _cross(n2, h1, kv, wq, wo, gm, wrh, wrl, br, *, batch, tm, name="cross"):
    n, d = h1.shape
    per_b = n // batch // tm
    mem_len = kv.shape[0] // batch

    def const(shape):
        return pl.BlockSpec(shape, lambda bi, i: (0, 0), pipeline_mode=pl.Buffered(1))

    def row(bi, i):
        return (bi * per_b + i, 0)

    return pl.pallas_call(
        functools.partial(_cross_kernel, tm=tm),
        out_shape=(
            jax.ShapeDtypeStruct((n, d), F32),
            jax.ShapeDtypeStruct((n, d), F32),
            jax.ShapeDtypeStruct((n, TOP_K), I32),
            jax.ShapeDtypeStruct((n, TOP_K), F32),
            jax.ShapeDtypeStruct((n, TOP_K), I32),
            jax.ShapeDtypeStruct((1, N_EXPERTS), F32),
        ),
        grid=(batch, per_b),
        in_specs=[
            pl.BlockSpec((tm, d), row),
            pl.BlockSpec((tm, d), row),
            pl.BlockSpec((mem_len, 2 * CROSS_WIDTH), lambda bi, i: (bi, 0)),
            const((d, CROSS_WIDTH)), const((CROSS_WIDTH, d)), const((1, d)),
            const((d, N_EXPERTS)), const((d, N_EXPERTS)), const((1, N_EXPERTS)),
        ],
        out_specs=(
            pl.BlockSpec((tm, d), row),
            pl.BlockSpec((tm, d), row),
            pl.BlockSpec((tm, TOP_K), row),
            pl.BlockSpec((tm, TOP_K), row),
            pl.BlockSpec((tm, TOP_K), row),
            pl.BlockSpec((1, N_EXPERTS), lambda bi, i: (0, 0)),
        ),
        scratch_shapes=[pltpu.VMEM((1, N_EXPERTS), F32)],
        compiler_params=_cparams(("arbitrary", "arbitrary")),
        name=name,
    )(n2, h1, kv, wq, wo, gm.reshape(1, d), wrh, wrl, br.reshape(1, N_EXPERTS))


_HI_MASK = 0xFFFF0000


def _pack_rows(x):
    half = x.shape[1] // 2
    lo = lax.bitcast_convert_type(x[:, :half].astype(BF16).astype(F32), U32) >> 16
    hi = lax.bitcast_convert_type(x[:, half:].astype(BF16).astype(F32), U32) & jnp.uint32(_HI_MASK)
    return lo | hi


def _unpack_rows(u):
    lo = lax.bitcast_convert_type(u << 16, F32)
    hi = lax.bitcast_convert_type(u & jnp.uint32(_HI_MASK), F32)
    return lo, hi


def _slab(ref, row):
    return ref.at[pl.ds(pl.multiple_of(row * MOE_SLAB, MOE_SLAB), MOE_SLAB)]


def _row_copies(make_copy, make_group_wait, count):
    ngroup = count // DMA_WAIT_ROWS

    def group(c, carry):
        def pair(j, carry2):
            a = c * DMA_WAIT_ROWS + 2 * j
            make_copy(a).start(priority=0)
            make_copy(a + 1).start(priority=1)
            return carry2

        lax.fori_loop(0, DMA_WAIT_ROWS // 2, pair, 0, unroll=DMA_ISSUE_UNROLL // 2)

        @pl.when(c > 0)
        def _():
            make_group_wait().wait()

        return carry

    lax.fori_loop(0, ngroup, group, 0)
    make_group_wait().wait()


def _dispatch_kernel(gstart_ref, cnt_ref, padded_ref, e_ref, rank_ref, n3_ref, xs_hbm, stage, sem,
                     *, real_steps, tm, rows_total):
    step = pl.program_id(0)
    group_rows = DMA_WAIT_ROWS * MOE_SLAB

    def group_wait():
        return pltpu.make_async_copy(stage.at[pl.ds(0, group_rows)], xs_hbm.at[pl.ds(0, group_rows)], sem)

    @pl.when(step < real_steps)
    def _():
        u = _pack_rows(n3_ref[...])
        for j in range(MOE_SLAB):
            stage[pl.ds(j, tm, stride=MOE_SLAB), :] = u[:, j * V7X_LANES:(j + 1) * V7X_LANES]

        def make_copy(a):
            slot = gstart_ref[e_ref[0, a]] + rank_ref[0, a]
            return pltpu.make_async_copy(_slab(stage, a >> TOP_K_LOG2), _slab(xs_hbm, slot), sem)

        _row_copies(make_copy, group_wait, tm * TOP_K)

    @pl.when(step == real_steps)
    def _():
        stage[0:MOE_SLAB, :] = jnp.zeros((MOE_SLAB, V7X_LANES), U32)

        def fill(lo, hi):
            def one(r, carry):
                pltpu.make_async_copy(_slab(stage, 0), _slab(xs_hbm, r), sem).start()
                return carry

            lax.fori_loop(lo, hi, one, 0)

        def per_expert(e, carry):
            fill(gstart_ref[e] + cnt_ref[e], gstart_ref[e] + padded_ref[e])
            return carry

        lax.fori_loop(0, N_EXPERTS, per_expert, 0)
        fill(gstart_ref[N_EXPERTS - 1] + padded_ref[N_EXPERTS - 1], rows_total)

        def retire(c, carry):
            group_wait().wait()
            return carry

        lax.fori_loop(0, (rows_total - real_steps * tm * TOP_K) // DMA_WAIT_ROWS, retire, 0)


def _dispatch(gstart, cnt, padded, top_e, rank, n3, *, rows_total, tm, name="moe_dispatch"):
    n, d = n3.shape
    real_steps = n // tm
    blk = pl.BlockSpec((None, 1, tm * TOP_K), lambda i, *_: (jnp.minimum(i, real_steps - 1), 0, 0),
                       memory_space=pltpu.SMEM)
    grid_spec = pltpu.PrefetchScalarGridSpec(
        num_scalar_prefetch=3,
        grid=(real_steps + 1,),
        in_specs=[blk, blk, pl.BlockSpec((tm, d), lambda i, *_: (jnp.minimum(i, real_steps - 1), 0))],
        out_specs=pl.BlockSpec(memory_space=pl.ANY),
        scratch_shapes=[pltpu.VMEM((tm * MOE_SLAB, V7X_LANES), U32), pltpu.SemaphoreType.DMA],
    )
    return pl.pallas_call(
        functools.partial(_dispatch_kernel, real_steps=real_steps, tm=tm, rows_total=rows_total),
        out_shape=jax.ShapeDtypeStruct((rows_total * MOE_SLAB, V7X_LANES), U32),
        grid_spec=grid_spec,
        compiler_params=_cparams(("arbitrary",)),
        name=name,
    )(gstart, cnt, padded, top_e.reshape(real_steps, 1, tm * TOP_K), rank.reshape(real_steps, 1, tm * TOP_K), n3)


def _moe_kernel(te_ref, tv_ref, tstart_ref, trows_ref,
                xs_hbm, w1_ref, b1_ref, w2_ref, b2_ref, perm_ref, yb_hbm,
                stage, x16, acc, w1p, sems, *, nf):
    i = pl.program_id(0)
    f = pl.program_id(1)
    rows = trows_ref[i]
    start = tstart_ref[i]
    nchunk = rows // MOE_ROW_ALIGN
    chunk_slab_rows = MOE_ROW_ALIGN * MOE_SLAB
    half = 2 * V7X_LANES
    nhalf = 2 * MOE_TF // half

    def slot_rows(c):
        return pl.multiple_of((c & 1) * chunk_slab_rows, chunk_slab_rows)

    def hbm_rows(ref, c):
        r0 = pl.multiple_of((start + c * MOE_ROW_ALIGN) * MOE_SLAB, chunk_slab_rows)
        return ref.at[pl.ds(r0, chunk_slab_rows)]

    def in_copy(c):
        return pltpu.make_async_copy(hbm_rows(xs_hbm, c), stage.at[pl.ds(slot_rows(c), chunk_slab_rows)],
                                     sems.at[c & 1])

    def out_copy(c):
        return pltpu.make_async_copy(stage.at[pl.ds(slot_rows(c), chunk_slab_rows)], hbm_rows(yb_hbm, c),
                                     sems.at[c & 1])

    @pl.when((rows > 0) & (f == 0))
    def _():
        in_copy(0).start()

        def land(c, carry):
            @pl.when(c + 1 < nchunk)
            def _():
                in_copy(c + 1).start()

            in_copy(c).wait()
            r0 = pl.multiple_of(c * MOE_ROW_ALIGN, MOE_ROW_ALIGN)
            s0 = slot_rows(c)
            for j in range(MOE_SLAB):
                lo, hi = _unpack_rows(stage[pl.ds(s0 + j, MOE_ROW_ALIGN, stride=MOE_SLAB), :])
                x16[pl.ds(r0, MOE_ROW_ALIGN), j * V7X_LANES:(j + 1) * V7X_LANES] = lo.astype(BF16)
                x16[pl.ds(r0, MOE_ROW_ALIGN), (MOE_SLAB + j) * V7X_LANES:(MOE_SLAB + j + 1) * V7X_LANES] = (
                    hi.astype(BF16))
            acc[pl.ds(r0, MOE_ROW_ALIGN), :] = jnp.zeros((MOE_ROW_ALIGN, D_MODEL), F32)
            return carry

        lax.fori_loop(0, nchunk, land, 0)

    @pl.when(rows > 0)
    def _():
        perm = perm_ref[...]
        for c in range(nhalf):
            wb = w1_ref[:, c * half:(c + 1) * half].astype(BF16)
            w1p[:, c * half:(c + 1) * half] = jnp.dot(wb, perm, preferred_element_type=F32).astype(BF16)
        w2 = w2_ref[...].astype(BF16)
        b1 = b1_ref[...]

        def mlp(c0, nc):
            r0 = pl.multiple_of(c0 * MOE_ROW_ALIGN, MOE_ROW_ALIGN)
            nr = nc * MOE_ROW_ALIGN
            x = x16[pl.ds(r0, nr), :]
            h = jnp.dot(x, w1p[...], preferred_element_type=F32) + b1
            parts = []
            for c in range(nhalf):
                hg = jnp.minimum(h[:, c * half:c * half + V7X_LANES], SWIGLU_LIMIT)
                hl = jnp.clip(h[:, c * half + V7X_LANES:(c + 1) * half], -SWIGLU_LIMIT, SWIGLU_LIMIT)
                parts.append(hg * jax.nn.sigmoid(SWIGLU_ALPHA * hg) * (hl + 1.0))
            a = jnp.concatenate(parts, axis=-1).astype(BF16)
            acc[pl.ds(r0, nr), :] += jnp.dot(a, w2, preferred_element_type=F32)

        nquad = nchunk // MOE_BLOCK_CHUNKS

        def quad(c, carry):
            mlp(c * MOE_BLOCK_CHUNKS, MOE_BLOCK_CHUNKS)
            return carry

        lax.fori_loop(0, nquad, quad, 0)
        size = MOE_BLOCK_CHUNKS // 2
        while size >= 1:
            done = nchunk & ~(2 * size - 1)

            @pl.when((nchunk & size) != 0)
            def _(done=done, size=size):
                mlp(done, size)

            size //= 2

    @pl.when((rows > 0) & (f == nf - 1))
    def _():
        def emit(c, carry):
            @pl.when(c >= 2)
            def _():
                out_copy(c - 2).wait()

            r0 = pl.multiple_of(c * MOE_ROW_ALIGN, MOE_ROW_ALIGN)
            s0 = slot_rows(c)
            u = _pack_rows(acc[pl.ds(r0, MOE_ROW_ALIGN), :] + b2_ref[...])
            for j in range(MOE_SLAB):
                stage[pl.ds(s0 + j, MOE_ROW_ALIGN, stride=MOE_SLAB), :] = u[:, j * V7X_LANES:(j + 1) * V7X_LANES]
            out_copy(c).start()
            return carry

        lax.fori_loop(0, nchunk, emit, 0)

        @pl.when(nchunk >= 2)
        def _():
            out_copy(nchunk - 2).wait()

        out_copy(nchunk - 1).wait()


def _moe(tile_e, tile_v, tile_start, tile_rows, xs, w1, b1p, w2, b2, perm, *, name="moe_mlp"):
    nt = tile_e.shape[0]
    nf = D_FF // MOE_TF

    def w1_map(i, f, te, tv, ts, tr):
        return (te[i], 0, jnp.where(tv[i] > 0, f, nf - 1))

    def w2_map(i, f, te, tv, ts, tr):
        return (te[i], jnp.where(tv[i] > 0, f, nf - 1), 0)

    def b2_map(i, f, te, tv, ts, tr):
        return (te[i], 0, 0)

    grid_spec = pltpu.PrefetchScalarGridSpec(
        num_scalar_prefetch=4,
        grid=(nt, nf),
        in_specs=[
            pl.BlockSpec(memory_space=pl.ANY),
            pl.BlockSpec((None, D_MODEL, 2 * MOE_TF), w1_map),
            pl.BlockSpec((None, 1, 2 * MOE_TF), w1_map),
            pl.BlockSpec((None, MOE_TF, D_MODEL), w2_map),
            pl.BlockSpec((None, 1, D_MODEL), b2_map),
            pl.BlockSpec((2 * V7X_LANES, 2 * V7X_LANES), lambda i, f, te, tv, ts, tr: (0, 0)),
        ],
        out_specs=pl.BlockSpec(memory_space=pl.ANY),
        scratch_shapes=[
            pltpu.VMEM((2 * MOE_ROW_ALIGN * MOE_SLAB, V7X_LANES), U32),
            pltpu.VMEM((MOE_TMAX, D_MODEL), BF16),
            pltpu.VMEM((MOE_TMAX, D_MODEL), F32),
            pltpu.VMEM((D_MODEL, 2 * MOE_TF), BF16),
            pltpu.SemaphoreType.DMA((2,)),
        ],
    )
    return pl.pallas_call(
        functools.partial(_moe_kernel, nf=nf),
        out_shape=jax.ShapeDtypeStruct(xs.shape, xs.dtype),
        grid_spec=grid_spec,
        input_output_aliases={4: 0},
        compiler_params=_cparams(("arbitrary", "arbitrary")),
        name=name,
    )(tile_e, tile_v, tile_start, tile_rows, xs, w1, b1p.reshape(N_EXPERTS, 1, 2 * D_FF), w2,
      b2.reshape(N_EXPERTS, 1, D_MODEL), perm)


def _combine_kernel(gstart_ref, e_ref, rank_ref, h2_ref, tw_ref, gf_ref, yb_hbm, o_ref, buf, sem, *, tm):
    group_rows = DMA_WAIT_ROWS * MOE_SLAB

    def make_copy(a):
        t = a >> TOP_K_LOG2
        k = a & (TOP_K - 1)
        slot = gstart_ref[e_ref[0, a]] + rank_ref[0, a]
        return pltpu.make_async_copy(_slab(yb_hbm, slot), _slab(buf, k * tm + t), sem)

    def group_wait():
        return pltpu.make_async_copy(yb_hbm.at[pl.ds(0, group_rows)], buf.at[pl.ds(0, group_rows)], sem)

    _row_copies(make_copy, group_wait, tm * TOP_K)

    tw = tw_ref[...]
    cols_lo, cols_hi = [], []
    for j in range(MOE_SLAB):
        clo = chi = None
        for k in range(TOP_K):
            lo, hi = _unpack_rows(buf[pl.ds(k * tm * MOE_SLAB + j, tm, stride=MOE_SLAB), :])
            w = tw[:, k:k + 1]
            clo = w * lo if clo is None else clo + w * lo
            chi = w * hi if chi is None else chi + w * hi
        cols_lo.append(clo)
        cols_hi.append(chi)
    h3 = h2_ref[...] + jnp.concatenate(cols_lo + cols_hi, axis=-1)
    o_ref[...] = _rms(h3, gf_ref[...])


def _combine(gstart, top_e, rank, h2, tw, gf, yb, *, tm, name="moe_combine"):
    n, d = h2.shape
    steps = n // tm
    blk = pl.BlockSpec((None, 1, tm * TOP_K), lambda i, gs: (i, 0, 0), memory_space=pltpu.SMEM)
    grid_spec = pltpu.PrefetchScalarGridSpec(
        num_scalar_prefetch=1,
        grid=(steps,),
        in_specs=[
            blk, blk,
            pl.BlockSpec((tm, d), lambda i, gs: (i, 0)),
            pl.BlockSpec((tm, TOP_K), lambda i, gs: (i, 0)),
            pl.BlockSpec((1, d), lambda i, gs: (0, 0)),
            pl.BlockSpec(memory_space=pl.ANY),
        ],
        out_specs=pl.BlockSpec((tm, d), lambda i, gs: (i, 0)),
        scratch_shapes=[
            pltpu.VMEM((tm * TOP_K * MOE_SLAB, V7X_LANES), U32),
            pltpu.SemaphoreType.DMA,
        ],
    )
    return pl.pallas_call(
        functools.partial(_combine_kernel, tm=tm),
        out_shape=jax.ShapeDtypeStruct((n, d), F32),
        grid_spec=grid_spec,
        compiler_params=_cparams(("arbitrary",)),
        name=name,
    )(gstart, top_e.reshape(steps, 1, tm * TOP_K), rank.reshape(steps, 1, tm * TOP_K), h2, tw,
      gf.reshape(1, d), yb)


def _rope_tables(positions):
    b, s = positions.shape
    half = HEAD_DIM // 2
    inv_freq = 1.0 / (ROPE_THETA ** (jnp.arange(half, dtype=F32) / half))
    ang = positions.astype(F32)[..., None] * inv_freq
    cos = jnp.cos(ang)
    sin = jnp.sin(ang)
    cs = jnp.concatenate([cos, cos], axis=-1)
    sn = jnp.concatenate([-sin, sin], axis=-1)

    def deint(a):
        a = a.reshape(b, s // ATT_TILE, WIN_BLOCK, DIL_MAX, HEAD_DIM)
        return a.transpose(0, 1, 3, 2, 4)

    return deint(cs), deint(sn)


def _routing(counts, n_tok):
    cnt = counts.reshape(N_EXPERTS).astype(I32)
    padded = (cnt + MOE_ROW_ALIGN - 1) // MOE_ROW_ALIGN * MOE_ROW_ALIGN
    gstart = (jnp.cumsum(padded) - padded).astype(I32)
    rows_total = n_tok * TOP_K + N_EXPERTS * MOE_ROW_ALIGN

    def first_above(ends, q):
        return jnp.minimum(jnp.sum((ends[None, :] <= q[:, None]).astype(I32), axis=1), N_EXPERTS - 1)

    main_rows = jnp.minimum(padded, MOE_TMAX)
    main = (jnp.arange(N_EXPERTS, dtype=I32), (main_rows > 0).astype(I32), gstart.astype(I32),
            main_rows.astype(I32))

    over = padded - main_rows
    nt_e = (over + MOE_TMAX - 1) // MOE_TMAX
    tend = jnp.cumsum(nt_e)
    tstart = tend - nt_e
    n_over = tend[-1]
    ti = jnp.arange(rows_total // MOE_TMAX, dtype=I32)
    valid = ti < n_over
    tic = jnp.clip(ti, 0, jnp.maximum(n_over - 1, 0))
    te = first_above(tend, tic)
    local = tic - tstart[te]
    row0 = gstart[te] + (local + 1) * MOE_TMAX
    rows = jnp.where(valid, jnp.clip(over[te] - local * MOE_TMAX, 0, MOE_TMAX), 0)
    overflow = (te, valid.astype(I32), row0.astype(I32), rows.astype(I32))
    return gstart, cnt, padded, rows_total, main, overflow, n_over > 0


def kernel(x, mem, positions, norm_mix_g, w_in, lb_raw, hgrn_norm_g, w_br_hgrn, w_br_attn, w_out,
           norm_cross_g, norm_mem_g, w_cq, w_ckv, w_co, norm_moe_g, w_router, b_router,
           w_mlp1, b_mlp1, w_mlp2, b_mlp2, norm_final_g):
    bsz, seq, d = x.shape
    assert w_in.shape[0] == 1 and d == D_MODEL and seq % ATT_TILE == 0
    n_tok = bsz * seq
    lower_bounds = jnp.cumsum(jax.nn.softmax(lb_raw.astype(F32), axis=0), axis=0)
    cs, sn = _rope_tables(positions)

    wl = w_in[0]
    c_h = 4 * HG_WIDTH
    c_a = c_h + ATT_Q_WIDTH + 2 * ATT_KV_WIDTH
    tn = 1024
    x2d = x.reshape(n_tok, d)
    proj_hg = _in_proj(x2d, norm_mix_g[0], wl, tm=1024, tn=tn, name="in_proj_nat", ncols=c_h + 2 * D_MODEL,
                       col_block=lambda j: jnp.where(j < c_h // tn, j, j + (c_a - c_h) // tn))
    pa = _in_proj_strided(x, norm_mix_g[0], wl, c=8, tn=tn, name="in_proj_attn", col0=c_h, ncols=c_a - c_h)

    o_h = _hgrn(proj_hg.reshape(bsz, seq, c_h + 2 * D_MODEL), lower_bounds[0], hgrn_norm_g[0], ts=512)
    o_a = _dilated_attn(pa, cs, sn)

    h1, n2 = _merge_out(o_h.reshape(n_tok, HG_WIDTH), o_a.reshape(n_tok, ATT_KV_WIDTH), proj_hg, x2d,
                        w_br_hgrn[0].astype(BF16), w_br_attn[0].astype(BF16), w_out[0].astype(BF16),
                        norm_cross_g[0], tm=256, gate_block=c_h // (2 * D_MODEL))

    mem2d = mem.reshape(-1, d)
    kv = _in_proj(mem2d, norm_mem_g, w_ckv[0], tm=mem2d.shape[0], tn=2 * CROSS_WIDTH, name="mem_kv")
    wrh = w_router[0].astype(BF16)
    wrl = (w_router[0] - wrh.astype(F32)).astype(BF16)
    h2, n3, top_e, top_w, rank, counts = _cross(
        n2, h1, kv, w_cq[0].astype(BF16), w_co[0].astype(BF16), norm_moe_g[0], wrh, wrl, b_router[0],
        batch=bsz, tm=512)

    gstart, cnt, padded, rows_total, main_tiles, over_tiles, has_over = _routing(counts, n_tok)
    xs = _dispatch(gstart, cnt, padded, top_e, rank, n3, rows_total=rows_total, tm=DISPATCH_TOKENS)

    ii = jnp.arange(2 * V7X_LANES)
    src = jnp.where(ii < V7X_LANES, 2 * ii, 2 * (ii - V7X_LANES) + 1)
    perm = (ii[:, None] == src[None, :]).astype(BF16)
    b1p = b_mlp1[0].reshape(N_EXPERTS, -1, V7X_LANES, 2).transpose(0, 1, 3, 2).reshape(N_EXPERTS, 2 * D_FF)
    def run_moe(tiles, rows_buf, name):
        return _moe(*tiles, rows_buf, w_mlp1[0], b1p, w_mlp2[0], b_mlp2[0], perm, name=name)

    yb = run_moe(main_tiles, xs, "moe_mlp")
    yb = lax.cond(has_over, lambda rows_buf: run_moe(over_tiles, rows_buf, "moe_mlp_overflow"),
                  lambda rows_buf: rows_buf, yb)

    out = _combine(gstart, top_e, rank, h2, top_w, norm_final_g, yb, tm=COMBINE_TOKENS)
    return out.reshape(bsz, seq, d)
```

```python
import functools

import jax
import jax.numpy as jnp
from jax import lax
from jax.experimental import pallas as pl
from jax.experimental.pallas import tpu as pltpu

F32 = jnp.float32
BF16 = jnp.bfloat16
I32 = jnp.int32
U32 = jnp.uint32

D_MODEL = 2048
HEAD_DIM = 128
HG_HEADS = 8
HG_WIDTH = HG_HEADS * HEAD_DIM
N_KV_HEADS = 8
N_GROUPS = 3
ATT_Q_WIDTH = N_GROUPS * N_KV_HEADS * HEAD_DIM
ATT_KV_WIDTH = N_KV_HEADS * HEAD_DIM
WIN_BLOCK = 128
ROPE_THETA = 10000.0
CROSS_HEADS = 4
CROSS_WIDTH = CROSS_HEADS * HEAD_DIM
N_EXPERTS = 32
TOP_K = 4
TOP_K_LOG2 = 2
D_FF = D_MODEL
SWIGLU_ALPHA = 1.702
SWIGLU_LIMIT = 7.0
NORM_EPS = 1e-6

V7X_LANES = 128
V7X_VMEM_LIMIT_BYTES = 56 * 1024 * 1024

DIL_MAX = 16
ATT_TILE = DIL_MAX * WIN_BLOCK
ATT_UNROLL = 16
HG_BLOCK = 16
HG_GROUP = 16
NEG_BIG = -1e30

MOE_TMAX = 1536
MOE_ROW_ALIGN = 128
MOE_BLOCK_CHUNKS = 8
MOE_TF = 512
MOE_SLAB = D_MODEL // 2 // V7X_LANES
DISPATCH_TOKENS = 512
COMBINE_TOKENS = 512
DMA_WAIT_ROWS = 128
DMA_ISSUE_UNROLL = 8


def _rms(x, g):
    ms = jnp.mean(x * x, axis=-1, keepdims=True)
    return x * lax.rsqrt(ms + NORM_EPS) * g


def _cparams(sem, vmem=V7X_VMEM_LIMIT_BYTES):
    return pltpu.CompilerParams(dimension_semantics=sem, vmem_limit_bytes=vmem)


def _in_proj_kernel(x_ref, g_ref, w_ref, o_ref, xn_ref):
    @pl.when(pl.program_id(1) == 0)
    def _():
        xn_ref[...] = _rms(x_ref[...], g_ref[...]).astype(BF16)

    o_ref[...] = jnp.dot(xn_ref[...], w_ref[...].astype(BF16), preferred_element_type=F32).astype(o_ref.dtype)


def _in_proj(x2d, g, w, *, tm, tn, name, col_block=lambda j: j, ncols=None):
    n, d = x2d.shape
    wc = w.shape[1] if ncols is None else ncols
    return pl.pallas_call(
        _in_proj_kernel,
        out_shape=jax.ShapeDtypeStruct((n, wc), BF16),
        grid=(n // tm, wc // tn),
        in_specs=[
            pl.BlockSpec((tm, d), lambda i, j: (i, 0)),
            pl.BlockSpec((1, d), lambda i, j: (0, 0)),
            pl.BlockSpec((d, tn), lambda i, j: (0, col_block(j))),
        ],
        out_specs=pl.BlockSpec((tm, tn), lambda i, j: (i, j)),
        scratch_shapes=[pltpu.VMEM((tm, d), BF16)],
        compiler_params=_cparams(("parallel", "arbitrary")),
        name=name,
    )(x2d, g.reshape(1, d), w)


def _hgrn_kernel(q_ref, f_ref, i_ref, g_ref, lb_ref, gn_ref, o_ref, st_ref, kin_s, b_s, v_s, *, ts):
    @pl.when(pl.program_id(2) == 0)
    def _():
        st_ref[...] = jnp.zeros_like(st_ref)

    lb = lb_ref[...]
    oml = 1.0 - lb
    gn = gn_ref[...]
    half = HG_BLOCK // 2
    row = lax.broadcasted_iota(I32, (HG_BLOCK, HEAD_DIM), 0)
    row8 = lax.broadcasted_iota(I32, (half, HEAD_DIM), 0)
    nt_dims = (((1,), (1,)), ((), ()))
    tn_dims = (((0,), (0,)), ((), ()))

    def front(g, t0):
        sl = pl.ds(t0 + g * HG_BLOCK, HG_BLOCK)
        q = q_ref[sl, :].astype(F32)
        hf = f_ref[sl, :].astype(F32)
        v = i_ref[sl, :].astype(F32)
        kin = oml * jax.nn.sigmoid(-hf)
        b = jnp.log2(lb + oml * jax.nn.sigmoid(hf))
        for sh in (1, 2, 4, 8):
            b = b + jnp.where(row >= sh, pltpu.roll(b, sh, 0), 0.0)
        kin_s[g] = kin
        b_s[g] = b
        v_s[g] = v
        q_lo, q_hi = q[:half], q[half:]
        b_lo, b_hi = b[:half], b[half:]
        o_lo = jnp.zeros((half, HEAD_DIM), F32)
        o_hi = jnp.zeros((half, HEAD_DIM), F32)
        for s in range(HG_BLOCK):
            ks = kin_s[g, s:s + 1, :]
            bs = b_s[g, s:s + 1, :]
            vs = v_s[g, s:s + 1, :]
            if s < half:
                w = q_lo * ks * jnp.exp2(b_lo - bs)
                if s > 0:
                    w = jnp.where(row8 >= s, w, 0.0)
                o_lo = o_lo + jnp.sum(w, axis=-1, keepdims=True) * vs
                w = q_hi * ks * jnp.exp2(b_hi - bs)
            else:
                w = q_hi * ks * jnp.exp2(b_hi - bs)
                if s > half:
                    w = jnp.where(row8 >= s - half, w, 0.0)
            o_hi = o_hi + jnp.sum(w, axis=-1, keepdims=True) * vs
        bl = b_s[g, HG_BLOCK - 1:HG_BLOCK, :]
        qd = (q * jnp.exp2(b)).astype(BF16)
        kd = (kin * jnp.exp2(bl - b)).astype(BF16)
        upd = lax.dot_general(v.astype(BF16), kd, tn_dims, preferred_element_type=F32)
        return jnp.concatenate([o_lo, o_hi], axis=0), qd, upd, jnp.exp2(bl)

    def body(i, carry):
        t0 = pl.multiple_of(i * (HG_GROUP * HG_BLOCK), HG_GROUP * HG_BLOCK)
        fronts = [front(g, t0) for g in range(HG_GROUP)]
        st = st_ref[...]
        for g, (o_diag, qd, upd, dec) in enumerate(fronts):
            o = o_diag + lax.dot_general(qd, st.astype(BF16), nt_dims, preferred_element_type=F32)
            st = st * dec + upd
            sl = pl.ds(t0 + g * HG_BLOCK, HG_BLOCK)
            hg = g_ref[sl, :].astype(F32)
            o_ref[sl, :] = (_rms(o, gn) * (hg * jax.nn.sigmoid(hg))).astype(o_ref.dtype)
        st_ref[...] = st
        return carry

    lax.fori_loop(0, ts // (HG_GROUP * HG_BLOCK), body, 0)


def _hgrn(proj_h, lb, gn, *, ts, name="hgrn"):
    b, s, _ = proj_h.shape
    h = HG_HEADS

    def spec(off):
        return pl.BlockSpec((None, ts, HEAD_DIM), lambda bi, hi, si: (bi, si, off + hi))

    vec = pl.BlockSpec((1, HEAD_DIM), lambda bi, hi, si: (0, hi))
    return pl.pallas_call(
        functools.partial(_hgrn_kernel, ts=ts),
        out_shape=jax.ShapeDtypeStruct((b, s, HG_WIDTH), BF16),
        grid=(b, h, s // ts),
        in_specs=[spec(0), spec(h), spec(2 * h), spec(3 * h), vec, vec],
        out_specs=pl.BlockSpec((None, ts, HEAD_DIM), lambda bi, hi, si: (bi, si, hi)),
        scratch_shapes=[
            pltpu.VMEM((HEAD_DIM, HEAD_DIM), F32),
            pltpu.VMEM((HG_GROUP, HG_BLOCK, HEAD_DIM), F32),
            pltpu.VMEM((HG_GROUP, HG_BLOCK, HEAD_DIM), F32),
            pltpu.VMEM((HG_GROUP, HG_BLOCK, HEAD_DIM), F32),
        ],
        compiler_params=_cparams(("parallel", "parallel", "arbitrary")),
        name=name,
    )(proj_h, proj_h, proj_h, proj_h, lb.reshape(1, HG_WIDTH), gn.reshape(1, HG_WIDTH))


def _attn_bias(kind):
    rq = lax.broadcasted_iota(I32, (WIN_BLOCK, 2 * WIN_BLOCK), 0)
    ck = lax.broadcasted_iota(I32, (WIN_BLOCK, 2 * WIN_BLOCK), 1)
    if kind == 2:
        dist = rq + WIN_BLOCK - ck
        first = ck < WIN_BLOCK
    elif kind == 1:
        dist = 4 * ((rq & 31) - (ck & 63) + 32) + ((rq >> 5) - (ck >> 6))
        first = (ck & 63) < 32
    else:
        dist = 16 * ((rq & 7) - (ck & 15) + 8) + ((rq >> 3) - (ck >> 4))
        first = (ck & 15) < 8
    valid = (dist >= 0) & (dist <= WIN_BLOCK)
    return (jnp.where(valid, 0.0, NEG_BIG).astype(F32),
            jnp.where(valid & jnp.logical_not(first), 0.0, NEG_BIG).astype(F32))


def _attn_kernel(q0_ref, q1_ref, q2_ref, k_ref, v_ref, cs_ref, sn_ref, o_ref,
                 qr, kext, vext, acc, mrun, lrun, bias, onat, u0, u1, u2, u3, u4):
    ti = pl.program_id(2)
    wb = WIN_BLOCK
    scale = HEAD_DIM ** -0.5

    ubufs = (u0, u1, u2, u3, u4)
    for src, ub in zip((q0_ref, q1_ref, q2_ref, k_ref, v_ref), ubufs):
        ub[...] = pltpu.bitcast(src[...], U32)

    @pl.when(ti == 0)
    def _():
        kext[:, 0:wb, :] = jnp.zeros((DIL_MAX, wb, HEAD_DIM), F32)
        vext[:, 0:wb, :] = jnp.zeros((DIL_MAX, wb, HEAD_DIM), F32)

    @pl.when(ti > 0)
    def _():
        kext[:, 0:wb, :] = kext[:, wb:2 * wb, :]
        vext[:, 0:wb, :] = vext[:, wb:2 * wb, :]

    for kind in range(N_GROUPS):
        full, nofirst = _attn_bias(kind)
        bias[2 * kind] = full
        bias[2 * kind + 1] = nofirst

    def rope_body(rp, carry):
        rows = pl.ds(rp, wb, stride=DIL_MAX // 2)
        parts = [_unpack_rows(ub[rows, :]) for ub in ubufs]
        for par in range(2):
            r = 2 * rp + par
            cs = cs_ref[r]
            sn = sn_ref[r]
            for g in range(N_GROUPS):
                q = parts[g][par]
                qr[g, r] = (q * cs + pltpu.roll(q, HEAD_DIM // 2, 1) * sn) * scale
            k = parts[N_GROUPS][par]
            kext[r, wb:2 * wb, :] = k * cs + pltpu.roll(k, HEAD_DIM // 2, 1) * sn
            vext[r, wb:2 * wb, :] = parts[N_GROUPS + 1][par]
            acc[r] = jnp.zeros((wb, HEAD_DIM), F32)
            mrun[r] = jnp.full((wb, HEAD_DIM), NEG_BIG, F32)
            lrun[r] = jnp.zeros((wb, HEAD_DIM), F32)
        return carry

    lax.fori_loop(0, DIL_MAX // 2, rope_body, 0)

    nt_dims = (((1,), (1,)), ((), ()))

    def block(qb, kb, vb, bias_blk):
        s = lax.dot_general(qb.astype(BF16), kb.astype(BF16), nt_dims, preferred_element_type=F32)
        s = s + bias_blk
        m = jnp.max(s, axis=-1, keepdims=True)
        p = jnp.exp(s - m)
        l = jnp.sum(p, axis=-1, keepdims=True)
        n = jnp.dot(p.astype(BF16), vb.astype(BF16), preferred_element_type=F32)
        return n, jnp.broadcast_to(m, (wb, HEAD_DIM)), jnp.broadcast_to(l, (wb, HEAD_DIM))

    def merge(r, rows, n, m, l):
        m_old = mrun[r, rows, :]
        m_new = jnp.maximum(m_old, m)
        a = jnp.exp(m_old - m_new)
        bb = jnp.exp(m - m_new)
        acc[r, rows, :] = acc[r, rows, :] * a + n * bb
        lrun[r, rows, :] = lrun[r, rows, :] * a + l * bb
        mrun[r, rows, :] = m_new

    first_tile = jnp.where(ti == 0, 1, 0)


    def g2_body(i, carry):
        rs = [i * ATT_UNROLL + u for u in range(ATT_UNROLL)]
        res = [block(qr[2, r], kext[r], vext[r], bias[4 + first_tile]) for r in rs]
        for r, (n, m, l) in zip(rs, res):
            merge(r, pl.ds(0, wb), n, m, l)
        return carry

    lax.fori_loop(0, DIL_MAX // ATT_UNROLL, g2_body, 0)

    def g1_body(i, carry):
        res = []
        for u in range(ATT_UNROLL // 4):
            mb = i * (ATT_UNROLL // 4) + u
            q0 = pl.multiple_of(32 * mb, 32)
            k0 = pl.multiple_of(96 + 32 * mb, 32)
            use_first = jnp.where(mb == 0, first_tile, 0)
            for r4 in range(4):
                qb = jnp.concatenate([qr[1, r4 + 4 * j, pl.ds(q0, 32), :] for j in range(4)], axis=0)
                kb = jnp.concatenate([kext[r4 + 4 * j, pl.ds(k0, 64), :] for j in range(4)], axis=0)
                vb = jnp.concatenate([vext[r4 + 4 * j, pl.ds(k0, 64), :] for j in range(4)], axis=0)
                res.append((r4, q0, block(qb, kb, vb, bias[2 + use_first])))
        for r4, q0, (n, m, l) in res:
            for j in range(4):
                sl = slice(32 * j, 32 * (j + 1))
                merge(r4 + 4 * j, pl.ds(q0, 32), n[sl], m[sl], l[sl])
        return carry

    lax.fori_loop(0, DIL_MAX // ATT_UNROLL, g1_body, 0)

    def g0_body(i, carry):
        res = []
        for u in range(ATT_UNROLL):
            mb = i * ATT_UNROLL + u
            q0 = pl.multiple_of(8 * mb, 8)
            k0 = pl.multiple_of(120 + 8 * mb, 8)
            qb = jnp.concatenate([qr[0, r, pl.ds(q0, 8), :] for r in range(DIL_MAX)], axis=0)
            kb = jnp.concatenate([kext[r, pl.ds(k0, 16), :] for r in range(DIL_MAX)], axis=0)
            vb = jnp.concatenate([vext[r, pl.ds(k0, 16), :] for r in range(DIL_MAX)], axis=0)
            use_first = jnp.where(mb == 0, first_tile, 0)
            res.append((q0, block(qb, kb, vb, bias[use_first])))
        for q0, (n, m, l) in res:
            for r in range(DIL_MAX):
                sl = slice(8 * r, 8 * (r + 1))
                merge(r, pl.ds(q0, 8), n[sl], m[sl], l[sl])
        return carry

    lax.fori_loop(0, DIL_MAX // ATT_UNROLL, g0_body, 0)

    for r in range(DIL_MAX):
        onat[pl.ds(r, wb, stride=DIL_MAX), :] = acc[r] / lrun[r]
    o_ref[...] = onat[...].astype(o_ref.dtype)


def _dilated_attn(proj, cs, sn, *, q_block, name="dilated_attn"):
    b, s, _ = proj.shape
    nt = s // ATT_TILE
    h = N_KV_HEADS

    def spec(off):
        return pl.BlockSpec((None, ATT_TILE, HEAD_DIM), lambda bi, hi, ti: (bi, ti, q_block + off + hi))

    tab = pl.BlockSpec((None, None, DIL_MAX, WIN_BLOCK, HEAD_DIM), lambda bi, hi, ti: (bi, ti, 0, 0, 0))
    ubuf = pltpu.VMEM((ATT_TILE // 2, HEAD_DIM), U32)
    return pl.pallas_call(
        _attn_kernel,
        out_shape=jax.ShapeDtypeStruct((b, nt * ATT_TILE, ATT_KV_WIDTH), BF16),
        grid=(b, h, nt),
        in_specs=[spec(0), spec(h), spec(2 * h), spec(3 * h), spec(4 * h), tab, tab],
        out_specs=pl.BlockSpec((None, ATT_TILE, HEAD_DIM), lambda bi, hi, ti: (bi, ti, hi)),
        scratch_shapes=[
            pltpu.VMEM((N_GROUPS, DIL_MAX, WIN_BLOCK, HEAD_DIM), F32),
            pltpu.VMEM((DIL_MAX, 2 * WIN_BLOCK, HEAD_DIM), F32),
            pltpu.VMEM((DIL_MAX, 2 * WIN_BLOCK, HEAD_DIM), F32),
            pltpu.VMEM((DIL_MAX, WIN_BLOCK, HEAD_DIM), F32),
            pltpu.VMEM((DIL_MAX, WIN_BLOCK, HEAD_DIM), F32),
            pltpu.VMEM((DIL_MAX, WIN_BLOCK, HEAD_DIM), F32),
            pltpu.VMEM((2 * N_GROUPS, WIN_BLOCK, 2 * WIN_BLOCK), F32),
            pltpu.VMEM((ATT_TILE, HEAD_DIM), F32),
            ubuf, ubuf, ubuf, ubuf, ubuf,
        ],
        compiler_params=_cparams(("arbitrary", "arbitrary", "arbitrary")),
        name=name,
    )(proj, proj, proj, proj, proj, cs, sn)


def _merge_out_kernel(oh_ref, oa_ref, gate_ref, x_ref, wh_ref, wa_ref, wo_ref, gc_ref, h1_ref, n2_ref):
    ga = gate_ref[:, :D_MODEL].astype(F32)
    gb = gate_ref[:, D_MODEL:].astype(F32)
    yh = jnp.dot(oh_ref[...], wh_ref[...], preferred_element_type=F32)
    ya = jnp.dot(oa_ref[...], wa_ref[...], preferred_element_type=F32)
    merged = jax.nn.sigmoid(ga) * yh + jax.nn.sigmoid(gb) * ya
    h1 = x_ref[...] + jnp.dot(merged.astype(BF16), wo_ref[...], preferred_element_type=F32)
    h1_ref[...] = h1
    n2_ref[...] = _rms(h1, gc_ref[...]).astype(BF16)


def _merge_out(oh, oa, gates, x2d, wh, wa, wo, gc, *, tm, gate_block, name="merge_out"):
    n, d = x2d.shape

    def const(shape):
        return pl.BlockSpec(shape, lambda i: (0, 0), pipeline_mode=pl.Buffered(1))

    return pl.pallas_call(
        _merge_out_kernel,
        out_shape=(jax.ShapeDtypeStruct((n, d), F32), jax.ShapeDtypeStruct((n, d), BF16)),
        grid=(n // tm,),
        in_specs=[
            pl.BlockSpec((tm, HG_WIDTH), lambda i: (i, 0)),
            pl.BlockSpec((tm, ATT_KV_WIDTH), lambda i: (i, 0)),
            pl.BlockSpec((tm, 2 * d), lambda i: (i, gate_block)),
            pl.BlockSpec((tm, d), lambda i: (i, 0)),
            const((HG_WIDTH, d)), const((ATT_KV_WIDTH, d)), const((d, d)), const((1, d)),
        ],
        out_specs=(pl.BlockSpec((tm, d), lambda i: (i, 0)), pl.BlockSpec((tm, d), lambda i: (i, 0))),
        compiler_params=_cparams(("parallel",)),
        name=name,
    )(oh, oa, gates, x2d, wh, wa, wo, gc.reshape(1, d))


def _cross_kernel(n2_ref, h1_ref, kv_ref, wq_ref, wo_ref, gm_ref, wrh_ref, wrl_ref, br_ref,
                  h2_ref, n3_ref, idx_ref, tw_ref, rank_ref, cnt_ref, carry_ref, *, tm):
    @pl.when((pl.program_id(0) == 0) & (pl.program_id(1) == 0))
    def _():
        carry_ref[...] = jnp.zeros_like(carry_ref)

    nt_dims = (((1,), (1,)), ((), ()))
    scale = HEAD_DIM ** -0.5
    q = (jnp.dot(n2_ref[...], wq_ref[...], preferred_element_type=F32) * scale).astype(BF16)
    outs = []
    for hh in range(CROSS_HEADS):
        sl = slice(hh * HEAD_DIM, (hh + 1) * HEAD_DIM)
        kh = kv_ref[:, sl]
        vh = kv_ref[:, CROSS_WIDTH + hh * HEAD_DIM:CROSS_WIDTH + (hh + 1) * HEAD_DIM]
        s = lax.dot_general(q[:, sl], kh, nt_dims, preferred_element_type=F32)
        p = jnp.exp(s - jnp.max(s, axis=-1, keepdims=True))
        l = jnp.sum(p, axis=-1, keepdims=True)
        outs.append(jnp.dot(p.astype(BF16), vh, preferred_element_type=F32) / l)
    o = jnp.concatenate(outs, axis=-1).astype(BF16)
    h2 = h1_ref[...] + jnp.dot(o, wo_ref[...], preferred_element_type=F32)
    h2_ref[...] = h2
    n3 = _rms(h2, gm_ref[...])
    n3_ref[...] = n3

    n3h = n3.astype(BF16)
    n3l = (n3 - n3h.astype(F32)).astype(BF16)
    wrh = wrh_ref[...]
    logits = (jnp.dot(n3h, wrh, preferred_element_type=F32)
              + jnp.dot(n3l, wrh, preferred_element_type=F32)
              + jnp.dot(n3h, wrl_ref[...], preferred_element_type=F32)
              + br_ref[...])
    lane = lax.broadcasted_iota(I32, (tm, N_EXPERTS), 1).astype(F32)
    vals, idxs, hots = [], [], []
    cur = logits
    for _ in range(TOP_K):
        mx = jnp.max(cur, axis=-1, keepdims=True)
        ix = jnp.min(jnp.where(cur == mx, lane, float(N_EXPERTS)), axis=-1, keepdims=True)
        hot = lane == ix
        vals.append(mx)
        idxs.append(ix)
        hots.append(hot)
        cur = jnp.where(hot, -jnp.inf, cur)
    es = [jnp.exp(v - vals[0]) for v in vals]
    den = es[0] + es[1] + es[2] + es[3]
    col = lax.broadcasted_iota(I32, (tm, TOP_K), 1)

    def pack(cols):
        out = jnp.broadcast_to(cols[TOP_K - 1], (tm, TOP_K))
        for k in range(TOP_K - 2, -1, -1):
            out = jnp.where(col == k, cols[k], out)
        return out

    idx_ref[...] = pack(idxs).astype(I32)
    tw_ref[...] = pack([e / den for e in es])

    cmat = (hots[0] | hots[1] | hots[2] | hots[3]).astype(F32)
    rr = lax.broadcasted_iota(I32, (tm, tm), 0)
    cc = lax.broadcasted_iota(I32, (tm, tm), 1)
    tri = (cc < rr).astype(BF16)
    before = jnp.dot(tri, cmat.astype(BF16), preferred_element_type=F32) + carry_ref[...]
    ranks = [jnp.sum(jnp.where(hot, before, 0.0), axis=-1, keepdims=True) for hot in hots]
    rank_ref[...] = pack(ranks).astype(I32)
    carry = carry_ref[...] + jnp.sum(cmat, axis=0, keepdims=True)
    carry_ref[...] = carry
    cnt_ref[...] = carry


def _cross(n2, h1, kv, wq, wo, gm, wrh, wrl, br, *, batch, tm, name="cross"):
    n, d = h1.shape
    per_b = n // batch // tm
    mem_len = kv.shape[0] // batch

    def const(shape):
        return pl.BlockSpec(shape, lambda bi, i: (0, 0), pipeline_mode=pl.Buffered(1))

    def row(bi, i):
        return (bi * per_b + i, 0)

    return pl.pallas_call(
        functools.partial(_cross_kernel, tm=tm),
        out_shape=(
            jax.ShapeDtypeStruct((n, d), F32),
            jax.ShapeDtypeStruct((n, d), F32),
            jax.ShapeDtypeStruct((n, TOP_K), I32),
            jax.ShapeDtypeStruct((n, TOP_K), F32),
            jax.ShapeDtypeStruct((n, TOP_K), I32),
            jax.ShapeDtypeStruct((1, N_EXPERTS), F32),
        ),
        grid=(batch, per_b),
        in_specs=[
            pl.BlockSpec((tm, d), row),
            pl.BlockSpec((tm, d), row),
            pl.BlockSpec((mem_len, 2 * CROSS_WIDTH), lambda bi, i: (bi, 0)),
            const((d, CROSS_WIDTH)), const((CROSS_WIDTH, d)), const((1, d)),
            const((d, N_EXPERTS)), const((d, N_EXPERTS)), const((1, N_EXPERTS)),
        ],
        out_specs=(
            pl.BlockSpec((tm, d), row),
            pl.BlockSpec((tm, d), row),
            pl.BlockSpec((tm, TOP_K), row),
            pl.BlockSpec((tm, TOP_K), row),
            pl.BlockSpec((tm, TOP_K), row),
            pl.BlockSpec((1, N_EXPERTS), lambda bi, i: (0, 0)),
        ),
        scratch_shapes=[pltpu.VMEM((1, N_EXPERTS), F32)],
        compiler_params=_cparams(("arbitrary", "arbitrary")),
        name=name,
    )(n2, h1, kv, wq, wo, gm.reshape(1, d), wrh, wrl, br.reshape(1, N_EXPERTS))


_HI_MASK = 0xFFFF0000


def _pack_rows(x):
    half = x.shape[1] // 2
    lo = lax.bitcast_convert_type(x[:, :half].astype(BF16).astype(F32), U32) >> 16
    hi = lax.bitcast_convert_type(x[:, half:].astype(BF16).astype(F32), U32) & jnp.uint32(_HI_MASK)
    return lo | hi


def _unpack_rows(u):
    lo = lax.bitcast_convert_type(u << 16, F32)
    hi = lax.bitcast_convert_type(u & jnp.uint32(_HI_MASK), F32)
    return lo, hi


def _slab(ref, row):
    return ref.at[pl.ds(pl.multiple_of(row * MOE_SLAB, MOE_SLAB), MOE_SLAB)]


def _row_copies(make_copy, make_group_wait, count):
    ngroup = count // DMA_WAIT_ROWS

    def group(c, carry):
        def pair(j, carry2):
            a = c * DMA_WAIT_ROWS + 2 * j
            make_copy(a).start(priority=0)
            make_copy(a + 1).start(priority=1)
            return carry2

        lax.fori_loop(0, DMA_WAIT_ROWS // 2, pair, 0, unroll=DMA_ISSUE_UNROLL // 2)

        @pl.when(c > 0)
        def _():
            make_group_wait().wait()

        return carry

    lax.fori_loop(0, ngroup, group, 0)
    make_group_wait().wait()


def _dispatch_kernel(gstart_ref, cnt_ref, padded_ref, e_ref, rank_ref, n3_ref, xs_hbm, stage, sem,
                     *, real_steps, tm, rows_total):
    step = pl.program_id(0)
    group_rows = DMA_WAIT_ROWS * MOE_SLAB

    def group_wait():
        return pltpu.make_async_copy(stage.at[pl.ds(0, group_rows)], xs_hbm.at[pl.ds(0, group_rows)], sem)

    @pl.when(step < real_steps)
    def _():
        u = _pack_rows(n3_ref[...])
        for j in range(MOE_SLAB):
            stage[pl.ds(j, tm, stride=MOE_SLAB), :] = u[:, j * V7X_LANES:(j + 1) * V7X_LANES]

        def make_copy(a):
            slot = gstart_ref[e_ref[0, a]] + rank_ref[0, a]
            return pltpu.make_async_copy(_slab(stage, a >> TOP_K_LOG2), _slab(xs_hbm, slot), sem)

        _row_copies(make_copy, group_wait, tm * TOP_K)

    @pl.when(step == real_steps)
    def _():
        stage[0:MOE_SLAB, :] = jnp.zeros((MOE_SLAB, V7X_LANES), U32)

        def fill(lo, hi):
            def one(r, carry):
                pltpu.make_async_copy(_slab(stage, 0), _slab(xs_hbm, r), sem).start()
                return carry

            lax.fori_loop(lo, hi, one, 0)

        def per_expert(e, carry):
            fill(gstart_ref[e] + cnt_ref[e], gstart_ref[e] + padded_ref[e])
            return carry

        lax.fori_loop(0, N_EXPERTS, per_expert, 0)
        fill(gstart_ref[N_EXPERTS - 1] + padded_ref[N_EXPERTS - 1], rows_total)

        def retire(c, carry):
            group_wait().wait()
            return carry

        lax.fori_loop(0, (rows_total - real_steps * tm * TOP_K) // DMA_WAIT_ROWS, retire, 0)


def _dispatch(gstart, cnt, padded, top_e, rank, n3, *, rows_total, tm, name="moe_dispatch"):
    n, d = n3.shape
    real_steps = n // tm
    blk = pl.BlockSpec((None, 1, tm * TOP_K), lambda i, *_: (jnp.minimum(i, real_steps - 1), 0, 0),
                       memory_space=pltpu.SMEM)
    grid_spec = pltpu.PrefetchScalarGridSpec(
        num_scalar_prefetch=3,
        grid=(real_steps + 1,),
        in_specs=[blk, blk, pl.BlockSpec((tm, d), lambda i, *_: (jnp.minimum(i, real_steps - 1), 0))],
        out_specs=pl.BlockSpec(memory_space=pl.ANY),
        scratch_shapes=[pltpu.VMEM((tm * MOE_SLAB, V7X_LANES), U32), pltpu.SemaphoreType.DMA],
    )
    return pl.pallas_call(
        functools.partial(_dispatch_kernel, real_steps=real_steps, tm=tm, rows_total=rows_total),
        out_shape=jax.ShapeDtypeStruct((rows_total * MOE_SLAB, V7X_LANES), U32),
        grid_spec=grid_spec,
        compiler_params=_cparams(("arbitrary",)),
        name=name,
    )(gstart, cnt, padded, top_e.reshape(real_steps, 1, tm * TOP_K), rank.reshape(real_steps, 1, tm * TOP_K), n3)


def _moe_kernel(te_ref, tv_ref, tstart_ref, trows_ref,
                xs_hbm, w1_ref, b1_ref, w2_ref, b2_ref, perm_ref, yb_hbm,
                stage, x16, acc, w1p, sems, *, nf):
    i = pl.program_id(0)
    f = pl.program_id(1)
    rows = trows_ref[i]
    start = tstart_ref[i]
    nchunk = rows // MOE_ROW_ALIGN
    chunk_slab_rows = MOE_ROW_ALIGN * MOE_SLAB
    half = 2 * V7X_LANES
    nhalf = 2 * MOE_TF // half

    def slot_rows(c):
        return pl.multiple_of((c & 1) * chunk_slab_rows, chunk_slab_rows)

    def hbm_rows(ref, c):
        r0 = pl.multiple_of((start + c * MOE_ROW_ALIGN) * MOE_SLAB, chunk_slab_rows)
        return ref.at[pl.ds(r0, chunk_slab_rows)]

    def in_copy(c):
        return pltpu.make_async_copy(hbm_rows(xs_hbm, c), stage.at[pl.ds(slot_rows(c), chunk_slab_rows)],
                                     sems.at[c & 1])

    def out_copy(c):
        return pltpu.make_async_copy(stage.at[pl.ds(slot_rows(c), chunk_slab_rows)], hbm_rows(yb_hbm, c),
                                     sems.at[c & 1])

    @pl.when((rows > 0) & (f == 0))
    def _():
        in_copy(0).start()

        def land(c, carry):
            @pl.when(c + 1 < nchunk)
            def _():
                in_copy(c + 1).start()

            in_copy(c).wait()
            r0 = pl.multiple_of(c * MOE_ROW_ALIGN, MOE_ROW_ALIGN)
            s0 = slot_rows(c)
            for j in range(MOE_SLAB):
                lo, hi = _unpack_rows(stage[pl.ds(s0 + j, MOE_ROW_ALIGN, stride=MOE_SLAB), :])
                x16[pl.ds(r0, MOE_ROW_ALIGN), j * V7X_LANES:(j + 1) * V7X_LANES] = lo.astype(BF16)
                x16[pl.ds(r0, MOE_ROW_ALIGN), (MOE_SLAB + j) * V7X_LANES:(MOE_SLAB + j + 1) * V7X_LANES] = (
                    hi.astype(BF16))
            acc[pl.ds(r0, MOE_ROW_ALIGN), :] = jnp.zeros((MOE_ROW_ALIGN, D_MODEL), F32)
            return carry

        lax.fori_loop(0, nchunk, land, 0)

    @pl.when(rows > 0)
    def _():
        perm = perm_ref[...]
        for c in range(nhalf):
            wb = w1_ref[:, c * half:(c + 1) * half].astype(BF16)
            w1p[:, c * half:(c + 1) * half] = jnp.dot(wb, perm, preferred_element_type=F32).astype(BF16)
        w2 = w2_ref[...].astype(BF16)
        b1 = b1_ref[...]

        def mlp(c0, nc):
            r0 = pl.multiple_of(c0 * MOE_ROW_ALIGN, MOE_ROW_ALIGN)
            nr = nc * MOE_ROW_ALIGN
            x = x16[pl.ds(r0, nr), :]
            h = jnp.dot(x, w1p[...], preferred_element_type=F32) + b1
            parts = []
            for c in range(nhalf):
                hg = jnp.minimum(h[:, c * half:c * half + V7X_LANES], SWIGLU_LIMIT)
                hl = jnp.clip(h[:, c * half + V7X_LANES:(c + 1) * half], -SWIGLU_LIMIT, SWIGLU_LIMIT)
                parts.append(hg * jax.nn.sigmoid(SWIGLU_ALPHA * hg) * (hl + 1.0))
            a = jnp.concatenate(parts, axis=-1).astype(BF16)
            acc[pl.ds(r0, nr), :] += jnp.dot(a, w2, preferred_element_type=F32)

        nquad = nchunk // MOE_BLOCK_CHUNKS

        def quad(c, carry):
            mlp(c * MOE_BLOCK_CHUNKS, MOE_BLOCK_CHUNKS)
            return carry

        lax.fori_loop(0, nquad, quad, 0)
        size = MOE_BLOCK_CHUNKS // 2
        while size >= 1:
            done = nchunk & ~(2 * size - 1)

            @pl.when((nchunk & size) != 0)
            def _(done=done, size=size):
                mlp(done, size)

            size //= 2

    @pl.when((rows > 0) & (f == nf - 1))
    def _():
        def emit(c, carry):
            @pl.when(c >= 2)
            def _():
                out_copy(c - 2).wait()

            r0 = pl.multiple_of(c * MOE_ROW_ALIGN, MOE_ROW_ALIGN)
            s0 = slot_rows(c)
            u = _pack_rows(acc[pl.ds(r0, MOE_ROW_ALIGN), :] + b2_ref[...])
            for j in range(MOE_SLAB):
                stage[pl.ds(s0 + j, MOE_ROW_ALIGN, stride=MOE_SLAB), :] = u[:, j * V7X_LANES:(j + 1) * V7X_LANES]
            out_copy(c).start()
            return carry

        lax.fori_loop(0, nchunk, emit, 0)

        @pl.when(nchunk >= 2)
        def _():
            out_copy(nchunk - 2).wait()

        out_copy(nchunk - 1).wait()


def _moe(tile_e, tile_v, tile_start, tile_rows, xs, w1, b1p, w2, b2, perm, *, name="moe_mlp"):
    nt = tile_e.shape[0]
    nf = D_FF // MOE_TF

    def w1_map(i, f, te, tv, ts, tr):
        return (te[i], 0, jnp.where(tv[i] > 0, f, nf - 1))

    def w2_map(i, f, te, tv, ts, tr):
        return (te[i], jnp.where(tv[i] > 0, f, nf - 1), 0)

    def b2_map(i, f, te, tv, ts, tr):
        return (te[i], 0, 0)

    grid_spec = pltpu.PrefetchScalarGridSpec(
        num_scalar_prefetch=4,
        grid=(nt, nf),
        in_specs=[
            pl.BlockSpec(memory_space=pl.ANY),
            pl.BlockSpec((None, D_MODEL, 2 * MOE_TF), w1_map),
            pl.BlockSpec((None, 1, 2 * MOE_TF), w1_map),
            pl.BlockSpec((None, MOE_TF, D_MODEL), w2_map),
            pl.BlockSpec((None, 1, D_MODEL), b2_map),
            pl.BlockSpec((2 * V7X_LANES, 2 * V7X_LANES), lambda i, f, te, tv, ts, tr: (0, 0)),
        ],
        out_specs=pl.BlockSpec(memory_space=pl.ANY),
        scratch_shapes=[
            pltpu.VMEM((2 * MOE_ROW_ALIGN * MOE_SLAB, V7X_LANES), U32),
            pltpu.VMEM((MOE_TMAX, D_MODEL), BF16),
            pltpu.VMEM((MOE_TMAX, D_MODEL), F32),
            pltpu.VMEM((D_MODEL, 2 * MOE_TF), BF16),
            pltpu.SemaphoreType.DMA((2,)),
        ],
    )
    return pl.pallas_call(
        functools.partial(_moe_kernel, nf=nf),
        out_shape=jax.ShapeDtypeStruct(xs.shape, xs.dtype),
        grid_spec=grid_spec,
        input_output_aliases={4: 0},
        compiler_params=_cparams(("arbitrary", "arbitrary")),
        name=name,
    )(tile_e, tile_v, tile_start, tile_rows, xs, w1, b1p.reshape(N_EXPERTS, 1, 2 * D_FF), w2,
      b2.reshape(N_EXPERTS, 1, D_MODEL), perm)


def _combine_kernel(gstart_ref, e_ref, rank_ref, h2_ref, tw_ref, gf_ref, yb_hbm, o_ref, buf, sem, *, tm):
    group_rows = DMA_WAIT_ROWS * MOE_SLAB

    def make_copy(a):
        t = a >> TOP_K_LOG2
        k = a & (TOP_K - 1)
        slot = gstart_ref[e_ref[0, a]] + rank_ref[0, a]
        return pltpu.make_async_copy(_slab(yb_hbm, slot), _slab(buf, k * tm + t), sem)

    def group_wait():
        return pltpu.make_async_copy(yb_hbm.at[pl.ds(0, group_rows)], buf.at[pl.ds(0, group_rows)], sem)

    _row_copies(make_copy, group_wait, tm * TOP_K)

    tw = tw_ref[...]
    cols_lo, cols_hi = [], []
    for j in range(MOE_SLAB):
        clo = chi = None
        for k in range(TOP_K):
            lo, hi = _unpack_rows(buf[pl.ds(k * tm * MOE_SLAB + j, tm, stride=MOE_SLAB), :])
            w = tw[:, k:k + 1]
            clo = w * lo if clo is None else clo + w * lo
            chi = w * hi if chi is None else chi + w * hi
        cols_lo.append(clo)
        cols_hi.append(chi)
    h3 = h2_ref[...] + jnp.concatenate(cols_lo + cols_hi, axis=-1)
    o_ref[...] = _rms(h3, gf_ref[...])


def _combine(gstart, top_e, rank, h2, tw, gf, yb, *, tm, name="moe_combine"):
    n, d = h2.shape
    steps = n // tm
    blk = pl.BlockSpec((None, 1, tm * TOP_K), lambda i, gs: (i, 0, 0), memory_space=pltpu.SMEM)
    grid_spec = pltpu.PrefetchScalarGridSpec(
        num_scalar_prefetch=1,
        grid=(steps,),
        in_specs=[
            blk, blk,
            pl.BlockSpec((tm, d), lambda i, gs: (i, 0)),
            pl.BlockSpec((tm, TOP_K), lambda i, gs: (i, 0)),
            pl.BlockSpec((1, d), lambda i, gs: (0, 0)),
            pl.BlockSpec(memory_space=pl.ANY),
        ],
        out_specs=pl.BlockSpec((tm, d), lambda i, gs: (i, 0)),
        scratch_shapes=[
            pltpu.VMEM((tm * TOP_K * MOE_SLAB, V7X_LANES), U32),
            pltpu.SemaphoreType.DMA,
        ],
    )
    return pl.pallas_call(
        functools.partial(_combine_kernel, tm=tm),
        out_shape=jax.ShapeDtypeStruct((n, d), F32),
        grid_spec=grid_spec,
        compiler_params=_cparams(("arbitrary",)),
        name=name,
    )(gstart, top_e.reshape(steps, 1, tm * TOP_K), rank.reshape(steps, 1, tm * TOP_K), h2, tw,
      gf.reshape(1, d), yb)


def _rope_tables(positions):
    b, s = positions.shape
    half = HEAD_DIM // 2
    inv_freq = 1.0 / (ROPE_THETA ** (jnp.arange(half, dtype=F32) / half))
    ang = positions.astype(F32)[..., None] * inv_freq
    cos = jnp.cos(ang)
    sin = jnp.sin(ang)
    cs = jnp.concatenate([cos, cos], axis=-1)
    sn = jnp.concatenate([-sin, sin], axis=-1)

    def deint(a):
        a = a.reshape(b, s // ATT_TILE, WIN_BLOCK, DIL_MAX, HEAD_DIM)
        return a.transpose(0, 1, 3, 2, 4)

    return deint(cs), deint(sn)


def _routing(counts, n_tok):
    cnt = counts.reshape(N_EXPERTS).astype(I32)
    padded = (cnt + MOE_ROW_ALIGN - 1) // MOE_ROW_ALIGN * MOE_ROW_ALIGN
    gstart = (jnp.cumsum(padded) - padded).astype(I32)
    rows_total = n_tok * TOP_K + N_EXPERTS * MOE_ROW_ALIGN

    def first_above(ends, q):
        return jnp.minimum(jnp.sum((ends[None, :] <= q[:, None]).astype(I32), axis=1), N_EXPERTS - 1)

    main_rows = jnp.minimum(padded, MOE_TMAX)
    main = (jnp.arange(N_EXPERTS, dtype=I32), (main_rows > 0).astype(I32), gstart.astype(I32),
            main_rows.astype(I32))

    over = padded - main_rows
    nt_e = (over + MOE_TMAX - 1) // MOE_TMAX
    tend = jnp.cumsum(nt_e)
    tstart = tend - nt_e
    n_over = tend[-1]
    ti = jnp.arange(rows_total // MOE_TMAX, dtype=I32)
    valid = ti < n_over
    tic = jnp.clip(ti, 0, jnp.maximum(n_over - 1, 0))
    te = first_above(tend, tic)
    local = tic - tstart[te]
    row0 = gstart[te] + (local + 1) * MOE_TMAX
    rows = jnp.where(valid, jnp.clip(over[te] - local * MOE_TMAX, 0, MOE_TMAX), 0)
    overflow = (te, valid.astype(I32), row0.astype(I32), rows.astype(I32))
    return gstart, cnt, padded, rows_total, main, overflow, n_over > 0


def kernel(x, mem, positions, norm_mix_g, w_in, lb_raw, hgrn_norm_g, w_br_hgrn, w_br_attn, w_out,
           norm_cross_g, norm_mem_g, w_cq, w_ckv, w_co, norm_moe_g, w_router, b_router,
           w_mlp1, b_mlp1, w_mlp2, b_mlp2, norm_final_g):
    bsz, seq, d = x.shape
    assert w_in.shape[0] == 1 and d == D_MODEL and seq % ATT_TILE == 0
    n_tok = bsz * seq
    lower_bounds = jnp.cumsum(jax.nn.softmax(lb_raw.astype(F32), axis=0), axis=0)
    cs, sn = _rope_tables(positions)

    wl = w_in[0]
    c_h = 4 * HG_WIDTH
    c_a = c_h + ATT_Q_WIDTH + 2 * ATT_KV_WIDTH
    tn = 1024
    x2d = x.reshape(n_tok, d)
    n_h, n_a, n_g = c_h // tn, (c_a - c_h) // tn, 2 * D_MODEL // tn
    proj = _in_proj(x2d, norm_mix_g[0], wl, tm=1024, tn=tn, name="in_proj",
                    col_block=lambda j: jnp.where(j < n_h, j, jnp.where(j < n_h + n_g, j + n_a, j - n_g)))
    proj3 = proj.reshape(bsz, seq, -1)

    o_h = _hgrn(proj3, lower_bounds[0], hgrn_norm_g[0], ts=512)
    o_a = _dilated_attn(proj3, cs, sn, q_block=(c_h + 2 * D_MODEL) // HEAD_DIM)

    h1, n2 = _merge_out(o_h.reshape(n_tok, HG_WIDTH), o_a.reshape(n_tok, ATT_KV_WIDTH), proj, x2d,
                        w_br_hgrn[0].astype(BF16), w_br_attn[0].astype(BF16), w_out[0].astype(BF16),
                        norm_cross_g[0], tm=256, gate_block=c_h // (2 * D_MODEL))

    mem2d = mem.reshape(-1, d)
    kv = _in_proj(mem2d, norm_mem_g, w_ckv[0], tm=mem2d.shape[0], tn=2 * CROSS_WIDTH, name="mem_kv")
    wrh = w_router[0].astype(BF16)
    wrl = (w_router[0] - wrh.astype(F32)).astype(BF16)
    h2, n3, top_e, top_w, rank, counts = _cross(
        n2, h1, kv, w_cq[0].astype(BF16), w_co[0].astype(BF16), norm_moe_g[0], wrh, wrl, b_router[0],
        batch=bsz, tm=512)

    gstart, cnt, padded, rows_total, main_tiles, over_tiles, has_over = _routing(counts, n_tok)
    xs = _dispatch(gstart, cnt, padded, top_e, rank, n3, rows_total=rows_total, tm=DISPATCH_TOKENS)

    ii = jnp.arange(2 * V7X_LANES)
    src = jnp.where(ii < V7X_LANES, 2 * ii, 2 * (ii - V7X_LANES) + 1)
    perm = (ii[:, None] == src[None, :]).astype(BF16)
    b1p = b_mlp1[0].reshape(N_EXPERTS, -1, V7X_LANES, 2).transpose(0, 1, 3, 2).reshape(N_EXPERTS, 2 * D_FF)
    def run_moe(tiles, rows_buf, name):
        return _moe(*tiles, rows_buf, w_mlp1[0], b1p, w_mlp2[0], b_mlp2[0], perm, name=name)

    yb = run_moe(main_tiles, xs, "moe_mlp")
    yb = lax.cond(has_over, lambda rows_buf: run_moe(over_tiles, rows_buf, "moe_mlp_overflow"),
                  lambda rows_buf: rows_buf, yb)

    out = _combine(gstart, top_e, rank, h2, top_w, norm_final_g, yb, tm=COMBINE_TOKENS)
    return out.reshape(bsz, seq, d)
```

```python
import functools

import jax
import jax.numpy as jnp
from jax import lax
from jax.experimental import pallas as pl
from jax.experimental.pallas import tpu as pltpu

F32 = jnp.float32
BF16 = jnp.bfloat16
I32 = jnp.int32
U32 = jnp.uint32

D_MODEL = 2048
HEAD_DIM = 128
HG_HEADS = 8
HG_WIDTH = HG_HEADS * HEAD_DIM
N_KV_HEADS = 8
N_GROUPS = 3
ATT_Q_WIDTH = N_GROUPS * N_KV_HEADS * HEAD_DIM
ATT_KV_WIDTH = N_KV_HEADS * HEAD_DIM
WIN_BLOCK = 128
ROPE_THETA = 10000.0
CROSS_HEADS = 4
CROSS_WIDTH = CROSS_HEADS * HEAD_DIM
N_EXPERTS = 32
TOP_K = 4
TOP_K_LOG2 = 2
D_FF = D_MODEL
SWIGLU_ALPHA = 1.702
SWIGLU_LIMIT = 7.0
NORM_EPS = 1e-6

V7X_LANES = 128
V7X_VMEM_LIMIT_BYTES = 56 * 1024 * 1024

DIL_MAX = 16
ATT_TILE = DIL_MAX * WIN_BLOCK
ATT_UNROLL = 16
HG_BLOCK = 16
HG_GROUP = 32
NEG_BIG = -1e30

MOE_TMAX = 1536
MOE_ROW_ALIGN = 128
MOE_BLOCK_CHUNKS = 8
MOE_TF = 512
MOE_SLAB = D_MODEL // 2 // V7X_LANES
DISPATCH_TOKENS = 512
COMBINE_TOKENS = 512
DMA_WAIT_ROWS = 128
DMA_ISSUE_UNROLL = 8


def _rms(x, g):
    ms = jnp.mean(x * x, axis=-1, keepdims=True)
    return x * lax.rsqrt(ms + NORM_EPS) * g


def _cparams(sem, vmem=V7X_VMEM_LIMIT_BYTES):
    return pltpu.CompilerParams(dimension_semantics=sem, vmem_limit_bytes=vmem)


def _in_proj_kernel(x_ref, g_ref, w_ref, o_ref, xn_ref):
    @pl.when(pl.program_id(1) == 0)
    def _():
        xn_ref[...] = _rms(x_ref[...], g_ref[...]).astype(BF16)

    o_ref[...] = jnp.dot(xn_ref[...], w_ref[...].astype(BF16), preferred_element_type=F32).astype(o_ref.dtype)


def _in_proj(x2d, g, w, *, tm, tn, name, col_block=lambda j: j, ncols=None):
    n, d = x2d.shape
    wc = w.shape[1] if ncols is None else ncols
    return pl.pallas_call(
        _in_proj_kernel,
        out_shape=jax.ShapeDtypeStruct((n, wc), BF16),
        grid=(n // tm, wc // tn),
        in_specs=[
            pl.BlockSpec((tm, d), lambda i, j: (i, 0)),
            pl.BlockSpec((1, d), lambda i, j: (0, 0)),
            pl.BlockSpec((d, tn), lambda i, j: (0, col_block(j))),
        ],
        out_specs=pl.BlockSpec((tm, tn), lambda i, j: (i, j)),
        scratch_shapes=[pltpu.VMEM((tm, d), BF16)],
        compiler_params=_cparams(("parallel", "arbitrary")),
        name=name,
    )(x2d, g.reshape(1, d), w)


def _hgrn_kernel(q_ref, f_ref, i_ref, g_ref, lb_ref, gn_ref, o_ref, st_ref, kin_s, b_s, v_s, *, ts):
    @pl.when(pl.program_id(2) == 0)
    def _():
        st_ref[...] = jnp.zeros_like(st_ref)

    lb = lb_ref[...]
    oml = 1.0 - lb
    gn = gn_ref[...]
    half = HG_BLOCK // 2
    row = lax.broadcasted_iota(I32, (HG_BLOCK, HEAD_DIM), 0)
    row8 = lax.broadcasted_iota(I32, (half, HEAD_DIM), 0)
    nt_dims = (((1,), (1,)), ((), ()))
    tn_dims = (((0,), (0,)), ((), ()))

    def front(g, t0):
        sl = pl.ds(t0 + g * HG_BLOCK, HG_BLOCK)
        q = q_ref[sl, :].astype(F32)
        hf = f_ref[sl, :].astype(F32)
        v = i_ref[sl, :].astype(F32)
        kin = oml * jax.nn.sigmoid(-hf)
        b = jnp.log2(lb + oml * jax.nn.sigmoid(hf))
        for sh in (1, 2, 4, 8):
            b = b + jnp.where(row >= sh, pltpu.roll(b, sh, 0), 0.0)
        kin_s[g] = kin
        b_s[g] = b
        v_s[g] = v
        q_lo, q_hi = q[:half], q[half:]
        b_lo, b_hi = b[:half], b[half:]
        o_lo = jnp.zeros((half, HEAD_DIM), F32)
        o_hi = jnp.zeros((half, HEAD_DIM), F32)
        for s in range(HG_BLOCK):
            ks = kin_s[g, s:s + 1, :]
            bs = b_s[g, s:s + 1, :]
            vs = v_s[g, s:s + 1, :]
            if s < half:
                w = q_lo * ks * jnp.exp2(b_lo - bs)
                if s > 0:
                    w = jnp.where(row8 >= s, w, 0.0)
                o_lo = o_lo + jnp.sum(w, axis=-1, keepdims=True) * vs
                w = q_hi * ks * jnp.exp2(b_hi - bs)
            else:
                w = q_hi * ks * jnp.exp2(b_hi - bs)
                if s > half:
                    w = jnp.where(row8 >= s - half, w, 0.0)
            o_hi = o_hi + jnp.sum(w, axis=-1, keepdims=True) * vs
        bl = b_s[g, HG_BLOCK - 1:HG_BLOCK, :]
        qd = (q * jnp.exp2(b)).astype(BF16)
        kd = (kin * jnp.exp2(bl - b)).astype(BF16)
        upd = lax.dot_general(v.astype(BF16), kd, tn_dims, preferred_element_type=F32)
        return jnp.concatenate([o_lo, o_hi], axis=0), qd, upd, jnp.exp2(bl)

    def body(i, carry):
        t0 = pl.multiple_of(i * (HG_GROUP * HG_BLOCK), HG_GROUP * HG_BLOCK)
        fronts = [front(g, t0) for g in range(HG_GROUP)]
        st = st_ref[...]
        for g, (o_diag, qd, upd, dec) in enumerate(fronts):
            o = o_diag + lax.dot_general(qd, st.astype(BF16), nt_dims, preferred_element_type=F32)
            st = st * dec + upd
            sl = pl.ds(t0 + g * HG_BLOCK, HG_BLOCK)
            hg = g_ref[sl, :].astype(F32)
            o_ref[sl, :] = (_rms(o, gn) * (hg * jax.nn.sigmoid(hg))).astype(o_ref.dtype)
        st_ref[...] = st
        return carry

    lax.fori_loop(0, ts // (HG_GROUP * HG_BLOCK), body, 0)


def _hgrn(proj_h, lb, gn, *, ts, name="hgrn"):
    b, s, _ = proj_h.shape
    h = HG_HEADS

    def spec(off):
        return pl.BlockSpec((None, ts, HEAD_DIM), lambda bi, hi, si: (bi, si, off + hi))

    vec = pl.BlockSpec((1, HEAD_DIM), lambda bi, hi, si: (0, hi))
    return pl.pallas_call(
        functools.partial(_hgrn_kernel, ts=ts),
        out_shape=jax.ShapeDtypeStruct((b, s, HG_WIDTH), BF16),
        grid=(b, h, s // ts),
        in_specs=[spec(0), spec(h), spec(2 * h), spec(3 * h), vec, vec],
        out_specs=pl.BlockSpec((None, ts, HEAD_DIM), lambda bi, hi, si: (bi, si, hi)),
        scratch_shapes=[
            pltpu.VMEM((HEAD_DIM, HEAD_DIM), F32),
            pltpu.VMEM((HG_GROUP, HG_BLOCK, HEAD_DIM), F32),
            pltpu.VMEM((HG_GROUP, HG_BLOCK, HEAD_DIM), F32),
            pltpu.VMEM((HG_GROUP, HG_BLOCK, HEAD_DIM), F32),
        ],
        compiler_params=_cparams(("parallel", "parallel", "arbitrary")),
        name=name,
    )(proj_h, proj_h, proj_h, proj_h, lb.reshape(1, HG_WIDTH), gn.reshape(1, HG_WIDTH))


def _attn_bias(kind):
    rq = lax.broadcasted_iota(I32, (WIN_BLOCK, 2 * WIN_BLOCK), 0)
    ck = lax.broadcasted_iota(I32, (WIN_BLOCK, 2 * WIN_BLOCK), 1)
    if kind == 2:
        dist = rq + WIN_BLOCK - ck
        first = ck < WIN_BLOCK
    elif kind == 1:
        dist = 4 * ((rq & 31) - (ck & 63) + 32) + ((rq >> 5) - (ck >> 6))
        first = (ck & 63) < 32
    else:
        dist = 16 * ((rq & 7) - (ck & 15) + 8) + ((rq >> 3) - (ck >> 4))
        first = (ck & 15) < 8
    valid = (dist >= 0) & (dist <= WIN_BLOCK)
    return (jnp.where(valid, 0.0, NEG_BIG).astype(F32),
            jnp.where(valid & jnp.logical_not(first), 0.0, NEG_BIG).astype(F32))


def _attn_kernel(q0_ref, q1_ref, q2_ref, k_ref, v_ref, cs_ref, sn_ref, o_ref,
                 qr, kext, vext, acc, mrun, lrun, bias, onat, u0, u1, u2, u3, u4):
    ti = pl.program_id(2)
    wb = WIN_BLOCK
    scale = HEAD_DIM ** -0.5

    ubufs = (u0, u1, u2, u3, u4)
    for src, ub in zip((q0_ref, q1_ref, q2_ref, k_ref, v_ref), ubufs):
        ub[...] = pltpu.bitcast(src[...], U32)

    @pl.when(ti == 0)
    def _():
        kext[:, 0:wb, :] = jnp.zeros((DIL_MAX, wb, HEAD_DIM), F32)
        vext[:, 0:wb, :] = jnp.zeros((DIL_MAX, wb, HEAD_DIM), F32)

    @pl.when(ti > 0)
    def _():
        kext[:, 0:wb, :] = kext[:, wb:2 * wb, :]
        vext[:, 0:wb, :] = vext[:, wb:2 * wb, :]

    for kind in range(N_GROUPS):
        full, nofirst = _attn_bias(kind)
        bias[2 * kind] = full
        bias[2 * kind + 1] = nofirst

    def rope_body(rp, carry):
        rows = pl.ds(rp, wb, stride=DIL_MAX // 2)
        parts = [_unpack_rows(ub[rows, :]) for ub in ubufs]
        for par in range(2):
            r = 2 * rp + par
            cs = cs_ref[r]
            sn = sn_ref[r]
            for g in range(N_GROUPS):
                q = parts[g][par]
                qr[g, r] = (q * cs + pltpu.roll(q, HEAD_DIM // 2, 1) * sn) * scale
            k = parts[N_GROUPS][par]
            kext[r, wb:2 * wb, :] = k * cs + pltpu.roll(k, HEAD_DIM // 2, 1) * sn
            vext[r, wb:2 * wb, :] = parts[N_GROUPS + 1][par]
            acc[r] = jnp.zeros((wb, HEAD_DIM), F32)
            mrun[r] = jnp.full((wb, HEAD_DIM), NEG_BIG, F32)
            lrun[r] = jnp.zeros((wb, HEAD_DIM), F32)
        return carry

    lax.fori_loop(0, DIL_MAX // 2, rope_body, 0)

    nt_dims = (((1,), (1,)), ((), ()))

    def block(qb, kb, vb, bias_blk):
        s = lax.dot_general(qb.astype(BF16), kb.astype(BF16), nt_dims, preferred_element_type=F32)
        s = s + bias_blk
        m = jnp.max(s, axis=-1, keepdims=True)
        p = jnp.exp(s - m)
        l = jnp.sum(p, axis=-1, keepdims=True)
        n = jnp.dot(p.astype(BF16), vb.astype(BF16), preferred_element_type=F32)
        return n, jnp.broadcast_to(m, (wb, HEAD_DIM)), jnp.broadcast_to(l, (wb, HEAD_DIM))

    def merge(r, rows, n, m, l):
        m_old = mrun[r, rows, :]
        m_new = jnp.maximum(m_old, m)
        a = jnp.exp(m_old - m_new)
        bb = jnp.exp(m - m_new)
        acc[r, rows, :] = acc[r, rows, :] * a + n * bb
        lrun[r, rows, :] = lrun[r, rows, :] * a + l * bb
        mrun[r, rows, :] = m_new

    first_tile = jnp.where(ti == 0, 1, 0)


    def g2_body(i, carry):
        rs = [i * ATT_UNROLL + u for u in range(ATT_UNROLL)]
        res = [block(qr[2, r], kext[r], vext[r], bias[4 + first_tile]) for r in rs]
        for r, (n, m, l) in zip(rs, res):
            merge(r, pl.ds(0, wb), n, m, l)
        return carry

    lax.fori_loop(0, DIL_MAX // ATT_UNROLL, g2_body, 0)

    def g1_body(i, carry):
        res = []
        for u in range(ATT_UNROLL // 4):
            mb = i * (ATT_UNROLL // 4) + u
            q0 = pl.multiple_of(32 * mb, 32)
            k0 = pl.multiple_of(96 + 32 * mb, 32)
            use_first = jnp.where(mb == 0, first_tile, 0)
            for r4 in range(4):
                qb = jnp.concatenate([qr[1, r4 + 4 * j, pl.ds(q0, 32), :] for j in range(4)], axis=0)
                kb = jnp.concatenate([kext[r4 + 4 * j, pl.ds(k0, 64), :] for j in range(4)], axis=0)
                vb = jnp.concatenate([vext[r4 + 4 * j, pl.ds(k0, 64), :] for j in range(4)], axis=0)
                res.append((r4, q0, block(qb, kb, vb, bias[2 + use_first])))
        for r4, q0, (n, m, l) in res:
            for j in range(4):
                sl = slice(32 * j, 32 * (j + 1))
                merge(r4 + 4 * j, pl.ds(q0, 32), n[sl], m[sl], l[sl])
        return carry

    lax.fori_loop(0, DIL_MAX // ATT_UNROLL, g1_body, 0)

    def g0_body(i, carry):
        res = []
        for u in range(ATT_UNROLL):
            mb = i * ATT_UNROLL + u
            q0 = pl.multiple_of(8 * mb, 8)
            k0 = pl.multiple_of(120 + 8 * mb, 8)
            qb = jnp.concatenate([qr[0, r, pl.ds(q0, 8), :] for r in range(DIL_MAX)], axis=0)
            kb = jnp.concatenate([kext[r, pl.ds(k0, 16), :] for r in range(DIL_MAX)], axis=0)
            vb = jnp.concatenate([vext[r, pl.ds(k0, 16), :] for r in range(DIL_MAX)], axis=0)
            use_first = jnp.where(mb == 0, first_tile, 0)
            res.append((q0, block(qb, kb, vb, bias[use_first])))
        for q0, (n, m, l) in res:
            for r in range(DIL_MAX):
                sl = slice(8 * r, 8 * (r + 1))
                merge(r, pl.ds(q0, 8), n[sl], m[sl], l[sl])
        return carry

    lax.fori_loop(0, DIL_MAX // ATT_UNROLL, g0_body, 0)

    for r in range(DIL_MAX):
        onat[pl.ds(r, wb, stride=DIL_MAX), :] = acc[r] / lrun[r]
    o_ref[...] = onat[...].astype(o_ref.dtype)


def _dilated_attn(proj, cs, sn, *, q_block, name="dilated_attn"):
    b, s, _ = proj.shape
    nt = s // ATT_TILE
    h = N_KV_HEADS

    def spec(off):
        return pl.BlockSpec((None, ATT_TILE, HEAD_DIM), lambda bi, hi, ti: (bi, ti, q_block + off + hi))

    tab = pl.BlockSpec((None, None, DIL_MAX, WIN_BLOCK, HEAD_DIM), lambda bi, hi, ti: (bi, ti, 0, 0, 0))
    ubuf = pltpu.VMEM((ATT_TILE // 2, HEAD_DIM), U32)
    return pl.pallas_call(
        _attn_kernel,
        out_shape=jax.ShapeDtypeStruct((b, nt * ATT_TILE, ATT_KV_WIDTH), BF16),
        grid=(b, h, nt),
        in_specs=[spec(0), spec(h), spec(2 * h), spec(3 * h), spec(4 * h), tab, tab],
        out_specs=pl.BlockSpec((None, ATT_TILE, HEAD_DIM), lambda bi, hi, ti: (bi, ti, hi)),
        scratch_shapes=[
            pltpu.VMEM((N_GROUPS, DIL_MAX, WIN_BLOCK, HEAD_DIM), F32),
            pltpu.VMEM((DIL_MAX, 2 * WIN_BLOCK, HEAD_DIM), F32),
            pltpu.VMEM((DIL_MAX, 2 * WIN_BLOCK, HEAD_DIM), F32),
            pltpu.VMEM((DIL_MAX, WIN_BLOCK, HEAD_DIM), F32),
            pltpu.VMEM((DIL_MAX, WIN_BLOCK, HEAD_DIM), F32),
            pltpu.VMEM((DIL_MAX, WIN_BLOCK, HEAD_DIM), F32),
            pltpu.VMEM((2 * N_GROUPS, WIN_BLOCK, 2 * WIN_BLOCK), F32),
            pltpu.VMEM((ATT_TILE, HEAD_DIM), F32),
            ubuf, ubuf, ubuf, ubuf, ubuf,
        ],
        compiler_params=_cparams(("arbitrary", "arbitrary", "arbitrary")),
        name=name,
    )(proj, proj, proj, proj, proj, cs, sn)


def _merge_out_kernel(oh_ref, oa_ref, gate_ref, x_ref, wh_ref, wa_ref, wo_ref, gc_ref, h1_ref, n2_ref):
    ga = gate_ref[:, :D_MODEL].astype(F32)
    gb = gate_ref[:, D_MODEL:].astype(F32)
    yh = jnp.dot(oh_ref[...], wh_ref[...], preferred_element_type=F32)
    ya = jnp.dot(oa_ref[...], wa_ref[...], preferred_element_type=F32)
    merged = jax.nn.sigmoid(ga) * yh + jax.nn.sigmoid(gb) * ya
    h1 = x_ref[...] + jnp.dot(merged.astype(BF16), wo_ref[...], preferred_element_type=F32)
    h1_ref[...] = h1
    n2_ref[...] = _rms(h1, gc_ref[...]).astype(BF16)


def _merge_out(oh, oa, gates, x2d, wh, wa, wo, gc, *, tm, gate_block, name="merge_out"):
    n, d = x2d.shape

    def const(shape):
        return pl.BlockSpec(shape, lambda i: (0, 0), pipeline_mode=pl.Buffered(1))

    return pl.pallas_call(
        _merge_out_kernel,
        out_shape=(jax.ShapeDtypeStruct((n, d), F32), jax.ShapeDtypeStruct((n, d), BF16)),
        grid=(n // tm,),
        in_specs=[
            pl.BlockSpec((tm, HG_WIDTH), lambda i: (i, 0)),
            pl.BlockSpec((tm, ATT_KV_WIDTH), lambda i: (i, 0)),
            pl.BlockSpec((tm, 2 * d), lambda i: (i, gate_block)),
            pl.BlockSpec((tm, d), lambda i: (i, 0)),
            const((HG_WIDTH, d)), const((ATT_KV_WIDTH, d)), const((d, d)), const((1, d)),
        ],
        out_specs=(pl.BlockSpec((tm, d), lambda i: (i, 0)), pl.BlockSpec((tm, d), lambda i: (i, 0))),
        compiler_params=_cparams(("parallel",)),
        name=name,
    )(oh, oa, gates, x2d, wh, wa, wo, gc.reshape(1, d))


def _cross_kernel(n2_ref, h1_ref, kv_ref, wq_ref, wo_ref, gm_ref, wrh_ref, wrl_ref, br_ref,
                  h2_ref, n3_ref, idx_ref, tw_ref, rank_ref, cnt_ref, carry_ref, *, tm):
    @pl.when((pl.program_id(0) == 0) & (pl.program_id(1) == 0))
    def _():
        carry_ref[...] = jnp.zeros_like(carry_ref)

    nt_dims = (((1,), (1,)), ((), ()))
    scale = HEAD_DIM ** -0.5
    q = (jnp.dot(n2_ref[...], wq_ref[...], preferred_element_type=F32) * scale).astype(BF16)
    outs = []
    for hh in range(CROSS_HEADS):
        sl = slice(hh * HEAD_DIM, (hh + 1) * HEAD_DIM)
        kh = kv_ref[:, sl]
        vh = kv_ref[:, CROSS_WIDTH + hh * HEAD_DIM:CROSS_WIDTH + (hh + 1) * HEAD_DIM]
        s = lax.dot_general(q[:, sl], kh, nt_dims, preferred_element_type=F32)
        p = jnp.exp(s - jnp.max(s, axis=-1, keepdims=True))
        l = jnp.sum(p, axis=-1, keepdims=True)
        outs.append(jnp.dot(p.astype(BF16), vh, preferred_element_type=F32) / l)
    o = jnp.concatenate(outs, axis=-1).astype(BF16)
    h2 = h1_ref[...] + jnp.dot(o, wo_ref[...], preferred_element_type=F32)
    h2_ref[...] = h2
    n3 = _rms(h2, gm_ref[...])
    n3_ref[...] = n3

    n3h = n3.astype(BF16)
    n3l = (n3 - n3h.astype(F32)).astype(BF16)
    wrh = wrh_ref[...]
    logits = (jnp.dot(n3h, wrh, preferred_element_type=F32)
              + jnp.dot(n3l, wrh, preferred_element_type=F32)
              + jnp.dot(n3h, wrl_ref[...], preferred_element_type=F32)
              + br_ref[...])
    lane = lax.broadcasted_iota(I32, (tm, N_EXPERTS), 1).astype(F32)
    vals, idxs, hots = [], [], []
    cur = logits
    for _ in range(TOP_K):
        mx = jnp.max(cur, axis=-1, keepdims=True)
        ix = jnp.min(jnp.where(cur == mx, lane, float(N_EXPERTS)), axis=-1, keepdims=True)
        hot = lane == ix
        vals.append(mx)
        idxs.append(ix)
        hots.append(hot)
        cur = jnp.where(hot, -jnp.inf, cur)
    es = [jnp.exp(v - vals[0]) for v in vals]
    den = es[0] + es[1] + es[2] + es[3]
    col = lax.broadcasted_iota(I32, (tm, TOP_K), 1)

    def pack(cols):
        out = jnp.broadcast_to(cols[TOP_K - 1], (tm, TOP_K))
        for k in range(TOP_K - 2, -1, -1):
            out = jnp.where(col == k, cols[k], out)
        return out

    idx_ref[...] = pack(idxs).astype(I32)
    tw_ref[...] = pack([e / den for e in es])

    cmat = (hots[0] | hots[1] | hots[2] | hots[3]).astype(F32)
    rr = lax.broadcasted_iota(I32, (tm, tm), 0)
    cc = lax.broadcasted_iota(I32, (tm, tm), 1)
    tri = (cc < rr).astype(BF16)
    before = jnp.dot(tri, cmat.astype(BF16), preferred_element_type=F32) + carry_ref[...]
    ranks = [jnp.sum(jnp.where(hot, before, 0.0), axis=-1, keepdims=True) for hot in hots]
    rank_ref[...] = pack(ranks).astype(I32)
    carry = carry_ref[...] + jnp.sum(cmat, axis=0, keepdims=True)
    carry_ref[...] = carry
    cnt_ref[...] = carry


def _cross(n2, h1, kv, wq, wo, gm, wrh, wrl, br, *, batch, tm, name="cross"):
    n, d = h1.shape
    per_b = n // batch // tm
    mem_len = kv.shape[0] // batch

    def const(shape):
        return pl.BlockSpec(shape, lambda bi, i: (0, 0), pipeline_mode=pl.Buffered(1))

    def row(bi, i):
        return (bi * per_b + i, 0)

    return pl.pallas_call(
        functools.partial(_cross_kernel, tm=tm),
        out_shape=(
            jax.ShapeDtypeStruct((n, d), F32),
            jax.ShapeDtypeStruct((n, d), F32),
            jax.ShapeDtypeStruct((n, TOP_K), I32),
            jax.ShapeDtypeStruct((n, TOP_K), F32),
            jax.ShapeDtypeStruct((n, TOP_K), I32),
            jax.ShapeDtypeStruct((1, N_EXPERTS), F32),
        ),
        grid=(batch, per_b),
        in_specs=[
            pl.BlockSpec((tm, d), row),
            pl.BlockSpec((tm, d), row),
            pl.BlockSpec((mem_len, 2 * CROSS_WIDTH), lambda bi, i: (bi, 0)),
            const((d, CROSS_WIDTH)), const((CROSS_WIDTH, d)), const((1, d)),
            const((d, N_EXPERTS)), const((d, N_EXPERTS)), const((1, N_EXPERTS)),
        ],
        out_specs=(
            pl.BlockSpec((tm, d), row),
            pl.BlockSpec((tm, d), row),
            pl.BlockSpec((tm, TOP_K), row),
            pl.BlockSpec((tm, TOP_K), row),
            pl.BlockSpec((tm, TOP_K), row),
            pl.BlockSpec((1, N_EXPERTS), lambda bi, i: (0, 0)),
        ),
        scratch_shapes=[pltpu.VMEM((1, N_EXPERTS), F32)],
        compiler_params=_cparams(("arbitrary", "arbitrary")),
        name=name,
    )(n2, h1, kv, wq, wo, gm.reshape(1, d), wrh, wrl, br.reshape(1, N_EXPERTS))


_HI_MASK = 0xFFFF0000


def _pack_rows(x):
    half = x.shape[1] // 2
    lo = lax.bitcast_convert_type(x[:, :half].astype(BF16).astype(F32), U32) >> 16
    hi = lax.bitcast_convert_type(x[:, half:].astype(BF16).astype(F32), U32) & jnp.uint32(_HI_MASK)
    return lo | hi


def _unpack_rows(u):
    lo = lax.bitcast_convert_type(u << 16, F32)
    hi = lax.bitcast_convert_type(u & jnp.uint32(_HI_MASK), F32)
    return lo, hi


def _slab(ref, row):
    return ref.at[pl.ds(pl.multiple_of(row * MOE_SLAB, MOE_SLAB), MOE_SLAB)]


def _start_group(make_copy, c):
    def pair(j, carry):
        a = c * DMA_WAIT_ROWS + 2 * j
        make_copy(a).start(priority=0)
        make_copy(a + 1).start(priority=1)
        return carry

    lax.fori_loop(0, DMA_WAIT_ROWS // 2, pair, 0, unroll=DMA_ISSUE_UNROLL // 2)


def _row_copies(make_copy, make_group_wait, count):
    ngroup = count // DMA_WAIT_ROWS

    def group(c, carry):
        _start_group(make_copy, c)

        @pl.when(c > 0)
        def _():
            make_group_wait().wait()

        return carry

    lax.fori_loop(0, ngroup, group, 0)
    make_group_wait().wait()


def _dispatch_kernel(gstart_ref, cnt_ref, padded_ref, e_ref, rank_ref, n3_ref, xs_hbm, stage, sem,
                     *, real_steps, tm, rows_total):
    step = pl.program_id(0)
    group_rows = DMA_WAIT_ROWS * MOE_SLAB

    def group_wait():
        return pltpu.make_async_copy(stage.at[pl.ds(0, group_rows)], xs_hbm.at[pl.ds(0, group_rows)], sem)

    @pl.when(step < real_steps)
    def _():
        u = _pack_rows(n3_ref[...])
        for j in range(MOE_SLAB):
            stage[pl.ds(j, tm, stride=MOE_SLAB), :] = u[:, j * V7X_LANES:(j + 1) * V7X_LANES]

        def make_copy(a):
            slot = gstart_ref[e_ref[0, a]] + rank_ref[0, a]
            return pltpu.make_async_copy(_slab(stage, a >> TOP_K_LOG2), _slab(xs_hbm, slot), sem)

        _row_copies(make_copy, group_wait, tm * TOP_K)

    @pl.when(step == real_steps)
    def _():
        stage[0:MOE_SLAB, :] = jnp.zeros((MOE_SLAB, V7X_LANES), U32)

        def fill(lo, hi):
            def one(r, carry):
                pltpu.make_async_copy(_slab(stage, 0), _slab(xs_hbm, r), sem).start()
                return carry

            lax.fori_loop(lo, hi, one, 0)

        def per_expert(e, carry):
            fill(gstart_ref[e] + cnt_ref[e], gstart_ref[e] + padded_ref[e])
            return carry

        lax.fori_loop(0, N_EXPERTS, per_expert, 0)
        fill(gstart_ref[N_EXPERTS - 1] + padded_ref[N_EXPERTS - 1], rows_total)

        def retire(c, carry):
            group_wait().wait()
            return carry

        lax.fori_loop(0, (rows_total - real_steps * tm * TOP_K) // DMA_WAIT_ROWS, retire, 0)


def _dispatch(gstart, cnt, padded, top_e, rank, n3, *, rows_total, tm, name="moe_dispatch"):
    n, d = n3.shape
    real_steps = n // tm
    blk = pl.BlockSpec((None, 1, tm * TOP_K), lambda i, *_: (jnp.minimum(i, real_steps - 1), 0, 0),
                       memory_space=pltpu.SMEM)
    grid_spec = pltpu.PrefetchScalarGridSpec(
        num_scalar_prefetch=3,
        grid=(real_steps + 1,),
        in_specs=[blk, blk, pl.BlockSpec((tm, d), lambda i, *_: (jnp.minimum(i, real_steps - 1), 0))],
        out_specs=pl.BlockSpec(memory_space=pl.ANY),
        scratch_shapes=[pltpu.VMEM((tm * MOE_SLAB, V7X_LANES), U32), pltpu.SemaphoreType.DMA],
    )
    return pl.pallas_call(
        functools.partial(_dispatch_kernel, real_steps=real_steps, tm=tm, rows_total=rows_total),
        out_shape=jax.ShapeDtypeStruct((rows_total * MOE_SLAB, V7X_LANES), U32),
        grid_spec=grid_spec,
        compiler_params=_cparams(("arbitrary",)),
        name=name,
    )(gstart, cnt, padded, top_e.reshape(real_steps, 1, tm * TOP_K), rank.reshape(real_steps, 1, tm * TOP_K), n3)


def _moe_kernel(te_ref, tv_ref, tstart_ref, trows_ref,
                xs_hbm, w1_ref, b1_ref, w2_ref, b2_ref, perm_ref, yb_hbm,
                stage, x16, acc, w1p, sems, *, nf):
    i = pl.program_id(0)
    f = pl.program_id(1)
    rows = trows_ref[i]
    start = tstart_ref[i]
    nchunk = rows // MOE_ROW_ALIGN
    chunk_slab_rows = MOE_ROW_ALIGN * MOE_SLAB
    half = 2 * V7X_LANES
    nhalf = 2 * MOE_TF // half

    def slot_rows(c):
        return pl.multiple_of((c & 1) * chunk_slab_rows, chunk_slab_rows)

    def hbm_rows(ref, c):
        r0 = pl.multiple_of((start + c * MOE_ROW_ALIGN) * MOE_SLAB, chunk_slab_rows)
        return ref.at[pl.ds(r0, chunk_slab_rows)]

    def in_copy(c):
        return pltpu.make_async_copy(hbm_rows(xs_hbm, c), stage.at[pl.ds(slot_rows(c), chunk_slab_rows)],
                                     sems.at[c & 1])

    def out_copy(c):
        return pltpu.make_async_copy(stage.at[pl.ds(slot_rows(c), chunk_slab_rows)], hbm_rows(yb_hbm, c),
                                     sems.at[c & 1])

    @pl.when((rows > 0) & (f == 0))
    def _():
        in_copy(0).start()

        def land(c, carry):
            @pl.when(c + 1 < nchunk)
            def _():
                in_copy(c + 1).start()

            in_copy(c).wait()
            r0 = pl.multiple_of(c * MOE_ROW_ALIGN, MOE_ROW_ALIGN)
            s0 = slot_rows(c)
            for j in range(MOE_SLAB):
                lo, hi = _unpack_rows(stage[pl.ds(s0 + j, MOE_ROW_ALIGN, stride=MOE_SLAB), :])
                x16[pl.ds(r0, MOE_ROW_ALIGN), j * V7X_LANES:(j + 1) * V7X_LANES] = lo.astype(BF16)
                x16[pl.ds(r0, MOE_ROW_ALIGN), (MOE_SLAB + j) * V7X_LANES:(MOE_SLAB + j + 1) * V7X_LANES] = (
                    hi.astype(BF16))
            acc[pl.ds(r0, MOE_ROW_ALIGN), :] = jnp.zeros((MOE_ROW_ALIGN, D_MODEL), F32)
            return carry

        lax.fori_loop(0, nchunk, land, 0)

    @pl.when(rows > 0)
    def _():
        perm = perm_ref[...]
        for c in range(nhalf):
            wb = w1_ref[:, c * half:(c + 1) * half].astype(BF16)
            w1p[:, c * half:(c + 1) * half] = jnp.dot(wb, perm, preferred_element_type=F32).astype(BF16)
        w2 = w2_ref[...].astype(BF16)
        b1 = b1_ref[...]

        def mlp(c0, nc):
            r0 = pl.multiple_of(c0 * MOE_ROW_ALIGN, MOE_ROW_ALIGN)
            nr = nc * MOE_ROW_ALIGN
            x = x16[pl.ds(r0, nr), :]
            h = jnp.dot(x, w1p[...], preferred_element_type=F32) + b1
            parts = []
            for c in range(nhalf):
                hg = jnp.minimum(h[:, c * half:c * half + V7X_LANES], SWIGLU_LIMIT)
                hl = jnp.clip(h[:, c * half + V7X_LANES:(c + 1) * half], -SWIGLU_LIMIT, SWIGLU_LIMIT)
                parts.append(hg * jax.nn.sigmoid(SWIGLU_ALPHA * hg) * (hl + 1.0))
            a = jnp.concatenate(parts, axis=-1).astype(BF16)
            acc[pl.ds(r0, nr), :] += jnp.dot(a, w2, preferred_element_type=F32)

        nquad = nchunk // MOE_BLOCK_CHUNKS

        def quad(c, carry):
            mlp(c * MOE_BLOCK_CHUNKS, MOE_BLOCK_CHUNKS)
            return carry

        lax.fori_loop(0, nquad, quad, 0)
        size = MOE_BLOCK_CHUNKS // 2
        while size >= 1:
            done = nchunk & ~(2 * size - 1)

            @pl.when((nchunk & size) != 0)
            def _(done=done, size=size):
                mlp(done, size)

            size //= 2

    @pl.when((rows > 0) & (f == nf - 1))
    def _():
        def emit(c, carry):
            @pl.when(c >= 2)
            def _():
                out_copy(c - 2).wait()

            r0 = pl.multiple_of(c * MOE_ROW_ALIGN, MOE_ROW_ALIGN)
            s0 = slot_rows(c)
            u = _pack_rows(acc[pl.ds(r0, MOE_ROW_ALIGN), :] + b2_ref[...])
            for j in range(MOE_SLAB):
                stage[pl.ds(s0 + j, MOE_ROW_ALIGN, stride=MOE_SLAB), :] = u[:, j * V7X_LANES:(j + 1) * V7X_LANES]
            out_copy(c).start()
            return carry

        lax.fori_loop(0, nchunk, emit, 0)

        @pl.when(nchunk >= 2)
        def _():
            out_copy(nchunk - 2).wait()

        out_copy(nchunk - 1).wait()


def _moe(tile_e, tile_v, tile_start, tile_rows, xs, w1, b1p, w2, b2, perm, *, name="moe_mlp"):
    nt = tile_e.shape[0]
    nf = D_FF // MOE_TF

    def w1_map(i, f, te, tv, ts, tr):
        return (te[i], 0, jnp.where(tv[i] > 0, f, nf - 1))

    def w2_map(i, f, te, tv, ts, tr):
        return (te[i], jnp.where(tv[i] > 0, f, nf - 1), 0)

    def b2_map(i, f, te, tv, ts, tr):
        return (te[i], 0, 0)

    grid_spec = pltpu.PrefetchScalarGridSpec(
        num_scalar_prefetch=4,
        grid=(nt, nf),
        in_specs=[
            pl.BlockSpec(memory_space=pl.ANY),
            pl.BlockSpec((None, D_MODEL, 2 * MOE_TF), w1_map),
            pl.BlockSpec((None, 1, 2 * MOE_TF), w1_map),
            pl.BlockSpec((None, MOE_TF, D_MODEL), w2_map),
            pl.BlockSpec((None, 1, D_MODEL), b2_map),
            pl.BlockSpec((2 * V7X_LANES, 2 * V7X_LANES), lambda i, f, te, tv, ts, tr: (0, 0)),
        ],
        out_specs=pl.BlockSpec(memory_space=pl.ANY),
        scratch_shapes=[
            pltpu.VMEM((2 * MOE_ROW_ALIGN * MOE_SLAB, V7X_LANES), U32),
            pltpu.VMEM((MOE_TMAX, D_MODEL), BF16),
            pltpu.VMEM((MOE_TMAX, D_MODEL), F32),
            pltpu.VMEM((D_MODEL, 2 * MOE_TF), BF16),
            pltpu.SemaphoreType.DMA((2,)),
        ],
    )
    return pl.pallas_call(
        functools.partial(_moe_kernel, nf=nf),
        out_shape=jax.ShapeDtypeStruct(xs.shape, xs.dtype),
        grid_spec=grid_spec,
        input_output_aliases={4: 0},
        compiler_params=_cparams(("arbitrary", "arbitrary")),
        name=name,
    )(tile_e, tile_v, tile_start, tile_rows, xs, w1, b1p.reshape(N_EXPERTS, 1, 2 * D_FF), w2,
      b2.reshape(N_EXPERTS, 1, D_MODEL), perm)


def _combine_kernel(gstart_ref, e_ref, rank_ref, e_next, rank_next, h2_ref, tw_ref, gf_ref, yb_hbm, o_ref,
                    buf, sems, *, tm):
    i = pl.program_id(0)
    steps = pl.num_programs(0)
    group_rows = DMA_WAIT_ROWS * MOE_SLAB
    half_slabs = tm * TOP_K
    ngroup = half_slabs // DMA_WAIT_ROWS

    def issue(eref, rref, half):
        def make_copy(a):
            t = a >> TOP_K_LOG2
            k = a & (TOP_K - 1)
            slot = gstart_ref[eref[0, a]] + rref[0, a]
            return pltpu.make_async_copy(_slab(yb_hbm, slot), _slab(buf, half * half_slabs + k * tm + t),
                                         sems.at[half])

        def group(c, carry):
            _start_group(make_copy, c)
            return carry

        lax.fori_loop(0, ngroup, group, 0)

    @pl.when(i == 0)
    def _():
        issue(e_ref, rank_ref, 0)

    @pl.when(i + 1 < steps)
    def _():
        issue(e_next, rank_next, (i + 1) & 1)

    mine = i & 1

    def retire(c, carry):
        pltpu.make_async_copy(yb_hbm.at[pl.ds(0, group_rows)], buf.at[pl.ds(0, group_rows)], sems.at[mine]).wait()
        return carry

    lax.fori_loop(0, ngroup, retire, 0)

    base = pl.multiple_of(mine * (half_slabs * MOE_SLAB), MOE_SLAB)
    tw = tw_ref[...]
    cols_lo, cols_hi = [], []
    for j in range(MOE_SLAB):
        clo = chi = None
        for k in range(TOP_K):
            lo, hi = _unpack_rows(buf[pl.ds(base + k * tm * MOE_SLAB + j, tm, stride=MOE_SLAB), :])
            w = tw[:, k:k + 1]
            clo = w * lo if clo is None else clo + w * lo
            chi = w * hi if chi is None else chi + w * hi
        cols_lo.append(clo)
        cols_hi.append(chi)
    h3 = h2_ref[...] + jnp.concatenate(cols_lo + cols_hi, axis=-1)
    o_ref[...] = _rms(h3, gf_ref[...])


def _combine(gstart, top_e, rank, h2, tw, gf, yb, *, tm, name="moe_combine"):
    n, d = h2.shape
    steps = n // tm
    blk = pl.BlockSpec((None, 1, tm * TOP_K), lambda i, gs: (i, 0, 0), memory_space=pltpu.SMEM)
    nxt = pl.BlockSpec((None, 1, tm * TOP_K), lambda i, gs: (jnp.minimum(i + 1, steps - 1), 0, 0),
                       memory_space=pltpu.SMEM)
    grid_spec = pltpu.PrefetchScalarGridSpec(
        num_scalar_prefetch=1,
        grid=(steps,),
        in_specs=[
            blk, blk, nxt, nxt,
            pl.BlockSpec((tm, d), lambda i, gs: (i, 0)),
            pl.BlockSpec((tm, TOP_K), lambda i, gs: (i, 0)),
            pl.BlockSpec((1, d), lambda i, gs: (0, 0)),
            pl.BlockSpec(memory_space=pl.ANY),
        ],
        out_specs=pl.BlockSpec((tm, d), lambda i, gs: (i, 0)),
        scratch_shapes=[
            pltpu.VMEM((2 * tm * TOP_K * MOE_SLAB, V7X_LANES), U32),
            pltpu.SemaphoreType.DMA((2,)),
        ],
    )
    e3 = top_e.reshape(steps, 1, tm * TOP_K)
    r3 = rank.reshape(steps, 1, tm * TOP_K)
    return pl.pallas_call(
        functools.partial(_combine_kernel, tm=tm),
        out_shape=jax.ShapeDtypeStruct((n, d), F32),
        grid_spec=grid_spec,
        compiler_params=_cparams(("arbitrary",)),
        name=name,
    )(gstart, e3, r3, e3, r3, h2, tw, gf.reshape(1, d), yb)


def _rope_tables(positions):
    b, s = positions.shape
    half = HEAD_DIM // 2
    inv_freq = 1.0 / (ROPE_THETA ** (jnp.arange(half, dtype=F32) / half))
    ang = positions.astype(F32)[..., None] * inv_freq
    cos = jnp.cos(ang)
    sin = jnp.sin(ang)
    cs = jnp.concatenate([cos, cos], axis=-1)
    sn = jnp.concatenate([-sin, sin], axis=-1)

    def deint(a):
        a = a.reshape(b, s // ATT_TILE, WIN_BLOCK, DIL_MAX, HEAD_DIM)
        return a.transpose(0, 1, 3, 2, 4)

    return deint(cs), deint(sn)


def _routing(counts, n_tok):
    cnt = counts.reshape(N_EXPERTS).astype(I32)
    padded = (cnt + MOE_ROW_ALIGN - 1) // MOE_ROW_ALIGN * MOE_ROW_ALIGN
    gstart = (jnp.cumsum(padded) - padded).astype(I32)
    rows_total = n_tok * TOP_K + N_EXPERTS * MOE_ROW_ALIGN

    def first_above(ends, q):
        return jnp.minimum(jnp.sum((ends[None, :] <= q[:, None]).astype(I32), axis=1), N_EXPERTS - 1)

    main_rows = jnp.minimum(padded, MOE_TMAX)
    main = (jnp.arange(N_EXPERTS, dtype=I32), (main_rows > 0).astype(I32), gstart.astype(I32),
            main_rows.astype(I32))

    over = padded - main_rows
    nt_e = (over + MOE_TMAX - 1) // MOE_TMAX
    tend = jnp.cumsum(nt_e)
    tstart = tend - nt_e
    n_over = tend[-1]
    ti = jnp.arange(rows_total // MOE_TMAX, dtype=I32)
    valid = ti < n_over
    tic = jnp.clip(ti, 0, jnp.maximum(n_over - 1, 0))
    te = first_above(tend, tic)
    local = tic - tstart[te]
    row0 = gstart[te] + (local + 1) * MOE_TMAX
    rows = jnp.where(valid, jnp.clip(over[te] - local * MOE_TMAX, 0, MOE_TMAX), 0)
    overflow = (te, valid.astype(I32), row0.astype(I32), rows.astype(I32))
    return gstart, cnt, padded, rows_total, main, overflow, n_over > 0


def kernel(x, mem, positions, norm_mix_g, w_in, lb_raw, hgrn_norm_g, w_br_hgrn, w_br_attn, w_out,
           norm_cross_g, norm_mem_g, w_cq, w_ckv, w_co, norm_moe_g, w_router, b_router,
           w_mlp1, b_mlp1, w_mlp2, b_mlp2, norm_final_g):
    bsz, seq, d = x.shape
    assert w_in.shape[0] == 1 and d == D_MODEL and seq % ATT_TILE == 0
    n_tok = bsz * seq
    lower_bounds = jnp.cumsum(jax.nn.softmax(lb_raw.astype(F32), axis=0), axis=0)
    cs, sn = _rope_tables(positions)

    wl = w_in[0]
    c_h = 4 * HG_WIDTH
    c_a = c_h + ATT_Q_WIDTH + 2 * ATT_KV_WIDTH
    tn = 1024
    x2d = x.reshape(n_tok, d)
    n_h, n_a, n_g = c_h // tn, (c_a - c_h) // tn, 2 * D_MODEL // tn
    proj = _in_proj(x2d, norm_mix_g[0], wl, tm=1024, tn=tn, name="in_proj",
                    col_block=lambda j: jnp.where(j < n_h, j, jnp.where(j < n_h + n_g, j + n_a, j - n_g)))
    proj3 = proj.reshape(bsz, seq, -1)

    o_h = _hgrn(proj3, lower_bounds[0], hgrn_norm_g[0], ts=1024)
    o_a = _dilated_attn(proj3, cs, sn, q_block=(c_h + 2 * D_MODEL) // HEAD_DIM)

    h1, n2 = _merge_out(o_h.reshape(n_tok, HG_WIDTH), o_a.reshape(n_tok, ATT_KV_WIDTH), proj, x2d,
                        w_br_hgrn[0].astype(BF16), w_br_attn[0].astype(BF16), w_out[0].astype(BF16),
                        norm_cross_g[0], tm=256, gate_block=c_h // (2 * D_MODEL))

    mem2d = mem.reshape(-1, d)
    kv = _in_proj(mem2d, norm_mem_g, w_ckv[0], tm=mem2d.shape[0], tn=2 * CROSS_WIDTH, name="mem_kv")
    wrh = w_router[0].astype(BF16)
    wrl = (w_router[0] - wrh.astype(F32)).astype(BF16)
    h2, n3, top_e, top_w, rank, counts = _cross(
        n2, h1, kv, w_cq[0].astype(BF16), w_co[0].astype(BF16), norm_moe_g[0], wrh, wrl, b_router[0],
        batch=bsz, tm=512)

    gstart, cnt, padded, rows_total, main_tiles, over_tiles, has_over = _routing(counts, n_tok)
    xs = _dispatch(gstart, cnt, padded, top_e, rank, n3, rows_total=rows_total, tm=DISPATCH_TOKENS)

    ii = jnp.arange(2 * V7X_LANES)
    src = jnp.where(ii < V7X_LANES, 2 * ii, 2 * (ii - V7X_LANES) + 1)
    perm = (ii[:, None] == src[None, :]).astype(BF16)
    b1p = b_mlp1[0].reshape(N_EXPERTS, -1, V7X_LANES, 2).transpose(0, 1, 3, 2).reshape(N_EXPERTS, 2 * D_FF)
    def run_moe(tiles, rows_buf, name):
        return _moe(*tiles, rows_buf, w_mlp1[0], b1p, w_mlp2[0], b_mlp2[0], perm, name=name)

    yb = run_moe(main_tiles, xs, "moe_mlp")
    yb = lax.cond(has_over, lambda rows_buf: run_moe(over_tiles, rows_buf, "moe_mlp_overflow"),
                  lambda rows_buf: rows_buf, yb)

    out = _combine(gstart, top_e, rank, h2, top_w, norm_final_g, yb, tm=COMBINE_TOKENS)
    return out.reshape(bsz, seq, d)
```

```python
import functools

import jax
import jax.numpy as jnp
from jax import lax
from jax.experimental import pallas as pl
from jax.experimental.pallas import tpu as pltpu

F32 = jnp.float32
BF16 = jnp.bfloat16
I32 = jnp.int32
U32 = jnp.uint32

D_MODEL = 2048
HEAD_DIM = 128
HG_HEADS = 8
HG_WIDTH = HG_HEADS * HEAD_DIM
N_KV_HEADS = 8
N_GROUPS = 3
ATT_Q_WIDTH = N_GROUPS * N_KV_HEADS * HEAD_DIM
ATT_KV_WIDTH = N_KV_HEADS * HEAD_DIM
WIN_BLOCK = 128
ROPE_THETA = 10000.0
CROSS_HEADS = 4
CROSS_WIDTH = CROSS_HEADS * HEAD_DIM
N_EXPERTS = 32
TOP_K = 4
TOP_K_LOG2 = 2
D_FF = D_MODEL
SWIGLU_ALPHA = 1.702
SWIGLU_LIMIT = 7.0
NORM_EPS = 1e-6

V7X_LANES = 128
V7X_VMEM_LIMIT_BYTES = 56 * 1024 * 1024

DIL_MAX = 16
ATT_TILE = DIL_MAX * WIN_BLOCK
ATT_UNROLL = 16
HG_BLOCK = 16
HG_GROUP = 32
NEG_BIG = -1e30

MOE_TMAX = 1536
MOE_ROW_ALIGN = 128
MOE_BLOCK_CHUNKS = 8
MOE_TF = 512
MOE_SLAB = D_MODEL // 2 // V7X_LANES
DISPATCH_TOKENS = 512
COMBINE_TOKENS = 512
DMA_WAIT_ROWS = 128
DMA_ISSUE_UNROLL = 8


def _rms(x, g):
    ms = jnp.mean(x * x, axis=-1, keepdims=True)
    return x * lax.rsqrt(ms + NORM_EPS) * g


def _cparams(sem, vmem=V7X_VMEM_LIMIT_BYTES):
    return pltpu.CompilerParams(dimension_semantics=sem, vmem_limit_bytes=vmem)


def _in_proj_kernel(x_ref, g_ref, w_ref, o_ref, xn_ref):
    @pl.when(pl.program_id(1) == 0)
    def _():
        xn_ref[...] = _rms(x_ref[...], g_ref[...]).astype(BF16)

    o_ref[...] = jnp.dot(xn_ref[...], w_ref[...].astype(BF16), preferred_element_type=F32).astype(o_ref.dtype)


def _in_proj(x2d, g, w, *, tm, tn, name, col_block=lambda j: j, ncols=None):
    n, d = x2d.shape
    wc = w.shape[1] if ncols is None else ncols
    return pl.pallas_call(
        _in_proj_kernel,
        out_shape=jax.ShapeDtypeStruct((n, wc), BF16),
        grid=(n // tm, wc // tn),
        in_specs=[
            pl.BlockSpec((tm, d), lambda i, j: (i, 0)),
            pl.BlockSpec((1, d), lambda i, j: (0, 0)),
            pl.BlockSpec((d, tn), lambda i, j: (0, col_block(j))),
        ],
        out_specs=pl.BlockSpec((tm, tn), lambda i, j: (i, j)),
        scratch_shapes=[pltpu.VMEM((tm, d), BF16)],
        compiler_params=_cparams(("parallel", "arbitrary")),
        name=name,
    )(x2d, g.reshape(1, d), w)


def _hgrn_kernel(q_ref, f_ref, i_ref, g_ref, lb_ref, gn_ref, o_ref, st_ref, kin_s, b_s, v_s, *, ts):
    @pl.when(pl.program_id(2) == 0)
    def _():
        st_ref[...] = jnp.zeros_like(st_ref)

    lb = lb_ref[...]
    oml = 1.0 - lb
    gn = gn_ref[...]
    half = HG_BLOCK // 2
    row = lax.broadcasted_iota(I32, (HG_BLOCK, HEAD_DIM), 0)
    row8 = lax.broadcasted_iota(I32, (half, HEAD_DIM), 0)
    nt_dims = (((1,), (1,)), ((), ()))
    tn_dims = (((0,), (0,)), ((), ()))

    def front(g, t0):
        sl = pl.ds(t0 + g * HG_BLOCK, HG_BLOCK)
        q = q_ref[sl, :].astype(F32)
        hf = f_ref[sl, :].astype(F32)
        v = i_ref[sl, :].astype(F32)
        kin = oml * jax.nn.sigmoid(-hf)
        b = jnp.log2(lb + oml * jax.nn.sigmoid(hf))
        for sh in (1, 2, 4, 8):
            b = b + jnp.where(row >= sh, pltpu.roll(b, sh, 0), 0.0)
        kin_s[g] = kin
        b_s[g] = b
        v_s[g] = v
        q_lo, q_hi = q[:half], q[half:]
        b_lo, b_hi = b[:half], b[half:]
        o_lo = jnp.zeros((half, HEAD_DIM), F32)
        o_hi = jnp.zeros((half, HEAD_DIM), F32)
        for s in range(HG_BLOCK):
            ks = kin_s[g, s:s + 1, :]
            bs = b_s[g, s:s + 1, :]
            vs = v_s[g, s:s + 1, :]
            if s < half:
                w = q_lo * ks * jnp.exp2(b_lo - bs)
                if s > 0:
                    w = jnp.where(row8 >= s, w, 0.0)
                o_lo = o_lo + jnp.sum(w, axis=-1, keepdims=True) * vs
                w = q_hi * ks * jnp.exp2(b_hi - bs)
            else:
                w = q_hi * ks * jnp.exp2(b_hi - bs)
                if s > half:
                    w = jnp.where(row8 >= s - half, w, 0.0)
            o_hi = o_hi + jnp.sum(w, axis=-1, keepdims=True) * vs
        bl = b_s[g, HG_BLOCK - 1:HG_BLOCK, :]
        qd = (q * jnp.exp2(b)).astype(BF16)
        kd = (kin * jnp.exp2(bl - b)).astype(BF16)
        upd = lax.dot_general(v.astype(BF16), kd, tn_dims, preferred_element_type=F32)
        return jnp.concatenate([o_lo, o_hi], axis=0), qd, upd, jnp.exp2(bl)

    def body(i, carry):
        t0 = pl.multiple_of(i * (HG_GROUP * HG_BLOCK), HG_GROUP * HG_BLOCK)
        fronts = [front(g, t0) for g in range(HG_GROUP)]
        st = st_ref[...]
        for g, (o_diag, qd, upd, dec) in enumerate(fronts):
            o = o_diag + lax.dot_general(qd, st.astype(BF16), nt_dims, preferred_element_type=F32)
            st = st * dec + upd
            sl = pl.ds(t0 + g * HG_BLOCK, HG_BLOCK)
            hg = g_ref[sl, :].astype(F32)
            o_ref[sl, :] = (_rms(o, gn) * (hg * jax.nn.sigmoid(hg))).astype(o_ref.dtype)
        st_ref[...] = st
        return carry

    lax.fori_loop(0, ts // (HG_GROUP * HG_BLOCK), body, 0)


def _hgrn(proj_h, lb, gn, *, ts, name="hgrn"):
    b, s, _ = proj_h.shape
    h = HG_HEADS

    def spec(off):
        return pl.BlockSpec((None, ts, HEAD_DIM), lambda bi, hi, si: (bi, si, off + hi))

    vec = pl.BlockSpec((1, HEAD_DIM), lambda bi, hi, si: (0, hi))
    return pl.pallas_call(
        functools.partial(_hgrn_kernel, ts=ts),
        out_shape=jax.ShapeDtypeStruct((b, s, HG_WIDTH), BF16),
        grid=(b, h, s // ts),
        in_specs=[spec(0), spec(h), spec(2 * h), spec(3 * h), vec, vec],
        out_specs=pl.BlockSpec((None, ts, HEAD_DIM), lambda bi, hi, si: (bi, si, hi)),
        scratch_shapes=[
            pltpu.VMEM((HEAD_DIM, HEAD_DIM), F32),
            pltpu.VMEM((HG_GROUP, HG_BLOCK, HEAD_DIM), F32),
            pltpu.VMEM((HG_GROUP, HG_BLOCK, HEAD_DIM), F32),
            pltpu.VMEM((HG_GROUP, HG_BLOCK, HEAD_DIM), F32),
        ],
        compiler_params=_cparams(("parallel", "parallel", "arbitrary")),
        name=name,
    )(proj_h, proj_h, proj_h, proj_h, lb.reshape(1, HG_WIDTH), gn.reshape(1, HG_WIDTH))


def _attn_bias(kind):
    rq = lax.broadcasted_iota(I32, (WIN_BLOCK, 2 * WIN_BLOCK), 0)
    ck = lax.broadcasted_iota(I32, (WIN_BLOCK, 2 * WIN_BLOCK), 1)
    if kind == 2:
        dist = rq + WIN_BLOCK - ck
        first = ck < WIN_BLOCK
    elif kind == 1:
        dist = 4 * ((rq & 31) - (ck & 63) + 32) + ((rq >> 5) - (ck >> 6))
        first = (ck & 63) < 32
    else:
        dist = 16 * ((rq & 7) - (ck & 15) + 8) + ((rq >> 3) - (ck >> 4))
        first = (ck & 15) < 8
    valid = (dist >= 0) & (dist <= WIN_BLOCK)
    return (jnp.where(valid, 0.0, NEG_BIG).astype(F32),
            jnp.where(valid & jnp.logical_not(first), 0.0, NEG_BIG).astype(F32))


def _attn_kernel(q0_ref, q1_ref, q2_ref, k_ref, v_ref, cs_ref, sn_ref, o_ref,
                 qr, kext, vext, acc, mrun, lrun, bias, onat, u0, u1, u2, u3, u4):
    ti = pl.program_id(2)
    wb = WIN_BLOCK
    scale = HEAD_DIM ** -0.5

    ubufs = (u0, u1, u2, u3, u4)
    for src, ub in zip((q0_ref, q1_ref, q2_ref, k_ref, v_ref), ubufs):
        ub[...] = pltpu.bitcast(src[...], U32)

    @pl.when(ti == 0)
    def _():
        kext[:, 0:wb, :] = jnp.zeros((DIL_MAX, wb, HEAD_DIM), F32)
        vext[:, 0:wb, :] = jnp.zeros((DIL_MAX, wb, HEAD_DIM), F32)

    @pl.when(ti > 0)
    def _():
        kext[:, 0:wb, :] = kext[:, wb:2 * wb, :]
        vext[:, 0:wb, :] = vext[:, wb:2 * wb, :]

    @pl.when((pl.program_id(0) == 0) & (pl.program_id(1) == 0) & (ti == 0))
    def _():
        for kind in range(N_GROUPS):
            full, nofirst = _attn_bias(kind)
            bias[2 * kind] = full
            bias[2 * kind + 1] = nofirst

    def rope_body(rp, carry):
        rows = pl.ds(rp, wb, stride=DIL_MAX // 2)
        parts = [_unpack_rows(ub[rows, :]) for ub in ubufs]
        for par in range(2):
            r = 2 * rp + par
            cs = cs_ref[r]
            sn = sn_ref[r]
            for g in range(N_GROUPS):
                q = parts[g][par]
                qr[g, r] = (q * cs + pltpu.roll(q, HEAD_DIM // 2, 1) * sn) * scale
            k = parts[N_GROUPS][par]
            kext[r, wb:2 * wb, :] = k * cs + pltpu.roll(k, HEAD_DIM // 2, 1) * sn
            vext[r, wb:2 * wb, :] = parts[N_GROUPS + 1][par]
            acc[r] = jnp.zeros((wb, HEAD_DIM), F32)
            mrun[r] = jnp.full((wb, HEAD_DIM), NEG_BIG, F32)
            lrun[r] = jnp.zeros((wb, HEAD_DIM), F32)
        return carry

    lax.fori_loop(0, DIL_MAX // 2, rope_body, 0)

    nt_dims = (((1,), (1,)), ((), ()))

    def block(qb, kb, vb, bias_blk):
        s = lax.dot_general(qb.astype(BF16), kb.astype(BF16), nt_dims, preferred_element_type=F32)
        s = s + bias_blk
        m = jnp.max(s, axis=-1, keepdims=True)
        p = jnp.exp(s - m)
        l = jnp.sum(p, axis=-1, keepdims=True)
        n = jnp.dot(p.astype(BF16), vb.astype(BF16), preferred_element_type=F32)
        return n, jnp.broadcast_to(m, (wb, HEAD_DIM)), jnp.broadcast_to(l, (wb, HEAD_DIM))

    def merge(r, rows, n, m, l):
        m_old = mrun[r, rows, :]
        m_new = jnp.maximum(m_old, m)
        a = jnp.exp(m_old - m_new)
        bb = jnp.exp(m - m_new)
        acc[r, rows, :] = acc[r, rows, :] * a + n * bb
        lrun[r, rows, :] = lrun[r, rows, :] * a + l * bb
        mrun[r, rows, :] = m_new

    first_tile = jnp.where(ti == 0, 1, 0)


    def g2_body(i, carry):
        rs = [i * ATT_UNROLL + u for u in range(ATT_UNROLL)]
        res = [block(qr[2, r], kext[r], vext[r], bias[4 + first_tile]) for r in rs]
        for r, (n, m, l) in zip(rs, res):
            merge(r, pl.ds(0, wb), n, m, l)
        return carry

    lax.fori_loop(0, DIL_MAX // ATT_UNROLL, g2_body, 0)

    def g1_body(i, carry):
        res = []
        for u in range(ATT_UNROLL // 4):
            mb = i * (ATT_UNROLL // 4) + u
            q0 = pl.multiple_of(32 * mb, 32)
            k0 = pl.multiple_of(96 + 32 * mb, 32)
            use_first = jnp.where(mb == 0, first_tile, 0)
            for r4 in range(4):
                qb = jnp.concatenate([qr[1, r4 + 4 * j, pl.ds(q0, 32), :] for j in range(4)], axis=0)
                kb = jnp.concatenate([kext[r4 + 4 * j, pl.ds(k0, 64), :] for j in range(4)], axis=0)
                vb = jnp.concatenate([vext[r4 + 4 * j, pl.ds(k0, 64), :] for j in range(4)], axis=0)
                res.append((r4, q0, block(qb, kb, vb, bias[2 + use_first])))
        for r4, q0, (n, m, l) in res:
            for j in range(4):
                sl = slice(32 * j, 32 * (j + 1))
                merge(r4 + 4 * j, pl.ds(q0, 32), n[sl], m[sl], l[sl])
        return carry

    lax.fori_loop(0, DIL_MAX // ATT_UNROLL, g1_body, 0)

    def g0_body(i, carry):
        res = []
        for u in range(ATT_UNROLL):
            mb = i * ATT_UNROLL + u
            q0 = pl.multiple_of(8 * mb, 8)
            k0 = pl.multiple_of(120 + 8 * mb, 8)
            qb = jnp.concatenate([qr[0, r, pl.ds(q0, 8), :] for r in range(DIL_MAX)], axis=0)
            kb = jnp.concatenate([kext[r, pl.ds(k0, 16), :] for r in range(DIL_MAX)], axis=0)
            vb = jnp.concatenate([vext[r, pl.ds(k0, 16), :] for r in range(DIL_MAX)], axis=0)
            use_first = jnp.where(mb == 0, first_tile, 0)
            res.append((q0, block(qb, kb, vb, bias[use_first])))
        for q0, (n, m, l) in res:
            for r in range(DIL_MAX):
                sl = slice(8 * r, 8 * (r + 1))
                merge(r, pl.ds(q0, 8), n[sl], m[sl], l[sl])
        return carry

    lax.fori_loop(0, DIL_MAX // ATT_UNROLL, g0_body, 0)

    for r in range(DIL_MAX):
        onat[pl.ds(r, wb, stride=DIL_MAX), :] = acc[r] / lrun[r]
    o_ref[...] = onat[...].astype(o_ref.dtype)


def _dilated_attn(proj, cs, sn, *, q_block, name="dilated_attn"):
    b, s, _ = proj.shape
    nt = s // ATT_TILE
    h = N_KV_HEADS

    def spec(off):
        return pl.BlockSpec((None, ATT_TILE, HEAD_DIM), lambda bi, hi, ti: (bi, ti, q_block + off + hi))

    tab = pl.BlockSpec((None, None, DIL_MAX, WIN_BLOCK, HEAD_DIM), lambda bi, hi, ti: (bi, ti, 0, 0, 0))
    ubuf = pltpu.VMEM((ATT_TILE // 2, HEAD_DIM), U32)
    return pl.pallas_call(
        _attn_kernel,
        out_shape=jax.ShapeDtypeStruct((b, nt * ATT_TILE, ATT_KV_WIDTH), BF16),
        grid=(b, h, nt),
        in_specs=[spec(0), spec(h), spec(2 * h), spec(3 * h), spec(4 * h), tab, tab],
        out_specs=pl.BlockSpec((None, ATT_TILE, HEAD_DIM), lambda bi, hi, ti: (bi, ti, hi)),
        scratch_shapes=[
            pltpu.VMEM((N_GROUPS, DIL_MAX, WIN_BLOCK, HEAD_DIM), F32),
            pltpu.VMEM((DIL_MAX, 2 * WIN_BLOCK, HEAD_DIM), F32),
            pltpu.VMEM((DIL_MAX, 2 * WIN_BLOCK, HEAD_DIM), F32),
            pltpu.VMEM((DIL_MAX, WIN_BLOCK, HEAD_DIM), F32),
            pltpu.VMEM((DIL_MAX, WIN_BLOCK, HEAD_DIM), F32),
            pltpu.VMEM((DIL_MAX, WIN_BLOCK, HEAD_DIM), F32),
            pltpu.VMEM((2 * N_GROUPS, WIN_BLOCK, 2 * WIN_BLOCK), F32),
            pltpu.VMEM((ATT_TILE, HEAD_DIM), F32),
            ubuf, ubuf, ubuf, ubuf, ubuf,
        ],
        compiler_params=_cparams(("arbitrary", "arbitrary", "arbitrary")),
        name=name,
    )(proj, proj, proj, proj, proj, cs, sn)


def _merge_out_kernel(oh_ref, oa_ref, gate_ref, x_ref, wh_ref, wa_ref, wo_ref, gc_ref, h1_ref, n2_ref):
    ga = gate_ref[:, :D_MODEL].astype(F32)
    gb = gate_ref[:, D_MODEL:].astype(F32)
    yh = jnp.dot(oh_ref[...], wh_ref[...], preferred_element_type=F32)
    ya = jnp.dot(oa_ref[...], wa_ref[...], preferred_element_type=F32)
    merged = jax.nn.sigmoid(ga) * yh + jax.nn.sigmoid(gb) * ya
    h1 = x_ref[...] + jnp.dot(merged.astype(BF16), wo_ref[...], preferred_element_type=F32)
    h1_ref[...] = h1
    n2_ref[...] = _rms(h1, gc_ref[...]).astype(BF16)


def _merge_out(oh, oa, gates, x2d, wh, wa, wo, gc, *, tm, gate_block, name="merge_out"):
    n, d = x2d.shape

    def const(shape):
        return pl.BlockSpec(shape, lambda i: (0, 0), pipeline_mode=pl.Buffered(1))

    return pl.pallas_call(
        _merge_out_kernel,
        out_shape=(jax.ShapeDtypeStruct((n, d), F32), jax.ShapeDtypeStruct((n, d), BF16)),
        grid=(n // tm,),
        in_specs=[
            pl.BlockSpec((tm, HG_WIDTH), lambda i: (i, 0)),
            pl.BlockSpec((tm, ATT_KV_WIDTH), lambda i: (i, 0)),
            pl.BlockSpec((tm, 2 * d), lambda i: (i, gate_block)),
            pl.BlockSpec((tm, d), lambda i: (i, 0)),
            const((HG_WIDTH, d)), const((ATT_KV_WIDTH, d)), const((d, d)), const((1, d)),
        ],
        out_specs=(pl.BlockSpec((tm, d), lambda i: (i, 0)), pl.BlockSpec((tm, d), lambda i: (i, 0))),
        compiler_params=_cparams(("parallel",)),
        name=name,
    )(oh, oa, gates, x2d, wh, wa, wo, gc.reshape(1, d))


def _cross_kernel(n2_ref, h1_ref, kv_ref, wq_ref, wo_ref, gm_ref, wrh_ref, wrl_ref, br_ref,
                  h2_ref, n3_ref, idx_ref, tw_ref, rank_ref, cnt_ref, carry_ref, *, tm):
    @pl.when((pl.program_id(0) == 0) & (pl.program_id(1) == 0))
    def _():
        carry_ref[...] = jnp.zeros_like(carry_ref)

    nt_dims = (((1,), (1,)), ((), ()))
    scale = HEAD_DIM ** -0.5
    q = (jnp.dot(n2_ref[...], wq_ref[...], preferred_element_type=F32) * scale).astype(BF16)
    outs = []
    for hh in range(CROSS_HEADS):
        sl = slice(hh * HEAD_DIM, (hh + 1) * HEAD_DIM)
        kh = kv_ref[:, sl]
        vh = kv_ref[:, CROSS_WIDTH + hh * HEAD_DIM:CROSS_WIDTH + (hh + 1) * HEAD_DIM]
        s = lax.dot_general(q[:, sl], kh, nt_dims, preferred_element_type=F32)
        p = jnp.exp(s - jnp.max(s, axis=-1, keepdims=True))
        l = jnp.sum(p, axis=-1, keepdims=True)
        outs.append(jnp.dot(p.astype(BF16), vh, preferred_element_type=F32) / l)
    o = jnp.concatenate(outs, axis=-1).astype(BF16)
    h2 = h1_ref[...] + jnp.dot(o, wo_ref[...], preferred_element_type=F32)
    h2_ref[...] = h2
    n3 = _rms(h2, gm_ref[...])
    n3_ref[...] = n3

    n3h = n3.astype(BF16)
    n3l = (n3 - n3h.astype(F32)).astype(BF16)
    wrh = wrh_ref[...]
    logits = (jnp.dot(n3h, wrh, preferred_element_type=F32)
              + jnp.dot(n3l, wrh, preferred_element_type=F32)
              + jnp.dot(n3h, wrl_ref[...], preferred_element_type=F32)
              + br_ref[...])
    lane = lax.broadcasted_iota(I32, (tm, N_EXPERTS), 1).astype(F32)
    vals, idxs, hots = [], [], []
    cur = logits
    for _ in range(TOP_K):
        mx = jnp.max(cur, axis=-1, keepdims=True)
        ix = jnp.min(jnp.where(cur == mx, lane, float(N_EXPERTS)), axis=-1, keepdims=True)
        hot = lane == ix
        vals.append(mx)
        idxs.append(ix)
        hots.append(hot)
        cur = jnp.where(hot, -jnp.inf, cur)
    es = [jnp.exp(v - vals[0]) for v in vals]
    den = es[0] + es[1] + es[2] + es[3]
    col = lax.broadcasted_iota(I32, (tm, TOP_K), 1)

    def pack(cols):
        out = jnp.broadcast_to(cols[TOP_K - 1], (tm, TOP_K))
        for k in range(TOP_K - 2, -1, -1):
            out = jnp.where(col == k, cols[k], out)
        return out

    idx_ref[...] = pack(idxs).astype(I32)
    tw_ref[...] = pack([e / den for e in es])

    cmat = (hots[0] | hots[1] | hots[2] | hots[3]).astype(F32)
    rr = lax.broadcasted_iota(I32, (tm, tm), 0)
    cc = lax.broadcasted_iota(I32, (tm, tm), 1)
    tri = (cc < rr).astype(BF16)
    before = jnp.dot(tri, cmat.astype(BF16), preferred_element_type=F32) + carry_ref[...]
    ranks = [jnp.sum(jnp.where(hot, before, 0.0), axis=-1, keepdims=True) for hot in hots]
    rank_ref[...] = pack(ranks).astype(I32)
    carry = carry_ref[...] + jnp.sum(cmat, axis=0, keepdims=True)
    carry_ref[...] = carry
    cnt_ref[...] = carry


def _cross(n2, h1, kv, wq, wo, gm, wrh, wrl, br, *, batch, tm, name="cross"):
    n, d = h1.shape
    per_b = n // batch // tm
    mem_len = kv.shape[0] // batch

    def const(shape):
        return pl.BlockSpec(shape, lambda bi, i: (0, 0), pipeline_mode=pl.Buffered(1))

    def row(bi, i):
        return (bi * per_b + i, 0)

    return pl.pallas_call(
        functools.partial(_cross_kernel, tm=tm),
        out_shape=(
            jax.ShapeDtypeStruct((n, d), F32),
            jax.ShapeDtypeStruct((n, d), F32),
            jax.ShapeDtypeStruct((n, TOP_K), I32),
            jax.ShapeDtypeStruct((n, TOP_K), F32),
            jax.ShapeDtypeStruct((n, TOP_K), I32),
            jax.ShapeDtypeStruct((1, N_EXPERTS), F32),
        ),
        grid=(batch, per_b),
        in_specs=[
            pl.BlockSpec((tm, d), row),
            pl.BlockSpec((tm, d), row),
            pl.BlockSpec((mem_len, 2 * CROSS_WIDTH), lambda bi, i: (bi, 0)),
            const((d, CROSS_WIDTH)), const((CROSS_WIDTH, d)), const((1, d)),
            const((d, N_EXPERTS)), const((d, N_EXPERTS)), const((1, N_EXPERTS)),
        ],
        out_specs=(
            pl.BlockSpec((tm, d), row),
            pl.BlockSpec((tm, d), row),
            pl.BlockSpec((tm, TOP_K), row),
            pl.BlockSpec((tm, TOP_K), row),
            pl.BlockSpec((tm, TOP_K), row),
            pl.BlockSpec((1, N_EXPERTS), lambda bi, i: (0, 0)),
        ),
        scratch_shapes=[pltpu.VMEM((1, N_EXPERTS), F32)],
        compiler_params=_cparams(("arbitrary", "arbitrary")),
        name=name,
    )(n2, h1, kv, wq, wo, gm.reshape(1, d), wrh, wrl, br.reshape(1, N_EXPERTS))


_HI_MASK = 0xFFFF0000


def _pack_rows(x):
    half = x.shape[1] // 2
    lo = lax.bitcast_convert_type(x[:, :half].astype(BF16).astype(F32), U32) >> 16
    hi = lax.bitcast_convert_type(x[:, half:].astype(BF16).astype(F32), U32) & jnp.uint32(_HI_MASK)
    return lo | hi


def _unpack_rows(u):
    lo = lax.bitcast_convert_type(u << 16, F32)
    hi = lax.bitcast_convert_type(u & jnp.uint32(_HI_MASK), F32)
    return lo, hi


def _slab(ref, row):
    return ref.at[pl.ds(pl.multiple_of(row * MOE_SLAB, MOE_SLAB), MOE_SLAB)]


def _start_group(make_copy, c):
    def pair(j, carry):
        a = c * DMA_WAIT_ROWS + 2 * j
        make_copy(a).start(priority=0)
        make_copy(a + 1).start(priority=1)
        return carry

    lax.fori_loop(0, DMA_WAIT_ROWS // 2, pair, 0, unroll=DMA_ISSUE_UNROLL // 2)


def _dispatch_kernel(gstart_ref, cnt_ref, padded_ref, e_ref, rank_ref, n3_ref, xs_hbm, stage, sems,
                     *, real_steps, tm, rows_total):
    step = pl.program_id(0)
    group_rows = DMA_WAIT_ROWS * MOE_SLAB
    ngroup = tm * TOP_K // DMA_WAIT_ROWS
    half = step & 1
    base = half * tm

    def retire(ngroups, which):
        def one(c, carry):
            pltpu.make_async_copy(stage.at[pl.ds(0, group_rows)], xs_hbm.at[pl.ds(0, group_rows)],
                                  sems.at[which]).wait()
            return carry

        lax.fori_loop(0, ngroups, one, 0)

    @pl.when(step < real_steps)
    def _():
        u = _pack_rows(n3_ref[...])
        s0 = pl.multiple_of(base * MOE_SLAB, MOE_SLAB)
        for j in range(MOE_SLAB):
            stage[pl.ds(s0 + j, tm, stride=MOE_SLAB), :] = u[:, j * V7X_LANES:(j + 1) * V7X_LANES]

        def make_copy(a):
            slot = gstart_ref[e_ref[0, a]] + rank_ref[0, a]
            return pltpu.make_async_copy(_slab(stage, base + (a >> TOP_K_LOG2)), _slab(xs_hbm, slot),
                                         sems.at[half])

        def group(c, carry):
            _start_group(make_copy, c)
            return carry

        lax.fori_loop(0, ngroup, group, 0)

    @pl.when(step > 0)
    def _():
        retire(ngroup, 1 - half)

    @pl.when(step == real_steps)
    def _():
        zero = pl.multiple_of(base * MOE_SLAB, MOE_SLAB)
        stage[pl.ds(zero, MOE_SLAB), :] = jnp.zeros((MOE_SLAB, V7X_LANES), U32)

        def fill(lo, hi):
            def one(r, carry):
                pltpu.make_async_copy(_slab(stage, base), _slab(xs_hbm, r), sems.at[half]).start()
                return carry

            lax.fori_loop(lo, hi, one, 0)

        def per_expert(e, carry):
            fill(gstart_ref[e] + cnt_ref[e], gstart_ref[e] + padded_ref[e])
            return carry

        lax.fori_loop(0, N_EXPERTS, per_expert, 0)
        fill(gstart_ref[N_EXPERTS - 1] + padded_ref[N_EXPERTS - 1], rows_total)

        retire((rows_total - real_steps * tm * TOP_K) // DMA_WAIT_ROWS, half)


def _dispatch(gstart, cnt, padded, top_e, rank, n3, *, rows_total, tm, name="moe_dispatch"):
    n, d = n3.shape
    real_steps = n // tm
    blk = pl.BlockSpec((None, 1, tm * TOP_K), lambda i, *_: (jnp.minimum(i, real_steps - 1), 0, 0),
                       memory_space=pltpu.SMEM)
    grid_spec = pltpu.PrefetchScalarGridSpec(
        num_scalar_prefetch=3,
        grid=(real_steps + 1,),
        in_specs=[blk, blk, pl.BlockSpec((tm, d), lambda i, *_: (jnp.minimum(i, real_steps - 1), 0))],
        out_specs=pl.BlockSpec(memory_space=pl.ANY),
        scratch_shapes=[pltpu.VMEM((2 * tm * MOE_SLAB, V7X_LANES), U32), pltpu.SemaphoreType.DMA((2,))],
    )
    return pl.pallas_call(
        functools.partial(_dispatch_kernel, real_steps=real_steps, tm=tm, rows_total=rows_total),
        out_shape=jax.ShapeDtypeStruct((rows_total * MOE_SLAB, V7X_LANES), U32),
        grid_spec=grid_spec,
        compiler_params=_cparams(("arbitrary",)),
        name=name,
    )(gstart, cnt, padded, top_e.reshape(real_steps, 1, tm * TOP_K), rank.reshape(real_steps, 1, tm * TOP_K), n3)


def _moe_kernel(te_ref, tv_ref, tstart_ref, trows_ref,
                xs_hbm, w1_ref, b1_ref, w2_ref, b2_ref, perm_ref, yb_hbm,
                stage, x16, acc, w1p, sems, *, nf):
    i = pl.program_id(0)
    f = pl.program_id(1)
    rows = trows_ref[i]
    start = tstart_ref[i]
    nchunk = rows // MOE_ROW_ALIGN
    chunk_slab_rows = MOE_ROW_ALIGN * MOE_SLAB
    half = 2 * V7X_LANES
    nhalf = 2 * MOE_TF // half

    def slot_rows(c):
        return pl.multiple_of((c & 1) * chunk_slab_rows, chunk_slab_rows)

    def hbm_rows(ref, c):
        r0 = pl.multiple_of((start + c * MOE_ROW_ALIGN) * MOE_SLAB, chunk_slab_rows)
        return ref.at[pl.ds(r0, chunk_slab_rows)]

    def in_copy(c):
        return pltpu.make_async_copy(hbm_rows(xs_hbm, c), stage.at[pl.ds(slot_rows(c), chunk_slab_rows)],
                                     sems.at[c & 1])

    def out_copy(c):
        return pltpu.make_async_copy(stage.at[pl.ds(slot_rows(c), chunk_slab_rows)], hbm_rows(yb_hbm, c),
                                     sems.at[c & 1])

    @pl.when((rows > 0) & (f == 0))
    def _():
        in_copy(0).start()

        def land(c, carry):
            @pl.when(c + 1 < nchunk)
            def _():
                in_copy(c + 1).start()

            in_copy(c).wait()
            r0 = pl.multiple_of(c * MOE_ROW_ALIGN, MOE_ROW_ALIGN)
            s0 = slot_rows(c)
            for j in range(MOE_SLAB):
                lo, hi = _unpack_rows(stage[pl.ds(s0 + j, MOE_ROW_ALIGN, stride=MOE_SLAB), :])
                x16[pl.ds(r0, MOE_ROW_ALIGN), j * V7X_LANES:(j + 1) * V7X_LANES] = lo.astype(BF16)
                x16[pl.ds(r0, MOE_ROW_ALIGN), (MOE_SLAB + j) * V7X_LANES:(MOE_SLAB + j + 1) * V7X_LANES] = (
                    hi.astype(BF16))
            acc[pl.ds(r0, MOE_ROW_ALIGN), :] = jnp.zeros((MOE_ROW_ALIGN, D_MODEL), F32)
            return carry

        lax.fori_loop(0, nchunk, land, 0)

    @pl.when(rows > 0)
    def _():
        perm = perm_ref[...]
        for c in range(nhalf):
            wb = w1_ref[:, c * half:(c + 1) * half].astype(BF16)
            w1p[:, c * half:(c + 1) * half] = jnp.dot(wb, perm, preferred_element_type=F32).astype(BF16)
        w2 = w2_ref[...].astype(BF16)
        b1 = b1_ref[...]

        def mlp(c0, nc):
            r0 = pl.multiple_of(c0 * MOE_ROW_ALIGN, MOE_ROW_ALIGN)
            nr = nc * MOE_ROW_ALIGN
            x = x16[pl.ds(r0, nr), :]
            h = jnp.dot(x, w1p[...], preferred_element_type=F32) + b1
            parts = []
            for c in range(nhalf):
                hg = jnp.minimum(h[:, c * half:c * half + V7X_LANES], SWIGLU_LIMIT)
                hl = jnp.clip(h[:, c * half + V7X_LANES:(c + 1) * half], -SWIGLU_LIMIT, SWIGLU_LIMIT)
                parts.append(hg * jax.nn.sigmoid(SWIGLU_ALPHA * hg) * (hl + 1.0))
            a = jnp.concatenate(parts, axis=-1).astype(BF16)
            acc[pl.ds(r0, nr), :] += jnp.dot(a, w2, preferred_element_type=F32)

        nquad = nchunk // MOE_BLOCK_CHUNKS

        def quad(c, carry):
            mlp(c * MOE_BLOCK_CHUNKS, MOE_BLOCK_CHUNKS)
            return carry

        lax.fori_loop(0, nquad, quad, 0)
        size = MOE_BLOCK_CHUNKS // 2
        while size >= 1:
            done = nchunk & ~(2 * size - 1)

            @pl.when((nchunk & size) != 0)
            def _(done=done, size=size):
                mlp(done, size)

            size //= 2

    @pl.when((rows > 0) & (f == nf - 1))
    def _():
        def emit(c, carry):
            @pl.when(c >= 2)
            def _():
                out_copy(c - 2).wait()

            r0 = pl.multiple_of(c * MOE_ROW_ALIGN, MOE_ROW_ALIGN)
            s0 = slot_rows(c)
            u = _pack_rows(acc[pl.ds(r0, MOE_ROW_ALIGN), :] + b2_ref[...])
            for j in range(MOE_SLAB):
                stage[pl.ds(s0 + j, MOE_ROW_ALIGN, stride=MOE_SLAB), :] = u[:, j * V7X_LANES:(j + 1) * V7X_LANES]
            out_copy(c).start()
            return carry

        lax.fori_loop(0, nchunk, emit, 0)

        @pl.when(nchunk >= 2)
        def _():
            out_copy(nchunk - 2).wait()

        out_copy(nchunk - 1).wait()


def _moe(tile_e, tile_v, tile_start, tile_rows, xs, w1, b1p, w2, b2, perm, *, name="moe_mlp"):
    nt = tile_e.shape[0]
    nf = D_FF // MOE_TF

    def w1_map(i, f, te, tv, ts, tr):
        return (te[i], 0, jnp.where(tv[i] > 0, f, nf - 1))

    def w2_map(i, f, te, tv, ts, tr):
        return (te[i], jnp.where(tv[i] > 0, f, nf - 1), 0)

    def b2_map(i, f, te, tv, ts, tr):
        return (te[i], 0, 0)

    grid_spec = pltpu.PrefetchScalarGridSpec(
        num_scalar_prefetch=4,
        grid=(nt, nf),
        in_specs=[
            pl.BlockSpec(memory_space=pl.ANY),
            pl.BlockSpec((None, D_MODEL, 2 * MOE_TF), w1_map),
            pl.BlockSpec((None, 1, 2 * MOE_TF), w1_map),
            pl.BlockSpec((None, MOE_TF, D_MODEL), w2_map),
            pl.BlockSpec((None, 1, D_MODEL), b2_map),
            pl.BlockSpec((2 * V7X_LANES, 2 * V7X_LANES), lambda i, f, te, tv, ts, tr: (0, 0)),
        ],
        out_specs=pl.BlockSpec(memory_space=pl.ANY),
        scratch_shapes=[
            pltpu.VMEM((2 * MOE_ROW_ALIGN * MOE_SLAB, V7X_LANES), U32),
            pltpu.VMEM((MOE_TMAX, D_MODEL), BF16),
            pltpu.VMEM((MOE_TMAX, D_MODEL), F32),
            pltpu.VMEM((D_MODEL, 2 * MOE_TF), BF16),
            pltpu.SemaphoreType.DMA((2,)),
        ],
    )
    return pl.pallas_call(
        functools.partial(_moe_kernel, nf=nf),
        out_shape=jax.ShapeDtypeStruct(xs.shape, xs.dtype),
        grid_spec=grid_spec,
        input_output_aliases={4: 0},
        compiler_params=_cparams(("arbitrary", "arbitrary")),
        name=name,
    )(tile_e, tile_v, tile_start, tile_rows, xs, w1, b1p.reshape(N_EXPERTS, 1, 2 * D_FF), w2,
      b2.reshape(N_EXPERTS, 1, D_MODEL), perm)


def _combine_kernel(gstart_ref, e_ref, rank_ref, e_next, rank_next, h2_ref, tw_ref, gf_ref, yb_hbm, o_ref,
                    buf, sems, *, tm):
    i = pl.program_id(0)
    steps = pl.num_programs(0)
    group_rows = DMA_WAIT_ROWS * MOE_SLAB
    half_slabs = tm * TOP_K
    ngroup = half_slabs // DMA_WAIT_ROWS

    def issue(eref, rref, half):
        def make_copy(a):
            t = a >> TOP_K_LOG2
            k = a & (TOP_K - 1)
            slot = gstart_ref[eref[0, a]] + rref[0, a]
            return pltpu.make_async_copy(_slab(yb_hbm, slot), _slab(buf, half * half_slabs + k * tm + t),
                                         sems.at[half])

        def group(c, carry):
            _start_group(make_copy, c)
            return carry

        lax.fori_loop(0, ngroup, group, 0)

    @pl.when(i == 0)
    def _():
        issue(e_ref, rank_ref, 0)

    @pl.when(i + 1 < steps)
    def _():
        issue(e_next, rank_next, (i + 1) & 1)

    mine = i & 1

    def retire(c, carry):
        pltpu.make_async_copy(yb_hbm.at[pl.ds(0, group_rows)], buf.at[pl.ds(0, group_rows)], sems.at[mine]).wait()
        return carry

    lax.fori_loop(0, ngroup, retire, 0)

    base = pl.multiple_of(mine * (half_slabs * MOE_SLAB), MOE_SLAB)
    tw = tw_ref[...]
    cols_lo, cols_hi = [], []
    for j in range(MOE_SLAB):
        clo = chi = None
        for k in range(TOP_K):
            lo, hi = _unpack_rows(buf[pl.ds(base + k * tm * MOE_SLAB + j, tm, stride=MOE_SLAB), :])
            w = tw[:, k:k + 1]
            clo = w * lo if clo is None else clo + w * lo
            chi = w * hi if chi is None else chi + w * hi
        cols_lo.append(clo)
        cols_hi.append(chi)
    h3 = h2_ref[...] + jnp.concatenate(cols_lo + cols_hi, axis=-1)
    o_ref[...] = _rms(h3, gf_ref[...])


def _combine(gstart, top_e, rank, h2, tw, gf, yb, *, tm, name="moe_combine"):
    n, d = h2.shape
    steps = n // tm
    blk = pl.BlockSpec((None, 1, tm * TOP_K), lambda i, gs: (i, 0, 0), memory_space=pltpu.SMEM)
    nxt = pl.BlockSpec((None, 1, tm * TOP_K), lambda i, gs: (jnp.minimum(i + 1, steps - 1), 0, 0),
                       memory_space=pltpu.SMEM)
    grid_spec = pltpu.PrefetchScalarGridSpec(
        num_scalar_prefetch=1,
        grid=(steps,),
        in_specs=[
            blk, blk, nxt, nxt,
            pl.BlockSpec((tm, d), lambda i, gs: (i, 0)),
            pl.BlockSpec((tm, TOP_K), lambda i, gs: (i, 0)),
            pl.BlockSpec((1, d), lambda i, gs: (0, 0)),
            pl.BlockSpec(memory_space=pl.ANY),
        ],
        out_specs=pl.BlockSpec((tm, d), lambda i, gs: (i, 0)),
        scratch_shapes=[
            pltpu.VMEM((2 * tm * TOP_K * MOE_SLAB, V7X_LANES), U32),
            pltpu.SemaphoreType.DMA((2,)),
        ],
    )
    e3 = top_e.reshape(steps, 1, tm * TOP_K)
    r3 = rank.reshape(steps, 1, tm * TOP_K)
    return pl.pallas_call(
        functools.partial(_combine_kernel, tm=tm),
        out_shape=jax.ShapeDtypeStruct((n, d), F32),
        grid_spec=grid_spec,
        compiler_params=_cparams(("arbitrary",)),
        name=name,
    )(gstart, e3, r3, e3, r3, h2, tw, gf.reshape(1, d), yb)


def _rope_tables(positions):
    b, s = positions.shape
    half = HEAD_DIM // 2
    inv_freq = 1.0 / (ROPE_THETA ** (jnp.arange(half, dtype=F32) / half))
    ang = positions.astype(F32)[..., None] * inv_freq
    cos = jnp.cos(ang)
    sin = jnp.sin(ang)
    cs = jnp.concatenate([cos, cos], axis=-1)
    sn = jnp.concatenate([-sin, sin], axis=-1)

    def deint(a):
        a = a.reshape(b, s // ATT_TILE, WIN_BLOCK, DIL_MAX, HEAD_DIM)
        return a.transpose(0, 1, 3, 2, 4)

    return deint(cs), deint(sn)


def _routing(counts, n_tok):
    cnt = counts.reshape(N_EXPERTS).astype(I32)
    padded = (cnt + MOE_ROW_ALIGN - 1) // MOE_ROW_ALIGN * MOE_ROW_ALIGN
    gstart = (jnp.cumsum(padded) - padded).astype(I32)
    rows_total = n_tok * TOP_K + N_EXPERTS * MOE_ROW_ALIGN

    def first_above(ends, q):
        return jnp.minimum(jnp.sum((ends[None, :] <= q[:, None]).astype(I32), axis=1), N_EXPERTS - 1)

    main_rows = jnp.minimum(padded, MOE_TMAX)
    main = (jnp.arange(N_EXPERTS, dtype=I32), (main_rows > 0).astype(I32), gstart.astype(I32),
            main_rows.astype(I32))

    over = padded - main_rows
    nt_e = (over + MOE_TMAX - 1) // MOE_TMAX
    tend = jnp.cumsum(nt_e)
    tstart = tend - nt_e
    n_over = tend[-1]
    ti = jnp.arange(rows_total // MOE_TMAX, dtype=I32)
    valid = ti < n_over
    tic = jnp.clip(ti, 0, jnp.maximum(n_over - 1, 0))
    te = first_above(tend, tic)
    local = tic - tstart[te]
    row0 = gstart[te] + (local + 1) * MOE_TMAX
    rows = jnp.where(valid, jnp.clip(over[te] - local * MOE_TMAX, 0, MOE_TMAX), 0)
    overflow = (te, valid.astype(I32), row0.astype(I32), rows.astype(I32))
    return gstart, cnt, padded, rows_total, main, overflow, n_over > 0


def kernel(x, mem, positions, norm_mix_g, w_in, lb_raw, hgrn_norm_g, w_br_hgrn, w_br_attn, w_out,
           norm_cross_g, norm_mem_g, w_cq, w_ckv, w_co, norm_moe_g, w_router, b_router,
           w_mlp1, b_mlp1, w_mlp2, b_mlp2, norm_final_g):
    bsz, seq, d = x.shape
    assert w_in.shape[0] == 1 and d == D_MODEL and seq % ATT_TILE == 0
    n_tok = bsz * seq
    lower_bounds = jnp.cumsum(jax.nn.softmax(lb_raw.astype(F32), axis=0), axis=0)
    cs, sn = _rope_tables(positions)

    wl = w_in[0]
    c_h = 4 * HG_WIDTH
    c_a = c_h + ATT_Q_WIDTH + 2 * ATT_KV_WIDTH
    tn = 1024
    x2d = x.reshape(n_tok, d)
    n_h, n_a, n_g = c_h // tn, (c_a - c_h) // tn, 2 * D_MODEL // tn
    proj = _in_proj(x2d, norm_mix_g[0], wl, tm=1024, tn=tn, name="in_proj",
                    col_block=lambda j: jnp.where(j < n_h, j, jnp.where(j < n_h + n_g, j + n_a, j - n_g)))
    proj3 = proj.reshape(bsz, seq, -1)

    o_h = _hgrn(proj3, lower_bounds[0], hgrn_norm_g[0], ts=1024)
    o_a = _dilated_attn(proj3, cs, sn, q_block=(c_h + 2 * D_MODEL) // HEAD_DIM)

    h1, n2 = _merge_out(o_h.reshape(n_tok, HG_WIDTH), o_a.reshape(n_tok, ATT_KV_WIDTH), proj, x2d,
                        w_br_hgrn[0].astype(BF16), w_br_attn[0].astype(BF16), w_out[0].astype(BF16),
                        norm_cross_g[0], tm=256, gate_block=c_h // (2 * D_MODEL))

    mem2d = mem.reshape(-1, d)
    kv = _in_proj(mem2d, norm_mem_g, w_ckv[0], tm=mem2d.shape[0], tn=2 * CROSS_WIDTH, name="mem_kv")
    wrh = w_router[0].astype(BF16)
    wrl = (w_router[0] - wrh.astype(F32)).astype(BF16)
    h2, n3, top_e, top_w, rank, counts = _cross(
        n2, h1, kv, w_cq[0].astype(BF16), w_co[0].astype(BF16), norm_moe_g[0], wrh, wrl, b_router[0],
        batch=bsz, tm=512)

    gstart, cnt, padded, rows_total, main_tiles, over_tiles, has_over = _routing(counts, n_tok)
    xs = _dispatch(gstart, cnt, padded, top_e, rank, n3, rows_total=rows_total, tm=DISPATCH_TOKENS)

    ii = jnp.arange(2 * V7X_LANES)
    src = jnp.where(ii < V7X_LANES, 2 * ii, 2 * (ii - V7X_LANES) + 1)
    perm = (ii[:, None] == src[None, :]).astype(BF16)
    b1p = b_mlp1[0].reshape(N_EXPERTS, -1, V7X_LANES, 2).transpose(0, 1, 3, 2).reshape(N_EXPERTS, 2 * D_FF)
    def run_moe(tiles, rows_buf, name):
        return _moe(*tiles, rows_buf, w_mlp1[0], b1p, w_mlp2[0], b_mlp2[0], perm, name=name)

    yb = run_moe(main_tiles, xs, "moe_mlp")
    yb = lax.cond(has_over, lambda rows_buf: run_moe(over_tiles, rows_buf, "moe_mlp_overflow"),
                  lambda rows_buf: rows_buf, yb)

    out = _combine(gstart, top_e, rank, h2, top_w, norm_final_g, yb, tm=COMBINE_TOKENS)
    return out.reshape(bsz, seq, d)
```

```python
import functools

import jax
import jax.numpy as jnp
from jax import lax
from jax.experimental import pallas as pl
from jax.experimental.pallas import tpu as pltpu
from jax.experimental.pallas import tpu_sc as plsc

F32 = jnp.float32
BF16 = jnp.bfloat16
I32 = jnp.int32
U32 = jnp.uint32

D_MODEL = 2048
HEAD_DIM = 128
HG_HEADS = 8
HG_WIDTH = HG_HEADS * HEAD_DIM
N_KV_HEADS = 8
N_GROUPS = 3
ATT_Q_WIDTH = N_GROUPS * N_KV_HEADS * HEAD_DIM
ATT_KV_WIDTH = N_KV_HEADS * HEAD_DIM
WIN_BLOCK = 128
ROPE_THETA = 10000.0
CROSS_HEADS = 4
CROSS_WIDTH = CROSS_HEADS * HEAD_DIM
N_EXPERTS = 32
TOP_K = 4
TOP_K_LOG2 = 2
D_FF = D_MODEL
SWIGLU_ALPHA = 1.702
SWIGLU_LIMIT = 7.0
NORM_EPS = 1e-6

V7X_LANES = 128
V7X_VMEM_LIMIT_BYTES = 56 * 1024 * 1024

DIL_MAX = 16
ATT_TILE = DIL_MAX * WIN_BLOCK
ATT_UNROLL = 16
HG_BLOCK = 16
HG_GROUP = 32
NEG_BIG = -1e30

MOE_TMAX = 1536
MOE_ROW_ALIGN = 128
MOE_BLOCK_CHUNKS = 8
MOE_TF = 512
MOE_SLAB = D_MODEL // 2 // V7X_LANES
SC_CORES = 2
SC_SUBCORES = 16
SC_CHUNK_TOKENS = 64
COMBINE_TOKENS = 512
DMA_WAIT_ROWS = 128
DMA_ISSUE_UNROLL = 8


def _rms(x, g):
    ms = jnp.mean(x * x, axis=-1, keepdims=True)
    return x * lax.rsqrt(ms + NORM_EPS) * g


def _cparams(sem, vmem=V7X_VMEM_LIMIT_BYTES):
    return pltpu.CompilerParams(dimension_semantics=sem, vmem_limit_bytes=vmem)


def _in_proj_kernel(x_ref, g_ref, w_ref, o_ref, xn_ref):
    @pl.when(pl.program_id(1) == 0)
    def _():
        xn_ref[...] = _rms(x_ref[...], g_ref[...]).astype(BF16)

    o_ref[...] = jnp.dot(xn_ref[...], w_ref[...].astype(BF16), preferred_element_type=F32).astype(o_ref.dtype)


def _in_proj(x2d, g, w, *, tm, tn, name, col_block=lambda j: j, ncols=None):
    n, d = x2d.shape
    wc = w.shape[1] if ncols is None else ncols
    return pl.pallas_call(
        _in_proj_kernel,
        out_shape=jax.ShapeDtypeStruct((n, wc), BF16),
        grid=(n // tm, wc // tn),
        in_specs=[
            pl.BlockSpec((tm, d), lambda i, j: (i, 0)),
            pl.BlockSpec((1, d), lambda i, j: (0, 0)),
            pl.BlockSpec((d, tn), lambda i, j: (0, col_block(j))),
        ],
        out_specs=pl.BlockSpec((tm, tn), lambda i, j: (i, j)),
        scratch_shapes=[pltpu.VMEM((tm, d), BF16)],
        compiler_params=_cparams(("parallel", "arbitrary")),
        name=name,
    )(x2d, g.reshape(1, d), w)


def _hgrn_kernel(q_ref, f_ref, i_ref, g_ref, lb_ref, gn_ref, o_ref, st_ref, kin_s, b_s, v_s, *, ts):
    @pl.when(pl.program_id(2) == 0)
    def _():
        st_ref[...] = jnp.zeros_like(st_ref)

    lb = lb_ref[...]
    oml = 1.0 - lb
    gn = gn_ref[...]
    half = HG_BLOCK // 2
    row = lax.broadcasted_iota(I32, (HG_BLOCK, HEAD_DIM), 0)
    row8 = lax.broadcasted_iota(I32, (half, HEAD_DIM), 0)
    nt_dims = (((1,), (1,)), ((), ()))
    tn_dims = (((0,), (0,)), ((), ()))

    def front(g, t0):
        sl = pl.ds(t0 + g * HG_BLOCK, HG_BLOCK)
        q = q_ref[sl, :].astype(F32)
        hf = f_ref[sl, :].astype(F32)
        v = i_ref[sl, :].astype(F32)
        kin = oml * jax.nn.sigmoid(-hf)
        b = jnp.log2(lb + oml * jax.nn.sigmoid(hf))
        for sh in (1, 2, 4, 8):
            b = b + jnp.where(row >= sh, pltpu.roll(b, sh, 0), 0.0)
        kin_s[g] = kin
        b_s[g] = b
        v_s[g] = v
        q_lo, q_hi = q[:half], q[half:]
        b_lo, b_hi = b[:half], b[half:]
        o_lo = jnp.zeros((half, HEAD_DIM), F32)
        o_hi = jnp.zeros((half, HEAD_DIM), F32)
        for s in range(HG_BLOCK):
            ks = kin_s[g, s:s + 1, :]
            bs = b_s[g, s:s + 1, :]
            vs = v_s[g, s:s + 1, :]
            if s < half:
                w = q_lo * ks * jnp.exp2(b_lo - bs)
                if s > 0:
                    w = jnp.where(row8 >= s, w, 0.0)
                o_lo = o_lo + jnp.sum(w, axis=-1, keepdims=True) * vs
                w = q_hi * ks * jnp.exp2(b_hi - bs)
            else:
                w = q_hi * ks * jnp.exp2(b_hi - bs)
                if s > half:
                    w = jnp.where(row8 >= s - half, w, 0.0)
            o_hi = o_hi + jnp.sum(w, axis=-1, keepdims=True) * vs
        bl = b_s[g, HG_BLOCK - 1:HG_BLOCK, :]
        qd = (q * jnp.exp2(b)).astype(BF16)
        kd = (kin * jnp.exp2(bl - b)).astype(BF16)
        upd = lax.dot_general(v.astype(BF16), kd, tn_dims, preferred_element_type=F32)
        return jnp.concatenate([o_lo, o_hi], axis=0), qd, upd, jnp.exp2(bl)

    def body(i, carry):
        t0 = pl.multiple_of(i * (HG_GROUP * HG_BLOCK), HG_GROUP * HG_BLOCK)
        fronts = [front(g, t0) for g in range(HG_GROUP)]
        st = st_ref[...]
        for g, (o_diag, qd, upd, dec) in enumerate(fronts):
            o = o_diag + lax.dot_general(qd, st.astype(BF16), nt_dims, preferred_element_type=F32)
            st = st * dec + upd
            sl = pl.ds(t0 + g * HG_BLOCK, HG_BLOCK)
            hg = g_ref[sl, :].astype(F32)
            o_ref[sl, :] = (_rms(o, gn) * (hg * jax.nn.sigmoid(hg))).astype(o_ref.dtype)
        st_ref[...] = st
        return carry

    lax.fori_loop(0, ts // (HG_GROUP * HG_BLOCK), body, 0)


def _hgrn(proj_h, lb, gn, *, ts, name="hgrn"):
    b, s, _ = proj_h.shape
    h = HG_HEADS

    def spec(off):
        return pl.BlockSpec((None, ts, HEAD_DIM), lambda bi, hi, si: (bi, si, off + hi))

    vec = pl.BlockSpec((1, HEAD_DIM), lambda bi, hi, si: (0, hi))
    return pl.pallas_call(
        functools.partial(_hgrn_kernel, ts=ts),
        out_shape=jax.ShapeDtypeStruct((b, s, HG_WIDTH), BF16),
        grid=(b, h, s // ts),
        in_specs=[spec(0), spec(h), spec(2 * h), spec(3 * h), vec, vec],
        out_specs=pl.BlockSpec((None, ts, HEAD_DIM), lambda bi, hi, si: (bi, si, hi)),
        scratch_shapes=[
            pltpu.VMEM((HEAD_DIM, HEAD_DIM), F32),
            pltpu.VMEM((HG_GROUP, HG_BLOCK, HEAD_DIM), F32),
            pltpu.VMEM((HG_GROUP, HG_BLOCK, HEAD_DIM), F32),
            pltpu.VMEM((HG_GROUP, HG_BLOCK, HEAD_DIM), F32),
        ],
        compiler_params=_cparams(("parallel", "parallel", "arbitrary")),
        name=name,
    )(proj_h, proj_h, proj_h, proj_h, lb.reshape(1, HG_WIDTH), gn.reshape(1, HG_WIDTH))


def _attn_bias(kind):
    rq = lax.broadcasted_iota(I32, (WIN_BLOCK, 2 * WIN_BLOCK), 0)
    ck = lax.broadcasted_iota(I32, (WIN_BLOCK, 2 * WIN_BLOCK), 1)
    if kind == 2:
        dist = rq + WIN_BLOCK - ck
        first = ck < WIN_BLOCK
    elif kind == 1:
        dist = 4 * ((rq & 31) - (ck & 63) + 32) + ((rq >> 5) - (ck >> 6))
        first = (ck & 63) < 32
    else:
        dist = 16 * ((rq & 7) - (ck & 15) + 8) + ((rq >> 3) - (ck >> 4))
        first = (ck & 15) < 8
    valid = (dist >= 0) & (dist <= WIN_BLOCK)
    return (jnp.where(valid, 0.0, NEG_BIG).astype(F32),
            jnp.where(valid & jnp.logical_not(first), 0.0, NEG_BIG).astype(F32))


def _attn_kernel(q0_ref, q1_ref, q2_ref, k_ref, v_ref, cs_ref, sn_ref, o_ref,
                 qr, kext, vext, acc, mrun, lrun, bias, onat, u0, u1, u2, u3, u4):
    ti = pl.program_id(2)
    wb = WIN_BLOCK
    scale = HEAD_DIM ** -0.5

    ubufs = (u0, u1, u2, u3, u4)
    for src, ub in zip((q0_ref, q1_ref, q2_ref, k_ref, v_ref), ubufs):
        ub[...] = pltpu.bitcast(src[...], U32)

    @pl.when(ti == 0)
    def _():
        kext[:, 0:wb, :] = jnp.zeros((DIL_MAX, wb, HEAD_DIM), F32)
        vext[:, 0:wb, :] = jnp.zeros((DIL_MAX, wb, HEAD_DIM), F32)

    @pl.when(ti > 0)
    def _():
        kext[:, 0:wb, :] = kext[:, wb:2 * wb, :]
        vext[:, 0:wb, :] = vext[:, wb:2 * wb, :]

    @pl.when((pl.program_id(0) == 0) & (pl.program_id(1) == 0) & (ti == 0))
    def _():
        for kind in range(N_GROUPS):
            full, nofirst = _attn_bias(kind)
            bias[2 * kind] = full
            bias[2 * kind + 1] = nofirst

    def rope_body(rp, carry):
        rows = pl.ds(rp, wb, stride=DIL_MAX // 2)
        parts = [_unpack_rows(ub[rows, :]) for ub in ubufs]
        for par in range(2):
            r = 2 * rp + par
            cs = cs_ref[r]
            sn = sn_ref[r]
            for g in range(N_GROUPS):
                q = parts[g][par]
                qr[g, r] = (q * cs + pltpu.roll(q, HEAD_DIM // 2, 1) * sn) * scale
            k = parts[N_GROUPS][par]
            kext[r, wb:2 * wb, :] = k * cs + pltpu.roll(k, HEAD_DIM // 2, 1) * sn
            vext[r, wb:2 * wb, :] = parts[N_GROUPS + 1][par]
            acc[r] = jnp.zeros((wb, HEAD_DIM), F32)
            mrun[r] = jnp.full((wb, HEAD_DIM), NEG_BIG, F32)
            lrun[r] = jnp.zeros((wb, HEAD_DIM), F32)
        return carry

    lax.fori_loop(0, DIL_MAX // 2, rope_body, 0)

    nt_dims = (((1,), (1,)), ((), ()))

    def block(qb, kb, vb, bias_blk):
        s = lax.dot_general(qb.astype(BF16), kb.astype(BF16), nt_dims, preferred_element_type=F32)
        s = s + bias_blk
        m = jnp.max(s, axis=-1, keepdims=True)
        p = jnp.exp(s - m)
        l = jnp.sum(p, axis=-1, keepdims=True)
        n = jnp.dot(p.astype(BF16), vb.astype(BF16), preferred_element_type=F32)
        return n, jnp.broadcast_to(m, (wb, HEAD_DIM)), jnp.broadcast_to(l, (wb, HEAD_DIM))

    def merge(r, rows, n, m, l):
        m_old = mrun[r, rows, :]
        m_new = jnp.maximum(m_old, m)
        a = jnp.exp(m_old - m_new)
        bb = jnp.exp(m - m_new)
        acc[r, rows, :] = acc[r, rows, :] * a + n * bb
        lrun[r, rows, :] = lrun[r, rows, :] * a + l * bb
        mrun[r, rows, :] = m_new

    first_tile = jnp.where(ti == 0, 1, 0)


    def g2_body(i, carry):
        rs = [i * ATT_UNROLL + u for u in range(ATT_UNROLL)]
        res = [block(qr[2, r], kext[r], vext[r], bias[4 + first_tile]) for r in rs]
        for r, (n, m, l) in zip(rs, res):
            merge(r, pl.ds(0, wb), n, m, l)
        return carry

    lax.fori_loop(0, DIL_MAX // ATT_UNROLL, g2_body, 0)

    def g1_body(i, carry):
        res = []
        for u in range(ATT_UNROLL // 4):
            mb = i * (ATT_UNROLL // 4) + u
            q0 = pl.multiple_of(32 * mb, 32)
            k0 = pl.multiple_of(96 + 32 * mb, 32)
            use_first = jnp.where(mb == 0, first_tile, 0)
            for r4 in range(4):
                qb = jnp.concatenate([qr[1, r4 + 4 * j, pl.ds(q0, 32), :] for j in range(4)], axis=0)
                kb = jnp.concatenate([kext[r4 + 4 * j, pl.ds(k0, 64), :] for j in range(4)], axis=0)
                vb = jnp.concatenate([vext[r4 + 4 * j, pl.ds(k0, 64), :] for j in range(4)], axis=0)
                res.append((r4, q0, block(qb, kb, vb, bias[2 + use_first])))
        for r4, q0, (n, m, l) in res:
            for j in range(4):
                sl = slice(32 * j, 32 * (j + 1))
                merge(r4 + 4 * j, pl.ds(q0, 32), n[sl], m[sl], l[sl])
        return carry

    lax.fori_loop(0, DIL_MAX // ATT_UNROLL, g1_body, 0)

    def g0_body(i, carry):
        res = []
        for u in range(ATT_UNROLL):
            mb = i * ATT_UNROLL + u
            q0 = pl.multiple_of(8 * mb, 8)
            k0 = pl.multiple_of(120 + 8 * mb, 8)
            qb = jnp.concatenate([qr[0, r, pl.ds(q0, 8), :] for r in range(DIL_MAX)], axis=0)
            kb = jnp.concatenate([kext[r, pl.ds(k0, 16), :] for r in range(DIL_MAX)], axis=0)
            vb = jnp.concatenate([vext[r, pl.ds(k0, 16), :] for r in range(DIL_MAX)], axis=0)
            use_first = jnp.where(mb == 0, first_tile, 0)
            res.append((q0, block(qb, kb, vb, bias[use_first])))
        for q0, (n, m, l) in res:
            for r in range(DIL_MAX):
                sl = slice(8 * r, 8 * (r + 1))
                merge(r, pl.ds(q0, 8), n[sl], m[sl], l[sl])
        return carry

    lax.fori_loop(0, DIL_MAX // ATT_UNROLL, g0_body, 0)

    for r in range(DIL_MAX):
        onat[pl.ds(r, wb, stride=DIL_MAX), :] = acc[r] / lrun[r]
    o_ref[...] = onat[...].astype(o_ref.dtype)


def _dilated_attn(proj, cs, sn, *, q_block, name="dilated_attn"):
    b, s, _ = proj.shape
    nt = s // ATT_TILE
    h = N_KV_HEADS

    def spec(off):
        return pl.BlockSpec((None, ATT_TILE, HEAD_DIM), lambda bi, hi, ti: (bi, ti, q_block + off + hi))

    tab = pl.BlockSpec((None, None, DIL_MAX, WIN_BLOCK, HEAD_DIM), lambda bi, hi, ti: (bi, ti, 0, 0, 0))
    ubuf = pltpu.VMEM((ATT_TILE // 2, HEAD_DIM), U32)
    return pl.pallas_call(
        _attn_kernel,
        out_shape=jax.ShapeDtypeStruct((b, nt * ATT_TILE, ATT_KV_WIDTH), BF16),
        grid=(b, h, nt),
        in_specs=[spec(0), spec(h), spec(2 * h), spec(3 * h), spec(4 * h), tab, tab],
        out_specs=pl.BlockSpec((None, ATT_TILE, HEAD_DIM), lambda bi, hi, ti: (bi, ti, hi)),
        scratch_shapes=[
            pltpu.VMEM((N_GROUPS, DIL_MAX, WIN_BLOCK, HEAD_DIM), F32),
            pltpu.VMEM((DIL_MAX, 2 * WIN_BLOCK, HEAD_DIM), F32),
            pltpu.VMEM((DIL_MAX, 2 * WIN_BLOCK, HEAD_DIM), F32),
            pltpu.VMEM((DIL_MAX, WIN_BLOCK, HEAD_DIM), F32),
            pltpu.VMEM((DIL_MAX, WIN_BLOCK, HEAD_DIM), F32),
            pltpu.VMEM((DIL_MAX, WIN_BLOCK, HEAD_DIM), F32),
            pltpu.VMEM((2 * N_GROUPS, WIN_BLOCK, 2 * WIN_BLOCK), F32),
            pltpu.VMEM((ATT_TILE, HEAD_DIM), F32),
            ubuf, ubuf, ubuf, ubuf, ubuf,
        ],
        compiler_params=_cparams(("arbitrary", "arbitrary", "arbitrary")),
        name=name,
    )(proj, proj, proj, proj, proj, cs, sn)


def _merge_out_kernel(oh_ref, oa_ref, gate_ref, x_ref, wh_ref, wa_ref, wo_ref, gc_ref, h1_ref, n2_ref):
    ga = gate_ref[:, :D_MODEL].astype(F32)
    gb = gate_ref[:, D_MODEL:].astype(F32)
    yh = jnp.dot(oh_ref[...], wh_ref[...], preferred_element_type=F32)
    ya = jnp.dot(oa_ref[...], wa_ref[...], preferred_element_type=F32)
    merged = jax.nn.sigmoid(ga) * yh + jax.nn.sigmoid(gb) * ya
    h1 = x_ref[...] + jnp.dot(merged.astype(BF16), wo_ref[...], preferred_element_type=F32)
    h1_ref[...] = h1
    n2_ref[...] = _rms(h1, gc_ref[...]).astype(BF16)


def _merge_out(oh, oa, gates, x2d, wh, wa, wo, gc, *, tm, gate_block, name="merge_out"):
    n, d = x2d.shape

    def const(shape):
        return pl.BlockSpec(shape, lambda i: (0, 0), pipeline_mode=pl.Buffered(1))

    return pl.pallas_call(
        _merge_out_kernel,
        out_shape=(jax.ShapeDtypeStruct((n, d), F32), jax.ShapeDtypeStruct((n, d), BF16)),
        grid=(n // tm,),
        in_specs=[
            pl.BlockSpec((tm, HG_WIDTH), lambda i: (i, 0)),
            pl.BlockSpec((tm, ATT_KV_WIDTH), lambda i: (i, 0)),
            pl.BlockSpec((tm, 2 * d), lambda i: (i, gate_block)),
            pl.BlockSpec((tm, d), lambda i: (i, 0)),
            const((HG_WIDTH, d)), const((ATT_KV_WIDTH, d)), const((d, d)), const((1, d)),
        ],
        out_specs=(pl.BlockSpec((tm, d), lambda i: (i, 0)), pl.BlockSpec((tm, d), lambda i: (i, 0))),
        compiler_params=_cparams(("parallel",)),
        name=name,
    )(oh, oa, gates, x2d, wh, wa, wo, gc.reshape(1, d))


def _cross_kernel(n2_ref, h1_ref, kv_ref, wq_ref, wo_ref, gm_ref, wrh_ref, wrl_ref, br_ref,
                  h2_ref, n3_ref, idx_ref, tw_ref, rank_ref, cnt_ref, carry_ref, *, tm):
    @pl.when((pl.program_id(0) == 0) & (pl.program_id(1) == 0))
    def _():
        carry_ref[...] = jnp.zeros_like(carry_ref)

    nt_dims = (((1,), (1,)), ((), ()))
    scale = HEAD_DIM ** -0.5
    q = (jnp.dot(n2_ref[...], wq_ref[...], preferred_element_type=F32) * scale).astype(BF16)
    outs = []
    for hh in range(CROSS_HEADS):
        sl = slice(hh * HEAD_DIM, (hh + 1) * HEAD_DIM)
        kh = kv_ref[:, sl]
        vh = kv_ref[:, CROSS_WIDTH + hh * HEAD_DIM:CROSS_WIDTH + (hh + 1) * HEAD_DIM]
        s = lax.dot_general(q[:, sl], kh, nt_dims, preferred_element_type=F32)
        p = jnp.exp(s - jnp.max(s, axis=-1, keepdims=True))
        l = jnp.sum(p, axis=-1, keepdims=True)
        outs.append(jnp.dot(p.astype(BF16), vh, preferred_element_type=F32) / l)
    o = jnp.concatenate(outs, axis=-1).astype(BF16)
    h2 = h1_ref[...] + jnp.dot(o, wo_ref[...], preferred_element_type=F32)
    h2_ref[...] = h2
    n3 = _rms(h2, gm_ref[...])
    u = _pack_rows(n3)
    for j in range(MOE_SLAB):
        n3_ref[pl.ds(j, tm, stride=MOE_SLAB), :] = u[:, j * V7X_LANES:(j + 1) * V7X_LANES]

    n3h = n3.astype(BF16)
    n3l = (n3 - n3h.astype(F32)).astype(BF16)
    wrh = wrh_ref[...]
    logits = (jnp.dot(n3h, wrh, preferred_element_type=F32)
              + jnp.dot(n3l, wrh, preferred_element_type=F32)
              + jnp.dot(n3h, wrl_ref[...], preferred_element_type=F32)
              + br_ref[...])
    lane = lax.broadcasted_iota(I32, (tm, N_EXPERTS), 1).astype(F32)
    vals, idxs, hots = [], [], []
    cur = logits
    for _ in range(TOP_K):
        mx = jnp.max(cur, axis=-1, keepdims=True)
        ix = jnp.min(jnp.where(cur == mx, lane, float(N_EXPERTS)), axis=-1, keepdims=True)
        hot = lane == ix
        vals.append(mx)
        idxs.append(ix)
        hots.append(hot)
        cur = jnp.where(hot, -jnp.inf, cur)
    es = [jnp.exp(v - vals[0]) for v in vals]
    den = es[0] + es[1] + es[2] + es[3]
    col = lax.broadcasted_iota(I32, (tm, TOP_K), 1)

    def pack(cols):
        out = jnp.broadcast_to(cols[TOP_K - 1], (tm, TOP_K))
        for k in range(TOP_K - 2, -1, -1):
            out = jnp.where(col == k, cols[k], out)
        return out

    idx_ref[...] = pack(idxs).astype(I32)
    tw_ref[...] = pack([e / den for e in es])

    cmat = (hots[0] | hots[1] | hots[2] | hots[3]).astype(F32)
    rr = lax.broadcasted_iota(I32, (tm, tm), 0)
    cc = lax.broadcasted_iota(I32, (tm, tm), 1)
    tri = (cc < rr).astype(BF16)
    before = jnp.dot(tri, cmat.astype(BF16), preferred_element_type=F32) + carry_ref[...]
    ranks = [jnp.sum(jnp.where(hot, before, 0.0), axis=-1, keepdims=True) for hot in hots]
    rank_ref[...] = pack(ranks).astype(I32)
    carry = carry_ref[...] + jnp.sum(cmat, axis=0, keepdims=True)
    carry_ref[...] = carry
    cnt_ref[...] = carry


def _cross(n2, h1, kv, wq, wo, gm, wrh, wrl, br, *, batch, tm, name="cross"):
    n, d = h1.shape
    per_b = n // batch // tm
    mem_len = kv.shape[0] // batch

    def const(shape):
        return pl.BlockSpec(shape, lambda bi, i: (0, 0), pipeline_mode=pl.Buffered(1))

    def row(bi, i):
        return (bi * per_b + i, 0)

    return pl.pallas_call(
        functools.partial(_cross_kernel, tm=tm),
        out_shape=(
            jax.ShapeDtypeStruct((n, d), F32),
            jax.ShapeDtypeStruct((n * MOE_SLAB, V7X_LANES), U32),
            jax.ShapeDtypeStruct((n, TOP_K), I32),
            jax.ShapeDtypeStruct((n, TOP_K), F32),
            jax.ShapeDtypeStruct((n, TOP_K), I32),
            jax.ShapeDtypeStruct((1, N_EXPERTS), F32),
        ),
        grid=(batch, per_b),
        in_specs=[
            pl.BlockSpec((tm, d), row),
            pl.BlockSpec((tm, d), row),
            pl.BlockSpec((mem_len, 2 * CROSS_WIDTH), lambda bi, i: (bi, 0)),
            const((d, CROSS_WIDTH)), const((CROSS_WIDTH, d)), const((1, d)),
            const((d, N_EXPERTS)), const((d, N_EXPERTS)), const((1, N_EXPERTS)),
        ],
        out_specs=(
            pl.BlockSpec((tm, d), row),
            pl.BlockSpec((tm * MOE_SLAB, V7X_LANES), row),
            pl.BlockSpec((tm, TOP_K), row),
            pl.BlockSpec((tm, TOP_K), row),
            pl.BlockSpec((tm, TOP_K), row),
            pl.BlockSpec((1, N_EXPERTS), lambda bi, i: (0, 0)),
        ),
        scratch_shapes=[pltpu.VMEM((1, N_EXPERTS), F32)],
        compiler_params=_cparams(("arbitrary", "arbitrary")),
        name=name,
    )(n2, h1, kv, wq, wo, gm.reshape(1, d), wrh, wrl, br.reshape(1, N_EXPERTS))


_HI_MASK = 0xFFFF0000


def _pack_rows(x):
    half = x.shape[1] // 2
    lo = lax.bitcast_convert_type(x[:, :half].astype(BF16).astype(F32), U32) >> 16
    hi = lax.bitcast_convert_type(x[:, half:].astype(BF16).astype(F32), U32) & jnp.uint32(_HI_MASK)
    return lo | hi


def _unpack_rows(u):
    lo = lax.bitcast_convert_type(u << 16, F32)
    hi = lax.bitcast_convert_type(u & jnp.uint32(_HI_MASK), F32)
    return lo, hi


def _slab(ref, row):
    return ref.at[pl.ds(pl.multiple_of(row * MOE_SLAB, MOE_SLAB), MOE_SLAB)]


def _start_group(make_copy, c):
    def pair(j, carry):
        a = c * DMA_WAIT_ROWS + 2 * j
        make_copy(a).start(priority=0)
        make_copy(a + 1).start(priority=1)
        return carry

    lax.fori_loop(0, DMA_WAIT_ROWS // 2, pair, 0, unroll=DMA_ISSUE_UNROLL // 2)


def _sc_dispatch(slots, n3p, *, rows_total, name="moe_dispatch_sc"):
    n = n3p.shape[0]
    per_w = n // (SC_CORES * SC_SUBCORES)
    mesh = plsc.VectorSubcoreMesh(core_axis_name="c", subcore_axis_name="s",
                                  num_cores=SC_CORES, num_subcores=SC_SUBCORES)

    def body(slots_hbm, n3p_hbm, out_hbm, idx_v, rows_v, sem):
        wid = lax.axis_index("s") * SC_CORES + lax.axis_index("c")

        def chunk(ci, carry):
            base = pl.multiple_of(wid * per_w + ci * SC_CHUNK_TOKENS, SC_CHUNK_TOKENS)
            pltpu.sync_copy(n3p_hbm.at[pl.ds(base, SC_CHUNK_TOKENS)], rows_v)
            for k in range(TOP_K):
                pltpu.sync_copy(slots_hbm.at[pl.ds(k * n + base, SC_CHUNK_TOKENS)], idx_v)
                pltpu.async_copy(rows_v, out_hbm.at[idx_v], sem).wait()
            return carry

        lax.fori_loop(0, per_w // SC_CHUNK_TOKENS, chunk, 0)

    return pl.kernel(
        body,
        out_type=jax.ShapeDtypeStruct((rows_total, MOE_SLAB, V7X_LANES), U32),
        mesh=mesh,
        scratch_types=[
            pltpu.VMEM((SC_CHUNK_TOKENS,), I32),
            pltpu.VMEM((SC_CHUNK_TOKENS, MOE_SLAB, V7X_LANES), U32),
            pltpu.SemaphoreType.DMA,
        ],
        name=name,
    )(slots, n3p)


def _fill_kernel(gstart_ref, cnt_ref, padded_ref, xs_in, xs_hbm, zero, sem, *, rows_total, n_unused):
    del xs_in
    group_rows = DMA_WAIT_ROWS * MOE_SLAB
    zero[...] = jnp.zeros((group_rows, V7X_LANES), U32)

    def fill(lo, hi):
        def one(r, carry):
            pltpu.make_async_copy(_slab(zero, 0), _slab(xs_hbm, r), sem).start()
            return carry

        lax.fori_loop(lo, hi, one, 0)

    def per_expert(e, carry):
        fill(gstart_ref[e] + cnt_ref[e], gstart_ref[e] + padded_ref[e])
        return carry

    lax.fori_loop(0, N_EXPERTS, per_expert, 0)
    fill(gstart_ref[N_EXPERTS - 1] + padded_ref[N_EXPERTS - 1], rows_total)

    def retire(c, carry):
        pltpu.make_async_copy(zero, xs_hbm.at[pl.ds(0, group_rows)], sem).wait()
        return carry

    lax.fori_loop(0, n_unused // DMA_WAIT_ROWS, retire, 0)


def _fill_unused(gstart, cnt, padded, xs, *, rows_total, n_unused, name="moe_fill"):
    grid_spec = pltpu.PrefetchScalarGridSpec(
        num_scalar_prefetch=3,
        grid=(1,),
        in_specs=[pl.BlockSpec(memory_space=pl.ANY)],
        out_specs=pl.BlockSpec(memory_space=pl.ANY),
        scratch_shapes=[pltpu.VMEM((DMA_WAIT_ROWS * MOE_SLAB, V7X_LANES), U32), pltpu.SemaphoreType.DMA],
    )
    return pl.pallas_call(
        functools.partial(_fill_kernel, rows_total=rows_total, n_unused=n_unused),
        out_shape=jax.ShapeDtypeStruct(xs.shape, xs.dtype),
        grid_spec=grid_spec,
        input_output_aliases={3: 0},
        compiler_params=_cparams(("arbitrary",)),
        name=name,
    )(gstart, cnt, padded, xs)


def _moe_kernel(te_ref, tv_ref, tstart_ref, trows_ref,
                xs_hbm, w1_ref, b1_ref, w2_ref, b2_ref, perm_ref, yb_hbm,
                stage, x16, acc, w1p, sems, *, nf):
    i = pl.program_id(0)
    f = pl.program_id(1)
    rows = trows_ref[i]
    start = tstart_ref[i]
    nchunk = rows // MOE_ROW_ALIGN
    chunk_slab_rows = MOE_ROW_ALIGN * MOE_SLAB
    half = 2 * V7X_LANES
    nhalf = 2 * MOE_TF // half

    def slot_rows(c):
        return pl.multiple_of((c & 1) * chunk_slab_rows, chunk_slab_rows)

    def hbm_rows(ref, c):
        r0 = pl.multiple_of((start + c * MOE_ROW_ALIGN) * MOE_SLAB, chunk_slab_rows)
        return ref.at[pl.ds(r0, chunk_slab_rows)]

    def in_copy(c):
        return pltpu.make_async_copy(hbm_rows(xs_hbm, c), stage.at[pl.ds(slot_rows(c), chunk_slab_rows)],
                                     sems.at[c & 1])

    def out_copy(c):
        return pltpu.make_async_copy(stage.at[pl.ds(slot_rows(c), chunk_slab_rows)], hbm_rows(yb_hbm, c),
                                     sems.at[c & 1])

    @pl.when((rows > 0) & (f == 0))
    def _():
        in_copy(0).start()

        def land(c, carry):
            @pl.when(c + 1 < nchunk)
            def _():
                in_copy(c + 1).start()

            in_copy(c).wait()
            r0 = pl.multiple_of(c * MOE_ROW_ALIGN, MOE_ROW_ALIGN)
            s0 = slot_rows(c)
            for j in range(MOE_SLAB):
                lo, hi = _unpack_rows(stage[pl.ds(s0 + j, MOE_ROW_ALIGN, stride=MOE_SLAB), :])
                x16[pl.ds(r0, MOE_ROW_ALIGN), j * V7X_LANES:(j + 1) * V7X_LANES] = lo.astype(BF16)
                x16[pl.ds(r0, MOE_ROW_ALIGN), (MOE_SLAB + j) * V7X_LANES:(MOE_SLAB + j + 1) * V7X_LANES] = (
                    hi.astype(BF16))
            acc[pl.ds(r0, MOE_ROW_ALIGN), :] = jnp.zeros((MOE_ROW_ALIGN, D_MODEL), F32)
            return carry

        lax.fori_loop(0, nchunk, land, 0)

    @pl.when(rows > 0)
    def _():
        perm = perm_ref[...]
        for c in range(nhalf):
            wb = w1_ref[:, c * half:(c + 1) * half].astype(BF16)
            w1p[:, c * half:(c + 1) * half] = jnp.dot(wb, perm, preferred_element_type=F32).astype(BF16)
        w2 = w2_ref[...].astype(BF16)
        b1 = b1_ref[...]

        def mlp(c0, nc):
            r0 = pl.multiple_of(c0 * MOE_ROW_ALIGN, MOE_ROW_ALIGN)
            nr = nc * MOE_ROW_ALIGN
            x = x16[pl.ds(r0, nr), :]
            h = jnp.dot(x, w1p[...], preferred_element_type=F32) + b1
            parts = []
            for c in range(nhalf):
                hg = jnp.minimum(h[:, c * half:c * half + V7X_LANES], SWIGLU_LIMIT)
                hl = jnp.clip(h[:, c * half + V7X_LANES:(c + 1) * half], -SWIGLU_LIMIT, SWIGLU_LIMIT)
                parts.append(hg * jax.nn.sigmoid(SWIGLU_ALPHA * hg) * (hl + 1.0))
            a = jnp.concatenate(parts, axis=-1).astype(BF16)
            acc[pl.ds(r0, nr), :] += jnp.dot(a, w2, preferred_element_type=F32)

        nquad = nchunk // MOE_BLOCK_CHUNKS

        def quad(c, carry):
            mlp(c * MOE_BLOCK_CHUNKS, MOE_BLOCK_CHUNKS)
            return carry

        lax.fori_loop(0, nquad, quad, 0)
        size = MOE_BLOCK_CHUNKS // 2
        while size >= 1:
            done = nchunk & ~(2 * size - 1)

            @pl.when((nchunk & size) != 0)
            def _(done=done, size=size):
                mlp(done, size)

            size //= 2

    @pl.when((rows > 0) & (f == nf - 1))
    def _():
        def emit(c, carry):
            @pl.when(c >= 2)
            def _():
                out_copy(c - 2).wait()

            r0 = pl.multiple_of(c * MOE_ROW_ALIGN, MOE_ROW_ALIGN)
            s0 = slot_rows(c)
            u = _pack_rows(acc[pl.ds(r0, MOE_ROW_ALIGN), :] + b2_ref[...])
            for j in range(MOE_SLAB):
                stage[pl.ds(s0 + j, MOE_ROW_ALIGN, stride=MOE_SLAB), :] = u[:, j * V7X_LANES:(j + 1) * V7X_LANES]
            out_copy(c).start()
            return carry

        lax.fori_loop(0, nchunk, emit, 0)

        @pl.when(nchunk >= 2)
        def _():
            out_copy(nchunk - 2).wait()

        out_copy(nchunk - 1).wait()


def _moe(tile_e, tile_v, tile_start, tile_rows, xs, w1, b1p, w2, b2, perm, *, name="moe_mlp"):
    nt = tile_e.shape[0]
    nf = D_FF // MOE_TF

    def w1_map(i, f, te, tv, ts, tr):
        return (te[i], 0, jnp.where(tv[i] > 0, f, nf - 1))

    def w2_map(i, f, te, tv, ts, tr):
        return (te[i], jnp.where(tv[i] > 0, f, nf - 1), 0)

    def b2_map(i, f, te, tv, ts, tr):
        return (te[i], 0, 0)

    grid_spec = pltpu.PrefetchScalarGridSpec(
        num_scalar_prefetch=4,
        grid=(nt, nf),
        in_specs=[
            pl.BlockSpec(memory_space=pl.ANY),
            pl.BlockSpec((None, D_MODEL, 2 * MOE_TF), w1_map),
            pl.BlockSpec((None, 1, 2 * MOE_TF), w1_map),
            pl.BlockSpec((None, MOE_TF, D_MODEL), w2_map),
            pl.BlockSpec((None, 1, D_MODEL), b2_map),
            pl.BlockSpec((2 * V7X_LANES, 2 * V7X_LANES), lambda i, f, te, tv, ts, tr: (0, 0)),
        ],
        out_specs=pl.BlockSpec(memory_space=pl.ANY),
        scratch_shapes=[
            pltpu.VMEM((2 * MOE_ROW_ALIGN * MOE_SLAB, V7X_LANES), U32),
            pltpu.VMEM((MOE_TMAX, D_MODEL), BF16),
            pltpu.VMEM((MOE_TMAX, D_MODEL), F32),
            pltpu.VMEM((D_MODEL, 2 * MOE_TF), BF16),
            pltpu.SemaphoreType.DMA((2,)),
        ],
    )
    return pl.pallas_call(
        functools.partial(_moe_kernel, nf=nf),
        out_shape=jax.ShapeDtypeStruct(xs.shape, xs.dtype),
        grid_spec=grid_spec,
        input_output_aliases={4: 0},
        compiler_params=_cparams(("arbitrary", "arbitrary")),
        name=name,
    )(tile_e, tile_v, tile_start, tile_rows, xs, w1, b1p.reshape(N_EXPERTS, 1, 2 * D_FF), w2,
      b2.reshape(N_EXPERTS, 1, D_MODEL), perm)


def _combine_kernel(gstart_ref, e_ref, rank_ref, e_next, rank_next, h2_ref, tw_ref, gf_ref, yb_hbm, o_ref,
                    buf, sems, *, tm):
    i = pl.program_id(0)
    steps = pl.num_programs(0)
    group_rows = DMA_WAIT_ROWS * MOE_SLAB
    half_slabs = tm * TOP_K
    ngroup = half_slabs // DMA_WAIT_ROWS

    def issue(eref, rref, half):
        def make_copy(a):
            t = a >> TOP_K_LOG2
            k = a & (TOP_K - 1)
            slot = gstart_ref[eref[0, a]] + rref[0, a]
            return pltpu.make_async_copy(_slab(yb_hbm, slot), _slab(buf, half * half_slabs + k * tm + t),
                                         sems.at[half])

        def group(c, carry):
            _start_group(make_copy, c)
            return carry

        lax.fori_loop(0, ngroup, group, 0)

    @pl.when(i == 0)
    def _():
        issue(e_ref, rank_ref, 0)

    @pl.when(i + 1 < steps)
    def _():
        issue(e_next, rank_next, (i + 1) & 1)

    mine = i & 1

    def retire(c, carry):
        pltpu.make_async_copy(yb_hbm.at[pl.ds(0, group_rows)], buf.at[pl.ds(0, group_rows)], sems.at[mine]).wait()
        return carry

    lax.fori_loop(0, ngroup, retire, 0)

    base = pl.multiple_of(mine * (half_slabs * MOE_SLAB), MOE_SLAB)
    tw = tw_ref[...]
    cols_lo, cols_hi = [], []
    for j in range(MOE_SLAB):
        clo = chi = None
        for k in range(TOP_K):
            lo, hi = _unpack_rows(buf[pl.ds(base + k * tm * MOE_SLAB + j, tm, stride=MOE_SLAB), :])
            w = tw[:, k:k + 1]
            clo = w * lo if clo is None else clo + w * lo
            chi = w * hi if chi is None else chi + w * hi
        cols_lo.append(clo)
        cols_hi.append(chi)
    h3 = h2_ref[...] + jnp.concatenate(cols_lo + cols_hi, axis=-1)
    o_ref[...] = _rms(h3, gf_ref[...])


def _combine(gstart, top_e, rank, h2, tw, gf, yb, *, tm, name="moe_combine"):
    n, d = h2.shape
    steps = n // tm
    blk = pl.BlockSpec((None, 1, tm * TOP_K), lambda i, gs: (i, 0, 0), memory_space=pltpu.SMEM)
    nxt = pl.BlockSpec((None, 1, tm * TOP_K), lambda i, gs: (jnp.minimum(i + 1, steps - 1), 0, 0),
                       memory_space=pltpu.SMEM)
    grid_spec = pltpu.PrefetchScalarGridSpec(
        num_scalar_prefetch=1,
        grid=(steps,),
        in_specs=[
            blk, blk, nxt, nxt,
            pl.BlockSpec((tm, d), lambda i, gs: (i, 0)),
            pl.BlockSpec((tm, TOP_K), lambda i, gs: (i, 0)),
            pl.BlockSpec((1, d), lambda i, gs: (0, 0)),
            pl.BlockSpec(memory_space=pl.ANY),
        ],
        out_specs=pl.BlockSpec((tm, d), lambda i, gs: (i, 0)),
        scratch_shapes=[
            pltpu.VMEM((2 * tm * TOP_K * MOE_SLAB, V7X_LANES), U32),
            pltpu.SemaphoreType.DMA((2,)),
        ],
    )
    e3 = top_e.reshape(steps, 1, tm * TOP_K)
    r3 = rank.reshape(steps, 1, tm * TOP_K)
    return pl.pallas_call(
        functools.partial(_combine_kernel, tm=tm),
        out_shape=jax.ShapeDtypeStruct((n, d), F32),
        grid_spec=grid_spec,
        compiler_params=_cparams(("arbitrary",)),
        name=name,
    )(gstart, e3, r3, e3, r3, h2, tw, gf.reshape(1, d), yb)


def _rope_tables(positions):
    b, s = positions.shape
    half = HEAD_DIM // 2
    inv_freq = 1.0 / (ROPE_THETA ** (jnp.arange(half, dtype=F32) / half))
    ang = positions.astype(F32)[..., None] * inv_freq
    cos = jnp.cos(ang)
    sin = jnp.sin(ang)
    cs = jnp.concatenate([cos, cos], axis=-1)
    sn = jnp.concatenate([-sin, sin], axis=-1)

    def deint(a):
        a = a.reshape(b, s // ATT_TILE, WIN_BLOCK, DIL_MAX, HEAD_DIM)
        return a.transpose(0, 1, 3, 2, 4)

    return deint(cs), deint(sn)


def _routing(counts, n_tok):
    cnt = counts.reshape(N_EXPERTS).astype(I32)
    padded = (cnt + MOE_ROW_ALIGN - 1) // MOE_ROW_ALIGN * MOE_ROW_ALIGN
    gstart = (jnp.cumsum(padded) - padded).astype(I32)
    rows_total = n_tok * TOP_K + N_EXPERTS * MOE_ROW_ALIGN

    def first_above(ends, q):
        return jnp.minimum(jnp.sum((ends[None, :] <= q[:, None]).astype(I32), axis=1), N_EXPERTS - 1)

    main_rows = jnp.minimum(padded, MOE_TMAX)
    main = (jnp.arange(N_EXPERTS, dtype=I32), (main_rows > 0).astype(I32), gstart.astype(I32),
            main_rows.astype(I32))

    over = padded - main_rows
    nt_e = (over + MOE_TMAX - 1) // MOE_TMAX
    tend = jnp.cumsum(nt_e)
    tstart = tend - nt_e
    n_over = tend[-1]
    ti = jnp.arange(rows_total // MOE_TMAX, dtype=I32)
    valid = ti < n_over
    tic = jnp.clip(ti, 0, jnp.maximum(n_over - 1, 0))
    te = first_above(tend, tic)
    local = tic - tstart[te]
    row0 = gstart[te] + (local + 1) * MOE_TMAX
    rows = jnp.where(valid, jnp.clip(over[te] - local * MOE_TMAX, 0, MOE_TMAX), 0)
    overflow = (te, valid.astype(I32), row0.astype(I32), rows.astype(I32))
    return gstart, cnt, padded, rows_total, main, overflow, n_over > 0


def kernel(x, mem, positions, norm_mix_g, w_in, lb_raw, hgrn_norm_g, w_br_hgrn, w_br_attn, w_out,
           norm_cross_g, norm_mem_g, w_cq, w_ckv, w_co, norm_moe_g, w_router, b_router,
           w_mlp1, b_mlp1, w_mlp2, b_mlp2, norm_final_g):
    bsz, seq, d = x.shape
    assert w_in.shape[0] == 1 and d == D_MODEL and seq % ATT_TILE == 0
    n_tok = bsz * seq
    lower_bounds = jnp.cumsum(jax.nn.softmax(lb_raw.astype(F32), axis=0), axis=0)
    cs, sn = _rope_tables(positions)

    wl = w_in[0]
    c_h = 4 * HG_WIDTH
    c_a = c_h + ATT_Q_WIDTH + 2 * ATT_KV_WIDTH
    tn = 1024
    x2d = x.reshape(n_tok, d)
    n_h, n_a, n_g = c_h // tn, (c_a - c_h) // tn, 2 * D_MODEL // tn
    proj = _in_proj(x2d, norm_mix_g[0], wl, tm=1024, tn=tn, name="in_proj",
                    col_block=lambda j: jnp.where(j < n_h, j, jnp.where(j < n_h + n_g, j + n_a, j - n_g)))
    proj3 = proj.reshape(bsz, seq, -1)

    o_h = _hgrn(proj3, lower_bounds[0], hgrn_norm_g[0], ts=1024)
    o_a = _dilated_attn(proj3, cs, sn, q_block=(c_h + 2 * D_MODEL) // HEAD_DIM)

    h1, n2 = _merge_out(o_h.reshape(n_tok, HG_WIDTH), o_a.reshape(n_tok, ATT_KV_WIDTH), proj, x2d,
                        w_br_hgrn[0].astype(BF16), w_br_attn[0].astype(BF16), w_out[0].astype(BF16),
                        norm_cross_g[0], tm=256, gate_block=c_h // (2 * D_MODEL))

    mem2d = mem.reshape(-1, d)
    kv = _in_proj(mem2d, norm_mem_g, w_ckv[0], tm=mem2d.shape[0], tn=2 * CROSS_WIDTH, name="mem_kv")
    wrh = w_router[0].astype(BF16)
    wrl = (w_router[0] - wrh.astype(F32)).astype(BF16)
    h2, n3, top_e, top_w, rank, counts = _cross(
        n2, h1, kv, w_cq[0].astype(BF16), w_co[0].astype(BF16), norm_moe_g[0], wrh, wrl, b_router[0],
        batch=bsz, tm=512)

    gstart, cnt, padded, rows_total, main_tiles, over_tiles, has_over = _routing(counts, n_tok)
    slots = (gstart[top_e] + rank).T.reshape(-1)
    xs = _sc_dispatch(slots, n3.reshape(n_tok, MOE_SLAB, V7X_LANES), rows_total=rows_total)
    xs = _fill_unused(gstart, cnt, padded, xs.reshape(rows_total * MOE_SLAB, V7X_LANES),
                      rows_total=rows_total, n_unused=N_EXPERTS * MOE_ROW_ALIGN)

    ii = jnp.arange(2 * V7X_LANES)
    src = jnp.where(ii < V7X_LANES, 2 * ii, 2 * (ii - V7X_LANES) + 1)
    perm = (ii[:, None] == src[None, :]).astype(BF16)
    b1p = b_mlp1[0].reshape(N_EXPERTS, -1, V7X_LANES, 2).transpose(0, 1, 3, 2).reshape(N_EXPERTS, 2 * D_FF)
    def run_moe(tiles, rows_buf, name):
        return _moe(*tiles, rows_buf, w_mlp1[0], b1p, w_mlp2[0], b_mlp2[0], perm, name=name)

    yb = run_moe(main_tiles, xs, "moe_mlp")
    yb = lax.cond(has_over, lambda rows_buf: run_moe(over_tiles, rows_buf, "moe_mlp_overflow"),
                  lambda rows_buf: rows_buf, yb)

    out = _combine(gstart, top_e, rank, h2, top_w, norm_final_g, yb, tm=COMBINE_TOKENS)
    return out.reshape(bsz, seq, d)
```

```python
import functools

import jax
import jax.numpy as jnp
from jax import lax
from jax.experimental import pallas as pl
from jax.experimental.pallas import tpu as pltpu
from jax.experimental.pallas import tpu_sc as plsc

F32 = jnp.float32
BF16 = jnp.bfloat16
I32 = jnp.int32
U32 = jnp.uint32

D_MODEL = 2048
HEAD_DIM = 128
HG_HEADS = 8
HG_WIDTH = HG_HEADS * HEAD_DIM
N_KV_HEADS = 8
N_GROUPS = 3
ATT_Q_WIDTH = N_GROUPS * N_KV_HEADS * HEAD_DIM
ATT_KV_WIDTH = N_KV_HEADS * HEAD_DIM
WIN_BLOCK = 128
ROPE_THETA = 10000.0
CROSS_HEADS = 4
CROSS_WIDTH = CROSS_HEADS * HEAD_DIM
N_EXPERTS = 32
TOP_K = 4
D_FF = D_MODEL
SWIGLU_ALPHA = 1.702
SWIGLU_LIMIT = 7.0
NORM_EPS = 1e-6

V7X_LANES = 128
V7X_VMEM_LIMIT_BYTES = 56 * 1024 * 1024

DIL_MAX = 16
ATT_TILE = DIL_MAX * WIN_BLOCK
ATT_UNROLL = 16
HG_BLOCK = 16
HG_GROUP = 32
NEG_BIG = -1e30

MOE_TMAX = 1536
MOE_ROW_ALIGN = 128
MOE_BLOCK_CHUNKS = 8
MOE_TF = 512
MOE_SLAB = D_MODEL // 2 // V7X_LANES
SC_CORES = 2
SC_SUBCORES = 16
SC_CHUNK_TOKENS = 64
COMBINE_TOKENS = 512
DMA_WAIT_ROWS = 128


def _rms(x, g):
    ms = jnp.mean(x * x, axis=-1, keepdims=True)
    return x * lax.rsqrt(ms + NORM_EPS) * g


def _cparams(sem, vmem=V7X_VMEM_LIMIT_BYTES):
    return pltpu.CompilerParams(dimension_semantics=sem, vmem_limit_bytes=vmem)


def _in_proj_kernel(x_ref, g_ref, w_ref, o_ref, xn_ref):
    @pl.when(pl.program_id(1) == 0)
    def _():
        xn_ref[...] = _rms(x_ref[...], g_ref[...]).astype(BF16)

    o_ref[...] = jnp.dot(xn_ref[...], w_ref[...].astype(BF16), preferred_element_type=F32).astype(o_ref.dtype)


def _in_proj(x2d, g, w, *, tm, tn, name, col_block=lambda j: j, ncols=None):
    n, d = x2d.shape
    wc = w.shape[1] if ncols is None else ncols
    return pl.pallas_call(
        _in_proj_kernel,
        out_shape=jax.ShapeDtypeStruct((n, wc), BF16),
        grid=(n // tm, wc // tn),
        in_specs=[
            pl.BlockSpec((tm, d), lambda i, j: (i, 0)),
            pl.BlockSpec((1, d), lambda i, j: (0, 0)),
            pl.BlockSpec((d, tn), lambda i, j: (0, col_block(j))),
        ],
        out_specs=pl.BlockSpec((tm, tn), lambda i, j: (i, j)),
        scratch_shapes=[pltpu.VMEM((tm, d), BF16)],
        compiler_params=_cparams(("parallel", "arbitrary")),
        name=name,
    )(x2d, g.reshape(1, d), w)


def _hgrn_kernel(q_ref, f_ref, i_ref, g_ref, lb_ref, gn_ref, o_ref, st_ref, kin_s, b_s, v_s, *, ts):
    @pl.when(pl.program_id(2) == 0)
    def _():
        st_ref[...] = jnp.zeros_like(st_ref)

    lb = lb_ref[...]
    oml = 1.0 - lb
    gn = gn_ref[...]
    half = HG_BLOCK // 2
    row = lax.broadcasted_iota(I32, (HG_BLOCK, HEAD_DIM), 0)
    row8 = lax.broadcasted_iota(I32, (half, HEAD_DIM), 0)
    nt_dims = (((1,), (1,)), ((), ()))
    tn_dims = (((0,), (0,)), ((), ()))

    def front(g, t0):
        sl = pl.ds(t0 + g * HG_BLOCK, HG_BLOCK)
        q = q_ref[sl, :].astype(F32)
        hf = f_ref[sl, :].astype(F32)
        v = i_ref[sl, :].astype(F32)
        kin = oml * jax.nn.sigmoid(-hf)
        b = jnp.log2(lb + oml * jax.nn.sigmoid(hf))
        for sh in (1, 2, 4, 8):
            b = b + jnp.where(row >= sh, pltpu.roll(b, sh, 0), 0.0)
        kin_s[g] = kin
        b_s[g] = b
        v_s[g] = v
        q_lo, q_hi = q[:half], q[half:]
        b_lo, b_hi = b[:half], b[half:]
        o_lo = jnp.zeros((half, HEAD_DIM), F32)
        o_hi = jnp.zeros((half, HEAD_DIM), F32)
        for s in range(HG_BLOCK):
            ks = kin_s[g, s:s + 1, :]
            bs = b_s[g, s:s + 1, :]
            vs = v_s[g, s:s + 1, :]
            if s < half:
                w = q_lo * ks * jnp.exp2(b_lo - bs)
                if s > 0:
                    w = jnp.where(row8 >= s, w, 0.0)
                o_lo = o_lo + jnp.sum(w, axis=-1, keepdims=True) * vs
                w = q_hi * ks * jnp.exp2(b_hi - bs)
            else:
                w = q_hi * ks * jnp.exp2(b_hi - bs)
                if s > half:
                    w = jnp.where(row8 >= s - half, w, 0.0)
            o_hi = o_hi + jnp.sum(w, axis=-1, keepdims=True) * vs
        bl = b_s[g, HG_BLOCK - 1:HG_BLOCK, :]
        qd = (q * jnp.exp2(b)).astype(BF16)
        kd = (kin * jnp.exp2(bl - b)).astype(BF16)
        upd = lax.dot_general(v.astype(BF16), kd, tn_dims, preferred_element_type=F32)
        return jnp.concatenate([o_lo, o_hi], axis=0), qd, upd, jnp.exp2(bl)

    def body(i, carry):
        t0 = pl.multiple_of(i * (HG_GROUP * HG_BLOCK), HG_GROUP * HG_BLOCK)
        fronts = [front(g, t0) for g in range(HG_GROUP)]
        st = st_ref[...]
        for g, (o_diag, qd, upd, dec) in enumerate(fronts):
            o = o_diag + lax.dot_general(qd, st.astype(BF16), nt_dims, preferred_element_type=F32)
            st = st * dec + upd
            sl = pl.ds(t0 + g * HG_BLOCK, HG_BLOCK)
            hg = g_ref[sl, :].astype(F32)
            o_ref[sl, :] = (_rms(o, gn) * (hg * jax.nn.sigmoid(hg))).astype(o_ref.dtype)
        st_ref[...] = st
        return carry

    lax.fori_loop(0, ts // (HG_GROUP * HG_BLOCK), body, 0)


def _hgrn(proj_h, lb, gn, *, ts, name="hgrn"):
    b, s, _ = proj_h.shape
    h = HG_HEADS

    def spec(off):
        return pl.BlockSpec((None, ts, HEAD_DIM), lambda bi, hi, si: (bi, si, off + hi))

    vec = pl.BlockSpec((1, HEAD_DIM), lambda bi, hi, si: (0, hi))
    return pl.pallas_call(
        functools.partial(_hgrn_kernel, ts=ts),
        out_shape=jax.ShapeDtypeStruct((b, s, HG_WIDTH), BF16),
        grid=(b, h, s // ts),
        in_specs=[spec(0), spec(h), spec(2 * h), spec(3 * h), vec, vec],
        out_specs=pl.BlockSpec((None, ts, HEAD_DIM), lambda bi, hi, si: (bi, si, hi)),
        scratch_shapes=[
            pltpu.VMEM((HEAD_DIM, HEAD_DIM), F32),
            pltpu.VMEM((HG_GROUP, HG_BLOCK, HEAD_DIM), F32),
            pltpu.VMEM((HG_GROUP, HG_BLOCK, HEAD_DIM), F32),
            pltpu.VMEM((HG_GROUP, HG_BLOCK, HEAD_DIM), F32),
        ],
        compiler_params=_cparams(("parallel", "parallel", "arbitrary")),
        name=name,
    )(proj_h, proj_h, proj_h, proj_h, lb.reshape(1, HG_WIDTH), gn.reshape(1, HG_WIDTH))


def _attn_bias(kind):
    rq = lax.broadcasted_iota(I32, (WIN_BLOCK, 2 * WIN_BLOCK), 0)
    ck = lax.broadcasted_iota(I32, (WIN_BLOCK, 2 * WIN_BLOCK), 1)
    if kind == 2:
        dist = rq + WIN_BLOCK - ck
        first = ck < WIN_BLOCK
    elif kind == 1:
        dist = 4 * ((rq & 31) - (ck & 63) + 32) + ((rq >> 5) - (ck >> 6))
        first = (ck & 63) < 32
    else:
        dist = 16 * ((rq & 7) - (ck & 15) + 8) + ((rq >> 3) - (ck >> 4))
        first = (ck & 15) < 8
    valid = (dist >= 0) & (dist <= WIN_BLOCK)
    return (jnp.where(valid, 0.0, NEG_BIG).astype(F32),
            jnp.where(valid & jnp.logical_not(first), 0.0, NEG_BIG).astype(F32))


def _attn_kernel(q0_ref, q1_ref, q2_ref, k_ref, v_ref, cs_ref, sn_ref, o_ref,
                 qr, kext, vext, acc, mrun, lrun, bias, onat, u0, u1, u2, u3, u4):
    ti = pl.program_id(2)
    wb = WIN_BLOCK
    scale = HEAD_DIM ** -0.5

    ubufs = (u0, u1, u2, u3, u4)
    for src, ub in zip((q0_ref, q1_ref, q2_ref, k_ref, v_ref), ubufs):
        ub[...] = pltpu.bitcast(src[...], U32)

    @pl.when(ti == 0)
    def _():
        kext[:, 0:wb, :] = jnp.zeros((DIL_MAX, wb, HEAD_DIM), F32)
        vext[:, 0:wb, :] = jnp.zeros((DIL_MAX, wb, HEAD_DIM), F32)

    @pl.when(ti > 0)
    def _():
        kext[:, 0:wb, :] = kext[:, wb:2 * wb, :]
        vext[:, 0:wb, :] = vext[:, wb:2 * wb, :]

    @pl.when((pl.program_id(0) == 0) & (pl.program_id(1) == 0) & (ti == 0))
    def _():
        for kind in range(N_GROUPS):
            full, nofirst = _attn_bias(kind)
            bias[2 * kind] = full
            bias[2 * kind + 1] = nofirst

    def rope_body(rp, carry):
        rows = pl.ds(rp, wb, stride=DIL_MAX // 2)
        parts = [_unpack_rows(ub[rows, :]) for ub in ubufs]
        for par in range(2):
            r = 2 * rp + par
            cs = cs_ref[r]
            sn = sn_ref[r]
            for g in range(N_GROUPS):
                q = parts[g][par]
                qr[g, r] = (q * cs + pltpu.roll(q, HEAD_DIM // 2, 1) * sn) * scale
            k = parts[N_GROUPS][par]
            kext[r, wb:2 * wb, :] = k * cs + pltpu.roll(k, HEAD_DIM // 2, 1) * sn
            vext[r, wb:2 * wb, :] = parts[N_GROUPS + 1][par]
            acc[r] = jnp.zeros((wb, HEAD_DIM), F32)
            mrun[r] = jnp.full((wb, HEAD_DIM), NEG_BIG, F32)
            lrun[r] = jnp.zeros((wb, HEAD_DIM), F32)
        return carry

    lax.fori_loop(0, DIL_MAX // 2, rope_body, 0)

    nt_dims = (((1,), (1,)), ((), ()))

    def block(qb, kb, vb, bias_blk):
        s = lax.dot_general(qb.astype(BF16), kb.astype(BF16), nt_dims, preferred_element_type=F32)
        s = s + bias_blk
        m = jnp.max(s, axis=-1, keepdims=True)
        p = jnp.exp(s - m)
        l = jnp.sum(p, axis=-1, keepdims=True)
        n = jnp.dot(p.astype(BF16), vb.astype(BF16), preferred_element_type=F32)
        return n, jnp.broadcast_to(m, (wb, HEAD_DIM)), jnp.broadcast_to(l, (wb, HEAD_DIM))

    def merge(r, rows, n, m, l):
        m_old = mrun[r, rows, :]
        m_new = jnp.maximum(m_old, m)
        a = jnp.exp(m_old - m_new)
        bb = jnp.exp(m - m_new)
        acc[r, rows, :] = acc[r, rows, :] * a + n * bb
        lrun[r, rows, :] = lrun[r, rows, :] * a + l * bb
        mrun[r, rows, :] = m_new

    first_tile = jnp.where(ti == 0, 1, 0)


    def g2_body(i, carry):
        rs = [i * ATT_UNROLL + u for u in range(ATT_UNROLL)]
        res = [block(qr[2, r], kext[r], vext[r], bias[4 + first_tile]) for r in rs]
        for r, (n, m, l) in zip(rs, res):
            merge(r, pl.ds(0, wb), n, m, l)
        return carry

    lax.fori_loop(0, DIL_MAX // ATT_UNROLL, g2_body, 0)

    def g1_body(i, carry):
        res = []
        for u in range(ATT_UNROLL // 4):
            mb = i * (ATT_UNROLL // 4) + u
            q0 = pl.multiple_of(32 * mb, 32)
            k0 = pl.multiple_of(96 + 32 * mb, 32)
            use_first = jnp.where(mb == 0, first_tile, 0)
            for r4 in range(4):
                qb = jnp.concatenate([qr[1, r4 + 4 * j, pl.ds(q0, 32), :] for j in range(4)], axis=0)
                kb = jnp.concatenate([kext[r4 + 4 * j, pl.ds(k0, 64), :] for j in range(4)], axis=0)
                vb = jnp.concatenate([vext[r4 + 4 * j, pl.ds(k0, 64), :] for j in range(4)], axis=0)
                res.append((r4, q0, block(qb, kb, vb, bias[2 + use_first])))
        for r4, q0, (n, m, l) in res:
            for j in range(4):
                sl = slice(32 * j, 32 * (j + 1))
                merge(r4 + 4 * j, pl.ds(q0, 32), n[sl], m[sl], l[sl])
        return carry

    lax.fori_loop(0, DIL_MAX // ATT_UNROLL, g1_body, 0)

    def g0_body(i, carry):
        res = []
        for u in range(ATT_UNROLL):
            mb = i * ATT_UNROLL + u
            q0 = pl.multiple_of(8 * mb, 8)
            k0 = pl.multiple_of(120 + 8 * mb, 8)
            qb = jnp.concatenate([qr[0, r, pl.ds(q0, 8), :] for r in range(DIL_MAX)], axis=0)
            kb = jnp.concatenate([kext[r, pl.ds(k0, 16), :] for r in range(DIL_MAX)], axis=0)
            vb = jnp.concatenate([vext[r, pl.ds(k0, 16), :] for r in range(DIL_MAX)], axis=0)
            use_first = jnp.where(mb == 0, first_tile, 0)
            res.append((q0, block(qb, kb, vb, bias[use_first])))
        for q0, (n, m, l) in res:
            for r in range(DIL_MAX):
                sl = slice(8 * r, 8 * (r + 1))
                merge(r, pl.ds(q0, 8), n[sl], m[sl], l[sl])
        return carry

    lax.fori_loop(0, DIL_MAX // ATT_UNROLL, g0_body, 0)

    for r in range(DIL_MAX):
        onat[pl.ds(r, wb, stride=DIL_MAX), :] = acc[r] / lrun[r]
    o_ref[...] = onat[...].astype(o_ref.dtype)


def _dilated_attn(proj, cs, sn, *, q_block, name="dilated_attn"):
    b, s, _ = proj.shape
    nt = s // ATT_TILE
    h = N_KV_HEADS

    def spec(off):
        return pl.BlockSpec((None, ATT_TILE, HEAD_DIM), lambda bi, hi, ti: (bi, ti, q_block + off + hi))

    tab = pl.BlockSpec((None, None, DIL_MAX, WIN_BLOCK, HEAD_DIM), lambda bi, hi, ti: (bi, ti, 0, 0, 0))
    ubuf = pltpu.VMEM((ATT_TILE // 2, HEAD_DIM), U32)
    return pl.pallas_call(
        _attn_kernel,
        out_shape=jax.ShapeDtypeStruct((b, nt * ATT_TILE, ATT_KV_WIDTH), BF16),
        grid=(b, h, nt),
        in_specs=[spec(0), spec(h), spec(2 * h), spec(3 * h), spec(4 * h), tab, tab],
        out_specs=pl.BlockSpec((None, ATT_TILE, HEAD_DIM), lambda bi, hi, ti: (bi, ti, hi)),
        scratch_shapes=[
            pltpu.VMEM((N_GROUPS, DIL_MAX, WIN_BLOCK, HEAD_DIM), F32),
            pltpu.VMEM((DIL_MAX, 2 * WIN_BLOCK, HEAD_DIM), F32),
            pltpu.VMEM((DIL_MAX, 2 * WIN_BLOCK, HEAD_DIM), F32),
            pltpu.VMEM((DIL_MAX, WIN_BLOCK, HEAD_DIM), F32),
            pltpu.VMEM((DIL_MAX, WIN_BLOCK, HEAD_DIM), F32),
            pltpu.VMEM((DIL_MAX, WIN_BLOCK, HEAD_DIM), F32),
            pltpu.VMEM((2 * N_GROUPS, WIN_BLOCK, 2 * WIN_BLOCK), F32),
            pltpu.VMEM((ATT_TILE, HEAD_DIM), F32),
            ubuf, ubuf, ubuf, ubuf, ubuf,
        ],
        compiler_params=_cparams(("arbitrary", "arbitrary", "arbitrary")),
        name=name,
    )(proj, proj, proj, proj, proj, cs, sn)


def _merge_out_kernel(oh_ref, oa_ref, gate_ref, x_ref, wh_ref, wa_ref, wo_ref, gc_ref, h1_ref, n2_ref):
    ga = gate_ref[:, :D_MODEL].astype(F32)
    gb = gate_ref[:, D_MODEL:].astype(F32)
    yh = jnp.dot(oh_ref[...], wh_ref[...], preferred_element_type=F32)
    ya = jnp.dot(oa_ref[...], wa_ref[...], preferred_element_type=F32)
    merged = jax.nn.sigmoid(ga) * yh + jax.nn.sigmoid(gb) * ya
    h1 = x_ref[...] + jnp.dot(merged.astype(BF16), wo_ref[...], preferred_element_type=F32)
    h1_ref[...] = h1
    n2_ref[...] = _rms(h1, gc_ref[...]).astype(BF16)


def _merge_out(oh, oa, gates, x2d, wh, wa, wo, gc, *, tm, gate_block, name="merge_out"):
    n, d = x2d.shape

    def const(shape):
        return pl.BlockSpec(shape, lambda i: (0, 0), pipeline_mode=pl.Buffered(1))

    return pl.pallas_call(
        _merge_out_kernel,
        out_shape=(jax.ShapeDtypeStruct((n, d), F32), jax.ShapeDtypeStruct((n, d), BF16)),
        grid=(n // tm,),
        in_specs=[
            pl.BlockSpec((tm, HG_WIDTH), lambda i: (i, 0)),
            pl.BlockSpec((tm, ATT_KV_WIDTH), lambda i: (i, 0)),
            pl.BlockSpec((tm, 2 * d), lambda i: (i, gate_block)),
            pl.BlockSpec((tm, d), lambda i: (i, 0)),
            const((HG_WIDTH, d)), const((ATT_KV_WIDTH, d)), const((d, d)), const((1, d)),
        ],
        out_specs=(pl.BlockSpec((tm, d), lambda i: (i, 0)), pl.BlockSpec((tm, d), lambda i: (i, 0))),
        compiler_params=_cparams(("parallel",)),
        name=name,
    )(oh, oa, gates, x2d, wh, wa, wo, gc.reshape(1, d))


def _cross_kernel(n2_ref, h1_ref, kv_ref, wq_ref, wo_ref, gm_ref, wrh_ref, wrl_ref, br_ref,
                  h2_ref, n3_ref, idx_ref, tw_ref, rank_ref, cnt_ref, carry_ref, *, tm):
    @pl.when((pl.program_id(0) == 0) & (pl.program_id(1) == 0))
    def _():
        carry_ref[...] = jnp.zeros_like(carry_ref)

    nt_dims = (((1,), (1,)), ((), ()))
    scale = HEAD_DIM ** -0.5
    q = (jnp.dot(n2_ref[...], wq_ref[...], preferred_element_type=F32) * scale).astype(BF16)
    outs = []
    for hh in range(CROSS_HEADS):
        sl = slice(hh * HEAD_DIM, (hh + 1) * HEAD_DIM)
        kh = kv_ref[:, sl]
        vh = kv_ref[:, CROSS_WIDTH + hh * HEAD_DIM:CROSS_WIDTH + (hh + 1) * HEAD_DIM]
        s = lax.dot_general(q[:, sl], kh, nt_dims, preferred_element_type=F32)
        p = jnp.exp(s - jnp.max(s, axis=-1, keepdims=True))
        l = jnp.sum(p, axis=-1, keepdims=True)
        outs.append(jnp.dot(p.astype(BF16), vh, preferred_element_type=F32) / l)
    o = jnp.concatenate(outs, axis=-1).astype(BF16)
    h2 = h1_ref[...] + jnp.dot(o, wo_ref[...], preferred_element_type=F32)
    h2_ref[...] = h2
    n3 = _rms(h2, gm_ref[...])
    u = _pack_rows(n3)
    for j in range(MOE_SLAB):
        n3_ref[pl.ds(j, tm, stride=MOE_SLAB), :] = u[:, j * V7X_LANES:(j + 1) * V7X_LANES]

    n3h = n3.astype(BF16)
    n3l = (n3 - n3h.astype(F32)).astype(BF16)
    wrh = wrh_ref[...]
    logits = (jnp.dot(n3h, wrh, preferred_element_type=F32)
              + jnp.dot(n3l, wrh, preferred_element_type=F32)
              + jnp.dot(n3h, wrl_ref[...], preferred_element_type=F32)
              + br_ref[...])
    lane = lax.broadcasted_iota(I32, (tm, N_EXPERTS), 1).astype(F32)
    vals, idxs, hots = [], [], []
    cur = logits
    for _ in range(TOP_K):
        mx = jnp.max(cur, axis=-1, keepdims=True)
        ix = jnp.min(jnp.where(cur == mx, lane, float(N_EXPERTS)), axis=-1, keepdims=True)
        hot = lane == ix
        vals.append(mx)
        idxs.append(ix)
        hots.append(hot)
        cur = jnp.where(hot, -jnp.inf, cur)
    es = [jnp.exp(v - vals[0]) for v in vals]
    den = es[0] + es[1] + es[2] + es[3]
    col = lax.broadcasted_iota(I32, (tm, TOP_K), 1)

    def pack(cols):
        out = jnp.broadcast_to(cols[TOP_K - 1], (tm, TOP_K))
        for k in range(TOP_K - 2, -1, -1):
            out = jnp.where(col == k, cols[k], out)
        return out

    idx_ref[...] = pack(idxs).astype(I32)
    tw_ref[...] = pack([e / den for e in es])

    cmat = (hots[0] | hots[1] | hots[2] | hots[3]).astype(F32)
    rr = lax.broadcasted_iota(I32, (tm, tm), 0)
    cc = lax.broadcasted_iota(I32, (tm, tm), 1)
    tri = (cc < rr).astype(BF16)
    before = jnp.dot(tri, cmat.astype(BF16), preferred_element_type=F32) + carry_ref[...]
    ranks = [jnp.sum(jnp.where(hot, before, 0.0), axis=-1, keepdims=True) for hot in hots]
    rank_ref[...] = pack(ranks).astype(I32)
    carry = carry_ref[...] + jnp.sum(cmat, axis=0, keepdims=True)
    carry_ref[...] = carry
    cnt_ref[...] = carry


def _cross(n2, h1, kv, wq, wo, gm, wrh, wrl, br, *, batch, tm, name="cross"):
    n, d = h1.shape
    per_b = n // batch // tm
    mem_len = kv.shape[0] // batch

    def const(shape):
        return pl.BlockSpec(shape, lambda bi, i: (0, 0), pipeline_mode=pl.Buffered(1))

    def row(bi, i):
        return (bi * per_b + i, 0)

    return pl.pallas_call(
        functools.partial(_cross_kernel, tm=tm),
        out_shape=(
            jax.ShapeDtypeStruct((n, d), F32),
            jax.ShapeDtypeStruct((n * MOE_SLAB, V7X_LANES), U32),
            jax.ShapeDtypeStruct((n, TOP_K), I32),
            jax.ShapeDtypeStruct((n, TOP_K), F32),
            jax.ShapeDtypeStruct((n, TOP_K), I32),
            jax.ShapeDtypeStruct((1, N_EXPERTS), F32),
        ),
        grid=(batch, per_b),
        in_specs=[
            pl.BlockSpec((tm, d), row),
            pl.BlockSpec((tm, d), row),
            pl.BlockSpec((mem_len, 2 * CROSS_WIDTH), lambda bi, i: (bi, 0)),
            const((d, CROSS_WIDTH)), const((CROSS_WIDTH, d)), const((1, d)),
            const((d, N_EXPERTS)), const((d, N_EXPERTS)), const((1, N_EXPERTS)),
        ],
        out_specs=(
            pl.BlockSpec((tm, d), row),
            pl.BlockSpec((tm * MOE_SLAB, V7X_LANES), row),
            pl.BlockSpec((tm, TOP_K), row),
            pl.BlockSpec((tm, TOP_K), row),
            pl.BlockSpec((tm, TOP_K), row),
            pl.BlockSpec((1, N_EXPERTS), lambda bi, i: (0, 0)),
        ),
        scratch_shapes=[pltpu.VMEM((1, N_EXPERTS), F32)],
        compiler_params=_cparams(("arbitrary", "arbitrary")),
        name=name,
    )(n2, h1, kv, wq, wo, gm.reshape(1, d), wrh, wrl, br.reshape(1, N_EXPERTS))


_HI_MASK = 0xFFFF0000


def _pack_rows(x):
    half = x.shape[1] // 2
    lo = lax.bitcast_convert_type(x[:, :half].astype(BF16).astype(F32), U32) >> 16
    hi = lax.bitcast_convert_type(x[:, half:].astype(BF16).astype(F32), U32) & jnp.uint32(_HI_MASK)
    return lo | hi


def _unpack_rows(u):
    lo = lax.bitcast_convert_type(u << 16, F32)
    hi = lax.bitcast_convert_type(u & jnp.uint32(_HI_MASK), F32)
    return lo, hi


def _slab(ref, row):
    return ref.at[pl.ds(pl.multiple_of(row * MOE_SLAB, MOE_SLAB), MOE_SLAB)]


def _sc_dispatch(slots, n3p, *, rows_total, name="moe_dispatch_sc"):
    n = n3p.shape[0]
    per_w = n // (SC_CORES * SC_SUBCORES)
    mesh = plsc.VectorSubcoreMesh(core_axis_name="c", subcore_axis_name="s",
                                  num_cores=SC_CORES, num_subcores=SC_SUBCORES)

    def body(slots_hbm, n3p_hbm, out_hbm, idx_v, rows_v, sem):
        wid = lax.axis_index("s") * SC_CORES + lax.axis_index("c")

        def chunk(ci, carry):
            base = pl.multiple_of(wid * per_w + ci * SC_CHUNK_TOKENS, SC_CHUNK_TOKENS)
            pltpu.sync_copy(n3p_hbm.at[pl.ds(base, SC_CHUNK_TOKENS)], rows_v)
            for k in range(TOP_K):
                pltpu.sync_copy(slots_hbm.at[pl.ds(k * n + base, SC_CHUNK_TOKENS)], idx_v)
                pltpu.async_copy(rows_v, out_hbm.at[idx_v], sem).wait()
            return carry

        lax.fori_loop(0, per_w // SC_CHUNK_TOKENS, chunk, 0)

    return pl.kernel(
        body,
        out_type=jax.ShapeDtypeStruct((rows_total, MOE_SLAB, V7X_LANES), U32),
        mesh=mesh,
        scratch_types=[
            pltpu.VMEM((SC_CHUNK_TOKENS,), I32),
            pltpu.VMEM((SC_CHUNK_TOKENS, MOE_SLAB, V7X_LANES), U32),
            pltpu.SemaphoreType.DMA,
        ],
        name=name,
    )(slots, n3p)


def _fill_kernel(gstart_ref, cnt_ref, padded_ref, xs_in, xs_hbm, zero, sem, *, rows_total, n_unused):
    del xs_in
    group_rows = DMA_WAIT_ROWS * MOE_SLAB
    zero[...] = jnp.zeros((group_rows, V7X_LANES), U32)

    def fill(lo, hi):
        def one(r, carry):
            pltpu.make_async_copy(_slab(zero, 0), _slab(xs_hbm, r), sem).start()
            return carry

        lax.fori_loop(lo, hi, one, 0)

    def per_expert(e, carry):
        fill(gstart_ref[e] + cnt_ref[e], gstart_ref[e] + padded_ref[e])
        return carry

    lax.fori_loop(0, N_EXPERTS, per_expert, 0)
    fill(gstart_ref[N_EXPERTS - 1] + padded_ref[N_EXPERTS - 1], rows_total)

    def retire(c, carry):
        pltpu.make_async_copy(zero, xs_hbm.at[pl.ds(0, group_rows)], sem).wait()
        return carry

    lax.fori_loop(0, n_unused // DMA_WAIT_ROWS, retire, 0)


def _fill_unused(gstart, cnt, padded, xs, *, rows_total, n_unused, name="moe_fill"):
    grid_spec = pltpu.PrefetchScalarGridSpec(
        num_scalar_prefetch=3,
        grid=(1,),
        in_specs=[pl.BlockSpec(memory_space=pl.ANY)],
        out_specs=pl.BlockSpec(memory_space=pl.ANY),
        scratch_shapes=[pltpu.VMEM((DMA_WAIT_ROWS * MOE_SLAB, V7X_LANES), U32), pltpu.SemaphoreType.DMA],
    )
    return pl.pallas_call(
        functools.partial(_fill_kernel, rows_total=rows_total, n_unused=n_unused),
        out_shape=jax.ShapeDtypeStruct(xs.shape, xs.dtype),
        grid_spec=grid_spec,
        input_output_aliases={3: 0},
        compiler_params=_cparams(("arbitrary",)),
        name=name,
    )(gstart, cnt, padded, xs)


def _moe_kernel(te_ref, tv_ref, tstart_ref, trows_ref,
                xs_hbm, w1_ref, b1_ref, w2_ref, b2_ref, perm_ref, yb_hbm,
                stage, x16, acc, w1p, sems, *, nf):
    i = pl.program_id(0)
    f = pl.program_id(1)
    rows = trows_ref[i]
    start = tstart_ref[i]
    nchunk = rows // MOE_ROW_ALIGN
    chunk_slab_rows = MOE_ROW_ALIGN * MOE_SLAB
    half = 2 * V7X_LANES
    nhalf = 2 * MOE_TF // half

    def slot_rows(c):
        return pl.multiple_of((c & 1) * chunk_slab_rows, chunk_slab_rows)

    def hbm_rows(ref, c):
        r0 = pl.multiple_of((start + c * MOE_ROW_ALIGN) * MOE_SLAB, chunk_slab_rows)
        return ref.at[pl.ds(r0, chunk_slab_rows)]

    def in_copy(c):
        return pltpu.make_async_copy(hbm_rows(xs_hbm, c), stage.at[pl.ds(slot_rows(c), chunk_slab_rows)],
                                     sems.at[c & 1])

    def out_copy(c):
        return pltpu.make_async_copy(stage.at[pl.ds(slot_rows(c), chunk_slab_rows)], hbm_rows(yb_hbm, c),
                                     sems.at[c & 1])

    @pl.when((rows > 0) & (f == 0))
    def _():
        in_copy(0).start()

        def land(c, carry):
            @pl.when(c + 1 < nchunk)
            def _():
                in_copy(c + 1).start()

            in_copy(c).wait()
            r0 = pl.multiple_of(c * MOE_ROW_ALIGN, MOE_ROW_ALIGN)
            s0 = slot_rows(c)
            for j in range(MOE_SLAB):
                lo, hi = _unpack_rows(stage[pl.ds(s0 + j, MOE_ROW_ALIGN, stride=MOE_SLAB), :])
                x16[pl.ds(r0, MOE_ROW_ALIGN), j * V7X_LANES:(j + 1) * V7X_LANES] = lo.astype(BF16)
                x16[pl.ds(r0, MOE_ROW_ALIGN), (MOE_SLAB + j) * V7X_LANES:(MOE_SLAB + j + 1) * V7X_LANES] = (
                    hi.astype(BF16))
            acc[pl.ds(r0, MOE_ROW_ALIGN), :] = jnp.zeros((MOE_ROW_ALIGN, D_MODEL), F32)
            return carry

        lax.fori_loop(0, nchunk, land, 0)

    @pl.when(rows > 0)
    def _():
        perm = perm_ref[...]
        for c in range(nhalf):
            wb = w1_ref[:, c * half:(c + 1) * half].astype(BF16)
            w1p[:, c * half:(c + 1) * half] = jnp.dot(wb, perm, preferred_element_type=F32).astype(BF16)
        w2 = w2_ref[...].astype(BF16)
        b1 = b1_ref[...]

        def mlp(c0, nc):
            r0 = pl.multiple_of(c0 * MOE_ROW_ALIGN, MOE_ROW_ALIGN)
            nr = nc * MOE_ROW_ALIGN
            x = x16[pl.ds(r0, nr), :]
            h = jnp.dot(x, w1p[...], preferred_element_type=F32) + b1
            parts = []
            for c in range(nhalf):
                hg = jnp.minimum(h[:, c * half:c * half + V7X_LANES], SWIGLU_LIMIT)
                hl = jnp.clip(h[:, c * half + V7X_LANES:(c + 1) * half], -SWIGLU_LIMIT, SWIGLU_LIMIT)
                parts.append(hg * jax.nn.sigmoid(SWIGLU_ALPHA * hg) * (hl + 1.0))
            a = jnp.concatenate(parts, axis=-1).astype(BF16)
            acc[pl.ds(r0, nr), :] += jnp.dot(a, w2, preferred_element_type=F32)

        nquad = nchunk // MOE_BLOCK_CHUNKS

        def quad(c, carry):
            mlp(c * MOE_BLOCK_CHUNKS, MOE_BLOCK_CHUNKS)
            return carry

        lax.fori_loop(0, nquad, quad, 0)
        size = MOE_BLOCK_CHUNKS // 2
        while size >= 1:
            done = nchunk & ~(2 * size - 1)

            @pl.when((nchunk & size) != 0)
            def _(done=done, size=size):
                mlp(done, size)

            size //= 2

    @pl.when((rows > 0) & (f == nf - 1))
    def _():
        def emit(c, carry):
            @pl.when(c >= 2)
            def _():
                out_copy(c - 2).wait()

            r0 = pl.multiple_of(c * MOE_ROW_ALIGN, MOE_ROW_ALIGN)
            s0 = slot_rows(c)
            u = _pack_rows(acc[pl.ds(r0, MOE_ROW_ALIGN), :] + b2_ref[...])
            for j in range(MOE_SLAB):
                stage[pl.ds(s0 + j, MOE_ROW_ALIGN, stride=MOE_SLAB), :] = u[:, j * V7X_LANES:(j + 1) * V7X_LANES]
            out_copy(c).start()
            return carry

        lax.fori_loop(0, nchunk, emit, 0)

        @pl.when(nchunk >= 2)
        def _():
            out_copy(nchunk - 2).wait()

        out_copy(nchunk - 1).wait()


def _moe(tile_e, tile_v, tile_start, tile_rows, xs, w1, b1p, w2, b2, perm, *, name="moe_mlp"):
    nt = tile_e.shape[0]
    nf = D_FF // MOE_TF

    def w1_map(i, f, te, tv, ts, tr):
        return (te[i], 0, jnp.where(tv[i] > 0, f, nf - 1))

    def w2_map(i, f, te, tv, ts, tr):
        return (te[i], jnp.where(tv[i] > 0, f, nf - 1), 0)

    def b2_map(i, f, te, tv, ts, tr):
        return (te[i], 0, 0)

    grid_spec = pltpu.PrefetchScalarGridSpec(
        num_scalar_prefetch=4,
        grid=(nt, nf),
        in_specs=[
            pl.BlockSpec(memory_space=pl.ANY),
            pl.BlockSpec((None, D_MODEL, 2 * MOE_TF), w1_map),
            pl.BlockSpec((None, 1, 2 * MOE_TF), w1_map),
            pl.BlockSpec((None, MOE_TF, D_MODEL), w2_map),
            pl.BlockSpec((None, 1, D_MODEL), b2_map),
            pl.BlockSpec((2 * V7X_LANES, 2 * V7X_LANES), lambda i, f, te, tv, ts, tr: (0, 0)),
        ],
        out_specs=pl.BlockSpec(memory_space=pl.ANY),
        scratch_shapes=[
            pltpu.VMEM((2 * MOE_ROW_ALIGN * MOE_SLAB, V7X_LANES), U32),
            pltpu.VMEM((MOE_TMAX, D_MODEL), BF16),
            pltpu.VMEM((MOE_TMAX, D_MODEL), F32),
            pltpu.VMEM((D_MODEL, 2 * MOE_TF), BF16),
            pltpu.SemaphoreType.DMA((2,)),
        ],
    )
    return pl.pallas_call(
        functools.partial(_moe_kernel, nf=nf),
        out_shape=jax.ShapeDtypeStruct(xs.shape, xs.dtype),
        grid_spec=grid_spec,
        input_output_aliases={4: 0},
        compiler_params=_cparams(("arbitrary", "arbitrary")),
        name=name,
    )(tile_e, tile_v, tile_start, tile_rows, xs, w1, b1p.reshape(N_EXPERTS, 1, 2 * D_FF), w2,
      b2.reshape(N_EXPERTS, 1, D_MODEL), perm)


def _sc_gather(slots, yb3, *, n_tok, name="moe_gather_sc"):
    per_w = n_tok // (SC_CORES * SC_SUBCORES)
    mesh = plsc.VectorSubcoreMesh(core_axis_name="c", subcore_axis_name="s",
                                  num_cores=SC_CORES, num_subcores=SC_SUBCORES)

    def body(slots_hbm, yb_hbm, out_hbm, idx_v, rows_v, sem):
        wid = lax.axis_index("s") * SC_CORES + lax.axis_index("c")

        def chunk(ci, carry):
            base = pl.multiple_of(wid * per_w + ci * SC_CHUNK_TOKENS, SC_CHUNK_TOKENS)
            for k in range(TOP_K):
                pltpu.sync_copy(slots_hbm.at[pl.ds(k * n_tok + base, SC_CHUNK_TOKENS)], idx_v)
                pltpu.async_copy(yb_hbm.at[idx_v], rows_v, sem).wait()
                pltpu.sync_copy(rows_v, out_hbm.at[pl.ds(k * n_tok + base, SC_CHUNK_TOKENS)])
            return carry

        lax.fori_loop(0, per_w // SC_CHUNK_TOKENS, chunk, 0)

    return pl.kernel(
        body,
        out_type=jax.ShapeDtypeStruct((TOP_K * n_tok, MOE_SLAB, V7X_LANES), U32),
        mesh=mesh,
        scratch_types=[
            pltpu.VMEM((SC_CHUNK_TOKENS,), I32),
            pltpu.VMEM((SC_CHUNK_TOKENS, MOE_SLAB, V7X_LANES), U32),
            pltpu.SemaphoreType.DMA,
        ],
        name=name,
    )(slots, yb3)


def _combine_kernel(yg_ref, h2_ref, tw_ref, gf_ref, o_ref, *, tm):
    tw = tw_ref[...]
    cols_lo, cols_hi = [], []
    for j in range(MOE_SLAB):
        clo = chi = None
        for k in range(TOP_K):
            lo, hi = _unpack_rows(yg_ref[k, pl.ds(j, tm, stride=MOE_SLAB), :])
            w = tw[:, k:k + 1]
            clo = w * lo if clo is None else clo + w * lo
            chi = w * hi if chi is None else chi + w * hi
        cols_lo.append(clo)
        cols_hi.append(chi)
    h3 = h2_ref[...] + jnp.concatenate(cols_lo + cols_hi, axis=-1)
    o_ref[...] = _rms(h3, gf_ref[...])


def _combine(yg, h2, tw, gf, *, tm, name="moe_combine"):
    n, d = h2.shape
    return pl.pallas_call(
        functools.partial(_combine_kernel, tm=tm),
        out_shape=jax.ShapeDtypeStruct((n, d), F32),
        grid=(n // tm,),
        in_specs=[
            pl.BlockSpec((TOP_K, tm * MOE_SLAB, V7X_LANES), lambda i: (0, i, 0)),
            pl.BlockSpec((tm, d), lambda i: (i, 0)),
            pl.BlockSpec((tm, TOP_K), lambda i: (i, 0)),
            pl.BlockSpec((1, d), lambda i: (0, 0)),
        ],
        out_specs=pl.BlockSpec((tm, d), lambda i: (i, 0)),
        compiler_params=_cparams(("parallel",)),
        name=name,
    )(yg, h2, tw, gf.reshape(1, d))


def _rope_tables(positions):
    b, s = positions.shape
    half = HEAD_DIM // 2
    inv_freq = 1.0 / (ROPE_THETA ** (jnp.arange(half, dtype=F32) / half))
    ang = positions.astype(F32)[..., None] * inv_freq
    cos = jnp.cos(ang)
    sin = jnp.sin(ang)
    cs = jnp.concatenate([cos, cos], axis=-1)
    sn = jnp.concatenate([-sin, sin], axis=-1)

    def deint(a):
        a = a.reshape(b, s // ATT_TILE, WIN_BLOCK, DIL_MAX, HEAD_DIM)
        return a.transpose(0, 1, 3, 2, 4)

    return deint(cs), deint(sn)


def _routing(counts, n_tok):
    cnt = counts.reshape(N_EXPERTS).astype(I32)
    padded = (cnt + MOE_ROW_ALIGN - 1) // MOE_ROW_ALIGN * MOE_ROW_ALIGN
    gstart = (jnp.cumsum(padded) - padded).astype(I32)
    rows_total = n_tok * TOP_K + N_EXPERTS * MOE_ROW_ALIGN

    def first_above(ends, q):
        return jnp.minimum(jnp.sum((ends[None, :] <= q[:, None]).astype(I32), axis=1), N_EXPERTS - 1)

    main_rows = jnp.minimum(padded, MOE_TMAX)
    main = (jnp.arange(N_EXPERTS, dtype=I32), (main_rows > 0).astype(I32), gstart.astype(I32),
            main_rows.astype(I32))

    over = padded - main_rows
    nt_e = (over + MOE_TMAX - 1) // MOE_TMAX
    tend = jnp.cumsum(nt_e)
    tstart = tend - nt_e
    n_over = tend[-1]
    ti = jnp.arange(rows_total // MOE_TMAX, dtype=I32)
    valid = ti < n_over
    tic = jnp.clip(ti, 0, jnp.maximum(n_over - 1, 0))
    te = first_above(tend, tic)
    local = tic - tstart[te]
    row0 = gstart[te] + (local + 1) * MOE_TMAX
    rows = jnp.where(valid, jnp.clip(over[te] - local * MOE_TMAX, 0, MOE_TMAX), 0)
    overflow = (te, valid.astype(I32), row0.astype(I32), rows.astype(I32))
    return gstart, cnt, padded, rows_total, main, overflow, n_over > 0


def kernel(x, mem, positions, norm_mix_g, w_in, lb_raw, hgrn_norm_g, w_br_hgrn, w_br_attn, w_out,
           norm_cross_g, norm_mem_g, w_cq, w_ckv, w_co, norm_moe_g, w_router, b_router,
           w_mlp1, b_mlp1, w_mlp2, b_mlp2, norm_final_g):
    bsz, seq, d = x.shape
    assert w_in.shape[0] == 1 and d == D_MODEL and seq % ATT_TILE == 0
    n_tok = bsz * seq
    lower_bounds = jnp.cumsum(jax.nn.softmax(lb_raw.astype(F32), axis=0), axis=0)
    cs, sn = _rope_tables(positions)

    wl = w_in[0]
    c_h = 4 * HG_WIDTH
    c_a = c_h + ATT_Q_WIDTH + 2 * ATT_KV_WIDTH
    tn = 1024
    x2d = x.reshape(n_tok, d)
    n_h, n_a, n_g = c_h // tn, (c_a - c_h) // tn, 2 * D_MODEL // tn
    proj = _in_proj(x2d, norm_mix_g[0], wl, tm=1024, tn=tn, name="in_proj",
                    col_block=lambda j: jnp.where(j < n_h, j, jnp.where(j < n_h + n_g, j + n_a, j - n_g)))
    proj3 = proj.reshape(bsz, seq, -1)

    o_h = _hgrn(proj3, lower_bounds[0], hgrn_norm_g[0], ts=1024)
    o_a = _dilated_attn(proj3, cs, sn, q_block=(c_h + 2 * D_MODEL) // HEAD_DIM)

    h1, n2 = _merge_out(o_h.reshape(n_tok, HG_WIDTH), o_a.reshape(n_tok, ATT_KV_WIDTH), proj, x2d,
                        w_br_hgrn[0].astype(BF16), w_br_attn[0].astype(BF16), w_out[0].astype(BF16),
                        norm_cross_g[0], tm=256, gate_block=c_h // (2 * D_MODEL))

    mem2d = mem.reshape(-1, d)
    kv = _in_proj(mem2d, norm_mem_g, w_ckv[0], tm=mem2d.shape[0], tn=2 * CROSS_WIDTH, name="mem_kv")
    wrh = w_router[0].astype(BF16)
    wrl = (w_router[0] - wrh.astype(F32)).astype(BF16)
    h2, n3, top_e, top_w, rank, counts = _cross(
        n2, h1, kv, w_cq[0].astype(BF16), w_co[0].astype(BF16), norm_moe_g[0], wrh, wrl, b_router[0],
        batch=bsz, tm=512)

    gstart, cnt, padded, rows_total, main_tiles, over_tiles, has_over = _routing(counts, n_tok)
    hot = top_e[..., None] == jnp.arange(N_EXPERTS, dtype=I32)
    slots = (rank + jnp.sum(jnp.where(hot, gstart, 0), axis=-1)).T.reshape(-1)
    xs = _sc_dispatch(slots, n3.reshape(n_tok, MOE_SLAB, V7X_LANES), rows_total=rows_total)
    xs = _fill_unused(gstart, cnt, padded, xs.reshape(rows_total * MOE_SLAB, V7X_LANES),
                      rows_total=rows_total, n_unused=N_EXPERTS * MOE_ROW_ALIGN)

    ii = jnp.arange(2 * V7X_LANES)
    src = jnp.where(ii < V7X_LANES, 2 * ii, 2 * (ii - V7X_LANES) + 1)
    perm = (ii[:, None] == src[None, :]).astype(BF16)
    b1p = b_mlp1[0].reshape(N_EXPERTS, -1, V7X_LANES, 2).transpose(0, 1, 3, 2).reshape(N_EXPERTS, 2 * D_FF)
    def run_moe(tiles, rows_buf, name):
        return _moe(*tiles, rows_buf, w_mlp1[0], b1p, w_mlp2[0], b_mlp2[0], perm, name=name)

    yb = run_moe(main_tiles, xs, "moe_mlp")
    yb = lax.cond(has_over, lambda rows_buf: run_moe(over_tiles, rows_buf, "moe_mlp_overflow"),
                  lambda rows_buf: rows_buf, yb)

    yg = _sc_gather(slots, yb.reshape(rows_total, MOE_SLAB, V7X_LANES), n_tok=n_tok)
    out = _combine(yg.reshape(TOP_K, n_tok * MOE_SLAB, V7X_LANES), h2, top_w, norm_final_g, tm=COMBINE_TOKENS)
    return out.reshape(bsz, seq, d)
```

```python
import functools

import jax
import jax.numpy as jnp
from jax import lax
from jax.experimental import pallas as pl
from jax.experimental.pallas import tpu as pltpu
from jax.experimental.pallas import tpu_sc as plsc

F32 = jnp.float32
BF16 = jnp.bfloat16
I32 = jnp.int32
U32 = jnp.uint32

D_MODEL = 2048
HEAD_DIM = 128
HG_HEADS = 8
HG_WIDTH = HG_HEADS * HEAD_DIM
N_KV_HEADS = 8
N_GROUPS = 3
ATT_Q_WIDTH = N_GROUPS * N_KV_HEADS * HEAD_DIM
ATT_KV_WIDTH = N_KV_HEADS * HEAD_DIM
WIN_BLOCK = 128
ROPE_THETA = 10000.0
CROSS_HEADS = 4
CROSS_WIDTH = CROSS_HEADS * HEAD_DIM
N_EXPERTS = 32
TOP_K = 4
D_FF = D_MODEL
SWIGLU_ALPHA = 1.702
SWIGLU_LIMIT = 7.0
NORM_EPS = 1e-6

V7X_LANES = 128
V7X_VMEM_LIMIT_BYTES = 56 * 1024 * 1024

DIL_MAX = 16
ATT_TILE = DIL_MAX * WIN_BLOCK
ATT_UNROLL = 16
HG_BLOCK = 16
HG_GROUP = 32
NEG_BIG = -1e30

MOE_TMAX = 1536
MOE_ROW_ALIGN = 128
MOE_BLOCK_CHUNKS = 8
MOE_TF = 512
MOE_SLAB = D_MODEL // 2 // V7X_LANES
SC_CORES = 2
SC_SUBCORES = 16
SC_CHUNK_TOKENS = 32
COMBINE_TOKENS = 512
DMA_WAIT_ROWS = 128


def _rms(x, g):
    ms = jnp.mean(x * x, axis=-1, keepdims=True)
    return x * lax.rsqrt(ms + NORM_EPS) * g


def _cparams(sem, vmem=V7X_VMEM_LIMIT_BYTES):
    return pltpu.CompilerParams(dimension_semantics=sem, vmem_limit_bytes=vmem)


def _in_proj_kernel(x_ref, g_ref, w_ref, o_ref, xn_ref):
    @pl.when(pl.program_id(1) == 0)
    def _():
        xn_ref[...] = _rms(x_ref[...], g_ref[...]).astype(BF16)

    o_ref[...] = jnp.dot(xn_ref[...], w_ref[...].astype(BF16), preferred_element_type=F32).astype(o_ref.dtype)


def _in_proj(x2d, g, w, *, tm, tn, name, col_block=lambda j: j, ncols=None):
    n, d = x2d.shape
    wc = w.shape[1] if ncols is None else ncols
    return pl.pallas_call(
        _in_proj_kernel,
        out_shape=jax.ShapeDtypeStruct((n, wc), BF16),
        grid=(n // tm, wc // tn),
        in_specs=[
            pl.BlockSpec((tm, d), lambda i, j: (i, 0)),
            pl.BlockSpec((1, d), lambda i, j: (0, 0)),
            pl.BlockSpec((d, tn), lambda i, j: (0, col_block(j))),
        ],
        out_specs=pl.BlockSpec((tm, tn), lambda i, j: (i, j)),
        scratch_shapes=[pltpu.VMEM((tm, d), BF16)],
        compiler_params=_cparams(("parallel", "arbitrary")),
        name=name,
    )(x2d, g.reshape(1, d), w)


def _hgrn_kernel(q_ref, f_ref, i_ref, g_ref, lb_ref, gn_ref, o_ref, st_ref, kin_s, b_s, v_s, *, ts):
    @pl.when(pl.program_id(2) == 0)
    def _():
        st_ref[...] = jnp.zeros_like(st_ref)

    lb = lb_ref[...]
    oml = 1.0 - lb
    gn = gn_ref[...]
    half = HG_BLOCK // 2
    row = lax.broadcasted_iota(I32, (HG_BLOCK, HEAD_DIM), 0)
    row8 = lax.broadcasted_iota(I32, (half, HEAD_DIM), 0)
    nt_dims = (((1,), (1,)), ((), ()))
    tn_dims = (((0,), (0,)), ((), ()))

    def front(g, t0):
        sl = pl.ds(t0 + g * HG_BLOCK, HG_BLOCK)
        q = q_ref[sl, :].astype(F32)
        hf = f_ref[sl, :].astype(F32)
        v = i_ref[sl, :].astype(F32)
        kin = oml * jax.nn.sigmoid(-hf)
        b = jnp.log2(lb + oml * jax.nn.sigmoid(hf))
        for sh in (1, 2, 4, 8):
            b = b + jnp.where(row >= sh, pltpu.roll(b, sh, 0), 0.0)
        kin_s[g] = kin
        b_s[g] = b
        v_s[g] = v
        q_lo, q_hi = q[:half], q[half:]
        b_lo, b_hi = b[:half], b[half:]
        o_lo = jnp.zeros((half, HEAD_DIM), F32)
        o_hi = jnp.zeros((half, HEAD_DIM), F32)
        for s in range(HG_BLOCK):
            ks = kin_s[g, s:s + 1, :]
            bs = b_s[g, s:s + 1, :]
            vs = v_s[g, s:s + 1, :]
            if s < half:
                w = q_lo * ks * jnp.exp2(b_lo - bs)
                if s > 0:
                    w = jnp.where(row8 >= s, w, 0.0)
                o_lo = o_lo + jnp.sum(w, axis=-1, keepdims=True) * vs
                w = q_hi * ks * jnp.exp2(b_hi - bs)
            else:
                w = q_hi * ks * jnp.exp2(b_hi - bs)
                if s > half:
                    w = jnp.where(row8 >= s - half, w, 0.0)
            o_hi = o_hi + jnp.sum(w, axis=-1, keepdims=True) * vs
        bl = b_s[g, HG_BLOCK - 1:HG_BLOCK, :]
        qd = (q * jnp.exp2(b)).astype(BF16)
        kd = (kin * jnp.exp2(bl - b)).astype(BF16)
        upd = lax.dot_general(v.astype(BF16), kd, tn_dims, preferred_element_type=F32)
        return jnp.concatenate([o_lo, o_hi], axis=0), qd, upd, jnp.exp2(bl)

    def body(i, carry):
        t0 = pl.multiple_of(i * (HG_GROUP * HG_BLOCK), HG_GROUP * HG_BLOCK)
        fronts = [front(g, t0) for g in range(HG_GROUP)]
        st = st_ref[...]
        for g, (o_diag, qd, upd, dec) in enumerate(fronts):
            o = o_diag + lax.dot_general(qd, st.astype(BF16), nt_dims, preferred_element_type=F32)
            st = st * dec + upd
            sl = pl.ds(t0 + g * HG_BLOCK, HG_BLOCK)
            hg = g_ref[sl, :].astype(F32)
            o_ref[sl, :] = (_rms(o, gn) * (hg * jax.nn.sigmoid(hg))).astype(o_ref.dtype)
        st_ref[...] = st
        return carry

    lax.fori_loop(0, ts // (HG_GROUP * HG_BLOCK), body, 0)


def _hgrn(proj_h, lb, gn, *, ts, name="hgrn"):
    b, s, _ = proj_h.shape
    h = HG_HEADS

    def spec(off):
        return pl.BlockSpec((None, ts, HEAD_DIM), lambda bi, hi, si: (bi, si, off + hi))

    vec = pl.BlockSpec((1, HEAD_DIM), lambda bi, hi, si: (0, hi))
    return pl.pallas_call(
        functools.partial(_hgrn_kernel, ts=ts),
        out_shape=jax.ShapeDtypeStruct((b, s, HG_WIDTH), BF16),
        grid=(b, h, s // ts),
        in_specs=[spec(0), spec(h), spec(2 * h), spec(3 * h), vec, vec],
        out_specs=pl.BlockSpec((None, ts, HEAD_DIM), lambda bi, hi, si: (bi, si, hi)),
        scratch_shapes=[
            pltpu.VMEM((HEAD_DIM, HEAD_DIM), F32),
            pltpu.VMEM((HG_GROUP, HG_BLOCK, HEAD_DIM), F32),
            pltpu.VMEM((HG_GROUP, HG_BLOCK, HEAD_DIM), F32),
            pltpu.VMEM((HG_GROUP, HG_BLOCK, HEAD_DIM), F32),
        ],
        compiler_params=_cparams(("parallel", "parallel", "arbitrary")),
        name=name,
    )(proj_h, proj_h, proj_h, proj_h, lb.reshape(1, HG_WIDTH), gn.reshape(1, HG_WIDTH))


def _attn_bias(kind):
    rq = lax.broadcasted_iota(I32, (WIN_BLOCK, 2 * WIN_BLOCK), 0)
    ck = lax.broadcasted_iota(I32, (WIN_BLOCK, 2 * WIN_BLOCK), 1)
    if kind == 2:
        dist = rq + WIN_BLOCK - ck
        first = ck < WIN_BLOCK
    elif kind == 1:
        dist = 4 * ((rq & 31) - (ck & 63) + 32) + ((rq >> 5) - (ck >> 6))
        first = (ck & 63) < 32
    else:
        dist = 16 * ((rq & 7) - (ck & 15) + 8) + ((rq >> 3) - (ck >> 4))
        first = (ck & 15) < 8
    valid = (dist >= 0) & (dist <= WIN_BLOCK)
    return (jnp.where(valid, 0.0, NEG_BIG).astype(F32),
            jnp.where(valid & jnp.logical_not(first), 0.0, NEG_BIG).astype(F32))


def _attn_kernel(q0_ref, q1_ref, q2_ref, k_ref, v_ref, cs_ref, sn_ref, o_ref,
                 qr, kext, vext, acc, mrun, lrun, bias, onat, u0, u1, u2, u3, u4):
    ti = pl.program_id(2)
    wb = WIN_BLOCK
    scale = HEAD_DIM ** -0.5

    ubufs = (u0, u1, u2, u3, u4)
    for src, ub in zip((q0_ref, q1_ref, q2_ref, k_ref, v_ref), ubufs):
        ub[...] = pltpu.bitcast(src[...], U32)

    @pl.when(ti == 0)
    def _():
        kext[:, 0:wb, :] = jnp.zeros((DIL_MAX, wb, HEAD_DIM), F32)
        vext[:, 0:wb, :] = jnp.zeros((DIL_MAX, wb, HEAD_DIM), F32)

    @pl.when(ti > 0)
    def _():
        kext[:, 0:wb, :] = kext[:, wb:2 * wb, :]
        vext[:, 0:wb, :] = vext[:, wb:2 * wb, :]

    @pl.when((pl.program_id(0) == 0) & (pl.program_id(1) == 0) & (ti == 0))
    def _():
        for kind in range(N_GROUPS):
            full, nofirst = _attn_bias(kind)
            bias[2 * kind] = full
            bias[2 * kind + 1] = nofirst

    def rope_body(rp, carry):
        rows = pl.ds(rp, wb, stride=DIL_MAX // 2)
        parts = [_unpack_rows(ub[rows, :]) for ub in ubufs]
        for par in range(2):
            r = 2 * rp + par
            cs = cs_ref[r]
            sn = sn_ref[r]
            for g in range(N_GROUPS):
                q = parts[g][par]
                qr[g, r] = (q * cs + pltpu.roll(q, HEAD_DIM // 2, 1) * sn) * scale
            k = parts[N_GROUPS][par]
            kext[r, wb:2 * wb, :] = k * cs + pltpu.roll(k, HEAD_DIM // 2, 1) * sn
            vext[r, wb:2 * wb, :] = parts[N_GROUPS + 1][par]
            acc[r] = jnp.zeros((wb, HEAD_DIM), F32)
            mrun[r] = jnp.full((wb, HEAD_DIM), NEG_BIG, F32)
            lrun[r] = jnp.zeros((wb, HEAD_DIM), F32)
        return carry

    lax.fori_loop(0, DIL_MAX // 2, rope_body, 0)

    nt_dims = (((1,), (1,)), ((), ()))

    def block(qb, kb, vb, bias_blk):
        s = lax.dot_general(qb.astype(BF16), kb.astype(BF16), nt_dims, preferred_element_type=F32)
        s = s + bias_blk
        m = jnp.max(s, axis=-1, keepdims=True)
        p = jnp.exp(s - m)
        l = jnp.sum(p, axis=-1, keepdims=True)
        n = jnp.dot(p.astype(BF16), vb.astype(BF16), preferred_element_type=F32)
        return n, jnp.broadcast_to(m, (wb, HEAD_DIM)), jnp.broadcast_to(l, (wb, HEAD_DIM))

    def merge(r, rows, n, m, l):
        m_old = mrun[r, rows, :]
        m_new = jnp.maximum(m_old, m)
        a = jnp.exp(m_old - m_new)
        bb = jnp.exp(m - m_new)
        acc[r, rows, :] = acc[r, rows, :] * a + n * bb
        lrun[r, rows, :] = lrun[r, rows, :] * a + l * bb
        mrun[r, rows, :] = m_new

    first_tile = jnp.where(ti == 0, 1, 0)


    def g2_body(i, carry):
        rs = [i * ATT_UNROLL + u for u in range(ATT_UNROLL)]
        res = [block(qr[2, r], kext[r], vext[r], bias[4 + first_tile]) for r in rs]
        for r, (n, m, l) in zip(rs, res):
            merge(r, pl.ds(0, wb), n, m, l)
        return carry

    lax.fori_loop(0, DIL_MAX // ATT_UNROLL, g2_body, 0)

    def g1_body(i, carry):
        res = []
        for u in range(ATT_UNROLL // 4):
            mb = i * (ATT_UNROLL // 4) + u
            q0 = pl.multiple_of(32 * mb, 32)
            k0 = pl.multiple_of(96 + 32 * mb, 32)
            use_first = jnp.where(mb == 0, first_tile, 0)
            for r4 in range(4):
                qb = jnp.concatenate([qr[1, r4 + 4 * j, pl.ds(q0, 32), :] for j in range(4)], axis=0)
                kb = jnp.concatenate([kext[r4 + 4 * j, pl.ds(k0, 64), :] for j in range(4)], axis=0)
                vb = jnp.concatenate([vext[r4 + 4 * j, pl.ds(k0, 64), :] for j in range(4)], axis=0)
                res.append((r4, q0, block(qb, kb, vb, bias[2 + use_first])))
        for r4, q0, (n, m, l) in res:
            for j in range(4):
                sl = slice(32 * j, 32 * (j + 1))
                merge(r4 + 4 * j, pl.ds(q0, 32), n[sl], m[sl], l[sl])
        return carry

    lax.fori_loop(0, DIL_MAX // ATT_UNROLL, g1_body, 0)

    def g0_body(i, carry):
        res = []
        for u in range(ATT_UNROLL):
            mb = i * ATT_UNROLL + u
            q0 = pl.multiple_of(8 * mb, 8)
            k0 = pl.multiple_of(120 + 8 * mb, 8)
            qb = jnp.concatenate([qr[0, r, pl.ds(q0, 8), :] for r in range(DIL_MAX)], axis=0)
            kb = jnp.concatenate([kext[r, pl.ds(k0, 16), :] for r in range(DIL_MAX)], axis=0)
            vb = jnp.concatenate([vext[r, pl.ds(k0, 16), :] for r in range(DIL_MAX)], axis=0)
            use_first = jnp.where(mb == 0, first_tile, 0)
            res.append((q0, block(qb, kb, vb, bias[use_first])))
        for q0, (n, m, l) in res:
            for r in range(DIL_MAX):
                sl = slice(8 * r, 8 * (r + 1))
                merge(r, pl.ds(q0, 8), n[sl], m[sl], l[sl])
        return carry

    lax.fori_loop(0, DIL_MAX // ATT_UNROLL, g0_body, 0)

    for r in range(DIL_MAX):
        onat[pl.ds(r, wb, stride=DIL_MAX), :] = acc[r] / lrun[r]
    o_ref[...] = onat[...].astype(o_ref.dtype)


def _dilated_attn(proj, cs, sn, *, q_block, name="dilated_attn"):
    b, s, _ = proj.shape
    nt = s // ATT_TILE
    h = N_KV_HEADS

    def spec(off):
        return pl.BlockSpec((None, ATT_TILE, HEAD_DIM), lambda bi, hi, ti: (bi, ti, q_block + off + hi))

    tab = pl.BlockSpec((None, None, DIL_MAX, WIN_BLOCK, HEAD_DIM), lambda bi, hi, ti: (bi, ti, 0, 0, 0))
    ubuf = pltpu.VMEM((ATT_TILE // 2, HEAD_DIM), U32)
    return pl.pallas_call(
        _attn_kernel,
        out_shape=jax.ShapeDtypeStruct((b, nt * ATT_TILE, ATT_KV_WIDTH), BF16),
        grid=(b, h, nt),
        in_specs=[spec(0), spec(h), spec(2 * h), spec(3 * h), spec(4 * h), tab, tab],
        out_specs=pl.BlockSpec((None, ATT_TILE, HEAD_DIM), lambda bi, hi, ti: (bi, ti, hi)),
        scratch_shapes=[
            pltpu.VMEM((N_GROUPS, DIL_MAX, WIN_BLOCK, HEAD_DIM), F32),
            pltpu.VMEM((DIL_MAX, 2 * WIN_BLOCK, HEAD_DIM), F32),
            pltpu.VMEM((DIL_MAX, 2 * WIN_BLOCK, HEAD_DIM), F32),
            pltpu.VMEM((DIL_MAX, WIN_BLOCK, HEAD_DIM), F32),
            pltpu.VMEM((DIL_MAX, WIN_BLOCK, HEAD_DIM), F32),
            pltpu.VMEM((DIL_MAX, WIN_BLOCK, HEAD_DIM), F32),
            pltpu.VMEM((2 * N_GROUPS, WIN_BLOCK, 2 * WIN_BLOCK), F32),
            pltpu.VMEM((ATT_TILE, HEAD_DIM), F32),
            ubuf, ubuf, ubuf, ubuf, ubuf,
        ],
        compiler_params=_cparams(("arbitrary", "arbitrary", "arbitrary")),
        name=name,
    )(proj, proj, proj, proj, proj, cs, sn)


def _merge_out_kernel(oh_ref, oa_ref, gate_ref, x_ref, wh_ref, wa_ref, wo_ref, gc_ref, h1_ref, n2_ref):
    ga = gate_ref[:, :D_MODEL].astype(F32)
    gb = gate_ref[:, D_MODEL:].astype(F32)
    yh = jnp.dot(oh_ref[...], wh_ref[...], preferred_element_type=F32)
    ya = jnp.dot(oa_ref[...], wa_ref[...], preferred_element_type=F32)
    merged = jax.nn.sigmoid(ga) * yh + jax.nn.sigmoid(gb) * ya
    h1 = x_ref[...] + jnp.dot(merged.astype(BF16), wo_ref[...], preferred_element_type=F32)
    h1_ref[...] = h1
    n2_ref[...] = _rms(h1, gc_ref[...]).astype(BF16)


def _merge_out(oh, oa, gates, x2d, wh, wa, wo, gc, *, tm, gate_block, name="merge_out"):
    n, d = x2d.shape

    def const(shape):
        return pl.BlockSpec(shape, lambda i: (0, 0), pipeline_mode=pl.Buffered(1))

    return pl.pallas_call(
        _merge_out_kernel,
        out_shape=(jax.ShapeDtypeStruct((n, d), F32), jax.ShapeDtypeStruct((n, d), BF16)),
        grid=(n // tm,),
        in_specs=[
            pl.BlockSpec((tm, HG_WIDTH), lambda i: (i, 0)),
            pl.BlockSpec((tm, ATT_KV_WIDTH), lambda i: (i, 0)),
            pl.BlockSpec((tm, 2 * d), lambda i: (i, gate_block)),
            pl.BlockSpec((tm, d), lambda i: (i, 0)),
            const((HG_WIDTH, d)), const((ATT_KV_WIDTH, d)), const((d, d)), const((1, d)),
        ],
        out_specs=(pl.BlockSpec((tm, d), lambda i: (i, 0)), pl.BlockSpec((tm, d), lambda i: (i, 0))),
        compiler_params=_cparams(("parallel",)),
        name=name,
    )(oh, oa, gates, x2d, wh, wa, wo, gc.reshape(1, d))


def _cross_kernel(n2_ref, h1_ref, kv_ref, wq_ref, wo_ref, gm_ref, wrh_ref, wrl_ref, br_ref,
                  h2_ref, n3_ref, idx_ref, tw_ref, rank_ref, cnt_ref, carry_ref, *, tm):
    @pl.when((pl.program_id(0) == 0) & (pl.program_id(1) == 0))
    def _():
        carry_ref[...] = jnp.zeros_like(carry_ref)

    nt_dims = (((1,), (1,)), ((), ()))
    scale = HEAD_DIM ** -0.5
    q = (jnp.dot(n2_ref[...], wq_ref[...], preferred_element_type=F32) * scale).astype(BF16)
    outs = []
    for hh in range(CROSS_HEADS):
        sl = slice(hh * HEAD_DIM, (hh + 1) * HEAD_DIM)
        kh = kv_ref[:, sl]
        vh = kv_ref[:, CROSS_WIDTH + hh * HEAD_DIM:CROSS_WIDTH + (hh + 1) * HEAD_DIM]
        s = lax.dot_general(q[:, sl], kh, nt_dims, preferred_element_type=F32)
        p = jnp.exp(s - jnp.max(s, axis=-1, keepdims=True))
        l = jnp.sum(p, axis=-1, keepdims=True)
        outs.append(jnp.dot(p.astype(BF16), vh, preferred_element_type=F32) / l)
    o = jnp.concatenate(outs, axis=-1).astype(BF16)
    h2 = h1_ref[...] + jnp.dot(o, wo_ref[...], preferred_element_type=F32)
    h2_ref[...] = h2
    n3 = _rms(h2, gm_ref[...])
    u = _pack_rows(n3)
    for j in range(MOE_SLAB):
        n3_ref[pl.ds(j, tm, stride=MOE_SLAB), :] = u[:, j * V7X_LANES:(j + 1) * V7X_LANES]

    n3h = n3.astype(BF16)
    n3l = (n3 - n3h.astype(F32)).astype(BF16)
    wrh = wrh_ref[...]
    logits = (jnp.dot(n3h, wrh, preferred_element_type=F32)
              + jnp.dot(n3l, wrh, preferred_element_type=F32)
              + jnp.dot(n3h, wrl_ref[...], preferred_element_type=F32)
              + br_ref[...])
    lane = lax.broadcasted_iota(I32, (tm, N_EXPERTS), 1).astype(F32)
    vals, idxs, hots = [], [], []
    cur = logits
    for _ in range(TOP_K):
        mx = jnp.max(cur, axis=-1, keepdims=True)
        ix = jnp.min(jnp.where(cur == mx, lane, float(N_EXPERTS)), axis=-1, keepdims=True)
        hot = lane == ix
        vals.append(mx)
        idxs.append(ix)
        hots.append(hot)
        cur = jnp.where(hot, -jnp.inf, cur)
    es = [jnp.exp(v - vals[0]) for v in vals]
    den = es[0] + es[1] + es[2] + es[3]
    col = lax.broadcasted_iota(I32, (tm, TOP_K), 1)

    def pack(cols):
        out = jnp.broadcast_to(cols[TOP_K - 1], (tm, TOP_K))
        for k in range(TOP_K - 2, -1, -1):
            out = jnp.where(col == k, cols[k], out)
        return out

    idx_ref[...] = pack(idxs).astype(I32)
    tw_ref[...] = pack([e / den for e in es])

    cmat = (hots[0] | hots[1] | hots[2] | hots[3]).astype(F32)
    rr = lax.broadcasted_iota(I32, (tm, tm), 0)
    cc = lax.broadcasted_iota(I32, (tm, tm), 1)
    tri = (cc < rr).astype(BF16)
    before = jnp.dot(tri, cmat.astype(BF16), preferred_element_type=F32) + carry_ref[...]
    ranks = [jnp.sum(jnp.where(hot, before, 0.0), axis=-1, keepdims=True) for hot in hots]
    rank_ref[...] = pack(ranks).astype(I32)
    carry = carry_ref[...] + jnp.sum(cmat, axis=0, keepdims=True)
    carry_ref[...] = carry
    cnt_ref[...] = carry


def _cross(n2, h1, kv, wq, wo, gm, wrh, wrl, br, *, batch, tm, name="cross"):
    n, d = h1.shape
    per_b = n // batch // tm
    mem_len = kv.shape[0] // batch

    def const(shape):
        return pl.BlockSpec(shape, lambda bi, i: (0, 0), pipeline_mode=pl.Buffered(1))

    def row(bi, i):
        return (bi * per_b + i, 0)

    return pl.pallas_call(
        functools.partial(_cross_kernel, tm=tm),
        out_shape=(
            jax.ShapeDtypeStruct((n, d), F32),
            jax.ShapeDtypeStruct((n * MOE_SLAB, V7X_LANES), U32),
            jax.ShapeDtypeStruct((n, TOP_K), I32),
            jax.ShapeDtypeStruct((n, TOP_K), F32),
            jax.ShapeDtypeStruct((n, TOP_K), I32),
            jax.ShapeDtypeStruct((1, N_EXPERTS), F32),
        ),
        grid=(batch, per_b),
        in_specs=[
            pl.BlockSpec((tm, d), row),
            pl.BlockSpec((tm, d), row),
            pl.BlockSpec((mem_len, 2 * CROSS_WIDTH), lambda bi, i: (bi, 0)),
            const((d, CROSS_WIDTH)), const((CROSS_WIDTH, d)), const((1, d)),
            const((d, N_EXPERTS)), const((d, N_EXPERTS)), const((1, N_EXPERTS)),
        ],
        out_specs=(
            pl.BlockSpec((tm, d), row),
            pl.BlockSpec((tm * MOE_SLAB, V7X_LANES), row),
            pl.BlockSpec((tm, TOP_K), row),
            pl.BlockSpec((tm, TOP_K), row),
            pl.BlockSpec((tm, TOP_K), row),
            pl.BlockSpec((1, N_EXPERTS), lambda bi, i: (0, 0)),
        ),
        scratch_shapes=[pltpu.VMEM((1, N_EXPERTS), F32)],
        compiler_params=_cparams(("arbitrary", "arbitrary")),
        name=name,
    )(n2, h1, kv, wq, wo, gm.reshape(1, d), wrh, wrl, br.reshape(1, N_EXPERTS))


_HI_MASK = 0xFFFF0000


def _pack_rows(x):
    half = x.shape[1] // 2
    lo = lax.bitcast_convert_type(x[:, :half].astype(BF16).astype(F32), U32) >> 16
    hi = lax.bitcast_convert_type(x[:, half:].astype(BF16).astype(F32), U32) & jnp.uint32(_HI_MASK)
    return lo | hi


def _unpack_rows(u):
    lo = lax.bitcast_convert_type(u << 16, F32)
    hi = lax.bitcast_convert_type(u & jnp.uint32(_HI_MASK), F32)
    return lo, hi


def _slab(ref, row):
    return ref.at[pl.ds(pl.multiple_of(row * MOE_SLAB, MOE_SLAB), MOE_SLAB)]


def _sc_dispatch(slots, n3p, *, rows_total, name="moe_dispatch_sc"):
    n = n3p.shape[0]
    per_w = n // (SC_CORES * SC_SUBCORES)
    mesh = plsc.VectorSubcoreMesh(core_axis_name="c", subcore_axis_name="s",
                                  num_cores=SC_CORES, num_subcores=SC_SUBCORES)

    nchunk = per_w // SC_CHUNK_TOKENS

    def body(slots_hbm, n3p_hbm, out_hbm, idx_bufs, row_bufs, load_sems, scat_sems):
        wid = lax.axis_index("s") * SC_CORES + lax.axis_index("c")

        def base(c):
            return pl.multiple_of(wid * per_w + c * SC_CHUNK_TOKENS, SC_CHUNK_TOKENS)

        def load(c):
            return pltpu.async_copy(n3p_hbm.at[pl.ds(base(c), SC_CHUNK_TOKENS)], row_bufs.at[c % 2],
                                    load_sems.at[c % 2])

        def scatter(c, k):
            idx = idx_bufs.at[(c % 2) * TOP_K + k]
            pltpu.sync_copy(slots_hbm.at[pl.ds(k * n + base(c), SC_CHUNK_TOKENS)], idx)
            return pltpu.async_copy(row_bufs.at[c % 2], out_hbm.at[idx], scat_sems.at[c % 2])

        loads = [None] * nchunk
        scats = [None] * nchunk
        loads[0] = load(0)
        for c in range(nchunk):
            if c + 1 < nchunk:
                if c >= 1:
                    for s in scats[c - 1]:
                        s.wait()
                loads[c + 1] = load(c + 1)
            loads[c].wait()
            scats[c] = [scatter(c, k) for k in range(TOP_K)]
        for c in range(max(nchunk - 2, 0), nchunk):
            for s in scats[c]:
                s.wait()

    return pl.kernel(
        body,
        out_type=jax.ShapeDtypeStruct((rows_total, MOE_SLAB, V7X_LANES), U32),
        mesh=mesh,
        scratch_types=[
            pltpu.VMEM((2 * TOP_K, SC_CHUNK_TOKENS), I32),
            pltpu.VMEM((2, SC_CHUNK_TOKENS, MOE_SLAB, V7X_LANES), U32),
            pltpu.SemaphoreType.DMA((2,)),
            pltpu.SemaphoreType.DMA((2,)),
        ],
        name=name,
    )(slots, n3p)


def _fill_kernel(gstart_ref, cnt_ref, padded_ref, xs_in, xs_hbm, zero, sem, *, rows_total, n_unused):
    del xs_in
    group_rows = DMA_WAIT_ROWS * MOE_SLAB
    zero[...] = jnp.zeros((group_rows, V7X_LANES), U32)

    def fill(lo, hi):
        def one(r, carry):
            pltpu.make_async_copy(_slab(zero, 0), _slab(xs_hbm, r), sem).start()
            return carry

        lax.fori_loop(lo, hi, one, 0)

    def per_expert(e, carry):
        fill(gstart_ref[e] + cnt_ref[e], gstart_ref[e] + padded_ref[e])
        return carry

    lax.fori_loop(0, N_EXPERTS, per_expert, 0)
    fill(gstart_ref[N_EXPERTS - 1] + padded_ref[N_EXPERTS - 1], rows_total)

    def retire(c, carry):
        pltpu.make_async_copy(zero, xs_hbm.at[pl.ds(0, group_rows)], sem).wait()
        return carry

    lax.fori_loop(0, n_unused // DMA_WAIT_ROWS, retire, 0)


def _fill_unused(gstart, cnt, padded, xs, *, rows_total, n_unused, name="moe_fill"):
    grid_spec = pltpu.PrefetchScalarGridSpec(
        num_scalar_prefetch=3,
        grid=(1,),
        in_specs=[pl.BlockSpec(memory_space=pl.ANY)],
        out_specs=pl.BlockSpec(memory_space=pl.ANY),
        scratch_shapes=[pltpu.VMEM((DMA_WAIT_ROWS * MOE_SLAB, V7X_LANES), U32), pltpu.SemaphoreType.DMA],
    )
    return pl.pallas_call(
        functools.partial(_fill_kernel, rows_total=rows_total, n_unused=n_unused),
        out_shape=jax.ShapeDtypeStruct(xs.shape, xs.dtype),
        grid_spec=grid_spec,
        input_output_aliases={3: 0},
        compiler_params=_cparams(("arbitrary",)),
        name=name,
    )(gstart, cnt, padded, xs)


def _moe_kernel(te_ref, tv_ref, tstart_ref, trows_ref,
                xs_hbm, w1_ref, b1_ref, w2_ref, b2_ref, perm_ref, yb_hbm,
                stage, x16, acc, w1p, sems, *, nf):
    i = pl.program_id(0)
    f = pl.program_id(1)
    rows = trows_ref[i]
    start = tstart_ref[i]
    nchunk = rows // MOE_ROW_ALIGN
    chunk_slab_rows = MOE_ROW_ALIGN * MOE_SLAB
    half = 2 * V7X_LANES
    nhalf = 2 * MOE_TF // half

    def slot_rows(c):
        return pl.multiple_of((c & 1) * chunk_slab_rows, chunk_slab_rows)

    def hbm_rows(ref, c):
        r0 = pl.multiple_of((start + c * MOE_ROW_ALIGN) * MOE_SLAB, chunk_slab_rows)
        return ref.at[pl.ds(r0, chunk_slab_rows)]

    def in_copy(c):
        return pltpu.make_async_copy(hbm_rows(xs_hbm, c), stage.at[pl.ds(slot_rows(c), chunk_slab_rows)],
                                     sems.at[c & 1])

    def out_copy(c):
        return pltpu.make_async_copy(stage.at[pl.ds(slot_rows(c), chunk_slab_rows)], hbm_rows(yb_hbm, c),
                                     sems.at[c & 1])

    @pl.when((rows > 0) & (f == 0))
    def _():
        in_copy(0).start()

        def land(c, carry):
            @pl.when(c + 1 < nchunk)
            def _():
                in_copy(c + 1).start()

            in_copy(c).wait()
            r0 = pl.multiple_of(c * MOE_ROW_ALIGN, MOE_ROW_ALIGN)
            s0 = slot_rows(c)
            for j in range(MOE_SLAB):
                lo, hi = _unpack_rows(stage[pl.ds(s0 + j, MOE_ROW_ALIGN, stride=MOE_SLAB), :])
                x16[pl.ds(r0, MOE_ROW_ALIGN), j * V7X_LANES:(j + 1) * V7X_LANES] = lo.astype(BF16)
                x16[pl.ds(r0, MOE_ROW_ALIGN), (MOE_SLAB + j) * V7X_LANES:(MOE_SLAB + j + 1) * V7X_LANES] = (
                    hi.astype(BF16))
            acc[pl.ds(r0, MOE_ROW_ALIGN), :] = jnp.zeros((MOE_ROW_ALIGN, D_MODEL), F32)
            return carry

        lax.fori_loop(0, nchunk, land, 0)

    @pl.when(rows > 0)
    def _():
        perm = perm_ref[...]
        for c in range(nhalf):
            wb = w1_ref[:, c * half:(c + 1) * half].astype(BF16)
            w1p[:, c * half:(c + 1) * half] = jnp.dot(wb, perm, preferred_element_type=F32).astype(BF16)
        w2 = w2_ref[...].astype(BF16)
        b1 = b1_ref[...]

        def mlp(c0, nc):
            r0 = pl.multiple_of(c0 * MOE_ROW_ALIGN, MOE_ROW_ALIGN)
            nr = nc * MOE_ROW_ALIGN
            x = x16[pl.ds(r0, nr), :]
            h = jnp.dot(x, w1p[...], preferred_element_type=F32) + b1
            parts = []
            for c in range(nhalf):
                hg = jnp.minimum(h[:, c * half:c * half + V7X_LANES], SWIGLU_LIMIT)
                hl = jnp.clip(h[:, c * half + V7X_LANES:(c + 1) * half], -SWIGLU_LIMIT, SWIGLU_LIMIT)
                parts.append(hg * jax.nn.sigmoid(SWIGLU_ALPHA * hg) * (hl + 1.0))
            a = jnp.concatenate(parts, axis=-1).astype(BF16)
            acc[pl.ds(r0, nr), :] += jnp.dot(a, w2, preferred_element_type=F32)

        nquad = nchunk // MOE_BLOCK_CHUNKS

        def quad(c, carry):
            mlp(c * MOE_BLOCK_CHUNKS, MOE_BLOCK_CHUNKS)
            return carry

        lax.fori_loop(0, nquad, quad, 0)
        size = MOE_BLOCK_CHUNKS // 2
        while size >= 1:
            done = nchunk & ~(2 * size - 1)

            @pl.when((nchunk & size) != 0)
            def _(done=done, size=size):
                mlp(done, size)

            size //= 2

    @pl.when((rows > 0) & (f == nf - 1))
    def _():
        def emit(c, carry):
            @pl.when(c >= 2)
            def _():
                out_copy(c - 2).wait()

            r0 = pl.multiple_of(c * MOE_ROW_ALIGN, MOE_ROW_ALIGN)
            s0 = slot_rows(c)
            u = _pack_rows(acc[pl.ds(r0, MOE_ROW_ALIGN), :] + b2_ref[...])
            for j in range(MOE_SLAB):
                stage[pl.ds(s0 + j, MOE_ROW_ALIGN, stride=MOE_SLAB), :] = u[:, j * V7X_LANES:(j + 1) * V7X_LANES]
            out_copy(c).start()
            return carry

        lax.fori_loop(0, nchunk, emit, 0)

        @pl.when(nchunk >= 2)
        def _():
            out_copy(nchunk - 2).wait()

        out_copy(nchunk - 1).wait()


def _moe(tile_e, tile_v, tile_start, tile_rows, xs, w1, b1p, w2, b2, perm, *, name="moe_mlp"):
    nt = tile_e.shape[0]
    nf = D_FF // MOE_TF

    def w1_map(i, f, te, tv, ts, tr):
        return (te[i], 0, jnp.where(tv[i] > 0, f, nf - 1))

    def w2_map(i, f, te, tv, ts, tr):
        return (te[i], jnp.where(tv[i] > 0, f, nf - 1), 0)

    def b2_map(i, f, te, tv, ts, tr):
        return (te[i], 0, 0)

    grid_spec = pltpu.PrefetchScalarGridSpec(
        num_scalar_prefetch=4,
        grid=(nt, nf),
        in_specs=[
            pl.BlockSpec(memory_space=pl.ANY),
            pl.BlockSpec((None, D_MODEL, 2 * MOE_TF), w1_map),
            pl.BlockSpec((None, 1, 2 * MOE_TF), w1_map),
            pl.BlockSpec((None, MOE_TF, D_MODEL), w2_map),
            pl.BlockSpec((None, 1, D_MODEL), b2_map),
            pl.BlockSpec((2 * V7X_LANES, 2 * V7X_LANES), lambda i, f, te, tv, ts, tr: (0, 0)),
        ],
        out_specs=pl.BlockSpec(memory_space=pl.ANY),
        scratch_shapes=[
            pltpu.VMEM((2 * MOE_ROW_ALIGN * MOE_SLAB, V7X_LANES), U32),
            pltpu.VMEM((MOE_TMAX, D_MODEL), BF16),
            pltpu.VMEM((MOE_TMAX, D_MODEL), F32),
            pltpu.VMEM((D_MODEL, 2 * MOE_TF), BF16),
            pltpu.SemaphoreType.DMA((2,)),
        ],
    )
    return pl.pallas_call(
        functools.partial(_moe_kernel, nf=nf),
        out_shape=jax.ShapeDtypeStruct(xs.shape, xs.dtype),
        grid_spec=grid_spec,
        input_output_aliases={4: 0},
        compiler_params=_cparams(("arbitrary", "arbitrary")),
        name=name,
    )(tile_e, tile_v, tile_start, tile_rows, xs, w1, b1p.reshape(N_EXPERTS, 1, 2 * D_FF), w2,
      b2.reshape(N_EXPERTS, 1, D_MODEL), perm)


def _sc_gather(slots, yb3, *, n_tok, name="moe_gather_sc"):
    per_w = n_tok // (SC_CORES * SC_SUBCORES)
    mesh = plsc.VectorSubcoreMesh(core_axis_name="c", subcore_axis_name="s",
                                  num_cores=SC_CORES, num_subcores=SC_SUBCORES)

    items = [(c, k) for c in range(per_w // SC_CHUNK_TOKENS) for k in range(TOP_K)]

    def body(slots_hbm, yb_hbm, out_hbm, idx_bufs, row_bufs, gather_sems, write_sems):
        wid = lax.axis_index("s") * SC_CORES + lax.axis_index("c")

        def offset(j):
            c, k = items[j]
            return pl.multiple_of(k * n_tok + wid * per_w + c * SC_CHUNK_TOKENS, SC_CHUNK_TOKENS)

        def gather(j):
            idx = idx_bufs.at[j % 2]
            pltpu.sync_copy(slots_hbm.at[pl.ds(offset(j), SC_CHUNK_TOKENS)], idx)
            return pltpu.async_copy(yb_hbm.at[idx], row_bufs.at[j % 2], gather_sems.at[j % 2])

        def write(j):
            return pltpu.async_copy(row_bufs.at[j % 2], out_hbm.at[pl.ds(offset(j), SC_CHUNK_TOKENS)],
                                    write_sems.at[j % 2])

        n_items = len(items)
        gathers = [None] * n_items
        writes = [None] * n_items
        gathers[0] = gather(0)
        for j in range(n_items):
            if j + 1 < n_items:
                if j >= 1:
                    writes[j - 1].wait()
                gathers[j + 1] = gather(j + 1)
            gathers[j].wait()
            writes[j] = write(j)
        for j in range(max(n_items - 2, 0), n_items):
            writes[j].wait()

    return pl.kernel(
        body,
        out_type=jax.ShapeDtypeStruct((TOP_K * n_tok, MOE_SLAB, V7X_LANES), U32),
        mesh=mesh,
        scratch_types=[
            pltpu.VMEM((2, SC_CHUNK_TOKENS), I32),
            pltpu.VMEM((2, SC_CHUNK_TOKENS, MOE_SLAB, V7X_LANES), U32),
            pltpu.SemaphoreType.DMA((2,)),
            pltpu.SemaphoreType.DMA((2,)),
        ],
        name=name,
    )(slots, yb3)


def _combine_kernel(yg_ref, h2_ref, tw_ref, gf_ref, o_ref, *, tm):
    tw = tw_ref[...]
    cols_lo, cols_hi = [], []
    for j in range(MOE_SLAB):
        clo = chi = None
        for k in range(TOP_K):
            lo, hi = _unpack_rows(yg_ref[k, pl.ds(j, tm, stride=MOE_SLAB), :])
            w = tw[:, k:k + 1]
            clo = w * lo if clo is None else clo + w * lo
            chi = w * hi if chi is None else chi + w * hi
        cols_lo.append(clo)
        cols_hi.append(chi)
    h3 = h2_ref[...] + jnp.concatenate(cols_lo + cols_hi, axis=-1)
    o_ref[...] = _rms(h3, gf_ref[...])


def _combine(yg, h2, tw, gf, *, tm, name="moe_combine"):
    n, d = h2.shape
    return pl.pallas_call(
        functools.partial(_combine_kernel, tm=tm),
        out_shape=jax.ShapeDtypeStruct((n, d), F32),
        grid=(n // tm,),
        in_specs=[
            pl.BlockSpec((TOP_K, tm * MOE_SLAB, V7X_LANES), lambda i: (0, i, 0)),
            pl.BlockSpec((tm, d), lambda i: (i, 0)),
            pl.BlockSpec((tm, TOP_K), lambda i: (i, 0)),
            pl.BlockSpec((1, d), lambda i: (0, 0)),
        ],
        out_specs=pl.BlockSpec((tm, d), lambda i: (i, 0)),
        compiler_params=_cparams(("parallel",)),
        name=name,
    )(yg, h2, tw, gf.reshape(1, d))


def _rope_tables(positions):
    b, s = positions.shape
    half = HEAD_DIM // 2
    inv_freq = 1.0 / (ROPE_THETA ** (jnp.arange(half, dtype=F32) / half))
    ang = positions.astype(F32)[..., None] * inv_freq
    cos = jnp.cos(ang)
    sin = jnp.sin(ang)
    cs = jnp.concatenate([cos, cos], axis=-1)
    sn = jnp.concatenate([-sin, sin], axis=-1)

    def deint(a):
        a = a.reshape(b, s // ATT_TILE, WIN_BLOCK, DIL_MAX, HEAD_DIM)
        return a.transpose(0, 1, 3, 2, 4)

    return deint(cs), deint(sn)


def _routing(counts, n_tok):
    cnt = counts.reshape(N_EXPERTS).astype(I32)
    padded = (cnt + MOE_ROW_ALIGN - 1) // MOE_ROW_ALIGN * MOE_ROW_ALIGN
    gstart = (jnp.cumsum(padded) - padded).astype(I32)
    rows_total = n_tok * TOP_K + N_EXPERTS * MOE_ROW_ALIGN

    def first_above(ends, q):
        return jnp.minimum(jnp.sum((ends[None, :] <= q[:, None]).astype(I32), axis=1), N_EXPERTS - 1)

    main_rows = jnp.minimum(padded, MOE_TMAX)
    main = (jnp.arange(N_EXPERTS, dtype=I32), (main_rows > 0).astype(I32), gstart.astype(I32),
            main_rows.astype(I32))

    over = padded - main_rows
    nt_e = (over + MOE_TMAX - 1) // MOE_TMAX
    tend = jnp.cumsum(nt_e)
    tstart = tend - nt_e
    n_over = tend[-1]
    ti = jnp.arange(rows_total // MOE_TMAX, dtype=I32)
    valid = ti < n_over
    tic = jnp.clip(ti, 0, jnp.maximum(n_over - 1, 0))
    te = first_above(tend, tic)
    local = tic - tstart[te]
    row0 = gstart[te] + (local + 1) * MOE_TMAX
    rows = jnp.where(valid, jnp.clip(over[te] - local * MOE_TMAX, 0, MOE_TMAX), 0)
    overflow = (te, valid.astype(I32), row0.astype(I32), rows.astype(I32))
    return gstart, cnt, padded, rows_total, main, overflow, n_over > 0


def kernel(x, mem, positions, norm_mix_g, w_in, lb_raw, hgrn_norm_g, w_br_hgrn, w_br_attn, w_out,
           norm_cross_g, norm_mem_g, w_cq, w_ckv, w_co, norm_moe_g, w_router, b_router,
           w_mlp1, b_mlp1, w_mlp2, b_mlp2, norm_final_g):
    bsz, seq, d = x.shape
    assert w_in.shape[0] == 1 and d == D_MODEL and seq % ATT_TILE == 0
    n_tok = bsz * seq
    lower_bounds = jnp.cumsum(jax.nn.softmax(lb_raw.astype(F32), axis=0), axis=0)
    cs, sn = _rope_tables(positions)

    wl = w_in[0]
    c_h = 4 * HG_WIDTH
    c_a = c_h + ATT_Q_WIDTH + 2 * ATT_KV_WIDTH
    tn = 1024
    x2d = x.reshape(n_tok, d)
    n_h, n_a, n_g = c_h // tn, (c_a - c_h) // tn, 2 * D_MODEL // tn
    proj = _in_proj(x2d, norm_mix_g[0], wl, tm=1024, tn=tn, name="in_proj",
                    col_block=lambda j: jnp.where(j < n_h, j, jnp.where(j < n_h + n_g, j + n_a, j - n_g)))
    proj3 = proj.reshape(bsz, seq, -1)

    o_h = _hgrn(proj3, lower_bounds[0], hgrn_norm_g[0], ts=1024)
    o_a = _dilated_attn(proj3, cs, sn, q_block=(c_h + 2 * D_MODEL) // HEAD_DIM)

    h1, n2 = _merge_out(o_h.reshape(n_tok, HG_WIDTH), o_a.reshape(n_tok, ATT_KV_WIDTH), proj, x2d,
                        w_br_hgrn[0].astype(BF16), w_br_attn[0].astype(BF16), w_out[0].astype(BF16),
                        norm_cross_g[0], tm=256, gate_block=c_h // (2 * D_MODEL))

    mem2d = mem.reshape(-1, d)
    kv = _in_proj(mem2d, norm_mem_g, w_ckv[0], tm=mem2d.shape[0], tn=2 * CROSS_WIDTH, name="mem_kv")
    wrh = w_router[0].astype(BF16)
    wrl = (w_router[0] - wrh.astype(F32)).astype(BF16)
    h2, n3, top_e, top_w, rank, counts = _cross(
        n2, h1, kv, w_cq[0].astype(BF16), w_co[0].astype(BF16), norm_moe_g[0], wrh, wrl, b_router[0],
        batch=bsz, tm=512)

    gstart, cnt, padded, rows_total, main_tiles, over_tiles, has_over = _routing(counts, n_tok)
    hot = top_e[..., None] == jnp.arange(N_EXPERTS, dtype=I32)
    slots = (rank + jnp.sum(jnp.where(hot, gstart, 0), axis=-1)).T.reshape(-1)
    xs = _sc_dispatch(slots, n3.reshape(n_tok, MOE_SLAB, V7X_LANES), rows_total=rows_total)
    xs = _fill_unused(gstart, cnt, padded, xs.reshape(rows_total * MOE_SLAB, V7X_LANES),
                      rows_total=rows_total, n_unused=N_EXPERTS * MOE_ROW_ALIGN)

    ii = jnp.arange(2 * V7X_LANES)
    src = jnp.where(ii < V7X_LANES, 2 * ii, 2 * (ii - V7X_LANES) + 1)
    perm = (ii[:, None] == src[None, :]).astype(BF16)
    b1p = b_mlp1[0].reshape(N_EXPERTS, -1, V7X_LANES, 2).transpose(0, 1, 3, 2).reshape(N_EXPERTS, 2 * D_FF)
    def run_moe(tiles, rows_buf, name):
        return _moe(*tiles, rows_buf, w_mlp1[0], b1p, w_mlp2[0], b_mlp2[0], perm, name=name)

    yb = run_moe(main_tiles, xs, "moe_mlp")
    yb = lax.cond(has_over, lambda rows_buf: run_moe(over_tiles, rows_buf, "moe_mlp_overflow"),
                  lambda rows_buf: rows_buf, yb)

    yg = _sc_gather(slots, yb.reshape(rows_total, MOE_SLAB, V7X_LANES), n_tok=n_tok)
    out = _combine(yg.reshape(TOP_K, n_tok * MOE_SLAB, V7X_LANES), h2, top_w, norm_final_g, tm=COMBINE_TOKENS)
    return out.reshape(bsz, seq, d)
```

```python
import functools

import jax
import jax.numpy as jnp
from jax import lax
from jax.experimental import pallas as pl
from jax.experimental.pallas import tpu as pltpu
from jax.experimental.pallas import tpu_sc as plsc

F32 = jnp.float32
BF16 = jnp.bfloat16
I32 = jnp.int32
U32 = jnp.uint32

D_MODEL = 2048
HEAD_DIM = 128
HG_HEADS = 8
HG_WIDTH = HG_HEADS * HEAD_DIM
N_KV_HEADS = 8
N_GROUPS = 3
ATT_Q_WIDTH = N_GROUPS * N_KV_HEADS * HEAD_DIM
ATT_KV_WIDTH = N_KV_HEADS * HEAD_DIM
WIN_BLOCK = 128
ROPE_THETA = 10000.0
CROSS_HEADS = 4
CROSS_WIDTH = CROSS_HEADS * HEAD_DIM
N_EXPERTS = 32
TOP_K = 4
D_FF = D_MODEL
SWIGLU_ALPHA = 1.702
SWIGLU_LIMIT = 7.0
NORM_EPS = 1e-6

V7X_LANES = 128
V7X_VMEM_LIMIT_BYTES = 56 * 1024 * 1024

DIL_MAX = 16
ATT_TILE = DIL_MAX * WIN_BLOCK
ATT_UNROLL = 16
HG_BLOCK = 16
HG_GROUP = 32
NEG_BIG = -1e30

MOE_TMAX = 1536
MOE_ROW_ALIGN = 128
MOE_BLOCK_CHUNKS = 8
MOE_TF = 512
MOE_SLAB = D_MODEL // 2 // V7X_LANES
SC_CORES = 2
SC_SUBCORES = 16
SC_CHUNK_TOKENS = 32
COMBINE_TOKENS = 512


def _rms(x, g):
    ms = jnp.mean(x * x, axis=-1, keepdims=True)
    return x * lax.rsqrt(ms + NORM_EPS) * g


def _cparams(sem, vmem=V7X_VMEM_LIMIT_BYTES):
    return pltpu.CompilerParams(dimension_semantics=sem, vmem_limit_bytes=vmem)


def _in_proj_kernel(x_ref, g_ref, w_ref, o_ref, xn_ref):
    @pl.when(pl.program_id(1) == 0)
    def _():
        xn_ref[...] = _rms(x_ref[...], g_ref[...]).astype(BF16)

    o_ref[...] = jnp.dot(xn_ref[...], w_ref[...].astype(BF16), preferred_element_type=F32).astype(o_ref.dtype)


def _in_proj(x2d, g, w, *, tm, tn, name, col_block=lambda j: j, ncols=None):
    n, d = x2d.shape
    wc = w.shape[1] if ncols is None else ncols
    return pl.pallas_call(
        _in_proj_kernel,
        out_shape=jax.ShapeDtypeStruct((n, wc), BF16),
        grid=(n // tm, wc // tn),
        in_specs=[
            pl.BlockSpec((tm, d), lambda i, j: (i, 0)),
            pl.BlockSpec((1, d), lambda i, j: (0, 0)),
            pl.BlockSpec((d, tn), lambda i, j: (0, col_block(j))),
        ],
        out_specs=pl.BlockSpec((tm, tn), lambda i, j: (i, j)),
        scratch_shapes=[pltpu.VMEM((tm, d), BF16)],
        compiler_params=_cparams(("parallel", "arbitrary")),
        name=name,
    )(x2d, g.reshape(1, d), w)


def _hgrn_kernel(q_ref, f_ref, i_ref, g_ref, lb_ref, gn_ref, o_ref, st_ref, kin_s, b_s, v_s, *, ts):
    @pl.when(pl.program_id(2) == 0)
    def _():
        st_ref[...] = jnp.zeros_like(st_ref)

    lb = lb_ref[...]
    oml = 1.0 - lb
    gn = gn_ref[...]
    half = HG_BLOCK // 2
    row = lax.broadcasted_iota(I32, (HG_BLOCK, HEAD_DIM), 0)
    row8 = lax.broadcasted_iota(I32, (half, HEAD_DIM), 0)
    nt_dims = (((1,), (1,)), ((), ()))
    tn_dims = (((0,), (0,)), ((), ()))

    def front(g, t0):
        sl = pl.ds(t0 + g * HG_BLOCK, HG_BLOCK)
        q = q_ref[sl, :].astype(F32)
        hf = f_ref[sl, :].astype(F32)
        v = i_ref[sl, :].astype(F32)
        kin = oml * jax.nn.sigmoid(-hf)
        b = jnp.log2(lb + oml * jax.nn.sigmoid(hf))
        for sh in (1, 2, 4, 8):
            b = b + jnp.where(row >= sh, pltpu.roll(b, sh, 0), 0.0)
        kin_s[g] = kin
        b_s[g] = b
        v_s[g] = v
        q_lo, q_hi = q[:half], q[half:]
        b_lo, b_hi = b[:half], b[half:]
        o_lo = jnp.zeros((half, HEAD_DIM), F32)
        o_hi = jnp.zeros((half, HEAD_DIM), F32)
        for s in range(HG_BLOCK):
            ks = kin_s[g, s:s + 1, :]
            bs = b_s[g, s:s + 1, :]
            vs = v_s[g, s:s + 1, :]
            if s < half:
                w = q_lo * ks * jnp.exp2(b_lo - bs)
                if s > 0:
                    w = jnp.where(row8 >= s, w, 0.0)
                o_lo = o_lo + jnp.sum(w, axis=-1, keepdims=True) * vs
                w = q_hi * ks * jnp.exp2(b_hi - bs)
            else:
                w = q_hi * ks * jnp.exp2(b_hi - bs)
                if s > half:
                    w = jnp.where(row8 >= s - half, w, 0.0)
            o_hi = o_hi + jnp.sum(w, axis=-1, keepdims=True) * vs
        bl = b_s[g, HG_BLOCK - 1:HG_BLOCK, :]
        qd = (q * jnp.exp2(b)).astype(BF16)
        kd = (kin * jnp.exp2(bl - b)).astype(BF16)
        upd = lax.dot_general(v.astype(BF16), kd, tn_dims, preferred_element_type=F32)
        return jnp.concatenate([o_lo, o_hi], axis=0), qd, upd, jnp.exp2(bl)

    def body(i, carry):
        t0 = pl.multiple_of(i * (HG_GROUP * HG_BLOCK), HG_GROUP * HG_BLOCK)
        fronts = [front(g, t0) for g in range(HG_GROUP)]
        st = st_ref[...]
        for g, (o_diag, qd, upd, dec) in enumerate(fronts):
            o = o_diag + lax.dot_general(qd, st.astype(BF16), nt_dims, preferred_element_type=F32)
            st = st * dec + upd
            sl = pl.ds(t0 + g * HG_BLOCK, HG_BLOCK)
            hg = g_ref[sl, :].astype(F32)
            o_ref[sl, :] = (_rms(o, gn) * (hg * jax.nn.sigmoid(hg))).astype(o_ref.dtype)
        st_ref[...] = st
        return carry

    lax.fori_loop(0, ts // (HG_GROUP * HG_BLOCK), body, 0)


def _hgrn(proj_h, lb, gn, *, ts, name="hgrn"):
    b, s, _ = proj_h.shape
    h = HG_HEADS

    def spec(off):
        return pl.BlockSpec((None, ts, HEAD_DIM), lambda bi, hi, si: (bi, si, off + hi))

    vec = pl.BlockSpec((1, HEAD_DIM), lambda bi, hi, si: (0, hi))
    return pl.pallas_call(
        functools.partial(_hgrn_kernel, ts=ts),
        out_shape=jax.ShapeDtypeStruct((b, s, HG_WIDTH), BF16),
        grid=(b, h, s // ts),
        in_specs=[spec(0), spec(h), spec(2 * h), spec(3 * h), vec, vec],
        out_specs=pl.BlockSpec((None, ts, HEAD_DIM), lambda bi, hi, si: (bi, si, hi)),
        scratch_shapes=[
            pltpu.VMEM((HEAD_DIM, HEAD_DIM), F32),
            pltpu.VMEM((HG_GROUP, HG_BLOCK, HEAD_DIM), F32),
            pltpu.VMEM((HG_GROUP, HG_BLOCK, HEAD_DIM), F32),
            pltpu.VMEM((HG_GROUP, HG_BLOCK, HEAD_DIM), F32),
        ],
        compiler_params=_cparams(("parallel", "parallel", "arbitrary")),
        name=name,
    )(proj_h, proj_h, proj_h, proj_h, lb.reshape(1, HG_WIDTH), gn.reshape(1, HG_WIDTH))


def _attn_bias(kind):
    rq = lax.broadcasted_iota(I32, (WIN_BLOCK, 2 * WIN_BLOCK), 0)
    ck = lax.broadcasted_iota(I32, (WIN_BLOCK, 2 * WIN_BLOCK), 1)
    if kind == 2:
        dist = rq + WIN_BLOCK - ck
        first = ck < WIN_BLOCK
    elif kind == 1:
        dist = 4 * ((rq & 31) - (ck & 63) + 32) + ((rq >> 5) - (ck >> 6))
        first = (ck & 63) < 32
    else:
        dist = 16 * ((rq & 7) - (ck & 15) + 8) + ((rq >> 3) - (ck >> 4))
        first = (ck & 15) < 8
    valid = (dist >= 0) & (dist <= WIN_BLOCK)
    return (jnp.where(valid, 0.0, NEG_BIG).astype(F32),
            jnp.where(valid & jnp.logical_not(first), 0.0, NEG_BIG).astype(F32))


def _attn_kernel(q0_ref, q1_ref, q2_ref, k_ref, v_ref, cs_ref, sn_ref, o_ref,
                 qr, kext, vext, acc, mrun, lrun, bias, onat, u0, u1, u2, u3, u4):
    ti = pl.program_id(2)
    wb = WIN_BLOCK
    scale = HEAD_DIM ** -0.5

    ubufs = (u0, u1, u2, u3, u4)
    for src, ub in zip((q0_ref, q1_ref, q2_ref, k_ref, v_ref), ubufs):
        ub[...] = pltpu.bitcast(src[...], U32)

    @pl.when(ti == 0)
    def _():
        kext[:, 0:wb, :] = jnp.zeros((DIL_MAX, wb, HEAD_DIM), F32)
        vext[:, 0:wb, :] = jnp.zeros((DIL_MAX, wb, HEAD_DIM), F32)

    @pl.when(ti > 0)
    def _():
        kext[:, 0:wb, :] = kext[:, wb:2 * wb, :]
        vext[:, 0:wb, :] = vext[:, wb:2 * wb, :]

    @pl.when((pl.program_id(0) == 0) & (pl.program_id(1) == 0) & (ti == 0))
    def _():
        for kind in range(N_GROUPS):
            full, nofirst = _attn_bias(kind)
            bias[2 * kind] = full
            bias[2 * kind + 1] = nofirst

    def rope_body(rp, carry):
        rows = pl.ds(rp, wb, stride=DIL_MAX // 2)
        parts = [_unpack_rows(ub[rows, :]) for ub in ubufs]
        for par in range(2):
            r = 2 * rp + par
            cs = cs_ref[r]
            sn = sn_ref[r]
            for g in range(N_GROUPS):
                q = parts[g][par]
                qr[g, r] = (q * cs + pltpu.roll(q, HEAD_DIM // 2, 1) * sn) * scale
            k = parts[N_GROUPS][par]
            kext[r, wb:2 * wb, :] = k * cs + pltpu.roll(k, HEAD_DIM // 2, 1) * sn
            vext[r, wb:2 * wb, :] = parts[N_GROUPS + 1][par]
            acc[r] = jnp.zeros((wb, HEAD_DIM), F32)
            mrun[r] = jnp.full((wb, HEAD_DIM), NEG_BIG, F32)
            lrun[r] = jnp.zeros((wb, HEAD_DIM), F32)
        return carry

    lax.fori_loop(0, DIL_MAX // 2, rope_body, 0)

    nt_dims = (((1,), (1,)), ((), ()))

    def block(qb, kb, vb, bias_blk):
        s = lax.dot_general(qb.astype(BF16), kb.astype(BF16), nt_dims, preferred_element_type=F32)
        s = s + bias_blk
        m = jnp.max(s, axis=-1, keepdims=True)
        p = jnp.exp(s - m)
        l = jnp.sum(p, axis=-1, keepdims=True)
        n = jnp.dot(p.astype(BF16), vb.astype(BF16), preferred_element_type=F32)
        return n, jnp.broadcast_to(m, (wb, HEAD_DIM)), jnp.broadcast_to(l, (wb, HEAD_DIM))

    def merge(r, rows, n, m, l):
        m_old = mrun[r, rows, :]
        m_new = jnp.maximum(m_old, m)
        a = jnp.exp(m_old - m_new)
        bb = jnp.exp(m - m_new)
        acc[r, rows, :] = acc[r, rows, :] * a + n * bb
        lrun[r, rows, :] = lrun[r, rows, :] * a + l * bb
        mrun[r, rows, :] = m_new

    first_tile = jnp.where(ti == 0, 1, 0)


    def g2_body(i, carry):
        rs = [i * ATT_UNROLL + u for u in range(ATT_UNROLL)]
        res = [block(qr[2, r], kext[r], vext[r], bias[4 + first_tile]) for r in rs]
        for r, (n, m, l) in zip(rs, res):
            merge(r, pl.ds(0, wb), n, m, l)
        return carry

    lax.fori_loop(0, DIL_MAX // ATT_UNROLL, g2_body, 0)

    def g1_body(i, carry):
        res = []
        for u in range(ATT_UNROLL // 4):
            mb = i * (ATT_UNROLL // 4) + u
            q0 = pl.multiple_of(32 * mb, 32)
            k0 = pl.multiple_of(96 + 32 * mb, 32)
            use_first = jnp.where(mb == 0, first_tile, 0)
            for r4 in range(4):
                qb = jnp.concatenate([qr[1, r4 + 4 * j, pl.ds(q0, 32), :] for j in range(4)], axis=0)
                kb = jnp.concatenate([kext[r4 + 4 * j, pl.ds(k0, 64), :] for j in range(4)], axis=0)
                vb = jnp.concatenate([vext[r4 + 4 * j, pl.ds(k0, 64), :] for j in range(4)], axis=0)
                res.append((r4, q0, block(qb, kb, vb, bias[2 + use_first])))
        for r4, q0, (n, m, l) in res:
            for j in range(4):
                sl = slice(32 * j, 32 * (j + 1))
                merge(r4 + 4 * j, pl.ds(q0, 32), n[sl], m[sl], l[sl])
        return carry

    lax.fori_loop(0, DIL_MAX // ATT_UNROLL, g1_body, 0)

    def g0_body(i, carry):
        res = []
        for u in range(ATT_UNROLL):
            mb = i * ATT_UNROLL + u
            q0 = pl.multiple_of(8 * mb, 8)
            k0 = pl.multiple_of(120 + 8 * mb, 8)
            qb = jnp.concatenate([qr[0, r, pl.ds(q0, 8), :] for r in range(DIL_MAX)], axis=0)
            kb = jnp.concatenate([kext[r, pl.ds(k0, 16), :] for r in range(DIL_MAX)], axis=0)
            vb = jnp.concatenate([vext[r, pl.ds(k0, 16), :] for r in range(DIL_MAX)], axis=0)
            use_first = jnp.where(mb == 0, first_tile, 0)
            res.append((q0, block(qb, kb, vb, bias[use_first])))
        for q0, (n, m, l) in res:
            for r in range(DIL_MAX):
                sl = slice(8 * r, 8 * (r + 1))
                merge(r, pl.ds(q0, 8), n[sl], m[sl], l[sl])
        return carry

    lax.fori_loop(0, DIL_MAX // ATT_UNROLL, g0_body, 0)

    for r in range(DIL_MAX):
        onat[pl.ds(r, wb, stride=DIL_MAX), :] = acc[r] / lrun[r]
    o_ref[...] = onat[...].astype(o_ref.dtype)


def _dilated_attn(proj, cs, sn, *, q_block, name="dilated_attn"):
    b, s, _ = proj.shape
    nt = s // ATT_TILE
    h = N_KV_HEADS

    def spec(off):
        return pl.BlockSpec((None, ATT_TILE, HEAD_DIM), lambda bi, hi, ti: (bi, ti, q_block + off + hi))

    tab = pl.BlockSpec((None, None, DIL_MAX, WIN_BLOCK, HEAD_DIM), lambda bi, hi, ti: (bi, ti, 0, 0, 0))
    ubuf = pltpu.VMEM((ATT_TILE // 2, HEAD_DIM), U32)
    return pl.pallas_call(
        _attn_kernel,
        out_shape=jax.ShapeDtypeStruct((b, nt * ATT_TILE, ATT_KV_WIDTH), BF16),
        grid=(b, h, nt),
        in_specs=[spec(0), spec(h), spec(2 * h), spec(3 * h), spec(4 * h), tab, tab],
        out_specs=pl.BlockSpec((None, ATT_TILE, HEAD_DIM), lambda bi, hi, ti: (bi, ti, hi)),
        scratch_shapes=[
            pltpu.VMEM((N_GROUPS, DIL_MAX, WIN_BLOCK, HEAD_DIM), F32),
            pltpu.VMEM((DIL_MAX, 2 * WIN_BLOCK, HEAD_DIM), F32),
            pltpu.VMEM((DIL_MAX, 2 * WIN_BLOCK, HEAD_DIM), F32),
            pltpu.VMEM((DIL_MAX, WIN_BLOCK, HEAD_DIM), F32),
            pltpu.VMEM((DIL_MAX, WIN_BLOCK, HEAD_DIM), F32),
            pltpu.VMEM((DIL_MAX, WIN_BLOCK, HEAD_DIM), F32),
            pltpu.VMEM((2 * N_GROUPS, WIN_BLOCK, 2 * WIN_BLOCK), F32),
            pltpu.VMEM((ATT_TILE, HEAD_DIM), F32),
            ubuf, ubuf, ubuf, ubuf, ubuf,
        ],
        compiler_params=_cparams(("arbitrary", "arbitrary", "arbitrary")),
        name=name,
    )(proj, proj, proj, proj, proj, cs, sn)


def _merge_out_kernel(oh_ref, oa_ref, gate_ref, x_ref, wh_ref, wa_ref, wo_ref, gc_ref, h1_ref, n2_ref):
    ga = gate_ref[:, :D_MODEL].astype(F32)
    gb = gate_ref[:, D_MODEL:].astype(F32)
    yh = jnp.dot(oh_ref[...], wh_ref[...], preferred_element_type=F32)
    ya = jnp.dot(oa_ref[...], wa_ref[...], preferred_element_type=F32)
    merged = jax.nn.sigmoid(ga) * yh + jax.nn.sigmoid(gb) * ya
    h1 = x_ref[...] + jnp.dot(merged.astype(BF16), wo_ref[...], preferred_element_type=F32)
    h1_ref[...] = h1
    n2_ref[...] = _rms(h1, gc_ref[...]).astype(BF16)


def _merge_out(oh, oa, gates, x2d, wh, wa, wo, gc, *, tm, gate_block, name="merge_out"):
    n, d = x2d.shape

    def const(shape):
        return pl.BlockSpec(shape, lambda i: (0, 0), pipeline_mode=pl.Buffered(1))

    return pl.pallas_call(
        _merge_out_kernel,
        out_shape=(jax.ShapeDtypeStruct((n, d), F32), jax.ShapeDtypeStruct((n, d), BF16)),
        grid=(n // tm,),
        in_specs=[
            pl.BlockSpec((tm, HG_WIDTH), lambda i: (i, 0)),
            pl.BlockSpec((tm, ATT_KV_WIDTH), lambda i: (i, 0)),
            pl.BlockSpec((tm, 2 * d), lambda i: (i, gate_block)),
            pl.BlockSpec((tm, d), lambda i: (i, 0)),
            const((HG_WIDTH, d)), const((ATT_KV_WIDTH, d)), const((d, d)), const((1, d)),
        ],
        out_specs=(pl.BlockSpec((tm, d), lambda i: (i, 0)), pl.BlockSpec((tm, d), lambda i: (i, 0))),
        compiler_params=_cparams(("parallel",)),
        name=name,
    )(oh, oa, gates, x2d, wh, wa, wo, gc.reshape(1, d))


def _cross_kernel(n2_ref, h1_ref, kv_ref, wq_ref, wo_ref, gm_ref, wrh_ref, wrl_ref, br_ref,
                  h2_ref, n3_ref, idx_ref, tw_ref, rank_ref, cnt_ref, carry_ref, *, tm):
    @pl.when((pl.program_id(0) == 0) & (pl.program_id(1) == 0))
    def _():
        carry_ref[...] = jnp.zeros_like(carry_ref)

    nt_dims = (((1,), (1,)), ((), ()))
    scale = HEAD_DIM ** -0.5
    q = (jnp.dot(n2_ref[...], wq_ref[...], preferred_element_type=F32) * scale).astype(BF16)
    outs = []
    for hh in range(CROSS_HEADS):
        sl = slice(hh * HEAD_DIM, (hh + 1) * HEAD_DIM)
        kh = kv_ref[:, sl]
        vh = kv_ref[:, CROSS_WIDTH + hh * HEAD_DIM:CROSS_WIDTH + (hh + 1) * HEAD_DIM]
        s = lax.dot_general(q[:, sl], kh, nt_dims, preferred_element_type=F32)
        p = jnp.exp(s - jnp.max(s, axis=-1, keepdims=True))
        l = jnp.sum(p, axis=-1, keepdims=True)
        outs.append(jnp.dot(p.astype(BF16), vh, preferred_element_type=F32) / l)
    o = jnp.concatenate(outs, axis=-1).astype(BF16)
    h2 = h1_ref[...] + jnp.dot(o, wo_ref[...], preferred_element_type=F32)
    h2_ref[...] = h2
    n3 = _rms(h2, gm_ref[...])
    u = _pack_rows(n3)
    for j in range(MOE_SLAB):
        n3_ref[pl.ds(j, tm, stride=MOE_SLAB), :] = u[:, j * V7X_LANES:(j + 1) * V7X_LANES]

    n3h = n3.astype(BF16)
    n3l = (n3 - n3h.astype(F32)).astype(BF16)
    wrh = wrh_ref[...]
    logits = (jnp.dot(n3h, wrh, preferred_element_type=F32)
              + jnp.dot(n3l, wrh, preferred_element_type=F32)
              + jnp.dot(n3h, wrl_ref[...], preferred_element_type=F32)
              + br_ref[...])
    lane = lax.broadcasted_iota(I32, (tm, N_EXPERTS), 1).astype(F32)
    vals, idxs, hots = [], [], []
    cur = logits
    for _ in range(TOP_K):
        mx = jnp.max(cur, axis=-1, keepdims=True)
        ix = jnp.min(jnp.where(cur == mx, lane, float(N_EXPERTS)), axis=-1, keepdims=True)
        hot = lane == ix
        vals.append(mx)
        idxs.append(ix)
        hots.append(hot)
        cur = jnp.where(hot, -jnp.inf, cur)
    es = [jnp.exp(v - vals[0]) for v in vals]
    den = es[0] + es[1] + es[2] + es[3]
    col = lax.broadcasted_iota(I32, (tm, TOP_K), 1)

    def pack(cols):
        out = jnp.broadcast_to(cols[TOP_K - 1], (tm, TOP_K))
        for k in range(TOP_K - 2, -1, -1):
            out = jnp.where(col == k, cols[k], out)
        return out

    idx_ref[...] = pack(idxs).astype(I32)
    tw_ref[...] = pack([e / den for e in es])

    cmat = (hots[0] | hots[1] | hots[2] | hots[3]).astype(F32)
    rr = lax.broadcasted_iota(I32, (tm, tm), 0)
    cc = lax.broadcasted_iota(I32, (tm, tm), 1)
    tri = (cc < rr).astype(BF16)
    before = jnp.dot(tri, cmat.astype(BF16), preferred_element_type=F32) + carry_ref[...]
    ranks = [jnp.sum(jnp.where(hot, before, 0.0), axis=-1, keepdims=True) for hot in hots]
    rank_ref[...] = pack(ranks).astype(I32)
    carry = carry_ref[...] + jnp.sum(cmat, axis=0, keepdims=True)
    carry_ref[...] = carry
    cnt_ref[...] = carry


def _cross(n2, h1, kv, wq, wo, gm, wrh, wrl, br, *, batch, tm, name="cross"):
    n, d = h1.shape
    per_b = n // batch // tm
    mem_len = kv.shape[0] // batch

    def const(shape):
        return pl.BlockSpec(shape, lambda bi, i: (0, 0), pipeline_mode=pl.Buffered(1))

    def row(bi, i):
        return (bi * per_b + i, 0)

    return pl.pallas_call(
        functools.partial(_cross_kernel, tm=tm),
        out_shape=(
            jax.ShapeDtypeStruct((n, d), F32),
            jax.ShapeDtypeStruct((n * MOE_SLAB, V7X_LANES), U32),
            jax.ShapeDtypeStruct((n, TOP_K), I32),
            jax.ShapeDtypeStruct((n, TOP_K), F32),
            jax.ShapeDtypeStruct((n, TOP_K), I32),
            jax.ShapeDtypeStruct((1, N_EXPERTS), F32),
        ),
        grid=(batch, per_b),
        in_specs=[
            pl.BlockSpec((tm, d), row),
            pl.BlockSpec((tm, d), row),
            pl.BlockSpec((mem_len, 2 * CROSS_WIDTH), lambda bi, i: (bi, 0)),
            const((d, CROSS_WIDTH)), const((CROSS_WIDTH, d)), const((1, d)),
            const((d, N_EXPERTS)), const((d, N_EXPERTS)), const((1, N_EXPERTS)),
        ],
        out_specs=(
            pl.BlockSpec((tm, d), row),
            pl.BlockSpec((tm * MOE_SLAB, V7X_LANES), row),
            pl.BlockSpec((tm, TOP_K), row),
            pl.BlockSpec((tm, TOP_K), row),
            pl.BlockSpec((tm, TOP_K), row),
            pl.BlockSpec((1, N_EXPERTS), lambda bi, i: (0, 0)),
        ),
        scratch_shapes=[pltpu.VMEM((1, N_EXPERTS), F32)],
        compiler_params=_cparams(("arbitrary", "arbitrary")),
        name=name,
    )(n2, h1, kv, wq, wo, gm.reshape(1, d), wrh, wrl, br.reshape(1, N_EXPERTS))


_HI_MASK = 0xFFFF0000


def _pack_rows(x):
    half = x.shape[1] // 2
    lo = lax.bitcast_convert_type(x[:, :half].astype(BF16).astype(F32), U32) >> 16
    hi = lax.bitcast_convert_type(x[:, half:].astype(BF16).astype(F32), U32) & jnp.uint32(_HI_MASK)
    return lo | hi


def _unpack_rows(u):
    lo = lax.bitcast_convert_type(u << 16, F32)
    hi = lax.bitcast_convert_type(u & jnp.uint32(_HI_MASK), F32)
    return lo, hi


def _sc_dispatch(slots, n3p, *, rows_total, name="moe_dispatch_sc"):
    n = n3p.shape[0]
    per_w = n // (SC_CORES * SC_SUBCORES)
    mesh = plsc.VectorSubcoreMesh(core_axis_name="c", subcore_axis_name="s",
                                  num_cores=SC_CORES, num_subcores=SC_SUBCORES)

    nchunk = per_w // SC_CHUNK_TOKENS

    def body(slots_hbm, n3p_hbm, out_hbm, idx_bufs, row_bufs, load_sems, scat_sems):
        wid = lax.axis_index("s") * SC_CORES + lax.axis_index("c")

        def base(c):
            return pl.multiple_of(wid * per_w + c * SC_CHUNK_TOKENS, SC_CHUNK_TOKENS)

        def load(c):
            return pltpu.async_copy(n3p_hbm.at[pl.ds(base(c), SC_CHUNK_TOKENS)], row_bufs.at[c % 2],
                                    load_sems.at[c % 2])

        def scatter(c, k):
            idx = idx_bufs.at[(c % 2) * TOP_K + k]
            pltpu.sync_copy(slots_hbm.at[pl.ds(k * n + base(c), SC_CHUNK_TOKENS)], idx)
            return pltpu.async_copy(row_bufs.at[c % 2], out_hbm.at[idx], scat_sems.at[c % 2])

        loads = [None] * nchunk
        scats = [None] * nchunk
        loads[0] = load(0)
        for c in range(nchunk):
            if c + 1 < nchunk:
                if c >= 1:
                    for s in scats[c - 1]:
                        s.wait()
                loads[c + 1] = load(c + 1)
            loads[c].wait()
            scats[c] = [scatter(c, k) for k in range(TOP_K)]
        for c in range(max(nchunk - 2, 0), nchunk):
            for s in scats[c]:
                s.wait()

    return pl.kernel(
        body,
        out_type=jax.ShapeDtypeStruct((rows_total, MOE_SLAB, V7X_LANES), U32),
        mesh=mesh,
        scratch_types=[
            pltpu.VMEM((2 * TOP_K, SC_CHUNK_TOKENS), I32),
            pltpu.VMEM((2, SC_CHUNK_TOKENS, MOE_SLAB, V7X_LANES), U32),
            pltpu.SemaphoreType.DMA((2,)),
            pltpu.SemaphoreType.DMA((2,)),
        ],
        name=name,
    )(slots, n3p)


def _moe_kernel(te_ref, tv_ref, tstart_ref, trows_ref,
                xs_hbm, w1_ref, b1_ref, w2_ref, b2_ref, perm_ref, yb_hbm,
                stage, x16, acc, w1p, sems, *, nf):
    i = pl.program_id(0)
    f = pl.program_id(1)
    rows = trows_ref[i]
    start = tstart_ref[i]
    nchunk = rows // MOE_ROW_ALIGN
    chunk_slab_rows = MOE_ROW_ALIGN * MOE_SLAB
    half = 2 * V7X_LANES
    nhalf = 2 * MOE_TF // half

    def slot_rows(c):
        return pl.multiple_of((c & 1) * chunk_slab_rows, chunk_slab_rows)

    def hbm_rows(ref, c):
        r0 = pl.multiple_of((start + c * MOE_ROW_ALIGN) * MOE_SLAB, chunk_slab_rows)
        return ref.at[pl.ds(r0, chunk_slab_rows)]

    def in_copy(c):
        return pltpu.make_async_copy(hbm_rows(xs_hbm, c), stage.at[pl.ds(slot_rows(c), chunk_slab_rows)],
                                     sems.at[c & 1])

    def out_copy(c):
        return pltpu.make_async_copy(stage.at[pl.ds(slot_rows(c), chunk_slab_rows)], hbm_rows(yb_hbm, c),
                                     sems.at[c & 1])

    @pl.when((rows > 0) & (f == 0))
    def _():
        in_copy(0).start()

        def land(c, carry):
            @pl.when(c + 1 < nchunk)
            def _():
                in_copy(c + 1).start()

            in_copy(c).wait()
            r0 = pl.multiple_of(c * MOE_ROW_ALIGN, MOE_ROW_ALIGN)
            s0 = slot_rows(c)
            owned = lax.broadcasted_iota(I32, (MOE_ROW_ALIGN, V7X_LANES), 0) < tv_ref[i] - r0
            for j in range(MOE_SLAB):
                u = jnp.where(owned, stage[pl.ds(s0 + j, MOE_ROW_ALIGN, stride=MOE_SLAB), :], jnp.uint32(0))
                lo, hi = _unpack_rows(u)
                x16[pl.ds(r0, MOE_ROW_ALIGN), j * V7X_LANES:(j + 1) * V7X_LANES] = lo.astype(BF16)
                x16[pl.ds(r0, MOE_ROW_ALIGN), (MOE_SLAB + j) * V7X_LANES:(MOE_SLAB + j + 1) * V7X_LANES] = (
                    hi.astype(BF16))
            acc[pl.ds(r0, MOE_ROW_ALIGN), :] = jnp.zeros((MOE_ROW_ALIGN, D_MODEL), F32)
            return carry

        lax.fori_loop(0, nchunk, land, 0)

    @pl.when(rows > 0)
    def _():
        perm = perm_ref[...]
        for c in range(nhalf):
            wb = w1_ref[:, c * half:(c + 1) * half].astype(BF16)
            w1p[:, c * half:(c + 1) * half] = jnp.dot(wb, perm, preferred_element_type=F32).astype(BF16)
        w2 = w2_ref[...].astype(BF16)
        b1 = b1_ref[...]

        def mlp(c0, nc):
            r0 = pl.multiple_of(c0 * MOE_ROW_ALIGN, MOE_ROW_ALIGN)
            nr = nc * MOE_ROW_ALIGN
            x = x16[pl.ds(r0, nr), :]
            h = jnp.dot(x, w1p[...], preferred_element_type=F32) + b1
            parts = []
            for c in range(nhalf):
                hg = jnp.minimum(h[:, c * half:c * half + V7X_LANES], SWIGLU_LIMIT)
                hl = jnp.clip(h[:, c * half + V7X_LANES:(c + 1) * half], -SWIGLU_LIMIT, SWIGLU_LIMIT)
                parts.append(hg * jax.nn.sigmoid(SWIGLU_ALPHA * hg) * (hl + 1.0))
            a = jnp.concatenate(parts, axis=-1).astype(BF16)
            acc[pl.ds(r0, nr), :] += jnp.dot(a, w2, preferred_element_type=F32)

        nquad = nchunk // MOE_BLOCK_CHUNKS

        def quad(c, carry):
            mlp(c * MOE_BLOCK_CHUNKS, MOE_BLOCK_CHUNKS)
            return carry

        lax.fori_loop(0, nquad, quad, 0)
        size = MOE_BLOCK_CHUNKS // 2
        while size >= 1:
            done = nchunk & ~(2 * size - 1)

            @pl.when((nchunk & size) != 0)
            def _(done=done, size=size):
                mlp(done, size)

            size //= 2

    @pl.when((rows > 0) & (f == nf - 1))
    def _():
        def emit(c, carry):
            @pl.when(c >= 2)
            def _():
                out_copy(c - 2).wait()

            r0 = pl.multiple_of(c * MOE_ROW_ALIGN, MOE_ROW_ALIGN)
            s0 = slot_rows(c)
            u = _pack_rows(acc[pl.ds(r0, MOE_ROW_ALIGN), :] + b2_ref[...])
            for j in range(MOE_SLAB):
                stage[pl.ds(s0 + j, MOE_ROW_ALIGN, stride=MOE_SLAB), :] = u[:, j * V7X_LANES:(j + 1) * V7X_LANES]
            out_copy(c).start()
            return carry

        lax.fori_loop(0, nchunk, emit, 0)

        @pl.when(nchunk >= 2)
        def _():
            out_copy(nchunk - 2).wait()

        out_copy(nchunk - 1).wait()


def _moe(tile_e, tile_owned, tile_start, tile_rows, xs, w1, b1p, w2, b2, perm, *, name="moe_mlp"):
    nt = tile_e.shape[0]
    nf = D_FF // MOE_TF

    def w1_map(i, f, te, tv, ts, tr):
        return (te[i], 0, jnp.where(tv[i] > 0, f, nf - 1))

    def w2_map(i, f, te, tv, ts, tr):
        return (te[i], jnp.where(tv[i] > 0, f, nf - 1), 0)

    def b2_map(i, f, te, tv, ts, tr):
        return (te[i], 0, 0)

    grid_spec = pltpu.PrefetchScalarGridSpec(
        num_scalar_prefetch=4,
        grid=(nt, nf),
        in_specs=[
            pl.BlockSpec(memory_space=pl.ANY),
            pl.BlockSpec((None, D_MODEL, 2 * MOE_TF), w1_map),
            pl.BlockSpec((None, 1, 2 * MOE_TF), w1_map),
            pl.BlockSpec((None, MOE_TF, D_MODEL), w2_map),
            pl.BlockSpec((None, 1, D_MODEL), b2_map),
            pl.BlockSpec((2 * V7X_LANES, 2 * V7X_LANES), lambda i, f, te, tv, ts, tr: (0, 0)),
        ],
        out_specs=pl.BlockSpec(memory_space=pl.ANY),
        scratch_shapes=[
            pltpu.VMEM((2 * MOE_ROW_ALIGN * MOE_SLAB, V7X_LANES), U32),
            pltpu.VMEM((MOE_TMAX, D_MODEL), BF16),
            pltpu.VMEM((MOE_TMAX, D_MODEL), F32),
            pltpu.VMEM((D_MODEL, 2 * MOE_TF), BF16),
            pltpu.SemaphoreType.DMA((2,)),
        ],
    )
    return pl.pallas_call(
        functools.partial(_moe_kernel, nf=nf),
        out_shape=jax.ShapeDtypeStruct(xs.shape, xs.dtype),
        grid_spec=grid_spec,
        input_output_aliases={4: 0},
        compiler_params=_cparams(("arbitrary", "arbitrary")),
        name=name,
    )(tile_e, tile_owned, tile_start, tile_rows, xs, w1, b1p.reshape(N_EXPERTS, 1, 2 * D_FF), w2,
      b2.reshape(N_EXPERTS, 1, D_MODEL), perm)


def _sc_gather(slots, yb3, *, n_tok, name="moe_gather_sc"):
    per_w = n_tok // (SC_CORES * SC_SUBCORES)
    mesh = plsc.VectorSubcoreMesh(core_axis_name="c", subcore_axis_name="s",
                                  num_cores=SC_CORES, num_subcores=SC_SUBCORES)

    items = [(c, k) for c in range(per_w // SC_CHUNK_TOKENS) for k in range(TOP_K)]

    def body(slots_hbm, yb_hbm, out_hbm, idx_bufs, row_bufs, gather_sems, write_sems):
        wid = lax.axis_index("s") * SC_CORES + lax.axis_index("c")

        def offset(j):
            c, k = items[j]
            return pl.multiple_of(k * n_tok + wid * per_w + c * SC_CHUNK_TOKENS, SC_CHUNK_TOKENS)

        def gather(j):
            idx = idx_bufs.at[j % 2]
            pltpu.sync_copy(slots_hbm.at[pl.ds(offset(j), SC_CHUNK_TOKENS)], idx)
            return pltpu.async_copy(yb_hbm.at[idx], row_bufs.at[j % 2], gather_sems.at[j % 2])

        def write(j):
            return pltpu.async_copy(row_bufs.at[j % 2], out_hbm.at[pl.ds(offset(j), SC_CHUNK_TOKENS)],
                                    write_sems.at[j % 2])

        n_items = len(items)
        gathers = [None] * n_items
        writes = [None] * n_items
        gathers[0] = gather(0)
        for j in range(n_items):
            if j + 1 < n_items:
                if j >= 1:
                    writes[j - 1].wait()
                gathers[j + 1] = gather(j + 1)
            gathers[j].wait()
            writes[j] = write(j)
        for j in range(max(n_items - 2, 0), n_items):
            writes[j].wait()

    return pl.kernel(
        body,
        out_type=jax.ShapeDtypeStruct((TOP_K * n_tok, MOE_SLAB, V7X_LANES), U32),
        mesh=mesh,
        scratch_types=[
            pltpu.VMEM((2, SC_CHUNK_TOKENS), I32),
            pltpu.VMEM((2, SC_CHUNK_TOKENS, MOE_SLAB, V7X_LANES), U32),
            pltpu.SemaphoreType.DMA((2,)),
            pltpu.SemaphoreType.DMA((2,)),
        ],
        name=name,
    )(slots, yb3)


def _combine_kernel(yg_ref, h2_ref, tw_ref, gf_ref, o_ref, *, tm):
    tw = tw_ref[...]
    cols_lo, cols_hi = [], []
    for j in range(MOE_SLAB):
        clo = chi = None
        for k in range(TOP_K):
            lo, hi = _unpack_rows(yg_ref[k, pl.ds(j, tm, stride=MOE_SLAB), :])
            w = tw[:, k:k + 1]
            clo = w * lo if clo is None else clo + w * lo
            chi = w * hi if chi is None else chi + w * hi
        cols_lo.append(clo)
        cols_hi.append(chi)
    h3 = h2_ref[...] + jnp.concatenate(cols_lo + cols_hi, axis=-1)
    o_ref[...] = _rms(h3, gf_ref[...])


def _combine(yg, h2, tw, gf, *, tm, name="moe_combine"):
    n, d = h2.shape
    return pl.pallas_call(
        functools.partial(_combine_kernel, tm=tm),
        out_shape=jax.ShapeDtypeStruct((n, d), F32),
        grid=(n // tm,),
        in_specs=[
            pl.BlockSpec((TOP_K, tm * MOE_SLAB, V7X_LANES), lambda i: (0, i, 0)),
            pl.BlockSpec((tm, d), lambda i: (i, 0)),
            pl.BlockSpec((tm, TOP_K), lambda i: (i, 0)),
            pl.BlockSpec((1, d), lambda i: (0, 0)),
        ],
        out_specs=pl.BlockSpec((tm, d), lambda i: (i, 0)),
        compiler_params=_cparams(("parallel",)),
        name=name,
    )(yg, h2, tw, gf.reshape(1, d))


def _rope_tables(positions):
    b, s = positions.shape
    half = HEAD_DIM // 2
    inv_freq = 1.0 / (ROPE_THETA ** (jnp.arange(half, dtype=F32) / half))
    ang = positions.astype(F32)[..., None] * inv_freq
    cos = jnp.cos(ang)
    sin = jnp.sin(ang)
    cs = jnp.concatenate([cos, cos], axis=-1)
    sn = jnp.concatenate([-sin, sin], axis=-1)

    def deint(a):
        a = a.reshape(b, s // ATT_TILE, WIN_BLOCK, DIL_MAX, HEAD_DIM)
        return a.transpose(0, 1, 3, 2, 4)

    return deint(cs), deint(sn)


def _routing(counts, n_tok):
    cnt = counts.reshape(N_EXPERTS).astype(I32)
    padded = (cnt + MOE_ROW_ALIGN - 1) // MOE_ROW_ALIGN * MOE_ROW_ALIGN
    gstart = (jnp.cumsum(padded) - padded).astype(I32)
    rows_total = n_tok * TOP_K + N_EXPERTS * MOE_ROW_ALIGN

    def first_above(ends, q):
        return jnp.minimum(jnp.sum((ends[None, :] <= q[:, None]).astype(I32), axis=1), N_EXPERTS - 1)

    main_rows = jnp.minimum(padded, MOE_TMAX)
    main = (jnp.arange(N_EXPERTS, dtype=I32), jnp.minimum(cnt, MOE_TMAX).astype(I32), gstart.astype(I32),
            main_rows.astype(I32))

    over = padded - main_rows
    nt_e = (over + MOE_TMAX - 1) // MOE_TMAX
    tend = jnp.cumsum(nt_e)
    tstart = tend - nt_e
    n_over = tend[-1]
    ti = jnp.arange(rows_total // MOE_TMAX, dtype=I32)
    valid = ti < n_over
    tic = jnp.clip(ti, 0, jnp.maximum(n_over - 1, 0))
    te = first_above(tend, tic)
    local = tic - tstart[te]
    row0 = gstart[te] + (local + 1) * MOE_TMAX
    rows = jnp.where(valid, jnp.clip(over[te] - local * MOE_TMAX, 0, MOE_TMAX), 0)
    owned = jnp.where(valid, jnp.clip(cnt[te] - (local + 1) * MOE_TMAX, 0, MOE_TMAX), 0)
    overflow = (te, owned.astype(I32), row0.astype(I32), rows.astype(I32))
    return gstart, rows_total, main, overflow, n_over > 0


def kernel(x, mem, positions, norm_mix_g, w_in, lb_raw, hgrn_norm_g, w_br_hgrn, w_br_attn, w_out,
           norm_cross_g, norm_mem_g, w_cq, w_ckv, w_co, norm_moe_g, w_router, b_router,
           w_mlp1, b_mlp1, w_mlp2, b_mlp2, norm_final_g):
    bsz, seq, d = x.shape
    assert w_in.shape[0] == 1 and d == D_MODEL and seq % ATT_TILE == 0
    n_tok = bsz * seq
    lower_bounds = jnp.cumsum(jax.nn.softmax(lb_raw.astype(F32), axis=0), axis=0)
    cs, sn = _rope_tables(positions)

    wl = w_in[0]
    c_h = 4 * HG_WIDTH
    c_a = c_h + ATT_Q_WIDTH + 2 * ATT_KV_WIDTH
    tn = 1024
    x2d = x.reshape(n_tok, d)
    n_h, n_a, n_g = c_h // tn, (c_a - c_h) // tn, 2 * D_MODEL // tn
    proj = _in_proj(x2d, norm_mix_g[0], wl, tm=1024, tn=tn, name="in_proj",
                    col_block=lambda j: jnp.where(j < n_h, j, jnp.where(j < n_h + n_g, j + n_a, j - n_g)))
    proj3 = proj.reshape(bsz, seq, -1)

    o_h = _hgrn(proj3, lower_bounds[0], hgrn_norm_g[0], ts=1024)
    o_a = _dilated_attn(proj3, cs, sn, q_block=(c_h + 2 * D_MODEL) // HEAD_DIM)

    h1, n2 = _merge_out(o_h.reshape(n_tok, HG_WIDTH), o_a.reshape(n_tok, ATT_KV_WIDTH), proj, x2d,
                        w_br_hgrn[0].astype(BF16), w_br_attn[0].astype(BF16), w_out[0].astype(BF16),
                        norm_cross_g[0], tm=256, gate_block=c_h // (2 * D_MODEL))

    mem2d = mem.reshape(-1, d)
    kv = _in_proj(mem2d, norm_mem_g, w_ckv[0], tm=mem2d.shape[0], tn=2 * CROSS_WIDTH, name="mem_kv")
    wrh = w_router[0].astype(BF16)
    wrl = (w_router[0] - wrh.astype(F32)).astype(BF16)
    h2, n3, top_e, top_w, rank, counts = _cross(
        n2, h1, kv, w_cq[0].astype(BF16), w_co[0].astype(BF16), norm_moe_g[0], wrh, wrl, b_router[0],
        batch=bsz, tm=512)

    gstart, rows_total, main_tiles, over_tiles, has_over = _routing(counts, n_tok)
    hot = top_e[..., None] == jnp.arange(N_EXPERTS, dtype=I32)
    slots = (rank + jnp.sum(jnp.where(hot, gstart, 0), axis=-1)).T.reshape(-1)
    xs = _sc_dispatch(slots, n3.reshape(n_tok, MOE_SLAB, V7X_LANES), rows_total=rows_total)
    xs = xs.reshape(rows_total * MOE_SLAB, V7X_LANES)

    ii = jnp.arange(2 * V7X_LANES)
    src = jnp.where(ii < V7X_LANES, 2 * ii, 2 * (ii - V7X_LANES) + 1)
    perm = (ii[:, None] == src[None, :]).astype(BF16)
    b1p = b_mlp1[0].reshape(N_EXPERTS, -1, V7X_LANES, 2).transpose(0, 1, 3, 2).reshape(N_EXPERTS, 2 * D_FF)
    def run_moe(tiles, rows_buf, name):
        return _moe(*tiles, rows_buf, w_mlp1[0], b1p, w_mlp2[0], b_mlp2[0], perm, name=name)

    yb = run_moe(main_tiles, xs, "moe_mlp")
    yb = lax.cond(has_over, lambda rows_buf: run_moe(over_tiles, rows_buf, "moe_mlp_overflow"),
                  lambda rows_buf: rows_buf, yb)

    yg = _sc_gather(slots, yb.reshape(rows_total, MOE_SLAB, V7X_LANES), n_tok=n_tok)
    out = _combine(yg.reshape(TOP_K, n_tok * MOE_SLAB, V7X_LANES), h2, top_w, norm_final_g, tm=COMBINE_TOKENS)
    return out.reshape(bsz, seq, d)
```

```python
import functools

import jax
import jax.numpy as jnp
from jax import lax
from jax.experimental import pallas as pl
from jax.experimental.pallas import tpu as pltpu
from jax.experimental.pallas import tpu_sc as plsc

F32 = jnp.float32
BF16 = jnp.bfloat16
I32 = jnp.int32
U32 = jnp.uint32

D_MODEL = 2048
HEAD_DIM = 128
HG_HEADS = 8
HG_WIDTH = HG_HEADS * HEAD_DIM
N_KV_HEADS = 8
N_GROUPS = 3
ATT_Q_WIDTH = N_GROUPS * N_KV_HEADS * HEAD_DIM
ATT_KV_WIDTH = N_KV_HEADS * HEAD_DIM
WIN_BLOCK = 128
ROPE_THETA = 10000.0
CROSS_HEADS = 4
CROSS_WIDTH = CROSS_HEADS * HEAD_DIM
N_EXPERTS = 32
TOP_K = 4
D_FF = D_MODEL
SWIGLU_ALPHA = 1.702
SWIGLU_LIMIT = 7.0
NORM_EPS = 1e-6

V7X_LANES = 128
V7X_VMEM_LIMIT_BYTES = 56 * 1024 * 1024

DIL_MAX = 16
ATT_TILE = DIL_MAX * WIN_BLOCK
ATT_UNROLL = 16
HG_BLOCK = 16
HG_GROUP = 32
NEG_BIG = -1e30

MOE_TMAX = 1536
MOE_ROW_ALIGN = 128
MOE_BLOCK_CHUNKS = 8
MOE_TF = 512
MOE_SLAB = D_MODEL // 2 // V7X_LANES
SC_CORES = 2
SC_SUBCORES = 16
SC_CHUNK_TOKENS = 32
COMBINE_TOKENS = 512


def _rms(x, g):
    ms = jnp.mean(x * x, axis=-1, keepdims=True)
    return x * lax.rsqrt(ms + NORM_EPS) * g


def _cparams(sem, vmem=V7X_VMEM_LIMIT_BYTES):
    return pltpu.CompilerParams(dimension_semantics=sem, vmem_limit_bytes=vmem)


def _in_proj_kernel(x_ref, g_ref, w_ref, o_ref, xn_ref):
    @pl.when(pl.program_id(1) == 0)
    def _():
        xn_ref[...] = _rms(x_ref[...], g_ref[...]).astype(BF16)

    o_ref[...] = jnp.dot(xn_ref[...], w_ref[...].astype(BF16), preferred_element_type=F32).astype(o_ref.dtype)


def _in_proj(x2d, g, w, *, tm, tn, name, col_block=lambda j: j, ncols=None):
    n, d = x2d.shape
    wc = w.shape[1] if ncols is None else ncols
    return pl.pallas_call(
        _in_proj_kernel,
        out_shape=jax.ShapeDtypeStruct((n, wc), BF16),
        grid=(n // tm, wc // tn),
        in_specs=[
            pl.BlockSpec((tm, d), lambda i, j: (i, 0)),
            pl.BlockSpec((1, d), lambda i, j: (0, 0)),
            pl.BlockSpec((d, tn), lambda i, j: (0, col_block(j))),
        ],
        out_specs=pl.BlockSpec((tm, tn), lambda i, j: (i, j)),
        scratch_shapes=[pltpu.VMEM((tm, d), BF16)],
        compiler_params=_cparams(("parallel", "arbitrary")),
        name=name,
    )(x2d, g.reshape(1, d), w)


def _hgrn_kernel(q_ref, f_ref, i_ref, g_ref, lb_ref, gn_ref, o_ref, st_ref, kin_s, b_s, v_s, *, ts):
    @pl.when(pl.program_id(2) == 0)
    def _():
        st_ref[...] = jnp.zeros_like(st_ref)

    lb = lb_ref[...]
    oml = 1.0 - lb
    gn = gn_ref[...]
    half = HG_BLOCK // 2
    row = lax.broadcasted_iota(I32, (HG_BLOCK, HEAD_DIM), 0)
    row8 = lax.broadcasted_iota(I32, (half, HEAD_DIM), 0)
    nt_dims = (((1,), (1,)), ((), ()))
    tn_dims = (((0,), (0,)), ((), ()))

    def front(g, t0):
        sl = pl.ds(t0 + g * HG_BLOCK, HG_BLOCK)
        q = q_ref[sl, :].astype(F32)
        hf = f_ref[sl, :].astype(F32)
        v = i_ref[sl, :].astype(F32)
        kin = oml * jax.nn.sigmoid(-hf)
        b = jnp.log2(lb + oml * jax.nn.sigmoid(hf))
        for sh in (1, 2, 4, 8):
            b = b + jnp.where(row >= sh, pltpu.roll(b, sh, 0), 0.0)
        kin_s[g] = kin
        b_s[g] = b
        v_s[g] = v
        q_lo, q_hi = q[:half], q[half:]
        b_lo, b_hi = b[:half], b[half:]
        o_lo = jnp.zeros((half, HEAD_DIM), F32)
        o_hi = jnp.zeros((half, HEAD_DIM), F32)
        for s in range(HG_BLOCK):
            ks = kin_s[g, s:s + 1, :]
            bs = b_s[g, s:s + 1, :]
            vs = v_s[g, s:s + 1, :]
            if s < half:
                w = q_lo * ks * jnp.exp2(b_lo - bs)
                if s > 0:
                    w = jnp.where(row8 >= s, w, 0.0)
                o_lo = o_lo + jnp.sum(w, axis=-1, keepdims=True) * vs
                w = q_hi * ks * jnp.exp2(b_hi - bs)
            else:
                w = q_hi * ks * jnp.exp2(b_hi - bs)
                if s > half:
                    w = jnp.where(row8 >= s - half, w, 0.0)
            o_hi = o_hi + jnp.sum(w, axis=-1, keepdims=True) * vs
        bl = b_s[g, HG_BLOCK - 1:HG_BLOCK, :]
        qd = (q * jnp.exp2(b)).astype(BF16)
        kd = (kin * jnp.exp2(bl - b)).astype(BF16)
        upd = lax.dot_general(v.astype(BF16), kd, tn_dims, preferred_element_type=F32)
        return jnp.concatenate([o_lo, o_hi], axis=0), qd, upd, jnp.exp2(bl)

    def body(i, carry):
        t0 = pl.multiple_of(i * (HG_GROUP * HG_BLOCK), HG_GROUP * HG_BLOCK)
        fronts = [front(g, t0) for g in range(HG_GROUP)]
        st = st_ref[...]
        for g, (o_diag, qd, upd, dec) in enumerate(fronts):
            o = o_diag + lax.dot_general(qd, st.astype(BF16), nt_dims, preferred_element_type=F32)
            st = st * dec + upd
            sl = pl.ds(t0 + g * HG_BLOCK, HG_BLOCK)
            hg = g_ref[sl, :].astype(F32)
            o_ref[sl, :] = (_rms(o, gn) * (hg * jax.nn.sigmoid(hg))).astype(o_ref.dtype)
        st_ref[...] = st
        return carry

    lax.fori_loop(0, ts // (HG_GROUP * HG_BLOCK), body, 0)


def _hgrn(proj_h, lb, gn, *, ts, name="hgrn"):
    b, s, _ = proj_h.shape
    h = HG_HEADS

    def spec(off):
        return pl.BlockSpec((None, ts, HEAD_DIM), lambda bi, hi, si: (bi, si, off + hi))

    vec = pl.BlockSpec((1, HEAD_DIM), lambda bi, hi, si: (0, hi))
    return pl.pallas_call(
        functools.partial(_hgrn_kernel, ts=ts),
        out_shape=jax.ShapeDtypeStruct((b, s, HG_WIDTH), BF16),
        grid=(b, h, s // ts),
        in_specs=[spec(0), spec(h), spec(2 * h), spec(3 * h), vec, vec],
        out_specs=pl.BlockSpec((None, ts, HEAD_DIM), lambda bi, hi, si: (bi, si, hi)),
        scratch_shapes=[
            pltpu.VMEM((HEAD_DIM, HEAD_DIM), F32),
            pltpu.VMEM((HG_GROUP, HG_BLOCK, HEAD_DIM), F32),
            pltpu.VMEM((HG_GROUP, HG_BLOCK, HEAD_DIM), F32),
            pltpu.VMEM((HG_GROUP, HG_BLOCK, HEAD_DIM), F32),
        ],
        compiler_params=_cparams(("parallel", "parallel", "arbitrary")),
        name=name,
    )(proj_h, proj_h, proj_h, proj_h, lb.reshape(1, HG_WIDTH), gn.reshape(1, HG_WIDTH))


def _attn_bias(kind):
    rq = lax.broadcasted_iota(I32, (WIN_BLOCK, 2 * WIN_BLOCK), 0)
    ck = lax.broadcasted_iota(I32, (WIN_BLOCK, 2 * WIN_BLOCK), 1)
    if kind == 2:
        dist = rq + WIN_BLOCK - ck
        first = ck < WIN_BLOCK
    elif kind == 1:
        dist = 4 * ((rq & 31) - (ck & 63) + 32) + ((rq >> 5) - (ck >> 6))
        first = (ck & 63) < 32
    else:
        dist = 16 * ((rq & 7) - (ck & 15) + 8) + ((rq >> 3) - (ck >> 4))
        first = (ck & 15) < 8
    valid = (dist >= 0) & (dist <= WIN_BLOCK)
    return (jnp.where(valid, 0.0, NEG_BIG).astype(F32),
            jnp.where(valid & jnp.logical_not(first), 0.0, NEG_BIG).astype(F32))


def _attn_kernel(q0_ref, q1_ref, q2_ref, k_ref, v_ref, cs_ref, sn_ref, o_ref,
                 qr, kext, vext, acc, mrun, lrun, bias, onat, u0, u1, u2, u3, u4):
    ti = pl.program_id(2)
    wb = WIN_BLOCK
    scale = HEAD_DIM ** -0.5

    ubufs = (u0, u1, u2, u3, u4)
    for src, ub in zip((q0_ref, q1_ref, q2_ref, k_ref, v_ref), ubufs):
        ub[...] = pltpu.bitcast(src[...], U32)

    @pl.when(ti == 0)
    def _():
        kext[:, 0:wb, :] = jnp.zeros((DIL_MAX, wb, HEAD_DIM), F32)
        vext[:, 0:wb, :] = jnp.zeros((DIL_MAX, wb, HEAD_DIM), F32)

    @pl.when(ti > 0)
    def _():
        kext[:, 0:wb, :] = kext[:, wb:2 * wb, :]
        vext[:, 0:wb, :] = vext[:, wb:2 * wb, :]

    @pl.when((pl.program_id(0) == 0) & (pl.program_id(1) == 0) & (ti == 0))
    def _():
        for kind in range(N_GROUPS):
            full, nofirst = _attn_bias(kind)
            bias[2 * kind] = full
            bias[2 * kind + 1] = nofirst

    def rope_body(rp, carry):
        rows = pl.ds(rp, wb, stride=DIL_MAX // 2)
        parts = [_unpack_rows(ub[rows, :]) for ub in ubufs]
        for par in range(2):
            r = 2 * rp + par
            cs = cs_ref[r]
            sn = sn_ref[r]
            for g in range(N_GROUPS):
                q = parts[g][par]
                qr[g, r] = (q * cs + pltpu.roll(q, HEAD_DIM // 2, 1) * sn) * scale
            k = parts[N_GROUPS][par]
            kext[r, wb:2 * wb, :] = k * cs + pltpu.roll(k, HEAD_DIM // 2, 1) * sn
            vext[r, wb:2 * wb, :] = parts[N_GROUPS + 1][par]
            acc[r] = jnp.zeros((wb, HEAD_DIM), F32)
            mrun[r] = jnp.full((wb, HEAD_DIM), NEG_BIG, F32)
            lrun[r] = jnp.zeros((wb, HEAD_DIM), F32)
        return carry

    lax.fori_loop(0, DIL_MAX // 2, rope_body, 0)

    nt_dims = (((1,), (1,)), ((), ()))

    def block(qb, kb, vb, bias_blk):
        s = lax.dot_general(qb.astype(BF16), kb.astype(BF16), nt_dims, preferred_element_type=F32)
        s = s + bias_blk
        m = jnp.max(s, axis=-1, keepdims=True)
        p = jnp.exp(s - m)
        l = jnp.sum(p, axis=-1, keepdims=True)
        n = jnp.dot(p.astype(BF16), vb.astype(BF16), preferred_element_type=F32)
        return n, jnp.broadcast_to(m, (wb, HEAD_DIM)), jnp.broadcast_to(l, (wb, HEAD_DIM))

    def merge(r, rows, n, m, l):
        m_old = mrun[r, rows, :]
        m_new = jnp.maximum(m_old, m)
        a = jnp.exp(m_old - m_new)
        bb = jnp.exp(m - m_new)
        acc[r, rows, :] = acc[r, rows, :] * a + n * bb
        lrun[r, rows, :] = lrun[r, rows, :] * a + l * bb
        mrun[r, rows, :] = m_new

    first_tile = jnp.where(ti == 0, 1, 0)


    def g2_body(i, carry):
        rs = [i * ATT_UNROLL + u for u in range(ATT_UNROLL)]
        res = [block(qr[2, r], kext[r], vext[r], bias[4 + first_tile]) for r in rs]
        for r, (n, m, l) in zip(rs, res):
            merge(r, pl.ds(0, wb), n, m, l)
        return carry

    lax.fori_loop(0, DIL_MAX // ATT_UNROLL, g2_body, 0)

    def g1_body(i, carry):
        res = []
        for u in range(ATT_UNROLL // 4):
            mb = i * (ATT_UNROLL // 4) + u
            q0 = pl.multiple_of(32 * mb, 32)
            k0 = pl.multiple_of(96 + 32 * mb, 32)
            use_first = jnp.where(mb == 0, first_tile, 0)
            for r4 in range(4):
                qb = jnp.concatenate([qr[1, r4 + 4 * j, pl.ds(q0, 32), :] for j in range(4)], axis=0)
                kb = jnp.concatenate([kext[r4 + 4 * j, pl.ds(k0, 64), :] for j in range(4)], axis=0)
                vb = jnp.concatenate([vext[r4 + 4 * j, pl.ds(k0, 64), :] for j in range(4)], axis=0)
                res.append((r4, q0, block(qb, kb, vb, bias[2 + use_first])))
        for r4, q0, (n, m, l) in res:
            for j in range(4):
                sl = slice(32 * j, 32 * (j + 1))
                merge(r4 + 4 * j, pl.ds(q0, 32), n[sl], m[sl], l[sl])
        return carry

    lax.fori_loop(0, DIL_MAX // ATT_UNROLL, g1_body, 0)

    def g0_body(i, carry):
        res = []
        for u in range(ATT_UNROLL):
            mb = i * ATT_UNROLL + u
            q0 = pl.multiple_of(8 * mb, 8)
            k0 = pl.multiple_of(120 + 8 * mb, 8)
            qb = jnp.concatenate([qr[0, r, pl.ds(q0, 8), :] for r in range(DIL_MAX)], axis=0)
            kb = jnp.concatenate([kext[r, pl.ds(k0, 16), :] for r in range(DIL_MAX)], axis=0)
            vb = jnp.concatenate([vext[r, pl.ds(k0, 16), :] for r in range(DIL_MAX)], axis=0)
            use_first = jnp.where(mb == 0, first_tile, 0)
            res.append((q0, block(qb, kb, vb, bias[use_first])))
        for q0, (n, m, l) in res:
            for r in range(DIL_MAX):
                sl = slice(8 * r, 8 * (r + 1))
                merge(r, pl.ds(q0, 8), n[sl], m[sl], l[sl])
        return carry

    lax.fori_loop(0, DIL_MAX // ATT_UNROLL, g0_body, 0)

    for r in range(DIL_MAX):
        onat[pl.ds(r, wb, stride=DIL_MAX), :] = acc[r] / lrun[r]
    o_ref[...] = onat[...].astype(o_ref.dtype)


def _dilated_attn(proj, cs, sn, *, q_block, name="dilated_attn"):
    b, s, _ = proj.shape
    nt = s // ATT_TILE
    h = N_KV_HEADS

    def spec(off):
        return pl.BlockSpec((None, ATT_TILE, HEAD_DIM), lambda bi, hi, ti: (bi, ti, q_block + off + hi))

    tab = pl.BlockSpec((None, None, DIL_MAX, WIN_BLOCK, HEAD_DIM), lambda bi, hi, ti: (bi, ti, 0, 0, 0))
    ubuf = pltpu.VMEM((ATT_TILE // 2, HEAD_DIM), U32)
    return pl.pallas_call(
        _attn_kernel,
        out_shape=jax.ShapeDtypeStruct((b, nt * ATT_TILE, ATT_KV_WIDTH), BF16),
        grid=(b, h, nt),
        in_specs=[spec(0), spec(h), spec(2 * h), spec(3 * h), spec(4 * h), tab, tab],
        out_specs=pl.BlockSpec((None, ATT_TILE, HEAD_DIM), lambda bi, hi, ti: (bi, ti, hi)),
        scratch_shapes=[
            pltpu.VMEM((N_GROUPS, DIL_MAX, WIN_BLOCK, HEAD_DIM), F32),
            pltpu.VMEM((DIL_MAX, 2 * WIN_BLOCK, HEAD_DIM), F32),
            pltpu.VMEM((DIL_MAX, 2 * WIN_BLOCK, HEAD_DIM), F32),
            pltpu.VMEM((DIL_MAX, WIN_BLOCK, HEAD_DIM), F32),
            pltpu.VMEM((DIL_MAX, WIN_BLOCK, HEAD_DIM), F32),
            pltpu.VMEM((DIL_MAX, WIN_BLOCK, HEAD_DIM), F32),
            pltpu.VMEM((2 * N_GROUPS, WIN_BLOCK, 2 * WIN_BLOCK), F32),
            pltpu.VMEM((ATT_TILE, HEAD_DIM), F32),
            ubuf, ubuf, ubuf, ubuf, ubuf,
        ],
        compiler_params=_cparams(("arbitrary", "arbitrary", "arbitrary")),
        name=name,
    )(proj, proj, proj, proj, proj, cs, sn)


def _merge_out_kernel(oh_ref, oa_ref, gate_ref, x_ref, wh_ref, wa_ref, wo_ref, gc_ref, h1_ref, n2_ref):
    ga = gate_ref[:, :D_MODEL].astype(F32)
    gb = gate_ref[:, D_MODEL:].astype(F32)
    yh = jnp.dot(oh_ref[...], wh_ref[...], preferred_element_type=F32)
    ya = jnp.dot(oa_ref[...], wa_ref[...], preferred_element_type=F32)
    merged = jax.nn.sigmoid(ga) * yh + jax.nn.sigmoid(gb) * ya
    h1 = x_ref[...] + jnp.dot(merged.astype(BF16), wo_ref[...], preferred_element_type=F32)
    h1_ref[...] = h1
    n2_ref[...] = _rms(h1, gc_ref[...]).astype(BF16)


def _merge_out(oh, oa, gates, x2d, wh, wa, wo, gc, *, tm, gate_block, name="merge_out"):
    n, d = x2d.shape

    def const(shape):
        return pl.BlockSpec(shape, lambda i: (0, 0), pipeline_mode=pl.Buffered(1))

    return pl.pallas_call(
        _merge_out_kernel,
        out_shape=(jax.ShapeDtypeStruct((n, d), F32), jax.ShapeDtypeStruct((n, d), BF16)),
        grid=(n // tm,),
        in_specs=[
            pl.BlockSpec((tm, HG_WIDTH), lambda i: (i, 0)),
            pl.BlockSpec((tm, ATT_KV_WIDTH), lambda i: (i, 0)),
            pl.BlockSpec((tm, 2 * d), lambda i: (i, gate_block)),
            pl.BlockSpec((tm, d), lambda i: (i, 0)),
            const((HG_WIDTH, d)), const((ATT_KV_WIDTH, d)), const((d, d)), const((1, d)),
        ],
        out_specs=(pl.BlockSpec((tm, d), lambda i: (i, 0)), pl.BlockSpec((tm, d), lambda i: (i, 0))),
        compiler_params=_cparams(("parallel",)),
        name=name,
    )(oh, oa, gates, x2d, wh, wa, wo, gc.reshape(1, d))


def _cross_kernel(n2_ref, h1_ref, kv_ref, wq_ref, wo_ref, gm_ref, wrh_ref, wrl_ref, br_ref,
                  h2_ref, n3_ref, idx_ref, tw_ref, rank_ref, cnt_ref, carry_ref, *, tm):
    @pl.when((pl.program_id(0) == 0) & (pl.program_id(1) == 0))
    def _():
        carry_ref[...] = jnp.zeros_like(carry_ref)

    nt_dims = (((1,), (1,)), ((), ()))
    scale = HEAD_DIM ** -0.5
    q = (jnp.dot(n2_ref[...], wq_ref[...], preferred_element_type=F32) * scale).astype(BF16)
    outs = []
    for hh in range(CROSS_HEADS):
        sl = slice(hh * HEAD_DIM, (hh + 1) * HEAD_DIM)
        kh = kv_ref[:, sl]
        vh = kv_ref[:, CROSS_WIDTH + hh * HEAD_DIM:CROSS_WIDTH + (hh + 1) * HEAD_DIM]
        s = lax.dot_general(q[:, sl], kh, nt_dims, preferred_element_type=F32)
        p = jnp.exp(s - jnp.max(s, axis=-1, keepdims=True))
        l = jnp.sum(p, axis=-1, keepdims=True)
        outs.append(jnp.dot(p.astype(BF16), vh, preferred_element_type=F32) / l)
    o = jnp.concatenate(outs, axis=-1).astype(BF16)
    h2 = h1_ref[...] + jnp.dot(o, wo_ref[...], preferred_element_type=F32)
    h2_ref[...] = h2
    n3 = _rms(h2, gm_ref[...])
    u = _pack_rows(n3)
    for j in range(MOE_SLAB):
        n3_ref[pl.ds(j, tm, stride=MOE_SLAB), :] = u[:, j * V7X_LANES:(j + 1) * V7X_LANES]

    n3h = n3.astype(BF16)
    n3l = (n3 - n3h.astype(F32)).astype(BF16)
    wrh = wrh_ref[...]
    logits = (jnp.dot(n3h, wrh, preferred_element_type=F32)
              + jnp.dot(n3l, wrh, preferred_element_type=F32)
              + jnp.dot(n3h, wrl_ref[...], preferred_element_type=F32)
              + br_ref[...])
    lane = lax.broadcasted_iota(I32, (tm, N_EXPERTS), 1).astype(F32)
    vals, idxs, hots = [], [], []
    cur = logits
    for _ in range(TOP_K):
        mx = jnp.max(cur, axis=-1, keepdims=True)
        ix = jnp.min(jnp.where(cur == mx, lane, float(N_EXPERTS)), axis=-1, keepdims=True)
        hot = lane == ix
        vals.append(mx)
        idxs.append(ix)
        hots.append(hot)
        cur = jnp.where(hot, -jnp.inf, cur)
    es = [jnp.exp(v - vals[0]) for v in vals]
    den = es[0] + es[1] + es[2] + es[3]
    col = lax.broadcasted_iota(I32, (tm, TOP_K), 1)

    def pack(cols):
        out = jnp.broadcast_to(cols[TOP_K - 1], (tm, TOP_K))
        for k in range(TOP_K - 2, -1, -1):
            out = jnp.where(col == k, cols[k], out)
        return out

    idx_ref[...] = pack(idxs).astype(I32)
    tw_ref[...] = pack([e / den for e in es])

    cmat = (hots[0] | hots[1] | hots[2] | hots[3]).astype(F32)
    rr = lax.broadcasted_iota(I32, (tm, tm), 0)
    cc = lax.broadcasted_iota(I32, (tm, tm), 1)
    tri = (cc < rr).astype(BF16)
    before = jnp.dot(tri, cmat.astype(BF16), preferred_element_type=F32) + carry_ref[...]
    ranks = [jnp.sum(jnp.where(hot, before, 0.0), axis=-1, keepdims=True) for hot in hots]
    rank_ref[...] = pack(ranks).astype(I32)
    carry = carry_ref[...] + jnp.sum(cmat, axis=0, keepdims=True)
    carry_ref[...] = carry
    cnt_ref[...] = carry


def _cross(n2, h1, kv, wq, wo, gm, wrh, wrl, br, *, batch, tm, name="cross"):
    n, d = h1.shape
    per_b = n // batch // tm
    mem_len = kv.shape[0] // batch

    def const(shape):
        return pl.BlockSpec(shape, lambda bi, i: (0, 0), pipeline_mode=pl.Buffered(1))

    def row(bi, i):
        return (bi * per_b + i, 0)

    return pl.pallas_call(
        functools.partial(_cross_kernel, tm=tm),
        out_shape=(
            jax.ShapeDtypeStruct((n, d), F32),
            jax.ShapeDtypeStruct((n * MOE_SLAB, V7X_LANES), U32),
            jax.ShapeDtypeStruct((n, TOP_K), I32),
            jax.ShapeDtypeStruct((n, TOP_K), F32),
            jax.ShapeDtypeStruct((n, TOP_K), I32),
            jax.ShapeDtypeStruct((1, N_EXPERTS), F32),
        ),
        grid=(batch, per_b),
        in_specs=[
            pl.BlockSpec((tm, d), row),
            pl.BlockSpec((tm, d), row),
            pl.BlockSpec((mem_len, 2 * CROSS_WIDTH), lambda bi, i: (bi, 0)),
            const((d, CROSS_WIDTH)), const((CROSS_WIDTH, d)), const((1, d)),
            const((d, N_EXPERTS)), const((d, N_EXPERTS)), const((1, N_EXPERTS)),
        ],
        out_specs=(
            pl.BlockSpec((tm, d), row),
            pl.BlockSpec((tm * MOE_SLAB, V7X_LANES), row),
            pl.BlockSpec((tm, TOP_K), row),
            pl.BlockSpec((tm, TOP_K), row),
            pl.BlockSpec((tm, TOP_K), row),
            pl.BlockSpec((1, N_EXPERTS), lambda bi, i: (0, 0)),
        ),
        scratch_shapes=[pltpu.VMEM((1, N_EXPERTS), F32)],
        compiler_params=_cparams(("arbitrary", "arbitrary")),
        name=name,
    )(n2, h1, kv, wq, wo, gm.reshape(1, d), wrh, wrl, br.reshape(1, N_EXPERTS))


_HI_MASK = 0xFFFF0000


def _pack_rows(x):
    half = x.shape[1] // 2
    lo = lax.bitcast_convert_type(x[:, :half].astype(BF16).astype(F32), U32) >> 16
    hi = lax.bitcast_convert_type(x[:, half:].astype(BF16).astype(F32), U32) & jnp.uint32(_HI_MASK)
    return lo | hi


def _unpack_rows(u):
    lo = lax.bitcast_convert_type(u << 16, F32)
    hi = lax.bitcast_convert_type(u & jnp.uint32(_HI_MASK), F32)
    return lo, hi


def _sc_dispatch(slots, n3p, *, rows_total, name="moe_dispatch_sc"):
    n = n3p.shape[0]
    per_w = n // (SC_CORES * SC_SUBCORES)
    mesh = plsc.VectorSubcoreMesh(core_axis_name="c", subcore_axis_name="s",
                                  num_cores=SC_CORES, num_subcores=SC_SUBCORES)

    nchunk = per_w // SC_CHUNK_TOKENS

    def body(slots_hbm, n3p_hbm, out_hbm, idx_bufs, row_bufs, load_sems, scat_sems):
        wid = lax.axis_index("s") * SC_CORES + lax.axis_index("c")

        def base(c):
            return pl.multiple_of(wid * per_w + c * SC_CHUNK_TOKENS, SC_CHUNK_TOKENS)

        def load(c):
            return pltpu.async_copy(n3p_hbm.at[pl.ds(base(c), SC_CHUNK_TOKENS)], row_bufs.at[c % 2],
                                    load_sems.at[c % 2])

        def scatter(c, k):
            idx = idx_bufs.at[(c % 2) * TOP_K + k]
            pltpu.sync_copy(slots_hbm.at[pl.ds(k * n + base(c), SC_CHUNK_TOKENS)], idx)
            return pltpu.async_copy(row_bufs.at[c % 2], out_hbm.at[idx], scat_sems.at[c % 2])

        loads = [None] * nchunk
        scats = [None] * nchunk
        loads[0] = load(0)
        for c in range(nchunk):
            if c + 1 < nchunk:
                if c >= 1:
                    for s in scats[c - 1]:
                        s.wait()
                loads[c + 1] = load(c + 1)
            loads[c].wait()
            scats[c] = [scatter(c, k) for k in range(TOP_K)]
        for c in range(max(nchunk - 2, 0), nchunk):
            for s in scats[c]:
                s.wait()

    return pl.kernel(
        body,
        out_type=jax.ShapeDtypeStruct((rows_total, MOE_SLAB, V7X_LANES), U32),
        mesh=mesh,
        scratch_types=[
            pltpu.VMEM((2 * TOP_K, SC_CHUNK_TOKENS), I32),
            pltpu.VMEM((2, SC_CHUNK_TOKENS, MOE_SLAB, V7X_LANES), U32),
            pltpu.SemaphoreType.DMA((2,)),
            pltpu.SemaphoreType.DMA((2,)),
        ],
        name=name,
    )(slots, n3p)


def _moe_kernel(te_ref, tv_ref, tstart_ref, trows_ref,
                xs_hbm, w1_ref, b1_ref, w2_ref, b2_ref, yb_hbm,
                stage, x16, acc, w1p, sems, *, nf):
    i = pl.program_id(0)
    f = pl.program_id(1)
    rows = trows_ref[i]
    start = tstart_ref[i]
    nchunk = rows // MOE_ROW_ALIGN
    chunk_slab_rows = MOE_ROW_ALIGN * MOE_SLAB
    half = 2 * V7X_LANES
    nhalf = 2 * MOE_TF // half

    def slot_rows(c):
        return pl.multiple_of((c & 1) * chunk_slab_rows, chunk_slab_rows)

    def hbm_rows(ref, c):
        r0 = pl.multiple_of((start + c * MOE_ROW_ALIGN) * MOE_SLAB, chunk_slab_rows)
        return ref.at[pl.ds(r0, chunk_slab_rows)]

    def in_copy(c):
        return pltpu.make_async_copy(hbm_rows(xs_hbm, c), stage.at[pl.ds(slot_rows(c), chunk_slab_rows)],
                                     sems.at[c & 1])

    def out_copy(c):
        return pltpu.make_async_copy(stage.at[pl.ds(slot_rows(c), chunk_slab_rows)], hbm_rows(yb_hbm, c),
                                     sems.at[c & 1])

    @pl.when((rows > 0) & (f == 0))
    def _():
        in_copy(0).start()

        def land(c, carry):
            @pl.when(c + 1 < nchunk)
            def _():
                in_copy(c + 1).start()

            r0 = pl.multiple_of(c * MOE_ROW_ALIGN, MOE_ROW_ALIGN)

            @pl.when(c < nchunk)
            def _():
                in_copy(c).wait()
                s0 = slot_rows(c)
                owned = lax.broadcasted_iota(I32, (MOE_ROW_ALIGN, V7X_LANES), 0) < tv_ref[i] - r0
                for j in range(MOE_SLAB):
                    u = jnp.where(owned, stage[pl.ds(s0 + j, MOE_ROW_ALIGN, stride=MOE_SLAB), :], jnp.uint32(0))
                    lo, hi = _unpack_rows(u)
                    x16[pl.ds(r0, MOE_ROW_ALIGN), j * V7X_LANES:(j + 1) * V7X_LANES] = lo.astype(BF16)
                    x16[pl.ds(r0, MOE_ROW_ALIGN), (MOE_SLAB + j) * V7X_LANES:(MOE_SLAB + j + 1) * V7X_LANES] = (
                        hi.astype(BF16))

            @pl.when(c >= nchunk)
            def _():
                x16[pl.ds(r0, MOE_ROW_ALIGN), :] = jnp.zeros((MOE_ROW_ALIGN, D_MODEL), BF16)

            acc[pl.ds(r0, MOE_ROW_ALIGN), :] = jnp.zeros((MOE_ROW_ALIGN, D_MODEL), F32)
            return carry

        lax.fori_loop(0, jnp.maximum(nchunk, MOE_BLOCK_CHUNKS), land, 0)

    @pl.when(rows > 0)
    def _():
        lane = lax.broadcasted_iota(I32, (D_MODEL // 2, V7X_LANES), 1)
        first = lane < V7X_LANES // 2
        even = (2 * lane) & (V7X_LANES - 1)
        take = lambda a, ix: jnp.take_along_axis(a, ix, axis=1, mode="promise_in_bounds")

        def separate(c):
            u = pltpu.bitcast(w1_ref[:, c * half:(c + 1) * half].astype(BF16), U32)
            ua, ub = u[:, :V7X_LANES], u[:, V7X_LANES:]
            glu = jnp.where(first, take(ua, even), take(ub, even))
            lin = jnp.where(first, take(ua, even + 1), take(ub, even + 1))
            return pltpu.bitcast(jnp.concatenate([glu, lin], axis=1), BF16)

        def activation(h):
            hg = jnp.minimum(h[:, :V7X_LANES], SWIGLU_LIMIT)
            hl = jnp.clip(h[:, V7X_LANES:], -SWIGLU_LIMIT, SWIGLU_LIMIT)
            return hg * jax.nn.sigmoid(SWIGLU_ALPHA * hg) * (hl + 1.0)

        w2 = w2_ref[...].astype(BF16)
        b1 = b1_ref[...]

        block_rows = MOE_BLOCK_CHUNKS * MOE_ROW_ALIGN
        x = x16[0:block_rows, :]
        parts = []
        for c in range(nhalf):
            wc = separate(c)
            w1p[:, c * half:(c + 1) * half] = wc
            parts.append(activation(jnp.dot(x, wc, preferred_element_type=F32) + b1[:, c * half:(c + 1) * half]))
        a = jnp.concatenate(parts, axis=-1).astype(BF16)
        acc[0:block_rows, :] += jnp.dot(a, w2, preferred_element_type=F32)

        def mlp(c0, nc):
            r0 = pl.multiple_of(c0 * MOE_ROW_ALIGN, MOE_ROW_ALIGN)
            nr = nc * MOE_ROW_ALIGN
            x = x16[pl.ds(r0, nr), :]
            h = jnp.dot(x, w1p[...], preferred_element_type=F32) + b1
            a = jnp.concatenate([activation(h[:, c * half:(c + 1) * half]) for c in range(nhalf)], axis=-1)
            acc[pl.ds(r0, nr), :] += jnp.dot(a.astype(BF16), w2, preferred_element_type=F32)

        extra = jnp.maximum(nchunk - MOE_BLOCK_CHUNKS, 0)
        size = MOE_TMAX // MOE_ROW_ALIGN - MOE_BLOCK_CHUNKS
        assert size & (size - 1) == 0
        while size >= 1:
            done = MOE_BLOCK_CHUNKS + (extra & ~(2 * size - 1))

            @pl.when((extra & size) != 0)
            def _(done=done, size=size):
                mlp(done, size)

            size //= 2

    @pl.when((rows > 0) & (f == nf - 1))
    def _():
        def emit(c, carry):
            @pl.when(c >= 2)
            def _():
                out_copy(c - 2).wait()

            r0 = pl.multiple_of(c * MOE_ROW_ALIGN, MOE_ROW_ALIGN)
            s0 = slot_rows(c)
            u = _pack_rows(acc[pl.ds(r0, MOE_ROW_ALIGN), :] + b2_ref[...])
            for j in range(MOE_SLAB):
                stage[pl.ds(s0 + j, MOE_ROW_ALIGN, stride=MOE_SLAB), :] = u[:, j * V7X_LANES:(j + 1) * V7X_LANES]
            out_copy(c).start()
            return carry

        lax.fori_loop(0, nchunk, emit, 0)

        @pl.when(nchunk >= 2)
        def _():
            out_copy(nchunk - 2).wait()

        out_copy(nchunk - 1).wait()


def _moe(tile_e, tile_owned, tile_start, tile_rows, xs, w1, b1p, w2, b2, *, name="moe_mlp"):
    nt = tile_e.shape[0]
    nf = D_FF // MOE_TF

    def w1_map(i, f, te, tv, ts, tr):
        return (te[i], 0, jnp.where(tv[i] > 0, f, nf - 1))

    def w2_map(i, f, te, tv, ts, tr):
        return (te[i], jnp.where(tv[i] > 0, f, nf - 1), 0)

    def b2_map(i, f, te, tv, ts, tr):
        return (te[i], 0, 0)

    grid_spec = pltpu.PrefetchScalarGridSpec(
        num_scalar_prefetch=4,
        grid=(nt, nf),
        in_specs=[
            pl.BlockSpec(memory_space=pl.ANY),
            pl.BlockSpec((None, D_MODEL, 2 * MOE_TF), w1_map),
            pl.BlockSpec((None, 1, 2 * MOE_TF), w1_map),
            pl.BlockSpec((None, MOE_TF, D_MODEL), w2_map),
            pl.BlockSpec((None, 1, D_MODEL), b2_map),
        ],
        out_specs=pl.BlockSpec(memory_space=pl.ANY),
        scratch_shapes=[
            pltpu.VMEM((2 * MOE_ROW_ALIGN * MOE_SLAB, V7X_LANES), U32),
            pltpu.VMEM((MOE_TMAX, D_MODEL), BF16),
            pltpu.VMEM((MOE_TMAX, D_MODEL), F32),
            pltpu.VMEM((D_MODEL, 2 * MOE_TF), BF16),
            pltpu.SemaphoreType.DMA((2,)),
        ],
    )
    return pl.pallas_call(
        functools.partial(_moe_kernel, nf=nf),
        out_shape=jax.ShapeDtypeStruct(xs.shape, xs.dtype),
        grid_spec=grid_spec,
        input_output_aliases={4: 0},
        compiler_params=_cparams(("arbitrary", "arbitrary")),
        name=name,
    )(tile_e, tile_owned, tile_start, tile_rows, xs, w1, b1p.reshape(N_EXPERTS, 1, 2 * D_FF), w2,
      b2.reshape(N_EXPERTS, 1, D_MODEL))


def _sc_gather(slots, yb3, *, n_tok, name="moe_gather_sc"):
    per_w = n_tok // (SC_CORES * SC_SUBCORES)
    mesh = plsc.VectorSubcoreMesh(core_axis_name="c", subcore_axis_name="s",
                                  num_cores=SC_CORES, num_subcores=SC_SUBCORES)

    items = [(c, k) for c in range(per_w // SC_CHUNK_TOKENS) for k in range(TOP_K)]

    def body(slots_hbm, yb_hbm, out_hbm, idx_bufs, row_bufs, gather_sems, write_sems):
        wid = lax.axis_index("s") * SC_CORES + lax.axis_index("c")

        def offset(j):
            c, k = items[j]
            return pl.multiple_of(k * n_tok + wid * per_w + c * SC_CHUNK_TOKENS, SC_CHUNK_TOKENS)

        def gather(j):
            idx = idx_bufs.at[j % 2]
            pltpu.sync_copy(slots_hbm.at[pl.ds(offset(j), SC_CHUNK_TOKENS)], idx)
            return pltpu.async_copy(yb_hbm.at[idx], row_bufs.at[j % 2], gather_sems.at[j % 2])

        def write(j):
            return pltpu.async_copy(row_bufs.at[j % 2], out_hbm.at[pl.ds(offset(j), SC_CHUNK_TOKENS)],
                                    write_sems.at[j % 2])

        n_items = len(items)
        gathers = [None] * n_items
        writes = [None] * n_items
        gathers[0] = gather(0)
        for j in range(n_items):
            if j + 1 < n_items:
                if j >= 1:
                    writes[j - 1].wait()
                gathers[j + 1] = gather(j + 1)
            gathers[j].wait()
            writes[j] = write(j)
        for j in range(max(n_items - 2, 0), n_items):
            writes[j].wait()

    return pl.kernel(
        body,
        out_type=jax.ShapeDtypeStruct((TOP_K * n_tok, MOE_SLAB, V7X_LANES), U32),
        mesh=mesh,
        scratch_types=[
            pltpu.VMEM((2, SC_CHUNK_TOKENS), I32),
            pltpu.VMEM((2, SC_CHUNK_TOKENS, MOE_SLAB, V7X_LANES), U32),
            pltpu.SemaphoreType.DMA((2,)),
            pltpu.SemaphoreType.DMA((2,)),
        ],
        name=name,
    )(slots, yb3)


def _combine_kernel(yg_ref, h2_ref, tw_ref, gf_ref, o_ref, *, tm):
    tw = tw_ref[...]
    cols_lo, cols_hi = [], []
    for j in range(MOE_SLAB):
        clo = chi = None
        for k in range(TOP_K):
            lo, hi = _unpack_rows(yg_ref[k, pl.ds(j, tm, stride=MOE_SLAB), :])
            w = tw[:, k:k + 1]
            clo = w * lo if clo is None else clo + w * lo
            chi = w * hi if chi is None else chi + w * hi
        cols_lo.append(clo)
        cols_hi.append(chi)
    h3 = h2_ref[...] + jnp.concatenate(cols_lo + cols_hi, axis=-1)
    o_ref[...] = _rms(h3, gf_ref[...])


def _combine(yg, h2, tw, gf, *, tm, name="moe_combine"):
    n, d = h2.shape
    return pl.pallas_call(
        functools.partial(_combine_kernel, tm=tm),
        out_shape=jax.ShapeDtypeStruct((n, d), F32),
        grid=(n // tm,),
        in_specs=[
            pl.BlockSpec((TOP_K, tm * MOE_SLAB, V7X_LANES), lambda i: (0, i, 0)),
            pl.BlockSpec((tm, d), lambda i: (i, 0)),
            pl.BlockSpec((tm, TOP_K), lambda i: (i, 0)),
            pl.BlockSpec((1, d), lambda i: (0, 0)),
        ],
        out_specs=pl.BlockSpec((tm, d), lambda i: (i, 0)),
        compiler_params=_cparams(("parallel",)),
        name=name,
    )(yg, h2, tw, gf.reshape(1, d))


def _rope_tables(positions):
    b, s = positions.shape
    half = HEAD_DIM // 2
    inv_freq = 1.0 / (ROPE_THETA ** (jnp.arange(half, dtype=F32) / half))
    ang = positions.astype(F32)[..., None] * inv_freq
    cos = jnp.cos(ang)
    sin = jnp.sin(ang)
    cs = jnp.concatenate([cos, cos], axis=-1)
    sn = jnp.concatenate([-sin, sin], axis=-1)

    def deint(a):
        a = a.reshape(b, s // ATT_TILE, WIN_BLOCK, DIL_MAX, HEAD_DIM)
        return a.transpose(0, 1, 3, 2, 4)

    return deint(cs), deint(sn)


def _routing(counts, n_tok):
    cnt = counts.reshape(N_EXPERTS).astype(I32)
    padded = (cnt + MOE_ROW_ALIGN - 1) // MOE_ROW_ALIGN * MOE_ROW_ALIGN
    gstart = (jnp.cumsum(padded) - padded).astype(I32)
    rows_total = n_tok * TOP_K + N_EXPERTS * MOE_ROW_ALIGN

    def first_above(ends, q):
        return jnp.minimum(jnp.sum((ends[None, :] <= q[:, None]).astype(I32), axis=1), N_EXPERTS - 1)

    main_rows = jnp.minimum(padded, MOE_TMAX)
    main = (jnp.arange(N_EXPERTS, dtype=I32), jnp.minimum(cnt, MOE_TMAX).astype(I32), gstart.astype(I32),
            main_rows.astype(I32))

    over = padded - main_rows
    nt_e = (over + MOE_TMAX - 1) // MOE_TMAX
    tend = jnp.cumsum(nt_e)
    tstart = tend - nt_e
    n_over = tend[-1]
    ti = jnp.arange(rows_total // MOE_TMAX, dtype=I32)
    valid = ti < n_over
    tic = jnp.clip(ti, 0, jnp.maximum(n_over - 1, 0))
    te = first_above(tend, tic)
    local = tic - tstart[te]
    row0 = gstart[te] + (local + 1) * MOE_TMAX
    rows = jnp.where(valid, jnp.clip(over[te] - local * MOE_TMAX, 0, MOE_TMAX), 0)
    owned = jnp.where(valid, jnp.clip(cnt[te] - (local + 1) * MOE_TMAX, 0, MOE_TMAX), 0)
    overflow = (te, owned.astype(I32), row0.astype(I32), rows.astype(I32))
    return gstart, rows_total, main, overflow, n_over > 0


def kernel(x, mem, positions, norm_mix_g, w_in, lb_raw, hgrn_norm_g, w_br_hgrn, w_br_attn, w_out,
           norm_cross_g, norm_mem_g, w_cq, w_ckv, w_co, norm_moe_g, w_router, b_router,
           w_mlp1, b_mlp1, w_mlp2, b_mlp2, norm_final_g):
    bsz, seq, d = x.shape
    assert w_in.shape[0] == 1 and d == D_MODEL and seq % ATT_TILE == 0
    n_tok = bsz * seq
    lower_bounds = jnp.cumsum(jax.nn.softmax(lb_raw.astype(F32), axis=0), axis=0)
    cs, sn = _rope_tables(positions)

    wl = w_in[0]
    c_h = 4 * HG_WIDTH
    c_a = c_h + ATT_Q_WIDTH + 2 * ATT_KV_WIDTH
    tn = 1024
    x2d = x.reshape(n_tok, d)
    n_h, n_a, n_g = c_h // tn, (c_a - c_h) // tn, 2 * D_MODEL // tn
    proj = _in_proj(x2d, norm_mix_g[0], wl, tm=1024, tn=tn, name="in_proj",
                    col_block=lambda j: jnp.where(j < n_h, j, jnp.where(j < n_h + n_g, j + n_a, j - n_g)))
    proj3 = proj.reshape(bsz, seq, -1)

    o_h = _hgrn(proj3, lower_bounds[0], hgrn_norm_g[0], ts=1024)
    o_a = _dilated_attn(proj3, cs, sn, q_block=(c_h + 2 * D_MODEL) // HEAD_DIM)

    h1, n2 = _merge_out(o_h.reshape(n_tok, HG_WIDTH), o_a.reshape(n_tok, ATT_KV_WIDTH), proj, x2d,
                        w_br_hgrn[0].astype(BF16), w_br_attn[0].astype(BF16), w_out[0].astype(BF16),
                        norm_cross_g[0], tm=256, gate_block=c_h // (2 * D_MODEL))

    mem2d = mem.reshape(-1, d)
    kv = _in_proj(mem2d, norm_mem_g, w_ckv[0], tm=mem2d.shape[0], tn=2 * CROSS_WIDTH, name="mem_kv")
    wrh = w_router[0].astype(BF16)
    wrl = (w_router[0] - wrh.astype(F32)).astype(BF16)
    h2, n3, top_e, top_w, rank, counts = _cross(
        n2, h1, kv, w_cq[0].astype(BF16), w_co[0].astype(BF16), norm_moe_g[0], wrh, wrl, b_router[0],
        batch=bsz, tm=512)

    gstart, rows_total, main_tiles, over_tiles, has_over = _routing(counts, n_tok)
    hot = top_e[..., None] == jnp.arange(N_EXPERTS, dtype=I32)
    slots = (rank + jnp.sum(jnp.where(hot, gstart, 0), axis=-1)).T.reshape(-1)
    xs = _sc_dispatch(slots, n3.reshape(n_tok, MOE_SLAB, V7X_LANES), rows_total=rows_total)
    xs = xs.reshape(rows_total * MOE_SLAB, V7X_LANES)

    b1p = b_mlp1[0].reshape(N_EXPERTS, -1, V7X_LANES, 2).transpose(0, 1, 3, 2).reshape(N_EXPERTS, 2 * D_FF)
    def run_moe(tiles, rows_buf, name):
        return _moe(*tiles, rows_buf, w_mlp1[0], b1p, w_mlp2[0], b_mlp2[0], name=name)

    yb = run_moe(main_tiles, xs, "moe_mlp")
    yb = lax.cond(has_over, lambda rows_buf: run_moe(over_tiles, rows_buf, "moe_mlp_overflow"),
                  lambda rows_buf: rows_buf, yb)

    yg = _sc_gather(slots, yb.reshape(rows_total, MOE_SLAB, V7X_LANES), n_tok=n_tok)
    out = _combine(yg.reshape(TOP_K, n_tok * MOE_SLAB, V7X_LANES), h2, top_w, norm_final_g, tm=COMBINE_TOKENS)
    return out.reshape(bsz, seq, d)
```

```python
import functools

import jax
import jax.numpy as jnp
from jax import lax
from jax.experimental import pallas as pl
from jax.experimental.pallas import tpu as pltpu
from jax.experimental.pallas import tpu_sc as plsc

F32 = jnp.float32
BF16 = jnp.bfloat16
I32 = jnp.int32
U32 = jnp.uint32

D_MODEL = 2048
HEAD_DIM = 128
HG_HEADS = 8
HG_WIDTH = HG_HEADS * HEAD_DIM
N_KV_HEADS = 8
N_GROUPS = 3
ATT_Q_WIDTH = N_GROUPS * N_KV_HEADS * HEAD_DIM
ATT_KV_WIDTH = N_KV_HEADS * HEAD_DIM
WIN_BLOCK = 128
ROPE_THETA = 10000.0
CROSS_HEADS = 4
CROSS_WIDTH = CROSS_HEADS * HEAD_DIM
N_EXPERTS = 32
TOP_K = 4
D_FF = D_MODEL
SWIGLU_ALPHA = 1.702
SWIGLU_LIMIT = 7.0
NORM_EPS = 1e-6

V7X_LANES = 128
V7X_VMEM_LIMIT_BYTES = 56 * 1024 * 1024

DIL_MAX = 16
ATT_TILE = DIL_MAX * WIN_BLOCK
ATT_UNROLL = 16
HG_BLOCK = 16
HG_GROUP = 32
NEG_BIG = -1e30

MOE_TMAX = 1536
MOE_ROW_ALIGN = 128
MOE_BLOCK_CHUNKS = 8
MOE_TF = 512
MOE_SLAB = D_MODEL // 2 // V7X_LANES
SC_CORES = 2
SC_SUBCORES = 16
SC_CHUNK_TOKENS = 32
COMBINE_TOKENS = 512


def _rms(x, g):
    ms = jnp.mean(x * x, axis=-1, keepdims=True)
    return x * lax.rsqrt(ms + NORM_EPS) * g


def _cparams(sem, vmem=V7X_VMEM_LIMIT_BYTES):
    return pltpu.CompilerParams(dimension_semantics=sem, vmem_limit_bytes=vmem)


def _in_proj_kernel(x_ref, g_ref, w_ref, o_ref, xn_ref):
    @pl.when(pl.program_id(1) == 0)
    def _():
        xn_ref[...] = _rms(x_ref[...], g_ref[...]).astype(BF16)

    o_ref[...] = jnp.dot(xn_ref[...], w_ref[...].astype(BF16), preferred_element_type=F32).astype(o_ref.dtype)


def _in_proj(x2d, g, w, *, tm, tn, name, col_block=lambda j: j, ncols=None):
    n, d = x2d.shape
    wc = w.shape[1] if ncols is None else ncols
    return pl.pallas_call(
        _in_proj_kernel,
        out_shape=jax.ShapeDtypeStruct((n, wc), BF16),
        grid=(n // tm, wc // tn),
        in_specs=[
            pl.BlockSpec((tm, d), lambda i, j: (i, 0)),
            pl.BlockSpec((1, d), lambda i, j: (0, 0)),
            pl.BlockSpec((d, tn), lambda i, j: (0, col_block(j))),
        ],
        out_specs=pl.BlockSpec((tm, tn), lambda i, j: (i, j)),
        scratch_shapes=[pltpu.VMEM((tm, d), BF16)],
        compiler_params=_cparams(("parallel", "arbitrary")),
        name=name,
    )(x2d, g.reshape(1, d), w)


def _hgrn_kernel(q_ref, f_ref, i_ref, g_ref, lb_ref, gn_ref, o_ref, st_ref, kin_s, b_s, v_s, *, ts):
    @pl.when(pl.program_id(2) == 0)
    def _():
        st_ref[...] = jnp.zeros_like(st_ref)

    lb = lb_ref[...]
    oml = 1.0 - lb
    gn = gn_ref[...]
    half = HG_BLOCK // 2
    row = lax.broadcasted_iota(I32, (HG_BLOCK, HEAD_DIM), 0)
    row8 = lax.broadcasted_iota(I32, (half, HEAD_DIM), 0)
    nt_dims = (((1,), (1,)), ((), ()))
    tn_dims = (((0,), (0,)), ((), ()))

    def front(g, t0):
        sl = pl.ds(t0 + g * HG_BLOCK, HG_BLOCK)
        q = q_ref[sl, :].astype(F32)
        hf = f_ref[sl, :].astype(F32)
        v = i_ref[sl, :].astype(F32)
        kin = oml * jax.nn.sigmoid(-hf)
        b = jnp.log2(lb + oml * jax.nn.sigmoid(hf))
        for sh in (1, 2, 4, 8):
            b = b + jnp.where(row >= sh, pltpu.roll(b, sh, 0), 0.0)
        kin_s[g] = kin
        b_s[g] = b
        v_s[g] = v
        q_lo, q_hi = q[:half], q[half:]
        b_lo, b_hi = b[:half], b[half:]
        o_lo = jnp.zeros((half, HEAD_DIM), F32)
        o_hi = jnp.zeros((half, HEAD_DIM), F32)
        for s in range(HG_BLOCK):
            ks = kin_s[g, s:s + 1, :]
            bs = b_s[g, s:s + 1, :]
            vs = v_s[g, s:s + 1, :]
            if s < half:
                w = q_lo * ks * jnp.exp2(b_lo - bs)
                if s > 0:
                    w = jnp.where(row8 >= s, w, 0.0)
                o_lo = o_lo + jnp.sum(w, axis=-1, keepdims=True) * vs
                w = q_hi * ks * jnp.exp2(b_hi - bs)
            else:
                w = q_hi * ks * jnp.exp2(b_hi - bs)
                if s > half:
                    w = jnp.where(row8 >= s - half, w, 0.0)
            o_hi = o_hi + jnp.sum(w, axis=-1, keepdims=True) * vs
        bl = b_s[g, HG_BLOCK - 1:HG_BLOCK, :]
        qd = (q * jnp.exp2(b)).astype(BF16)
        kd = (kin * jnp.exp2(bl - b)).astype(BF16)
        upd = lax.dot_general(v.astype(BF16), kd, tn_dims, preferred_element_type=F32)
        return jnp.concatenate([o_lo, o_hi], axis=0), qd, upd, jnp.exp2(bl)

    def body(i, carry):
        t0 = pl.multiple_of(i * (HG_GROUP * HG_BLOCK), HG_GROUP * HG_BLOCK)
        fronts = [front(g, t0) for g in range(HG_GROUP)]
        st = st_ref[...]
        for g, (o_diag, qd, upd, dec) in enumerate(fronts):
            o = o_diag + lax.dot_general(qd, st.astype(BF16), nt_dims, preferred_element_type=F32)
            st = st * dec + upd
            sl = pl.ds(t0 + g * HG_BLOCK, HG_BLOCK)
            hg = g_ref[sl, :].astype(F32)
            o_ref[sl, :] = (_rms(o, gn) * (hg * jax.nn.sigmoid(hg))).astype(o_ref.dtype)
        st_ref[...] = st
        return carry

    lax.fori_loop(0, ts // (HG_GROUP * HG_BLOCK), body, 0)


def _hgrn(proj_h, lb, gn, *, ts, name="hgrn"):
    b, s, _ = proj_h.shape
    h = HG_HEADS

    def spec(off):
        return pl.BlockSpec((None, ts, HEAD_DIM), lambda bi, hi, si: (bi, si, off + hi))

    vec = pl.BlockSpec((1, HEAD_DIM), lambda bi, hi, si: (0, hi))
    return pl.pallas_call(
        functools.partial(_hgrn_kernel, ts=ts),
        out_shape=jax.ShapeDtypeStruct((b, s, HG_WIDTH), BF16),
        grid=(b, h, s // ts),
        in_specs=[spec(0), spec(h), spec(2 * h), spec(3 * h), vec, vec],
        out_specs=pl.BlockSpec((None, ts, HEAD_DIM), lambda bi, hi, si: (bi, si, hi)),
        scratch_shapes=[
            pltpu.VMEM((HEAD_DIM, HEAD_DIM), F32),
            pltpu.VMEM((HG_GROUP, HG_BLOCK, HEAD_DIM), F32),
            pltpu.VMEM((HG_GROUP, HG_BLOCK, HEAD_DIM), F32),
            pltpu.VMEM((HG_GROUP, HG_BLOCK, HEAD_DIM), F32),
        ],
        compiler_params=_cparams(("parallel", "parallel", "arbitrary")),
        name=name,
    )(proj_h, proj_h, proj_h, proj_h, lb.reshape(1, HG_WIDTH), gn.reshape(1, HG_WIDTH))


def _attn_bias(kind):
    rq = lax.broadcasted_iota(I32, (WIN_BLOCK, 2 * WIN_BLOCK), 0)
    ck = lax.broadcasted_iota(I32, (WIN_BLOCK, 2 * WIN_BLOCK), 1)
    if kind == 2:
        dist = rq + WIN_BLOCK - ck
        first = ck < WIN_BLOCK
    elif kind == 1:
        dist = 4 * ((rq & 31) - (ck & 63) + 32) + ((rq >> 5) - (ck >> 6))
        first = (ck & 63) < 32
    else:
        dist = 16 * ((rq & 7) - (ck & 15) + 8) + ((rq >> 3) - (ck >> 4))
        first = (ck & 15) < 8
    valid = (dist >= 0) & (dist <= WIN_BLOCK)
    return (jnp.where(valid, 0.0, NEG_BIG).astype(F32),
            jnp.where(valid & jnp.logical_not(first), 0.0, NEG_BIG).astype(F32))


def _attn_kernel(q0_ref, q1_ref, q2_ref, k_ref, v_ref, cs_ref, sn_ref, o_ref,
                 qr, kext, vext, acc, mrun, lrun, bias, onat, u0, u1, u2, u3, u4):
    ti = pl.program_id(2)
    wb = WIN_BLOCK
    scale = HEAD_DIM ** -0.5

    ubufs = (u0, u1, u2, u3, u4)
    for src, ub in zip((q0_ref, q1_ref, q2_ref, k_ref, v_ref), ubufs):
        ub[...] = pltpu.bitcast(src[...], U32)

    @pl.when(ti == 0)
    def _():
        kext[:, 0:wb, :] = jnp.zeros((DIL_MAX, wb, HEAD_DIM), F32)
        vext[:, 0:wb, :] = jnp.zeros((DIL_MAX, wb, HEAD_DIM), F32)

    @pl.when(ti > 0)
    def _():
        kext[:, 0:wb, :] = kext[:, wb:2 * wb, :]
        vext[:, 0:wb, :] = vext[:, wb:2 * wb, :]

    @pl.when((pl.program_id(0) == 0) & (pl.program_id(1) == 0) & (ti == 0))
    def _():
        for kind in range(N_GROUPS):
            full, nofirst = _attn_bias(kind)
            bias[2 * kind] = full
            bias[2 * kind + 1] = nofirst

    def rope_body(rp, carry):
        rows = pl.ds(rp, wb, stride=DIL_MAX // 2)
        parts = [_unpack_rows(ub[rows, :]) for ub in ubufs]
        for par in range(2):
            r = 2 * rp + par
            cs = cs_ref[r]
            sn = sn_ref[r]
            for g in range(N_GROUPS):
                q = parts[g][par]
                qr[g, r] = (q * cs + pltpu.roll(q, HEAD_DIM // 2, 1) * sn) * scale
            k = parts[N_GROUPS][par]
            kext[r, wb:2 * wb, :] = k * cs + pltpu.roll(k, HEAD_DIM // 2, 1) * sn
            vext[r, wb:2 * wb, :] = parts[N_GROUPS + 1][par]
            acc[r] = jnp.zeros((wb, HEAD_DIM), F32)
            mrun[r] = jnp.full((wb, HEAD_DIM), NEG_BIG, F32)
            lrun[r] = jnp.zeros((wb, HEAD_DIM), F32)
        return carry

    lax.fori_loop(0, DIL_MAX // 2, rope_body, 0)

    nt_dims = (((1,), (1,)), ((), ()))

    def block(qb, kb, vb, bias_blk):
        s = lax.dot_general(qb.astype(BF16), kb.astype(BF16), nt_dims, preferred_element_type=F32)
        s = s + bias_blk
        m = jnp.max(s, axis=-1, keepdims=True)
        p = jnp.exp(s - m)
        l = jnp.sum(p, axis=-1, keepdims=True)
        n = jnp.dot(p.astype(BF16), vb.astype(BF16), preferred_element_type=F32)
        return n, jnp.broadcast_to(m, (wb, HEAD_DIM)), jnp.broadcast_to(l, (wb, HEAD_DIM))

    def merge(r, rows, n, m, l):
        m_old = mrun[r, rows, :]
        m_new = jnp.maximum(m_old, m)
        a = jnp.exp(m_old - m_new)
        bb = jnp.exp(m - m_new)
        acc[r, rows, :] = acc[r, rows, :] * a + n * bb
        lrun[r, rows, :] = lrun[r, rows, :] * a + l * bb
        mrun[r, rows, :] = m_new

    first_tile = jnp.where(ti == 0, 1, 0)


    def g2_body(i, carry):
        rs = [i * ATT_UNROLL + u for u in range(ATT_UNROLL)]
        res = [block(qr[2, r], kext[r], vext[r], bias[4 + first_tile]) for r in rs]
        for r, (n, m, l) in zip(rs, res):
            merge(r, pl.ds(0, wb), n, m, l)
        return carry

    lax.fori_loop(0, DIL_MAX // ATT_UNROLL, g2_body, 0)

    def g1_body(i, carry):
        res = []
        for u in range(ATT_UNROLL // 4):
            mb = i * (ATT_UNROLL // 4) + u
            q0 = pl.multiple_of(32 * mb, 32)
            k0 = pl.multiple_of(96 + 32 * mb, 32)
            use_first = jnp.where(mb == 0, first_tile, 0)
            for r4 in range(4):
                qb = jnp.concatenate([qr[1, r4 + 4 * j, pl.ds(q0, 32), :] for j in range(4)], axis=0)
                kb = jnp.concatenate([kext[r4 + 4 * j, pl.ds(k0, 64), :] for j in range(4)], axis=0)
                vb = jnp.concatenate([vext[r4 + 4 * j, pl.ds(k0, 64), :] for j in range(4)], axis=0)
                res.append((r4, q0, block(qb, kb, vb, bias[2 + use_first])))
        for r4, q0, (n, m, l) in res:
            for j in range(4):
                sl = slice(32 * j, 32 * (j + 1))
                merge(r4 + 4 * j, pl.ds(q0, 32), n[sl], m[sl], l[sl])
        return carry

    lax.fori_loop(0, DIL_MAX // ATT_UNROLL, g1_body, 0)

    def g0_body(i, carry):
        res = []
        for u in range(ATT_UNROLL):
            mb = i * ATT_UNROLL + u
            q0 = pl.multiple_of(8 * mb, 8)
            k0 = pl.multiple_of(120 + 8 * mb, 8)
            qb = jnp.concatenate([qr[0, r, pl.ds(q0, 8), :] for r in range(DIL_MAX)], axis=0)
            kb = jnp.concatenate([kext[r, pl.ds(k0, 16), :] for r in range(DIL_MAX)], axis=0)
            vb = jnp.concatenate([vext[r, pl.ds(k0, 16), :] for r in range(DIL_MAX)], axis=0)
            use_first = jnp.where(mb == 0, first_tile, 0)
            res.append((q0, block(qb, kb, vb, bias[use_first])))
        for q0, (n, m, l) in res:
            for r in range(DIL_MAX):
                sl = slice(8 * r, 8 * (r + 1))
                merge(r, pl.ds(q0, 8), n[sl], m[sl], l[sl])
        return carry

    lax.fori_loop(0, DIL_MAX // ATT_UNROLL, g0_body, 0)

    for r in range(DIL_MAX):
        onat[pl.ds(r, wb, stride=DIL_MAX), :] = acc[r] / lrun[r]
    o_ref[...] = onat[...].astype(o_ref.dtype)


def _dilated_attn(proj, cs, sn, *, q_block, name="dilated_attn"):
    b, s, _ = proj.shape
    nt = s // ATT_TILE
    h = N_KV_HEADS

    def spec(off):
        return pl.BlockSpec((None, ATT_TILE, HEAD_DIM), lambda bi, hi, ti: (bi, ti, q_block + off + hi))

    tab = pl.BlockSpec((None, None, DIL_MAX, WIN_BLOCK, HEAD_DIM), lambda bi, hi, ti: (bi, ti, 0, 0, 0))
    ubuf = pltpu.VMEM((ATT_TILE // 2, HEAD_DIM), U32)
    return pl.pallas_call(
        _attn_kernel,
        out_shape=jax.ShapeDtypeStruct((b, nt * ATT_TILE, ATT_KV_WIDTH), BF16),
        grid=(b, h, nt),
        in_specs=[spec(0), spec(h), spec(2 * h), spec(3 * h), spec(4 * h), tab, tab],
        out_specs=pl.BlockSpec((None, ATT_TILE, HEAD_DIM), lambda bi, hi, ti: (bi, ti, hi)),
        scratch_shapes=[
            pltpu.VMEM((N_GROUPS, DIL_MAX, WIN_BLOCK, HEAD_DIM), F32),
            pltpu.VMEM((DIL_MAX, 2 * WIN_BLOCK, HEAD_DIM), F32),
            pltpu.VMEM((DIL_MAX, 2 * WIN_BLOCK, HEAD_DIM), F32),
            pltpu.VMEM((DIL_MAX, WIN_BLOCK, HEAD_DIM), F32),
            pltpu.VMEM((DIL_MAX, WIN_BLOCK, HEAD_DIM), F32),
            pltpu.VMEM((DIL_MAX, WIN_BLOCK, HEAD_DIM), F32),
            pltpu.VMEM((2 * N_GROUPS, WIN_BLOCK, 2 * WIN_BLOCK), F32),
            pltpu.VMEM((ATT_TILE, HEAD_DIM), F32),
            ubuf, ubuf, ubuf, ubuf, ubuf,
        ],
        compiler_params=_cparams(("arbitrary", "arbitrary", "arbitrary")),
        name=name,
    )(proj, proj, proj, proj, proj, cs, sn)


def _merge_out_kernel(oh_ref, oa_ref, gate_ref, x_ref, wh_ref, wa_ref, wo_ref, gc_ref, h1_ref, n2_ref):
    ga = gate_ref[:, :D_MODEL].astype(F32)
    gb = gate_ref[:, D_MODEL:].astype(F32)
    yh = jnp.dot(oh_ref[...], wh_ref[...], preferred_element_type=F32)
    ya = jnp.dot(oa_ref[...], wa_ref[...], preferred_element_type=F32)
    merged = jax.nn.sigmoid(ga) * yh + jax.nn.sigmoid(gb) * ya
    h1 = x_ref[...] + jnp.dot(merged.astype(BF16), wo_ref[...], preferred_element_type=F32)
    h1_ref[...] = h1
    n2_ref[...] = _rms(h1, gc_ref[...]).astype(BF16)


def _merge_out(oh, oa, gates, x2d, wh, wa, wo, gc, *, tm, gate_block, name="merge_out"):
    n, d = x2d.shape

    def const(shape):
        return pl.BlockSpec(shape, lambda i: (0, 0), pipeline_mode=pl.Buffered(1))

    return pl.pallas_call(
        _merge_out_kernel,
        out_shape=(jax.ShapeDtypeStruct((n, d), F32), jax.ShapeDtypeStruct((n, d), BF16)),
        grid=(n // tm,),
        in_specs=[
            pl.BlockSpec((tm, HG_WIDTH), lambda i: (i, 0)),
            pl.BlockSpec((tm, ATT_KV_WIDTH), lambda i: (i, 0)),
            pl.BlockSpec((tm, 2 * d), lambda i: (i, gate_block)),
            pl.BlockSpec((tm, d), lambda i: (i, 0)),
            const((HG_WIDTH, d)), const((ATT_KV_WIDTH, d)), const((d, d)), const((1, d)),
        ],
        out_specs=(pl.BlockSpec((tm, d), lambda i: (i, 0)), pl.BlockSpec((tm, d), lambda i: (i, 0))),
        compiler_params=_cparams(("parallel",)),
        name=name,
    )(oh, oa, gates, x2d, wh, wa, wo, gc.reshape(1, d))


def _cross_kernel(n2_ref, h1_ref, kv_ref, wq_ref, wo_ref, gm_ref, wrh_ref, wrl_ref, br_ref,
                  h2_ref, n3_ref, idx_ref, tw_ref, rank_ref, cnt_ref, carry_ref, *, tm):
    @pl.when((pl.program_id(0) == 0) & (pl.program_id(1) == 0))
    def _():
        carry_ref[...] = jnp.zeros_like(carry_ref)

    nt_dims = (((1,), (1,)), ((), ()))
    scale = HEAD_DIM ** -0.5
    q = (jnp.dot(n2_ref[...], wq_ref[...], preferred_element_type=F32) * scale).astype(BF16)
    outs = []
    for hh in range(CROSS_HEADS):
        sl = slice(hh * HEAD_DIM, (hh + 1) * HEAD_DIM)
        kh = kv_ref[:, sl]
        vh = kv_ref[:, CROSS_WIDTH + hh * HEAD_DIM:CROSS_WIDTH + (hh + 1) * HEAD_DIM]
        s = lax.dot_general(q[:, sl], kh, nt_dims, preferred_element_type=F32)
        p = jnp.exp(s - jnp.max(s, axis=-1, keepdims=True))
        l = jnp.sum(p, axis=-1, keepdims=True)
        outs.append(jnp.dot(p.astype(BF16), vh, preferred_element_type=F32) / l)
    o = jnp.concatenate(outs, axis=-1).astype(BF16)
    h2 = h1_ref[...] + jnp.dot(o, wo_ref[...], preferred_element_type=F32)
    h2_ref[...] = h2
    n3 = _rms(h2, gm_ref[...])
    u = _pack_rows(n3)
    for j in range(MOE_SLAB):
        n3_ref[pl.ds(j, tm, stride=MOE_SLAB), :] = u[:, j * V7X_LANES:(j + 1) * V7X_LANES]

    n3h = n3.astype(BF16)
    n3l = (n3 - n3h.astype(F32)).astype(BF16)
    wrh = wrh_ref[...]
    logits = (jnp.dot(n3h, wrh, preferred_element_type=F32)
              + jnp.dot(n3l, wrh, preferred_element_type=F32)
              + jnp.dot(n3h, wrl_ref[...], preferred_element_type=F32)
              + br_ref[...])
    lane = lax.broadcasted_iota(I32, (tm, N_EXPERTS), 1).astype(F32)
    vals, idxs, hots = [], [], []
    cur = logits
    for _ in range(TOP_K):
        mx = jnp.max(cur, axis=-1, keepdims=True)
        ix = jnp.min(jnp.where(cur == mx, lane, float(N_EXPERTS)), axis=-1, keepdims=True)
        hot = lane == ix
        vals.append(mx)
        idxs.append(ix)
        hots.append(hot)
        cur = jnp.where(hot, -jnp.inf, cur)
    es = [jnp.exp(v - vals[0]) for v in vals]
    den = es[0] + es[1] + es[2] + es[3]
    col = lax.broadcasted_iota(I32, (tm, TOP_K), 1)

    def pack(cols):
        out = jnp.broadcast_to(cols[TOP_K - 1], (tm, TOP_K))
        for k in range(TOP_K - 2, -1, -1):
            out = jnp.where(col == k, cols[k], out)
        return out

    idx_ref[...] = pack(idxs).astype(I32)
    tw_ref[...] = pack([e / den for e in es])

    cmat = (hots[0] | hots[1] | hots[2] | hots[3]).astype(F32)
    rr = lax.broadcasted_iota(I32, (tm, tm), 0)
    cc = lax.broadcasted_iota(I32, (tm, tm), 1)
    tri = (cc < rr).astype(BF16)
    before = jnp.dot(tri, cmat.astype(BF16), preferred_element_type=F32) + carry_ref[...]
    ranks = [jnp.sum(jnp.where(hot, before, 0.0), axis=-1, keepdims=True) for hot in hots]
    rank_ref[...] = pack(ranks).astype(I32)
    carry = carry_ref[...] + jnp.sum(cmat, axis=0, keepdims=True)
    carry_ref[...] = carry
    cnt_ref[...] = carry


def _cross(n2, h1, kv, wq, wo, gm, wrh, wrl, br, *, batch, tm, name="cross"):
    n, d = h1.shape
    per_b = n // batch // tm
    mem_len = kv.shape[0] // batch

    def const(shape):
        return pl.BlockSpec(shape, lambda bi, i: (0, 0), pipeline_mode=pl.Buffered(1))

    def row(bi, i):
        return (bi * per_b + i, 0)

    return pl.pallas_call(
        functools.partial(_cross_kernel, tm=tm),
        out_shape=(
            jax.ShapeDtypeStruct((n, d), F32),
            jax.ShapeDtypeStruct((n * MOE_SLAB, V7X_LANES), U32),
            jax.ShapeDtypeStruct((n, TOP_K), I32),
            jax.ShapeDtypeStruct((n, TOP_K), F32),
            jax.ShapeDtypeStruct((n, TOP_K), I32),
            jax.ShapeDtypeStruct((1, N_EXPERTS), F32),
        ),
        grid=(batch, per_b),
        in_specs=[
            pl.BlockSpec((tm, d), row),
            pl.BlockSpec((tm, d), row),
            pl.BlockSpec((mem_len, 2 * CROSS_WIDTH), lambda bi, i: (bi, 0)),
            const((d, CROSS_WIDTH)), const((CROSS_WIDTH, d)), const((1, d)),
            const((d, N_EXPERTS)), const((d, N_EXPERTS)), const((1, N_EXPERTS)),
        ],
        out_specs=(
            pl.BlockSpec((tm, d), row),
            pl.BlockSpec((tm * MOE_SLAB, V7X_LANES), row),
            pl.BlockSpec((tm, TOP_K), row),
            pl.BlockSpec((tm, TOP_K), row),
            pl.BlockSpec((tm, TOP_K), row),
            pl.BlockSpec((1, N_EXPERTS), lambda bi, i: (0, 0)),
        ),
        scratch_shapes=[pltpu.VMEM((1, N_EXPERTS), F32)],
        compiler_params=_cparams(("arbitrary", "arbitrary")),
        name=name,
    )(n2, h1, kv, wq, wo, gm.reshape(1, d), wrh, wrl, br.reshape(1, N_EXPERTS))


_HI_MASK = 0xFFFF0000


def _pack_rows(x):
    half = x.shape[1] // 2
    lo = lax.bitcast_convert_type(x[:, :half].astype(BF16).astype(F32), U32) >> 16
    hi = lax.bitcast_convert_type(x[:, half:].astype(BF16).astype(F32), U32) & jnp.uint32(_HI_MASK)
    return lo | hi


def _unpack_rows(u):
    lo = lax.bitcast_convert_type(u << 16, F32)
    hi = lax.bitcast_convert_type(u & jnp.uint32(_HI_MASK), F32)
    return lo, hi


def _sc_dispatch(slots, n3p, *, rows_total, name="moe_dispatch_sc"):
    n = n3p.shape[0]
    per_w = n // (SC_CORES * SC_SUBCORES)
    mesh = plsc.VectorSubcoreMesh(core_axis_name="c", subcore_axis_name="s",
                                  num_cores=SC_CORES, num_subcores=SC_SUBCORES)

    nchunk = per_w // SC_CHUNK_TOKENS

    def body(slots_hbm, n3p_hbm, out_hbm, idx_bufs, row_bufs, load_sems, scat_sems):
        wid = lax.axis_index("s") * SC_CORES + lax.axis_index("c")

        def base(c):
            return pl.multiple_of(wid * per_w + c * SC_CHUNK_TOKENS, SC_CHUNK_TOKENS)

        def load(c):
            return pltpu.async_copy(n3p_hbm.at[pl.ds(base(c), SC_CHUNK_TOKENS)], row_bufs.at[c % 2],
                                    load_sems.at[c % 2])

        def scatter(c, k):
            idx = idx_bufs.at[(c % 2) * TOP_K + k]
            pltpu.sync_copy(slots_hbm.at[pl.ds(k * n + base(c), SC_CHUNK_TOKENS)], idx)
            return pltpu.async_copy(row_bufs.at[c % 2], out_hbm.at[idx], scat_sems.at[c % 2])

        loads = [None] * nchunk
        scats = [None] * nchunk
        loads[0] = load(0)
        for c in range(nchunk):
            if c + 1 < nchunk:
                if c >= 1:
                    for s in scats[c - 1]:
                        s.wait()
                loads[c + 1] = load(c + 1)
            loads[c].wait()
            scats[c] = [scatter(c, k) for k in range(TOP_K)]
        for c in range(max(nchunk - 2, 0), nchunk):
            for s in scats[c]:
                s.wait()

    return pl.kernel(
        body,
        out_type=jax.ShapeDtypeStruct((rows_total, MOE_SLAB, V7X_LANES), U32),
        mesh=mesh,
        scratch_types=[
            pltpu.VMEM((2 * TOP_K, SC_CHUNK_TOKENS), I32),
            pltpu.VMEM((2, SC_CHUNK_TOKENS, MOE_SLAB, V7X_LANES), U32),
            pltpu.SemaphoreType.DMA((2,)),
            pltpu.SemaphoreType.DMA((2,)),
        ],
        name=name,
    )(slots, n3p)


def _moe_kernel(te_ref, tv_ref, tstart_ref, trows_ref,
                xs_hbm, w1_ref, b1_ref, w2_ref, b2_ref, yb_hbm,
                stage, x16, acc, w1p, sems, *, nf):
    i = pl.program_id(0)
    f = pl.program_id(1)
    rows = trows_ref[i]
    start = tstart_ref[i]
    nchunk = rows // MOE_ROW_ALIGN
    chunk_slab_rows = MOE_ROW_ALIGN * MOE_SLAB
    half = 2 * V7X_LANES
    nhalf = 2 * MOE_TF // half

    def slot_rows(c):
        return pl.multiple_of((c & 1) * chunk_slab_rows, chunk_slab_rows)

    def hbm_rows(ref, c):
        r0 = pl.multiple_of((start + c * MOE_ROW_ALIGN) * MOE_SLAB, chunk_slab_rows)
        return ref.at[pl.ds(r0, chunk_slab_rows)]

    def in_copy(c):
        return pltpu.make_async_copy(hbm_rows(xs_hbm, c), stage.at[pl.ds(slot_rows(c), chunk_slab_rows)],
                                     sems.at[c & 1])

    def out_copy(c):
        return pltpu.make_async_copy(stage.at[pl.ds(slot_rows(c), chunk_slab_rows)], hbm_rows(yb_hbm, c),
                                     sems.at[c & 1])

    @pl.when((rows > 0) & (f == 0))
    def _():
        in_copy(0).start()

        def land(c, carry):
            @pl.when(c + 1 < nchunk)
            def _():
                in_copy(c + 1).start()

            r0 = pl.multiple_of(c * MOE_ROW_ALIGN, MOE_ROW_ALIGN)

            @pl.when(c < nchunk)
            def _():
                in_copy(c).wait()
                s0 = slot_rows(c)
                owned = lax.broadcasted_iota(I32, (MOE_ROW_ALIGN, V7X_LANES), 0) < tv_ref[i] - r0
                for j in range(MOE_SLAB):
                    u = jnp.where(owned, stage[pl.ds(s0 + j, MOE_ROW_ALIGN, stride=MOE_SLAB), :], jnp.uint32(0))
                    lo, hi = _unpack_rows(u)
                    x16[pl.ds(r0, MOE_ROW_ALIGN), j * V7X_LANES:(j + 1) * V7X_LANES] = lo.astype(BF16)
                    x16[pl.ds(r0, MOE_ROW_ALIGN), (MOE_SLAB + j) * V7X_LANES:(MOE_SLAB + j + 1) * V7X_LANES] = (
                        hi.astype(BF16))

            @pl.when(c >= nchunk)
            def _():
                x16[pl.ds(r0, MOE_ROW_ALIGN), :] = jnp.zeros((MOE_ROW_ALIGN, D_MODEL), BF16)

            acc[pl.ds(r0, MOE_ROW_ALIGN), :] = jnp.zeros((MOE_ROW_ALIGN, D_MODEL), F32)
            return carry

        lax.fori_loop(0, jnp.maximum(nchunk, MOE_BLOCK_CHUNKS), land, 0)

    @pl.when(rows > 0)
    def _():
        lane = lax.broadcasted_iota(I32, (D_MODEL // 2, V7X_LANES), 1)
        first = lane < V7X_LANES // 2
        even = (2 * lane) & (V7X_LANES - 1)
        take = lambda a, ix: jnp.take_along_axis(a, ix, axis=1, mode="promise_in_bounds")

        def separate(c):
            u = pltpu.bitcast(w1_ref[:, c * half:(c + 1) * half].astype(BF16), U32)
            ua, ub = u[:, :V7X_LANES], u[:, V7X_LANES:]
            glu = jnp.where(first, take(ua, even), take(ub, even))
            lin = jnp.where(first, take(ua, even + 1), take(ub, even + 1))
            return pltpu.bitcast(jnp.concatenate([glu, lin], axis=1), BF16)

        def activation(h):
            hg = jnp.minimum(h[:, :V7X_LANES], SWIGLU_LIMIT)
            hl = jnp.clip(h[:, V7X_LANES:], -SWIGLU_LIMIT, SWIGLU_LIMIT)
            return hg * jax.nn.sigmoid(SWIGLU_ALPHA * hg) * (hl + 1.0)

        w2 = w2_ref[...].astype(BF16)
        b1 = b1_ref[...]

        block_rows = MOE_BLOCK_CHUNKS * MOE_ROW_ALIGN
        x = x16[0:block_rows, :]
        parts = []
        for c in range(nhalf):
            wc = separate(c)
            w1p[:, c * half:(c + 1) * half] = wc
            parts.append(activation(jnp.dot(x, wc, preferred_element_type=F32) + b1[:, c * half:(c + 1) * half]))
        a = jnp.concatenate(parts, axis=-1).astype(BF16)
        acc[0:block_rows, :] += jnp.dot(a, w2, preferred_element_type=F32)

        def mlp(c0, nc):
            r0 = pl.multiple_of(c0 * MOE_ROW_ALIGN, MOE_ROW_ALIGN)
            nr = nc * MOE_ROW_ALIGN
            x = x16[pl.ds(r0, nr), :]
            h = jnp.dot(x, w1p[...], preferred_element_type=F32) + b1
            a = jnp.concatenate([activation(h[:, c * half:(c + 1) * half]) for c in range(nhalf)], axis=-1)
            acc[pl.ds(r0, nr), :] += jnp.dot(a.astype(BF16), w2, preferred_element_type=F32)

        extra = jnp.maximum(nchunk - MOE_BLOCK_CHUNKS, 0)
        size = MOE_TMAX // MOE_ROW_ALIGN - MOE_BLOCK_CHUNKS
        assert size & (size - 1) == 0
        while size >= 1:
            done = MOE_BLOCK_CHUNKS + (extra & ~(2 * size - 1))

            @pl.when((extra & size) != 0)
            def _(done=done, size=size):
                mlp(done, size)

            size //= 2

    @pl.when((rows > 0) & (f == nf - 1))
    def _():
        def emit(c, carry):
            @pl.when(c >= 2)
            def _():
                out_copy(c - 2).wait()

            r0 = pl.multiple_of(c * MOE_ROW_ALIGN, MOE_ROW_ALIGN)
            s0 = slot_rows(c)
            u = _pack_rows(acc[pl.ds(r0, MOE_ROW_ALIGN), :] + b2_ref[...])
            for j in range(MOE_SLAB):
                stage[pl.ds(s0 + j, MOE_ROW_ALIGN, stride=MOE_SLAB), :] = u[:, j * V7X_LANES:(j + 1) * V7X_LANES]
            out_copy(c).start()
            return carry

        lax.fori_loop(0, nchunk, emit, 0)

        @pl.when(nchunk >= 2)
        def _():
            out_copy(nchunk - 2).wait()

        out_copy(nchunk - 1).wait()


def _moe(tile_e, tile_owned, tile_start, tile_rows, xs, w1, b1p, w2, b2, *, name="moe_mlp"):
    nt = tile_e.shape[0]
    nf = D_FF // MOE_TF

    def w1_map(i, f, te, tv, ts, tr):
        return (te[i], 0, jnp.where(tv[i] > 0, f, nf - 1))

    def w2_map(i, f, te, tv, ts, tr):
        return (te[i], jnp.where(tv[i] > 0, f, nf - 1), 0)

    def b2_map(i, f, te, tv, ts, tr):
        return (te[i], 0, 0)

    grid_spec = pltpu.PrefetchScalarGridSpec(
        num_scalar_prefetch=4,
        grid=(nt, nf),
        in_specs=[
            pl.BlockSpec(memory_space=pl.ANY),
            pl.BlockSpec((None, D_MODEL, 2 * MOE_TF), w1_map),
            pl.BlockSpec((None, 1, 2 * MOE_TF), w1_map),
            pl.BlockSpec((None, MOE_TF, D_MODEL), w2_map),
            pl.BlockSpec((None, 1, D_MODEL), b2_map),
        ],
        out_specs=pl.BlockSpec(memory_space=pl.ANY),
        scratch_shapes=[
            pltpu.VMEM((2 * MOE_ROW_ALIGN * MOE_SLAB, V7X_LANES), U32),
            pltpu.VMEM((MOE_TMAX, D_MODEL), BF16),
            pltpu.VMEM((MOE_TMAX, D_MODEL), F32),
            pltpu.VMEM((D_MODEL, 2 * MOE_TF), BF16),
            pltpu.SemaphoreType.DMA((2,)),
        ],
    )
    return pl.pallas_call(
        functools.partial(_moe_kernel, nf=nf),
        out_shape=jax.ShapeDtypeStruct(xs.shape, xs.dtype),
        grid_spec=grid_spec,
        input_output_aliases={4: 0},
        compiler_params=_cparams(("arbitrary", "arbitrary")),
        name=name,
    )(tile_e, tile_owned, tile_start, tile_rows, xs, w1, b1p.reshape(N_EXPERTS, 1, 2 * D_FF), w2,
      b2.reshape(N_EXPERTS, 1, D_MODEL))


def _sc_gather(slots, yb3, *, n_tok, name="moe_gather_sc"):
    per_w = n_tok // (SC_CORES * SC_SUBCORES)
    mesh = plsc.VectorSubcoreMesh(core_axis_name="c", subcore_axis_name="s",
                                  num_cores=SC_CORES, num_subcores=SC_SUBCORES)

    items = [(c, k) for c in range(per_w // SC_CHUNK_TOKENS) for k in range(TOP_K)]

    def body(slots_hbm, yb_hbm, out_hbm, idx_bufs, row_bufs, gather_sems, write_sems):
        wid = lax.axis_index("s") * SC_CORES + lax.axis_index("c")

        def offset(j):
            c, k = items[j]
            return pl.multiple_of(k * n_tok + wid * per_w + c * SC_CHUNK_TOKENS, SC_CHUNK_TOKENS)

        def gather(j):
            idx = idx_bufs.at[j % 2]
            pltpu.sync_copy(slots_hbm.at[pl.ds(offset(j), SC_CHUNK_TOKENS)], idx)
            return pltpu.async_copy(yb_hbm.at[idx], row_bufs.at[j % 2], gather_sems.at[j % 2])

        def write(j):
            return pltpu.async_copy(row_bufs.at[j % 2], out_hbm.at[pl.ds(offset(j), SC_CHUNK_TOKENS)],
                                    write_sems.at[j % 2])

        n_items = len(items)
        gathers = [None] * n_items
        writes = [None] * n_items
        gathers[0] = gather(0)
        for j in range(n_items):
            if j + 1 < n_items:
                if j >= 1:
                    writes[j - 1].wait()
                gathers[j + 1] = gather(j + 1)
            gathers[j].wait()
            writes[j] = write(j)
        for j in range(max(n_items - 2, 0), n_items):
            writes[j].wait()

    return pl.kernel(
        body,
        out_type=jax.ShapeDtypeStruct((TOP_K * n_tok, MOE_SLAB, V7X_LANES), U32),
        mesh=mesh,
        scratch_types=[
            pltpu.VMEM((2, SC_CHUNK_TOKENS), I32),
            pltpu.VMEM((2, SC_CHUNK_TOKENS, MOE_SLAB, V7X_LANES), U32),
            pltpu.SemaphoreType.DMA((2,)),
            pltpu.SemaphoreType.DMA((2,)),
        ],
        name=name,
    )(slots, yb3)


def _combine_kernel(yg_ref, h2_ref, tw_ref, gf_ref, o_ref, *, tm):
    tw = tw_ref[...]
    cols_lo, cols_hi = [], []
    for j in range(MOE_SLAB):
        clo = chi = None
        for k in range(TOP_K):
            lo, hi = _unpack_rows(yg_ref[k, pl.ds(j, tm, stride=MOE_SLAB), :])
            w = tw[:, k:k + 1]
            clo = w * lo if clo is None else clo + w * lo
            chi = w * hi if chi is None else chi + w * hi
        cols_lo.append(clo)
        cols_hi.append(chi)
    h3 = h2_ref[...] + jnp.concatenate(cols_lo + cols_hi, axis=-1)
    o_ref[...] = _rms(h3, gf_ref[...])


def _combine(yg, h2, tw, gf, *, tm, name="moe_combine"):
    n, d = h2.shape
    return pl.pallas_call(
        functools.partial(_combine_kernel, tm=tm),
        out_shape=jax.ShapeDtypeStruct((n, d), F32),
        grid=(n // tm,),
        in_specs=[
            pl.BlockSpec((TOP_K, tm * MOE_SLAB, V7X_LANES), lambda i: (0, i, 0)),
            pl.BlockSpec((tm, d), lambda i: (i, 0)),
            pl.BlockSpec((tm, TOP_K), lambda i: (i, 0)),
            pl.BlockSpec((1, d), lambda i: (0, 0)),
        ],
        out_specs=pl.BlockSpec((tm, d), lambda i: (i, 0)),
        compiler_params=_cparams(("parallel",)),
        name=name,
    )(yg, h2, tw, gf.reshape(1, d))


def _rope_tables(positions):
    b, s = positions.shape
    half = HEAD_DIM // 2
    inv_freq = 1.0 / (ROPE_THETA ** (jnp.arange(half, dtype=F32) / half))
    ang = positions.astype(F32)[..., None] * inv_freq
    cos = jnp.cos(ang)
    sin = jnp.sin(ang)
    cs = jnp.concatenate([cos, cos], axis=-1)
    sn = jnp.concatenate([-sin, sin], axis=-1)

    def deint(a):
        a = a.reshape(b, s // ATT_TILE, WIN_BLOCK, DIL_MAX, HEAD_DIM)
        return a.transpose(0, 1, 3, 2, 4)

    return deint(cs), deint(sn)


def _routing(counts, n_tok):
    cnt = counts.reshape(N_EXPERTS).astype(I32)
    padded = (cnt + MOE_ROW_ALIGN - 1) // MOE_ROW_ALIGN * MOE_ROW_ALIGN
    gstart = (jnp.cumsum(padded) - padded).astype(I32)
    rows_total = n_tok * TOP_K + N_EXPERTS * MOE_ROW_ALIGN

    def first_above(ends, q):
        return jnp.minimum(jnp.sum((ends[None, :] <= q[:, None]).astype(I32), axis=1), N_EXPERTS - 1)

    main_rows = jnp.minimum(padded, MOE_TMAX)
    main = (jnp.arange(N_EXPERTS, dtype=I32), jnp.minimum(cnt, MOE_TMAX).astype(I32), gstart.astype(I32),
            main_rows.astype(I32))

    over = padded - main_rows
    nt_e = (over + MOE_TMAX - 1) // MOE_TMAX
    tend = jnp.cumsum(nt_e)
    tstart = tend - nt_e
    n_over = tend[-1]
    ti = jnp.arange(rows_total // MOE_TMAX, dtype=I32)
    valid = ti < n_over
    tic = jnp.clip(ti, 0, jnp.maximum(n_over - 1, 0))
    te = first_above(tend, tic)
    local = tic - tstart[te]
    row0 = gstart[te] + (local + 1) * MOE_TMAX
    rows = jnp.where(valid, jnp.clip(over[te] - local * MOE_TMAX, 0, MOE_TMAX), 0)
    owned = jnp.where(valid, jnp.clip(cnt[te] - (local + 1) * MOE_TMAX, 0, MOE_TMAX), 0)
    overflow = (te, owned.astype(I32), row0.astype(I32), rows.astype(I32))
    return gstart, rows_total, main, overflow, n_over > 0


def kernel(x, mem, positions, norm_mix_g, w_in, lb_raw, hgrn_norm_g, w_br_hgrn, w_br_attn, w_out,
           norm_cross_g, norm_mem_g, w_cq, w_ckv, w_co, norm_moe_g, w_router, b_router,
           w_mlp1, b_mlp1, w_mlp2, b_mlp2, norm_final_g):
    bsz, seq, d = x.shape
    assert w_in.shape[0] == 1 and d == D_MODEL and seq % ATT_TILE == 0
    n_tok = bsz * seq
    lower_bounds = jnp.cumsum(jax.nn.softmax(lb_raw.astype(F32), axis=0), axis=0)
    cs, sn = _rope_tables(positions)

    wl = w_in[0]
    c_h = 4 * HG_WIDTH
    c_a = c_h + ATT_Q_WIDTH + 2 * ATT_KV_WIDTH
    tn = 1024
    x2d = x.reshape(n_tok, d)
    n_h, n_a, n_g = c_h // tn, (c_a - c_h) // tn, 2 * D_MODEL // tn
    proj = _in_proj(x2d, norm_mix_g[0], wl, tm=1024, tn=tn, name="in_proj",
                    col_block=lambda j: jnp.where(j < n_h, j, jnp.where(j < n_h + n_g, j + n_a, j - n_g)))
    proj3 = proj.reshape(bsz, seq, -1)

    o_h = _hgrn(proj3, lower_bounds[0], hgrn_norm_g[0], ts=1024)
    o_a = _dilated_attn(proj3, cs, sn, q_block=(c_h + 2 * D_MODEL) // HEAD_DIM)

    h1, n2 = _merge_out(o_h.reshape(n_tok, HG_WIDTH), o_a.reshape(n_tok, ATT_KV_WIDTH), proj, x2d,
                        w_br_hgrn[0].astype(BF16), w_br_attn[0].astype(BF16), w_out[0].astype(BF16),
                        norm_cross_g[0], tm=256, gate_block=c_h // (2 * D_MODEL))

    mem2d = mem.reshape(-1, d)
    kv = _in_proj(mem2d, norm_mem_g, w_ckv[0], tm=mem2d.shape[0], tn=2 * CROSS_WIDTH, name="mem_kv")
    wrh = w_router[0].astype(BF16)
    wrl = (w_router[0] - wrh.astype(F32)).astype(BF16)
    h2, n3, top_e, top_w, rank, counts = _cross(
        n2, h1, kv, w_cq[0].astype(BF16), w_co[0].astype(BF16), norm_moe_g[0], wrh, wrl, b_router[0],
        batch=bsz, tm=512)

    gstart, rows_total, main_tiles, over_tiles, has_over = _routing(counts, n_tok)
    hot = top_e.T[None] == jnp.arange(N_EXPERTS, dtype=I32)[:, None, None]
    slots = (rank.T + jnp.sum(jnp.where(hot, gstart[:, None, None], 0), axis=0)).reshape(-1)
    xs = _sc_dispatch(slots, n3.reshape(n_tok, MOE_SLAB, V7X_LANES), rows_total=rows_total)
    xs = xs.reshape(rows_total * MOE_SLAB, V7X_LANES)

    b1p = b_mlp1[0].reshape(N_EXPERTS, -1, V7X_LANES, 2).transpose(0, 1, 3, 2).reshape(N_EXPERTS, 2 * D_FF)
    def run_moe(tiles, rows_buf, name):
        return _moe(*tiles, rows_buf, w_mlp1[0], b1p, w_mlp2[0], b_mlp2[0], name=name)

    yb = run_moe(main_tiles, xs, "moe_mlp")
    yb = lax.cond(has_over, lambda rows_buf: run_moe(over_tiles, rows_buf, "moe_mlp_overflow"),
                  lambda rows_buf: rows_buf, yb)

    yg = _sc_gather(slots, yb.reshape(rows_total, MOE_SLAB, V7X_LANES), n_tok=n_tok)
    out = _combine(yg.reshape(TOP_K, n_tok * MOE_SLAB, V7X_LANES), h2, top_w, norm_final_g, tm=COMBINE_TOKENS)
    return out.reshape(bsz, seq, d)
```

```python
import functools

import jax
import jax.numpy as jnp
from jax import lax
from jax.experimental import pallas as pl
from jax.experimental.pallas import tpu as pltpu
from jax.experimental.pallas import tpu_sc as plsc

F32 = jnp.float32
BF16 = jnp.bfloat16
I32 = jnp.int32
U32 = jnp.uint32

D_MODEL = 2048
HEAD_DIM = 128
HG_HEADS = 8
HG_WIDTH = HG_HEADS * HEAD_DIM
N_KV_HEADS = 8
N_GROUPS = 3
ATT_Q_WIDTH = N_GROUPS * N_KV_HEADS * HEAD_DIM
ATT_KV_WIDTH = N_KV_HEADS * HEAD_DIM
WIN_BLOCK = 128
ROPE_THETA = 10000.0
CROSS_HEADS = 4
CROSS_WIDTH = CROSS_HEADS * HEAD_DIM
N_EXPERTS = 32
TOP_K = 4
D_FF = D_MODEL
SWIGLU_ALPHA = 1.702
SWIGLU_LIMIT = 7.0
NORM_EPS = 1e-6

V7X_LANES = 128
V7X_VMEM_LIMIT_BYTES = 56 * 1024 * 1024

DIL_MAX = 16
ATT_TILE = DIL_MAX * WIN_BLOCK
ATT_UNROLL = 16
HG_BLOCK = 16
HG_GROUP = 32
NEG_BIG = -1e30

MOE_TMAX = 1536
MOE_ROW_ALIGN = 128
MOE_BLOCK_CHUNKS = 8
MOE_TF = 512
MOE_SLAB = D_MODEL // 2 // V7X_LANES
SC_CORES = 2
SC_SUBCORES = 16
SC_CHUNK_TOKENS = 32
COMBINE_TOKENS = 512


def _rms(x, g):
    ms = jnp.mean(x * x, axis=-1, keepdims=True)
    return x * lax.rsqrt(ms + NORM_EPS) * g


def _cparams(sem, vmem=V7X_VMEM_LIMIT_BYTES):
    return pltpu.CompilerParams(dimension_semantics=sem, vmem_limit_bytes=vmem)


def _in_proj_kernel(x_ref, g_ref, w_ref, o_ref, xn_ref):
    @pl.when(pl.program_id(1) == 0)
    def _():
        xn_ref[...] = _rms(x_ref[...], g_ref[...]).astype(BF16)

    o_ref[...] = jnp.dot(xn_ref[...], w_ref[...].astype(BF16), preferred_element_type=F32).astype(o_ref.dtype)


def _in_proj(x2d, g, w, *, tm, tn, name, col_block=lambda j: j, ncols=None):
    n, d = x2d.shape
    wc = w.shape[1] if ncols is None else ncols
    return pl.pallas_call(
        _in_proj_kernel,
        out_shape=jax.ShapeDtypeStruct((n, wc), BF16),
        grid=(n // tm, wc // tn),
        in_specs=[
            pl.BlockSpec((tm, d), lambda i, j: (i, 0)),
            pl.BlockSpec((1, d), lambda i, j: (0, 0)),
            pl.BlockSpec((d, tn), lambda i, j: (0, col_block(j))),
        ],
        out_specs=pl.BlockSpec((tm, tn), lambda i, j: (i, j)),
        scratch_shapes=[pltpu.VMEM((tm, d), BF16)],
        compiler_params=_cparams(("parallel", "arbitrary")),
        name=name,
    )(x2d, g.reshape(1, d), w)


def _hgrn_kernel(q_ref, f_ref, i_ref, g_ref, lb_ref, gn_ref, o_ref, st_ref, kin_s, b_s, v_s, *, ts):
    @pl.when(pl.program_id(2) == 0)
    def _():
        st_ref[...] = jnp.zeros_like(st_ref)

    lb = lb_ref[...]
    oml = 1.0 - lb
    gn = gn_ref[...]
    half = HG_BLOCK // 2
    row = lax.broadcasted_iota(I32, (HG_BLOCK, HEAD_DIM), 0)
    row8 = lax.broadcasted_iota(I32, (half, HEAD_DIM), 0)
    nt_dims = (((1,), (1,)), ((), ()))
    tn_dims = (((0,), (0,)), ((), ()))

    def front(g, t0):
        sl = pl.ds(t0 + g * HG_BLOCK, HG_BLOCK)
        q = q_ref[sl, :].astype(F32)
        hf = f_ref[sl, :].astype(F32)
        v = i_ref[sl, :].astype(F32)
        kin = oml * jax.nn.sigmoid(-hf)
        b = jnp.log2(lb + oml * jax.nn.sigmoid(hf))
        for sh in (1, 2, 4, 8):
            b = b + jnp.where(row >= sh, pltpu.roll(b, sh, 0), 0.0)
        kin_s[g] = kin
        b_s[g] = b
        v_s[g] = v
        q_lo, q_hi = q[:half], q[half:]
        b_lo, b_hi = b[:half], b[half:]
        o_lo = jnp.zeros((half, HEAD_DIM), F32)
        o_hi = jnp.zeros((half, HEAD_DIM), F32)
        for s in range(HG_BLOCK):
            ks = kin_s[g, s:s + 1, :]
            bs = b_s[g, s:s + 1, :]
            vs = v_s[g, s:s + 1, :]
            if s < half:
                w = q_lo * ks * jnp.exp2(b_lo - bs)
                if s > 0:
                    w = jnp.where(row8 >= s, w, 0.0)
                o_lo = o_lo + jnp.sum(w, axis=-1, keepdims=True) * vs
                w = q_hi * ks * jnp.exp2(b_hi - bs)
            else:
                w = q_hi * ks * jnp.exp2(b_hi - bs)
                if s > half:
                    w = jnp.where(row8 >= s - half, w, 0.0)
            o_hi = o_hi + jnp.sum(w, axis=-1, keepdims=True) * vs
        bl = b_s[g, HG_BLOCK - 1:HG_BLOCK, :]
        qd = (q * jnp.exp2(b)).astype(BF16)
        kd = (kin * jnp.exp2(bl - b)).astype(BF16)
        upd = lax.dot_general(v.astype(BF16), kd, tn_dims, preferred_element_type=F32)
        return jnp.concatenate([o_lo, o_hi], axis=0), qd, upd, jnp.exp2(bl)

    def body(i, carry):
        t0 = pl.multiple_of(i * (HG_GROUP * HG_BLOCK), HG_GROUP * HG_BLOCK)
        fronts = [front(g, t0) for g in range(HG_GROUP)]
        st = st_ref[...]
        for g, (o_diag, qd, upd, dec) in enumerate(fronts):
            o = o_diag + lax.dot_general(qd, st.astype(BF16), nt_dims, preferred_element_type=F32)
            st = st * dec + upd
            sl = pl.ds(t0 + g * HG_BLOCK, HG_BLOCK)
            hg = g_ref[sl, :].astype(F32)
            o_ref[sl, :] = (_rms(o, gn) * (hg * jax.nn.sigmoid(hg))).astype(o_ref.dtype)
        st_ref[...] = st
        return carry

    lax.fori_loop(0, ts // (HG_GROUP * HG_BLOCK), body, 0)


def _hgrn(proj_h, lb, gn, *, ts, name="hgrn"):
    b, s, _ = proj_h.shape
    h = HG_HEADS

    def spec(off):
        return pl.BlockSpec((None, ts, HEAD_DIM), lambda bi, hi, si: (bi, si, off + hi))

    vec = pl.BlockSpec((1, HEAD_DIM), lambda bi, hi, si: (0, hi))
    return pl.pallas_call(
        functools.partial(_hgrn_kernel, ts=ts),
        out_shape=jax.ShapeDtypeStruct((b, s, HG_WIDTH), BF16),
        grid=(b, h, s // ts),
        in_specs=[spec(0), spec(h), spec(2 * h), spec(3 * h), vec, vec],
        out_specs=pl.BlockSpec((None, ts, HEAD_DIM), lambda bi, hi, si: (bi, si, hi)),
        scratch_shapes=[
            pltpu.VMEM((HEAD_DIM, HEAD_DIM), F32),
            pltpu.VMEM((HG_GROUP, HG_BLOCK, HEAD_DIM), F32),
            pltpu.VMEM((HG_GROUP, HG_BLOCK, HEAD_DIM), F32),
            pltpu.VMEM((HG_GROUP, HG_BLOCK, HEAD_DIM), F32),
        ],
        compiler_params=_cparams(("parallel", "parallel", "arbitrary")),
        name=name,
    )(proj_h, proj_h, proj_h, proj_h, lb.reshape(1, HG_WIDTH), gn.reshape(1, HG_WIDTH))


def _attn_bias(kind):
    rq = lax.broadcasted_iota(I32, (WIN_BLOCK, 2 * WIN_BLOCK), 0)
    ck = lax.broadcasted_iota(I32, (WIN_BLOCK, 2 * WIN_BLOCK), 1)
    if kind == 2:
        dist = rq + WIN_BLOCK - ck
        first = ck < WIN_BLOCK
    elif kind == 1:
        dist = 4 * ((rq & 31) - (ck & 63) + 32) + ((rq >> 5) - (ck >> 6))
        first = (ck & 63) < 32
    else:
        dist = 16 * ((rq & 7) - (ck & 15) + 8) + ((rq >> 3) - (ck >> 4))
        first = (ck & 15) < 8
    valid = (dist >= 0) & (dist <= WIN_BLOCK)
    return (jnp.where(valid, 0.0, NEG_BIG).astype(F32),
            jnp.where(valid & jnp.logical_not(first), 0.0, NEG_BIG).astype(F32))


def _attn_kernel(q0_ref, q1_ref, q2_ref, k_ref, v_ref, cs_ref, sn_ref, o_ref,
                 qr, kext, vext, acc, mrun, lrun, bias, onat, u0, u1, u2, u3, u4):
    ti = pl.program_id(2)
    wb = WIN_BLOCK
    scale = HEAD_DIM ** -0.5

    ubufs = (u0, u1, u2, u3, u4)
    for src, ub in zip((q0_ref, q1_ref, q2_ref, k_ref, v_ref), ubufs):
        ub[...] = pltpu.bitcast(src[...], U32)

    @pl.when(ti == 0)
    def _():
        kext[:, 0:wb, :] = jnp.zeros((DIL_MAX, wb, HEAD_DIM), F32)
        vext[:, 0:wb, :] = jnp.zeros((DIL_MAX, wb, HEAD_DIM), F32)

    @pl.when(ti > 0)
    def _():
        kext[:, 0:wb, :] = kext[:, wb:2 * wb, :]
        vext[:, 0:wb, :] = vext[:, wb:2 * wb, :]

    @pl.when((pl.program_id(0) == 0) & (pl.program_id(1) == 0) & (ti == 0))
    def _():
        for kind in range(N_GROUPS):
            full, nofirst = _attn_bias(kind)
            bias[2 * kind] = full
            bias[2 * kind + 1] = nofirst

    def rope_body(rp, carry):
        rows = pl.ds(rp, wb, stride=DIL_MAX // 2)
        parts = [_unpack_rows(ub[rows, :]) for ub in ubufs]
        for par in range(2):
            r = 2 * rp + par
            cs = cs_ref[r]
            sn = sn_ref[r]
            for g in range(N_GROUPS):
                q = parts[g][par]
                qr[g, r] = (q * cs + pltpu.roll(q, HEAD_DIM // 2, 1) * sn) * scale
            k = parts[N_GROUPS][par]
            kext[r, wb:2 * wb, :] = k * cs + pltpu.roll(k, HEAD_DIM // 2, 1) * sn
            vext[r, wb:2 * wb, :] = parts[N_GROUPS + 1][par]
            acc[r] = jnp.zeros((wb, HEAD_DIM), F32)
            mrun[r] = jnp.full((wb, HEAD_DIM), NEG_BIG, F32)
            lrun[r] = jnp.zeros((wb, HEAD_DIM), F32)
        return carry

    lax.fori_loop(0, DIL_MAX // 2, rope_body, 0)

    nt_dims = (((1,), (1,)), ((), ()))

    def block(qb, kb, vb, bias_blk):
        s = lax.dot_general(qb.astype(BF16), kb.astype(BF16), nt_dims, preferred_element_type=F32)
        s = s + bias_blk
        m = jnp.max(s, axis=-1, keepdims=True)
        p = jnp.exp(s - m)
        l = jnp.sum(p, axis=-1, keepdims=True)
        n = jnp.dot(p.astype(BF16), vb.astype(BF16), preferred_element_type=F32)
        return n, jnp.broadcast_to(m, (wb, HEAD_DIM)), jnp.broadcast_to(l, (wb, HEAD_DIM))

    def merge(r, rows, n, m, l):
        m_old = mrun[r, rows, :]
        m_new = jnp.maximum(m_old, m)
        a = jnp.exp(m_old - m_new)
        bb = jnp.exp(m - m_new)
        acc[r, rows, :] = acc[r, rows, :] * a + n * bb
        lrun[r, rows, :] = lrun[r, rows, :] * a + l * bb
        mrun[r, rows, :] = m_new

    first_tile = jnp.where(ti == 0, 1, 0)


    def g2_body(i, carry):
        rs = [i * ATT_UNROLL + u for u in range(ATT_UNROLL)]
        res = [block(qr[2, r], kext[r], vext[r], bias[4 + first_tile]) for r in rs]
        for r, (n, m, l) in zip(rs, res):
            merge(r, pl.ds(0, wb), n, m, l)
        return carry

    lax.fori_loop(0, DIL_MAX // ATT_UNROLL, g2_body, 0)

    def g1_body(i, carry):
        res = []
        for u in range(ATT_UNROLL // 4):
            mb = i * (ATT_UNROLL // 4) + u
            q0 = pl.multiple_of(32 * mb, 32)
            k0 = pl.multiple_of(96 + 32 * mb, 32)
            use_first = jnp.where(mb == 0, first_tile, 0)
            for r4 in range(4):
                qb = jnp.concatenate([qr[1, r4 + 4 * j, pl.ds(q0, 32), :] for j in range(4)], axis=0)
                kb = jnp.concatenate([kext[r4 + 4 * j, pl.ds(k0, 64), :] for j in range(4)], axis=0)
                vb = jnp.concatenate([vext[r4 + 4 * j, pl.ds(k0, 64), :] for j in range(4)], axis=0)
                res.append((r4, q0, block(qb, kb, vb, bias[2 + use_first])))
        for r4, q0, (n, m, l) in res:
            for j in range(4):
                sl = slice(32 * j, 32 * (j + 1))
                merge(r4 + 4 * j, pl.ds(q0, 32), n[sl], m[sl], l[sl])
        return carry

    lax.fori_loop(0, DIL_MAX // ATT_UNROLL, g1_body, 0)

    def g0_body(i, carry):
        res = []
        for u in range(ATT_UNROLL):
            mb = i * ATT_UNROLL + u
            q0 = pl.multiple_of(8 * mb, 8)
            k0 = pl.multiple_of(120 + 8 * mb, 8)
            qb = jnp.concatenate([qr[0, r, pl.ds(q0, 8), :] for r in range(DIL_MAX)], axis=0)
            kb = jnp.concatenate([kext[r, pl.ds(k0, 16), :] for r in range(DIL_MAX)], axis=0)
            vb = jnp.concatenate([vext[r, pl.ds(k0, 16), :] for r in range(DIL_MAX)], axis=0)
            use_first = jnp.where(mb == 0, first_tile, 0)
            res.append((q0, block(qb, kb, vb, bias[use_first])))
        for q0, (n, m, l) in res:
            for r in range(DIL_MAX):
                sl = slice(8 * r, 8 * (r + 1))
                merge(r, pl.ds(q0, 8), n[sl], m[sl], l[sl])
        return carry

    lax.fori_loop(0, DIL_MAX // ATT_UNROLL, g0_body, 0)

    for r in range(DIL_MAX):
        onat[pl.ds(r, wb, stride=DIL_MAX), :] = acc[r] / lrun[r]
    o_ref[...] = onat[...].astype(o_ref.dtype)


def _dilated_attn(proj, cs, sn, *, q_block, name="dilated_attn"):
    b, s, _ = proj.shape
    nt = s // ATT_TILE
    h = N_KV_HEADS

    def spec(off):
        return pl.BlockSpec((None, ATT_TILE, HEAD_DIM), lambda bi, hi, ti: (bi, ti, q_block + off + hi))

    tab = pl.BlockSpec((None, None, DIL_MAX, WIN_BLOCK, HEAD_DIM), lambda bi, hi, ti: (bi, ti, 0, 0, 0))
    ubuf = pltpu.VMEM((ATT_TILE // 2, HEAD_DIM), U32)
    return pl.pallas_call(
        _attn_kernel,
        out_shape=jax.ShapeDtypeStruct((b, nt * ATT_TILE, ATT_KV_WIDTH), BF16),
        grid=(b, h, nt),
        in_specs=[spec(0), spec(h), spec(2 * h), spec(3 * h), spec(4 * h), tab, tab],
        out_specs=pl.BlockSpec((None, ATT_TILE, HEAD_DIM), lambda bi, hi, ti: (bi, ti, hi)),
        scratch_shapes=[
            pltpu.VMEM((N_GROUPS, DIL_MAX, WIN_BLOCK, HEAD_DIM), F32),
            pltpu.VMEM((DIL_MAX, 2 * WIN_BLOCK, HEAD_DIM), F32),
            pltpu.VMEM((DIL_MAX, 2 * WIN_BLOCK, HEAD_DIM), F32),
            pltpu.VMEM((DIL_MAX, WIN_BLOCK, HEAD_DIM), F32),
            pltpu.VMEM((DIL_MAX, WIN_BLOCK, HEAD_DIM), F32),
            pltpu.VMEM((DIL_MAX, WIN_BLOCK, HEAD_DIM), F32),
            pltpu.VMEM((2 * N_GROUPS, WIN_BLOCK, 2 * WIN_BLOCK), F32),
            pltpu.VMEM((ATT_TILE, HEAD_DIM), F32),
            ubuf, ubuf, ubuf, ubuf, ubuf,
        ],
        compiler_params=_cparams(("arbitrary", "arbitrary", "arbitrary")),
        name=name,
    )(proj, proj, proj, proj, proj, cs, sn)


def _merge_out_kernel(oh_ref, oa_ref, gate_ref, x_ref, wh_ref, wa_ref, wo_ref, gc_ref, h1_ref, n2_ref):
    ga = gate_ref[:, :D_MODEL].astype(F32)
    gb = gate_ref[:, D_MODEL:].astype(F32)
    yh = jnp.dot(oh_ref[...], wh_ref[...], preferred_element_type=F32)
    ya = jnp.dot(oa_ref[...], wa_ref[...], preferred_element_type=F32)
    merged = jax.nn.sigmoid(ga) * yh + jax.nn.sigmoid(gb) * ya
    h1 = x_ref[...] + jnp.dot(merged.astype(BF16), wo_ref[...], preferred_element_type=F32)
    h1_ref[...] = h1
    n2_ref[...] = _rms(h1, gc_ref[...]).astype(BF16)


def _merge_out(oh, oa, gates, x2d, wh, wa, wo, gc, *, tm, gate_block, name="merge_out"):
    n, d = x2d.shape

    def const(shape):
        return pl.BlockSpec(shape, lambda i: (0, 0), pipeline_mode=pl.Buffered(1))

    return pl.pallas_call(
        _merge_out_kernel,
        out_shape=(jax.ShapeDtypeStruct((n, d), F32), jax.ShapeDtypeStruct((n, d), BF16)),
        grid=(n // tm,),
        in_specs=[
            pl.BlockSpec((tm, HG_WIDTH), lambda i: (i, 0)),
            pl.BlockSpec((tm, ATT_KV_WIDTH), lambda i: (i, 0)),
            pl.BlockSpec((tm, 2 * d), lambda i: (i, gate_block)),
            pl.BlockSpec((tm, d), lambda i: (i, 0)),
            const((HG_WIDTH, d)), const((ATT_KV_WIDTH, d)), const((d, d)), const((1, d)),
        ],
        out_specs=(pl.BlockSpec((tm, d), lambda i: (i, 0)), pl.BlockSpec((tm, d), lambda i: (i, 0))),
        compiler_params=_cparams(("parallel",)),
        name=name,
    )(oh, oa, gates, x2d, wh, wa, wo, gc.reshape(1, d))


def _cross_kernel(n2_ref, h1_ref, kv_ref, wq_ref, wo_ref, gm_ref, wrh_ref, wrl_ref, br_ref,
                  h2_ref, n3_ref, idx_ref, tw_ref, rank_ref, cnt_ref, carry_ref, *, tm):
    @pl.when((pl.program_id(0) == 0) & (pl.program_id(1) == 0))
    def _():
        carry_ref[...] = jnp.zeros_like(carry_ref)

    nt_dims = (((1,), (1,)), ((), ()))
    scale = HEAD_DIM ** -0.5
    q = (jnp.dot(n2_ref[...], wq_ref[...].astype(BF16), preferred_element_type=F32) * scale).astype(BF16)
    outs = []
    for hh in range(CROSS_HEADS):
        sl = slice(hh * HEAD_DIM, (hh + 1) * HEAD_DIM)
        kh = kv_ref[:, sl]
        vh = kv_ref[:, CROSS_WIDTH + hh * HEAD_DIM:CROSS_WIDTH + (hh + 1) * HEAD_DIM]
        s = lax.dot_general(q[:, sl], kh, nt_dims, preferred_element_type=F32)
        p = jnp.exp(s - jnp.max(s, axis=-1, keepdims=True))
        l = jnp.sum(p, axis=-1, keepdims=True)
        outs.append(jnp.dot(p.astype(BF16), vh, preferred_element_type=F32) / l)
    o = jnp.concatenate(outs, axis=-1).astype(BF16)
    h2 = h1_ref[...] + jnp.dot(o, wo_ref[...].astype(BF16), preferred_element_type=F32)
    h2_ref[...] = h2
    n3 = _rms(h2, gm_ref[...])
    u = _pack_rows(n3)
    for j in range(MOE_SLAB):
        n3_ref[pl.ds(j, tm, stride=MOE_SLAB), :] = u[:, j * V7X_LANES:(j + 1) * V7X_LANES]

    n3h = n3.astype(BF16)
    n3l = (n3 - n3h.astype(F32)).astype(BF16)
    wrh = wrh_ref[...]
    logits = (jnp.dot(n3h, wrh, preferred_element_type=F32)
              + jnp.dot(n3l, wrh, preferred_element_type=F32)
              + jnp.dot(n3h, wrl_ref[...], preferred_element_type=F32)
              + br_ref[...])
    lane = lax.broadcasted_iota(I32, (tm, N_EXPERTS), 1).astype(F32)
    vals, idxs, hots = [], [], []
    cur = logits
    for _ in range(TOP_K):
        mx = jnp.max(cur, axis=-1, keepdims=True)
        ix = jnp.min(jnp.where(cur == mx, lane, float(N_EXPERTS)), axis=-1, keepdims=True)
        hot = lane == ix
        vals.append(mx)
        idxs.append(ix)
        hots.append(hot)
        cur = jnp.where(hot, -jnp.inf, cur)
    es = [jnp.exp(v - vals[0]) for v in vals]
    den = es[0] + es[1] + es[2] + es[3]
    col = lax.broadcasted_iota(I32, (tm, TOP_K), 1)

    def pack(cols):
        out = jnp.broadcast_to(cols[TOP_K - 1], (tm, TOP_K))
        for k in range(TOP_K - 2, -1, -1):
            out = jnp.where(col == k, cols[k], out)
        return out

    idx_ref[...] = pack(idxs).astype(I32)
    tw_ref[...] = pack([e / den for e in es])

    cmat = (hots[0] | hots[1] | hots[2] | hots[3]).astype(F32)
    rr = lax.broadcasted_iota(I32, (tm, tm), 0)
    cc = lax.broadcasted_iota(I32, (tm, tm), 1)
    tri = (cc < rr).astype(BF16)
    before = jnp.dot(tri, cmat.astype(BF16), preferred_element_type=F32) + carry_ref[...]
    ranks = [jnp.sum(jnp.where(hot, before, 0.0), axis=-1, keepdims=True) for hot in hots]
    rank_ref[...] = pack(ranks).astype(I32)
    carry = carry_ref[...] + jnp.sum(cmat, axis=0, keepdims=True)
    carry_ref[...] = carry
    cnt_ref[...] = carry


def _cross(n2, h1, kv, wq, wo, gm, wrh, wrl, br, *, batch, tm, name="cross"):
    n, d = h1.shape
    per_b = n // batch // tm
    mem_len = kv.shape[0] // batch

    def const(shape):
        return pl.BlockSpec(shape, lambda bi, i: (0, 0), pipeline_mode=pl.Buffered(1))

    def row(bi, i):
        return (bi * per_b + i, 0)

    return pl.pallas_call(
        functools.partial(_cross_kernel, tm=tm),
        out_shape=(
            jax.ShapeDtypeStruct((n, d), F32),
            jax.ShapeDtypeStruct((n * MOE_SLAB, V7X_LANES), U32),
            jax.ShapeDtypeStruct((n, TOP_K), I32),
            jax.ShapeDtypeStruct((n, TOP_K), F32),
            jax.ShapeDtypeStruct((n, TOP_K), I32),
            jax.ShapeDtypeStruct((1, N_EXPERTS), F32),
        ),
        grid=(batch, per_b),
        in_specs=[
            pl.BlockSpec((tm, d), row),
            pl.BlockSpec((tm, d), row),
            pl.BlockSpec((mem_len, 2 * CROSS_WIDTH), lambda bi, i: (bi, 0)),
            const((d, CROSS_WIDTH)), const((CROSS_WIDTH, d)), const((1, d)),
            const((d, N_EXPERTS)), const((d, N_EXPERTS)), const((1, N_EXPERTS)),
        ],
        out_specs=(
            pl.BlockSpec((tm, d), row),
            pl.BlockSpec((tm * MOE_SLAB, V7X_LANES), row),
            pl.BlockSpec((tm, TOP_K), row),
            pl.BlockSpec((tm, TOP_K), row),
            pl.BlockSpec((tm, TOP_K), row),
            pl.BlockSpec((1, N_EXPERTS), lambda bi, i: (0, 0)),
        ),
        scratch_shapes=[pltpu.VMEM((1, N_EXPERTS), F32)],
        compiler_params=_cparams(("arbitrary", "arbitrary")),
        name=name,
    )(n2, h1, kv, wq, wo, gm.reshape(1, d), wrh, wrl, br.reshape(1, N_EXPERTS))


_HI_MASK = 0xFFFF0000


def _pack_rows(x):
    half = x.shape[1] // 2
    lo = lax.bitcast_convert_type(x[:, :half].astype(BF16).astype(F32), U32) >> 16
    hi = lax.bitcast_convert_type(x[:, half:].astype(BF16).astype(F32), U32) & jnp.uint32(_HI_MASK)
    return lo | hi


def _unpack_rows(u):
    lo = lax.bitcast_convert_type(u << 16, F32)
    hi = lax.bitcast_convert_type(u & jnp.uint32(_HI_MASK), F32)
    return lo, hi


def _sc_dispatch(slots, n3p, *, rows_total, name="moe_dispatch_sc"):
    n = n3p.shape[0]
    per_w = n // (SC_CORES * SC_SUBCORES)
    mesh = plsc.VectorSubcoreMesh(core_axis_name="c", subcore_axis_name="s",
                                  num_cores=SC_CORES, num_subcores=SC_SUBCORES)

    nchunk = per_w // SC_CHUNK_TOKENS

    def body(slots_hbm, n3p_hbm, out_hbm, idx_bufs, row_bufs, load_sems, scat_sems):
        wid = lax.axis_index("s") * SC_CORES + lax.axis_index("c")

        def base(c):
            return pl.multiple_of(wid * per_w + c * SC_CHUNK_TOKENS, SC_CHUNK_TOKENS)

        def load(c):
            return pltpu.async_copy(n3p_hbm.at[pl.ds(base(c), SC_CHUNK_TOKENS)], row_bufs.at[c % 2],
                                    load_sems.at[c % 2])

        def scatter(c, k):
            idx = idx_bufs.at[(c % 2) * TOP_K + k]
            pltpu.sync_copy(slots_hbm.at[pl.ds(k * n + base(c), SC_CHUNK_TOKENS)], idx)
            return pltpu.async_copy(row_bufs.at[c % 2], out_hbm.at[idx], scat_sems.at[c % 2])

        loads = [None] * nchunk
        scats = [None] * nchunk
        loads[0] = load(0)
        for c in range(nchunk):
            if c + 1 < nchunk:
                if c >= 1:
                    for s in scats[c - 1]:
                        s.wait()
                loads[c + 1] = load(c + 1)
            loads[c].wait()
            scats[c] = [scatter(c, k) for k in range(TOP_K)]
        for c in range(max(nchunk - 2, 0), nchunk):
            for s in scats[c]:
                s.wait()

    return pl.kernel(
        body,
        out_type=jax.ShapeDtypeStruct((rows_total, MOE_SLAB, V7X_LANES), U32),
        mesh=mesh,
        scratch_types=[
            pltpu.VMEM((2 * TOP_K, SC_CHUNK_TOKENS), I32),
            pltpu.VMEM((2, SC_CHUNK_TOKENS, MOE_SLAB, V7X_LANES), U32),
            pltpu.SemaphoreType.DMA((2,)),
            pltpu.SemaphoreType.DMA((2,)),
        ],
        name=name,
    )(slots, n3p)


def _moe_kernel(te_ref, tv_ref, tstart_ref, trows_ref,
                xs_hbm, w1_ref, b1_ref, w2_ref, b2_ref, yb_hbm,
                stage, x16, acc, w1p, sems, *, nf):
    i = pl.program_id(0)
    f = pl.program_id(1)
    rows = trows_ref[i]
    start = tstart_ref[i]
    nchunk = rows // MOE_ROW_ALIGN
    chunk_slab_rows = MOE_ROW_ALIGN * MOE_SLAB
    half = 2 * V7X_LANES
    nhalf = 2 * MOE_TF // half

    def slot_rows(c):
        return pl.multiple_of((c & 1) * chunk_slab_rows, chunk_slab_rows)

    def hbm_rows(ref, c):
        r0 = pl.multiple_of((start + c * MOE_ROW_ALIGN) * MOE_SLAB, chunk_slab_rows)
        return ref.at[pl.ds(r0, chunk_slab_rows)]

    def in_copy(c):
        return pltpu.make_async_copy(hbm_rows(xs_hbm, c), stage.at[pl.ds(slot_rows(c), chunk_slab_rows)],
                                     sems.at[c & 1])

    def out_copy(c):
        return pltpu.make_async_copy(stage.at[pl.ds(slot_rows(c), chunk_slab_rows)], hbm_rows(yb_hbm, c),
                                     sems.at[c & 1])

    @pl.when((rows > 0) & (f == 0))
    def _():
        in_copy(0).start()

        def land(c, carry):
            @pl.when(c + 1 < nchunk)
            def _():
                in_copy(c + 1).start()

            r0 = pl.multiple_of(c * MOE_ROW_ALIGN, MOE_ROW_ALIGN)

            @pl.when(c < nchunk)
            def _():
                in_copy(c).wait()
                s0 = slot_rows(c)
                owned = lax.broadcasted_iota(I32, (MOE_ROW_ALIGN, V7X_LANES), 0) < tv_ref[i] - r0
                for j in range(MOE_SLAB):
                    u = jnp.where(owned, stage[pl.ds(s0 + j, MOE_ROW_ALIGN, stride=MOE_SLAB), :], jnp.uint32(0))
                    lo, hi = _unpack_rows(u)
                    x16[pl.ds(r0, MOE_ROW_ALIGN), j * V7X_LANES:(j + 1) * V7X_LANES] = lo.astype(BF16)
                    x16[pl.ds(r0, MOE_ROW_ALIGN), (MOE_SLAB + j) * V7X_LANES:(MOE_SLAB + j + 1) * V7X_LANES] = (
                        hi.astype(BF16))

            @pl.when(c >= nchunk)
            def _():
                x16[pl.ds(r0, MOE_ROW_ALIGN), :] = jnp.zeros((MOE_ROW_ALIGN, D_MODEL), BF16)

            acc[pl.ds(r0, MOE_ROW_ALIGN), :] = jnp.zeros((MOE_ROW_ALIGN, D_MODEL), F32)
            return carry

        lax.fori_loop(0, jnp.maximum(nchunk, MOE_BLOCK_CHUNKS), land, 0)

    @pl.when(rows > 0)
    def _():
        lane = lax.broadcasted_iota(I32, (D_MODEL // 2, V7X_LANES), 1)
        first = lane < V7X_LANES // 2
        even = (2 * lane) & (V7X_LANES - 1)
        take = lambda a, ix: jnp.take_along_axis(a, ix, axis=1, mode="promise_in_bounds")

        def separate(c):
            u = pltpu.bitcast(w1_ref[:, c * half:(c + 1) * half].astype(BF16), U32)
            ua, ub = u[:, :V7X_LANES], u[:, V7X_LANES:]
            glu = jnp.where(first, take(ua, even), take(ub, even))
            lin = jnp.where(first, take(ua, even + 1), take(ub, even + 1))
            return pltpu.bitcast(jnp.concatenate([glu, lin], axis=1), BF16)

        def activation(h):
            hg = jnp.minimum(h[:, :V7X_LANES], SWIGLU_LIMIT)
            hl = jnp.clip(h[:, V7X_LANES:], -SWIGLU_LIMIT, SWIGLU_LIMIT)
            return hg * jax.nn.sigmoid(SWIGLU_ALPHA * hg) * (hl + 1.0)

        w2 = w2_ref[...].astype(BF16)
        b1 = b1_ref[...]

        block_rows = MOE_BLOCK_CHUNKS * MOE_ROW_ALIGN
        x = x16[0:block_rows, :]
        parts = []
        for c in range(nhalf):
            wc = separate(c)
            w1p[:, c * half:(c + 1) * half] = wc
            parts.append(activation(jnp.dot(x, wc, preferred_element_type=F32) + b1[:, c * half:(c + 1) * half]))
        a = jnp.concatenate(parts, axis=-1).astype(BF16)
        acc[0:block_rows, :] += jnp.dot(a, w2, preferred_element_type=F32)

        def mlp(c0, nc):
            r0 = pl.multiple_of(c0 * MOE_ROW_ALIGN, MOE_ROW_ALIGN)
            nr = nc * MOE_ROW_ALIGN
            x = x16[pl.ds(r0, nr), :]
            h = jnp.dot(x, w1p[...], preferred_element_type=F32) + b1
            a = jnp.concatenate([activation(h[:, c * half:(c + 1) * half]) for c in range(nhalf)], axis=-1)
            acc[pl.ds(r0, nr), :] += jnp.dot(a.astype(BF16), w2, preferred_element_type=F32)

        extra = jnp.maximum(nchunk - MOE_BLOCK_CHUNKS, 0)
        size = MOE_TMAX // MOE_ROW_ALIGN - MOE_BLOCK_CHUNKS
        assert size & (size - 1) == 0
        while size >= 1:
            done = MOE_BLOCK_CHUNKS + (extra & ~(2 * size - 1))

            @pl.when((extra & size) != 0)
            def _(done=done, size=size):
                mlp(done, size)

            size //= 2

    @pl.when((rows > 0) & (f == nf - 1))
    def _():
        def emit(c, carry):
            @pl.when(c >= 2)
            def _():
                out_copy(c - 2).wait()

            r0 = pl.multiple_of(c * MOE_ROW_ALIGN, MOE_ROW_ALIGN)
            s0 = slot_rows(c)
            u = _pack_rows(acc[pl.ds(r0, MOE_ROW_ALIGN), :] + b2_ref[...])
            for j in range(MOE_SLAB):
                stage[pl.ds(s0 + j, MOE_ROW_ALIGN, stride=MOE_SLAB), :] = u[:, j * V7X_LANES:(j + 1) * V7X_LANES]
            out_copy(c).start()
            return carry

        lax.fori_loop(0, nchunk, emit, 0)

        @pl.when(nchunk >= 2)
        def _():
            out_copy(nchunk - 2).wait()

        out_copy(nchunk - 1).wait()


def _moe(tile_e, tile_owned, tile_start, tile_rows, xs, w1, b1p, w2, b2, *, name="moe_mlp"):
    nt = tile_e.shape[0]
    nf = D_FF // MOE_TF

    def w1_map(i, f, te, tv, ts, tr):
        return (te[i], 0, jnp.where(tv[i] > 0, f, nf - 1))

    def w2_map(i, f, te, tv, ts, tr):
        return (te[i], jnp.where(tv[i] > 0, f, nf - 1), 0)

    def b2_map(i, f, te, tv, ts, tr):
        return (te[i], 0, 0)

    grid_spec = pltpu.PrefetchScalarGridSpec(
        num_scalar_prefetch=4,
        grid=(nt, nf),
        in_specs=[
            pl.BlockSpec(memory_space=pl.ANY),
            pl.BlockSpec((None, D_MODEL, 2 * MOE_TF), w1_map),
            pl.BlockSpec((None, 1, 2 * MOE_TF), w1_map),
            pl.BlockSpec((None, MOE_TF, D_MODEL), w2_map),
            pl.BlockSpec((None, 1, D_MODEL), b2_map),
        ],
        out_specs=pl.BlockSpec(memory_space=pl.ANY),
        scratch_shapes=[
            pltpu.VMEM((2 * MOE_ROW_ALIGN * MOE_SLAB, V7X_LANES), U32),
            pltpu.VMEM((MOE_TMAX, D_MODEL), BF16),
            pltpu.VMEM((MOE_TMAX, D_MODEL), F32),
            pltpu.VMEM((D_MODEL, 2 * MOE_TF), BF16),
            pltpu.SemaphoreType.DMA((2,)),
        ],
    )
    return pl.pallas_call(
        functools.partial(_moe_kernel, nf=nf),
        out_shape=jax.ShapeDtypeStruct(xs.shape, xs.dtype),
        grid_spec=grid_spec,
        input_output_aliases={4: 0},
        compiler_params=_cparams(("arbitrary", "arbitrary")),
        name=name,
    )(tile_e, tile_owned, tile_start, tile_rows, xs, w1, b1p.reshape(N_EXPERTS, 1, 2 * D_FF), w2,
      b2.reshape(N_EXPERTS, 1, D_MODEL))


def _sc_gather(slots, yb3, *, n_tok, name="moe_gather_sc"):
    per_w = n_tok // (SC_CORES * SC_SUBCORES)
    mesh = plsc.VectorSubcoreMesh(core_axis_name="c", subcore_axis_name="s",
                                  num_cores=SC_CORES, num_subcores=SC_SUBCORES)

    items = [(c, k) for c in range(per_w // SC_CHUNK_TOKENS) for k in range(TOP_K)]

    def body(slots_hbm, yb_hbm, out_hbm, idx_bufs, row_bufs, gather_sems, write_sems):
        wid = lax.axis_index("s") * SC_CORES + lax.axis_index("c")

        def offset(j):
            c, k = items[j]
            return pl.multiple_of(k * n_tok + wid * per_w + c * SC_CHUNK_TOKENS, SC_CHUNK_TOKENS)

        def gather(j):
            idx = idx_bufs.at[j % 2]
            pltpu.sync_copy(slots_hbm.at[pl.ds(offset(j), SC_CHUNK_TOKENS)], idx)
            return pltpu.async_copy(yb_hbm.at[idx], row_bufs.at[j % 2], gather_sems.at[j % 2])

        def write(j):
            return pltpu.async_copy(row_bufs.at[j % 2], out_hbm.at[pl.ds(offset(j), SC_CHUNK_TOKENS)],
                                    write_sems.at[j % 2])

        n_items = len(items)
        gathers = [None] * n_items
        writes = [None] * n_items
        gathers[0] = gather(0)
        for j in range(n_items):
            if j + 1 < n_items:
                if j >= 1:
                    writes[j - 1].wait()
                gathers[j + 1] = gather(j + 1)
            gathers[j].wait()
            writes[j] = write(j)
        for j in range(max(n_items - 2, 0), n_items):
            writes[j].wait()

    return pl.kernel(
        body,
        out_type=jax.ShapeDtypeStruct((TOP_K * n_tok, MOE_SLAB, V7X_LANES), U32),
        mesh=mesh,
        scratch_types=[
            pltpu.VMEM((2, SC_CHUNK_TOKENS), I32),
            pltpu.VMEM((2, SC_CHUNK_TOKENS, MOE_SLAB, V7X_LANES), U32),
            pltpu.SemaphoreType.DMA((2,)),
            pltpu.SemaphoreType.DMA((2,)),
        ],
        name=name,
    )(slots, yb3)


def _combine_kernel(yg_ref, h2_ref, tw_ref, gf_ref, o_ref, *, tm):
    tw = tw_ref[...]
    cols_lo, cols_hi = [], []
    for j in range(MOE_SLAB):
        clo = chi = None
        for k in range(TOP_K):
            lo, hi = _unpack_rows(yg_ref[k, pl.ds(j, tm, stride=MOE_SLAB), :])
            w = tw[:, k:k + 1]
            clo = w * lo if clo is None else clo + w * lo
            chi = w * hi if chi is None else chi + w * hi
        cols_lo.append(clo)
        cols_hi.append(chi)
    h3 = h2_ref[...] + jnp.concatenate(cols_lo + cols_hi, axis=-1)
    o_ref[...] = _rms(h3, gf_ref[...])


def _combine(yg, h2, tw, gf, *, tm, name="moe_combine"):
    n, d = h2.shape
    return pl.pallas_call(
        functools.partial(_combine_kernel, tm=tm),
        out_shape=jax.ShapeDtypeStruct((n, d), F32),
        grid=(n // tm,),
        in_specs=[
            pl.BlockSpec((TOP_K, tm * MOE_SLAB, V7X_LANES), lambda i: (0, i, 0)),
            pl.BlockSpec((tm, d), lambda i: (i, 0)),
            pl.BlockSpec((tm, TOP_K), lambda i: (i, 0)),
            pl.BlockSpec((1, d), lambda i: (0, 0)),
        ],
        out_specs=pl.BlockSpec((tm, d), lambda i: (i, 0)),
        compiler_params=_cparams(("parallel",)),
        name=name,
    )(yg, h2, tw, gf.reshape(1, d))


def _rope_tables(positions):
    b, s = positions.shape
    half = HEAD_DIM // 2
    inv_freq = 1.0 / (ROPE_THETA ** (jnp.arange(half, dtype=F32) / half))
    ang = positions.astype(F32)[..., None] * inv_freq
    cos = jnp.cos(ang)
    sin = jnp.sin(ang)
    cs = jnp.concatenate([cos, cos], axis=-1)
    sn = jnp.concatenate([-sin, sin], axis=-1)

    def deint(a):
        a = a.reshape(b, s // ATT_TILE, WIN_BLOCK, DIL_MAX, HEAD_DIM)
        return a.transpose(0, 1, 3, 2, 4)

    return deint(cs), deint(sn)


def _routing(counts, n_tok):
    cnt = counts.reshape(N_EXPERTS).astype(I32)
    padded = (cnt + MOE_ROW_ALIGN - 1) // MOE_ROW_ALIGN * MOE_ROW_ALIGN
    gstart = (jnp.cumsum(padded) - padded).astype(I32)
    rows_total = n_tok * TOP_K + N_EXPERTS * MOE_ROW_ALIGN

    def first_above(ends, q):
        return jnp.minimum(jnp.sum((ends[None, :] <= q[:, None]).astype(I32), axis=1), N_EXPERTS - 1)

    main_rows = jnp.minimum(padded, MOE_TMAX)
    main = (jnp.arange(N_EXPERTS, dtype=I32), jnp.minimum(cnt, MOE_TMAX).astype(I32), gstart.astype(I32),
            main_rows.astype(I32))

    over = padded - main_rows
    nt_e = (over + MOE_TMAX - 1) // MOE_TMAX
    tend = jnp.cumsum(nt_e)
    tstart = tend - nt_e
    n_over = tend[-1]
    ti = jnp.arange(rows_total // MOE_TMAX, dtype=I32)
    valid = ti < n_over
    tic = jnp.clip(ti, 0, jnp.maximum(n_over - 1, 0))
    te = first_above(tend, tic)
    local = tic - tstart[te]
    row0 = gstart[te] + (local + 1) * MOE_TMAX
    rows = jnp.where(valid, jnp.clip(over[te] - local * MOE_TMAX, 0, MOE_TMAX), 0)
    owned = jnp.where(valid, jnp.clip(cnt[te] - (local + 1) * MOE_TMAX, 0, MOE_TMAX), 0)
    overflow = (te, owned.astype(I32), row0.astype(I32), rows.astype(I32))
    return gstart, rows_total, main, overflow, n_over > 0


def kernel(x, mem, positions, norm_mix_g, w_in, lb_raw, hgrn_norm_g, w_br_hgrn, w_br_attn, w_out,
           norm_cross_g, norm_mem_g, w_cq, w_ckv, w_co, norm_moe_g, w_router, b_router,
           w_mlp1, b_mlp1, w_mlp2, b_mlp2, norm_final_g):
    bsz, seq, d = x.shape
    assert w_in.shape[0] == 1 and d == D_MODEL and seq % ATT_TILE == 0
    n_tok = bsz * seq
    lower_bounds = jnp.cumsum(jax.nn.softmax(lb_raw.astype(F32), axis=0), axis=0)
    cs, sn = _rope_tables(positions)

    wl = w_in[0]
    c_h = 4 * HG_WIDTH
    c_a = c_h + ATT_Q_WIDTH + 2 * ATT_KV_WIDTH
    tn = 1024
    x2d = x.reshape(n_tok, d)
    n_h, n_a, n_g = c_h // tn, (c_a - c_h) // tn, 2 * D_MODEL // tn
    proj = _in_proj(x2d, norm_mix_g[0], wl, tm=1024, tn=tn, name="in_proj",
                    col_block=lambda j: jnp.where(j < n_h, j, jnp.where(j < n_h + n_g, j + n_a, j - n_g)))
    proj3 = proj.reshape(bsz, seq, -1)

    o_h = _hgrn(proj3, lower_bounds[0], hgrn_norm_g[0], ts=1024)
    o_a = _dilated_attn(proj3, cs, sn, q_block=(c_h + 2 * D_MODEL) // HEAD_DIM)

    h1, n2 = _merge_out(o_h.reshape(n_tok, HG_WIDTH), o_a.reshape(n_tok, ATT_KV_WIDTH), proj, x2d,
                        w_br_hgrn[0].astype(BF16), w_br_attn[0].astype(BF16), w_out[0].astype(BF16),
                        norm_cross_g[0], tm=256, gate_block=c_h // (2 * D_MODEL))

    mem2d = mem.reshape(-1, d)
    kv = _in_proj(mem2d, norm_mem_g, w_ckv[0], tm=mem2d.shape[0], tn=2 * CROSS_WIDTH, name="mem_kv")
    wrh = w_router[0].astype(BF16)
    wrl = (w_router[0] - wrh.astype(F32)).astype(BF16)
    h2, n3, top_e, top_w, rank, counts = _cross(
        n2, h1, kv, w_cq[0], w_co[0], norm_moe_g[0], wrh, wrl, b_router[0],
        batch=bsz, tm=512)

    gstart, rows_total, main_tiles, over_tiles, has_over = _routing(counts, n_tok)
    hot = top_e.T[None] == jnp.arange(N_EXPERTS, dtype=I32)[:, None, None]
    slots = (rank.T + jnp.sum(jnp.where(hot, gstart[:, None, None], 0), axis=0)).reshape(-1)
    xs = _sc_dispatch(slots, n3.reshape(n_tok, MOE_SLAB, V7X_LANES), rows_total=rows_total)
    xs = xs.reshape(rows_total * MOE_SLAB, V7X_LANES)

    b1p = b_mlp1[0].reshape(N_EXPERTS, -1, V7X_LANES, 2).transpose(0, 1, 3, 2).reshape(N_EXPERTS, 2 * D_FF)
    def run_moe(tiles, rows_buf, name):
        return _moe(*tiles, rows_buf, w_mlp1[0], b1p, w_mlp2[0], b_mlp2[0], name=name)

    yb = run_moe(main_tiles, xs, "moe_mlp")
    yb = lax.cond(has_over, lambda rows_buf: run_moe(over_tiles, rows_buf, "moe_mlp_overflow"),
                  lambda rows_buf: rows_buf, yb)

    yg = _sc_gather(slots, yb.reshape(rows_total, MOE_SLAB, V7X_LANES), n_tok=n_tok)
    out = _combine(yg.reshape(TOP_K, n_tok * MOE_SLAB, V7X_LANES), h2, top_w, norm_final_g, tm=COMBINE_TOKENS)
    return out.reshape(bsz, seq, d)
```
